```python
import math
import jax, jax.numpy as jnp
from jax import lax
import numpy as np

D_MODEL = 4096
BATCH = 1
SEQ = 16384
DEPTH = 4
DEC_BATCH = 16
DEC_SEQ = 16
PAST_LEN = 2048

CHUNK = 64
Q_BLOCK = 128
EPS = 1e-6
N_BRANCH = 3
BRANCH_WIDTH = D_MODEL // 4
GDN_DK = 128
GDN_DV = 128
GDN_HEADS = BRANCH_WIDTH // GDN_DV
GDN_CONV = 4
GDN_QKV_DIM = GDN_HEADS * (2 * GDN_DK + GDN_DV)
SSM_D_INNER = BRANCH_WIDTH
SSM_HEAD_DIM = 64
SSM_HEADS = SSM_D_INNER // SSM_HEAD_DIM
SSM_GROUPS = 2
SSM_STATE = 128
SSM_CONV = 4
SSM_CONV_DIM = SSM_D_INNER + 2 * SSM_GROUPS * SSM_STATE
MLA_V = 128
MLA_HEADS = BRANCH_WIDTH // MLA_V
MLA_NOPE = 128
MLA_ROPE = 64
MLA_Q_RANK = 768
MLA_KV_RANK = 512
ROPE_THETA = 10000.0
D_FF = 256 * ((8 * D_MODEL // 3 + 255) // 256)
FFN_CONV = 3
SPLIT_SIZES = (GDN_QKV_DIM, GDN_HEADS * GDN_DV, GDN_HEADS, GDN_HEADS,
               SSM_D_INNER, SSM_CONV_DIM, SSM_HEADS,
               MLA_Q_RANK, MLA_KV_RANK, MLA_ROPE, N_BRANCH * D_MODEL)
D_IN = (GDN_QKV_DIM + GDN_HEADS * GDN_DV + 2 * GDN_HEADS + SSM_D_INNER + SSM_CONV_DIM + SSM_HEADS
        + MLA_Q_RANK + MLA_KV_RANK + MLA_ROPE + N_BRANCH * D_MODEL)

kernel_name = 'hybrid_gdn_ssd_mla_streaming_step'


def _rmsnorm(x, w):
    xf = x.astype(jnp.float32)
    y = xf * lax.rsqrt(jnp.mean(xf * xf, axis=-1, keepdims=True) + EPS)
    return (y * w.astype(jnp.float32)).astype(x.dtype)


def _l2norm(x):
    xf = x.astype(jnp.float32)
    return xf * lax.rsqrt(jnp.sum(xf * xf, axis=-1, keepdims=True) + EPS)


def _split(p):
    outs, off = [], 0
    for n in SPLIT_SIZES:
        outs.append(p[..., off:off + n])
        off += n
    return outs


def _causal_dwconv(x, buf, w, b=None):
    width, s = w.shape[0], x.shape[1]
    xp = jnp.concatenate([buf.astype(x.dtype), x], axis=1)
    y = xp[:, 0:s] * w[0]
    for j in range(1, width):
        y = y + xp[:, j:j + s] * w[j]
    if b is not None:
        y = y + b
    return y, xp[:, -(width - 1):].astype(buf.dtype)


def _rope(x, pos):
    half = x.shape[-1] // 2
    inv = ROPE_THETA ** (-jnp.arange(half, dtype=jnp.float32) / half)
    ang = pos.astype(jnp.float32)[:, None] * inv[None, :]
    cos, sin = jnp.cos(ang)[None, :, None, :], jnp.sin(ang)[None, :, None, :]
    x1, x2 = x[..., :half].astype(jnp.float32), x[..., half:].astype(jnp.float32)
    return jnp.concatenate([x1 * cos - x2 * sin, x2 * cos + x1 * sin], axis=-1).astype(x.dtype)


def _gated_delta_chunked(q, k, v, g, beta, s0):
    f32 = jnp.float32
    b, s, h, dk = q.shape
    dv = v.shape[-1]
    c = min(CHUNK, s)
    n = s // c
    blk = lambda t: t.astype(f32).reshape(b, n, c, *t.shape[2:])
    q = blk(q) * (dk ** -0.5)
    k, v, g, beta = blk(k), blk(v), blk(g), blk(beta)
    gc = jnp.cumsum(g, axis=2)
    gt = jnp.moveaxis(gc, 3, 2)
    tri = jnp.tril(jnp.ones((c, c), bool))
    strict = jnp.tril(jnp.ones((c, c), bool), -1)
    diff = gt[..., :, None] - gt[..., None, :]
    decay = jnp.where(tri, jnp.exp(jnp.where(tri, diff, 0.0)), 0.0)
    kb = k * beta[..., None]
    m = jnp.where(strict, jnp.einsum('bnihd,bnjhd->bnhij', kb, k) * decay, 0.0)
    eye = jnp.eye(c, dtype=f32)
    tinv = lax.linalg.triangular_solve(eye + m, jnp.broadcast_to(eye, m.shape),
                                       left_side=True, lower=True, unit_diagonal=True)
    u = jnp.einsum('bnhij,bnjhd->bnihd', tinv, v * beta[..., None])
    w = jnp.einsum('bnhij,bnjhd->bnihd', tinv, kb * jnp.exp(gc)[..., None])
    a_intra = jnp.einsum('bnihd,bnjhd->bnhij', q, k) * decay
    q_dec = q * jnp.exp(gc)[..., None]
    k_dec = k * jnp.exp(gc[:, :, -1:] - gc)[..., None]
    g_last = jnp.exp(gc[:, :, -1])

    def step(S, xs):
        u_n, w_n, qd_n, kd_n, a_n, gl_n = xs
        v_new = u_n - jnp.einsum('bchk,bhkv->bchv', w_n, S)
        o_n = jnp.einsum('bchk,bhkv->bchv', qd_n, S) + jnp.einsum('bhij,bjhv->bihv', a_n, v_new)
        S = S * gl_n[..., None, None] + jnp.einsum('bchk,bchv->bhkv', kd_n, v_new)
        return S, o_n

    xs = tuple(jnp.moveaxis(arr, 1, 0) for arr in (u, w, q_dec, k_dec, a_intra, g_last))
    S, o = lax.scan(step, s0.astype(f32), xs)
    return jnp.moveaxis(o, 0, 1).reshape(b, s, h, dv), S


def _ssd_chunked(x, dt, a_neg, bm, cm, h0):
    f32 = jnp.float32
    b, s, H, P = x.shape
    G, N = bm.shape[2], bm.shape[3]
    R = H // G
    c = min(CHUNK, s)
    n = s // c
    x = x.astype(f32).reshape(b, n, c, G, R, P)
    dt = dt.astype(f32).reshape(b, n, c, G, R)
    bm = bm.astype(f32).reshape(b, n, c, G, N)
    cm = cm.astype(f32).reshape(b, n, c, G, N)
    acum = jnp.cumsum(dt * a_neg.astype(f32).reshape(G, R), axis=2)
    at = jnp.moveaxis(acum, 2, -1)
    tri = jnp.tril(jnp.ones((c, c), bool))
    diff = at[..., :, None] - at[..., None, :]
    lmat = jnp.where(tri, jnp.exp(jnp.where(tri, diff, 0.0)), 0.0)
    cb = jnp.einsum('bnige,bnjge->bngij', cm, bm)
    xdt = x * dt[..., None]
    y_diag = jnp.einsum('bngrij,bnjgrp->bnigrp', lmat * cb[:, :, :, None], xdt)
    decay_end = jnp.exp(acum[:, :, -1:] - acum)
    chunk_state = jnp.einsum('bncge,bncgrp->bngrpe', bm, xdt * decay_end[..., None])
    a_last = jnp.exp(acum[:, :, -1])
    a_in = jnp.exp(acum)

    def step(hs, xs):
        cs_n, al_n, c_n, ai_n = xs
        y_off = jnp.einsum('bcge,bgrpe->bcgrp', c_n, hs) * ai_n[..., None]
        hs = hs * al_n[..., None, None] + cs_n
        return hs, y_off

    xs = tuple(jnp.moveaxis(arr, 1, 0) for arr in (chunk_state, a_last, cm, a_in))
    hs, y_off = lax.scan(step, h0.astype(f32).reshape(b, G, R, P, N), xs)
    y = y_diag + jnp.moveaxis(y_off, 0, 1)
    return y.reshape(b, s, H, P), hs.reshape(b, H, P, N)


def _mla_attend(q_nope, q_rope, k_nope, k_rope, v, q_chunk, k_chunk):
    scale = (MLA_NOPE + MLA_ROPE) ** -0.5

    def block(args):
        qn, qr, qc = args
        sc = (jnp.einsum('bqhd,bkhd->bhqk', qn, k_nope)
              + jnp.einsum('bqhr,bkr->bhqk', qr, k_rope)).astype(jnp.float32) * scale
        sc = jnp.where(k_chunk[None, :] <= qc[:, None], sc, -jnp.inf)
        pr = jax.nn.softmax(sc, axis=-1).astype(v.dtype)
        return jnp.einsum('bhqk,bkhd->bqhd', pr, v)

    nq = q_nope.shape[1]
    if nq <= Q_BLOCK:
        return block((q_nope, q_rope, q_chunk))
    nb = nq // Q_BLOCK
    split = lambda t: jnp.moveaxis(t.reshape(t.shape[0], nb, Q_BLOCK, *t.shape[2:]), 1, 0)
    out = lax.map(block, (split(q_nope), split(q_rope), q_chunk.reshape(nb, Q_BLOCK)))
    return jnp.moveaxis(out, 0, 1).reshape(q_nope.shape[0], nq, *out.shape[3:])


def _layer(x, pos, state, lp):
    lat_c, kr_c, gconv, gssm, mconv, mssm, fconv = state
    (attn_norm, w_in, gdn_conv_w, gdn_A_log, gdn_dt_bias, gdn_norm, ssm_conv_w, ssm_conv_b, ssm_A_log,
     ssm_dt_bias, ssm_D, ssm_norm, mla_q_norm, mla_w_uq, mla_kv_norm, mla_w_uk, mla_w_uv, w_branch, w_out,
     ffn_norm, ffn_w_up, ffn_conv_w, ffn_conv_b, ffn_w_down) = lp
    f32 = jnp.float32
    b, s, _ = x.shape
    p = _rmsnorm(x, attn_norm) @ w_in
    (qkv, z_a, beta_raw, alpha_raw, z_b, xbc, dt_raw, c_q, c_kv, k_r, gate_raw) = _split(p)

    qkv, gconv_new = _causal_dwconv(qkv, gconv, gdn_conv_w)
    qkv = jax.nn.silu(qkv)
    nqk = GDN_HEADS * GDN_DK
    q_a = _l2norm(qkv[..., :nqk].reshape(b, s, GDN_HEADS, GDN_DK))
    k_a = _l2norm(qkv[..., nqk:2 * nqk].reshape(b, s, GDN_HEADS, GDN_DK))
    v_a = qkv[..., 2 * nqk:].reshape(b, s, GDN_HEADS, GDN_DV)
    beta = jax.nn.sigmoid(beta_raw.astype(f32))
    g = -jnp.exp(gdn_A_log.astype(f32)) * jax.nn.softplus(alpha_raw.astype(f32) + gdn_dt_bias.astype(f32))
    o_a, S_a = _gated_delta_chunked(q_a, k_a, v_a, g, beta, gssm)
    o_a = (_rmsnorm(o_a, gdn_norm) * jax.nn.silu(z_a.reshape(b, s, GDN_HEADS, GDN_DV).astype(f32)))
    o_a = o_a.astype(x.dtype).reshape(b, s, BRANCH_WIDTH)
    gssm_new = S_a.astype(gssm.dtype)

    xbc, mconv_new = _causal_dwconv(xbc, mconv, ssm_conv_w, ssm_conv_b)
    xbc = jax.nn.silu(xbc)
    nbc = SSM_GROUPS * SSM_STATE
    x_b = xbc[..., :SSM_D_INNER].reshape(b, s, SSM_HEADS, SSM_HEAD_DIM)
    b_m = xbc[..., SSM_D_INNER:SSM_D_INNER + nbc].reshape(b, s, SSM_GROUPS, SSM_STATE)
    c_m = xbc[..., SSM_D_INNER + nbc:].reshape(b, s, SSM_GROUPS, SSM_STATE)
    dt = jax.nn.softplus(dt_raw.astype(f32) + ssm_dt_bias.astype(f32))
    y_b, H_b = _ssd_chunked(x_b, dt, -jnp.exp(ssm_A_log.astype(f32)), b_m, c_m, mssm)
    y_b = y_b + ssm_D.astype(f32)[:, None] * x_b.astype(f32)
    y_b = y_b.reshape(b, s, SSM_D_INNER) * jax.nn.silu(z_b.astype(f32))
    o_b = _rmsnorm(y_b.reshape(b, s, SSM_GROUPS, SSM_D_INNER // SSM_GROUPS),
                   ssm_norm.reshape(SSM_GROUPS, -1)).reshape(b, s, SSM_D_INNER).astype(x.dtype)
    mssm_new = H_b.astype(mssm.dtype)

    qh = jnp.einsum('bsr,rhd->bshd', _rmsnorm(c_q, mla_q_norm), mla_w_uq)
    q_nope = qh[..., :MLA_NOPE]
    q_rope = _rope(qh[..., MLA_NOPE:], pos)
    lat = _rmsnorm(c_kv, mla_kv_norm)
    kr_new = _rope(k_r[:, :, None, :], pos)[:, :, 0]
    lat_all = jnp.concatenate([lat_c.astype(lat.dtype), lat], axis=1)
    kr_all = jnp.concatenate([kr_c.astype(kr_new.dtype), kr_new], axis=1)
    k_nope = jnp.einsum('bkc,chd->bkhd', lat_all, mla_w_uk)
    v_c = jnp.einsum('bkc,chd->bkhd', lat_all, mla_w_uv)
    k_pos = jnp.concatenate([jnp.arange(lat_c.shape[1], dtype=jnp.int32), pos])
    o_c = _mla_attend(q_nope, q_rope, k_nope, kr_all, v_c, pos // CHUNK, k_pos // CHUNK)
    o_c = o_c.reshape(b, s, BRANCH_WIDTH)

    branches = jnp.stack([o_a, o_b, o_c], axis=2)
    u = jnp.einsum('bsnc,ncd->bsnd', branches, w_branch)
    gate = jax.nn.sigmoid(gate_raw.reshape(b, s, N_BRANCH, D_MODEL))
    x = x + jnp.einsum('bsnd,bsnd->bsd', gate, u) @ w_out

    up = _rmsnorm(x, ffn_norm) @ ffn_w_up
    up, fconv_new = _causal_dwconv(up, fconv, ffn_conv_w, ffn_conv_b)
    x = x + (jax.nn.silu(up[..., :D_FF]) * up[..., D_FF:]) @ ffn_w_down
    return x, (lat, kr_new, gconv_new, gssm_new, mconv_new, mssm_new, fconv_new)


def setup_inputs(seed: int = 0) -> dict:
    key = jax.random.key(seed)
    ks = iter(jax.random.split(key, 48))
    f32 = jnp.float32
    L = DEPTH
    res = (2 * DEPTH) ** -0.5

    def nrm(shape, scale):
        return jax.random.normal(next(ks), shape, f32) * scale

    def gain(shape):
        return 1.0 + nrm(shape, 0.02)

    def a_log(shape):
        return jnp.log(jax.random.uniform(next(ks), shape, f32, minval=1.0, maxval=16.0))

    def dt_bias(shape):
        dt = jnp.exp(jax.random.uniform(next(ks), shape, f32, minval=math.log(1e-3), maxval=math.log(1e-1)))
        return dt + jnp.log(-jnp.expm1(-dt))

    return {
        'x_prompt': nrm((BATCH, SEQ, D_MODEL), 1.0),
        'x_sample': nrm((DEC_BATCH, DEC_SEQ, D_MODEL), 1.0),
        'cache_mla_latent': nrm((L, DEC_BATCH, PAST_LEN, MLA_KV_RANK), 1.0),
        'cache_mla_krope': nrm((L, DEC_BATCH, PAST_LEN, MLA_ROPE), 1.0),
        'state_gdn_conv': nrm((L, DEC_BATCH, GDN_CONV - 1, GDN_QKV_DIM), 1.0),
        'state_gdn_ssm': nrm((L, DEC_BATCH, GDN_HEADS, GDN_DK, GDN_DV), GDN_DK ** -0.5),
        'state_ssm_conv': nrm((L, DEC_BATCH, SSM_CONV - 1, SSM_CONV_DIM), 1.0),
        'state_ssm': nrm((L, DEC_BATCH, SSM_HEADS, SSM_HEAD_DIM, SSM_STATE), 0.1),
        'state_ffn_conv': nrm((L, DEC_BATCH, FFN_CONV - 1, 2 * D_FF), 1.0),
        'attn_norm': gain((L, D_MODEL)),
        'w_in': nrm((L, D_MODEL, D_IN), D_MODEL ** -0.5),
        'gdn_conv_w': nrm((L, GDN_CONV, GDN_QKV_DIM), GDN_CONV ** -0.5),
        'gdn_A_log': a_log((L, GDN_HEADS)),
        'gdn_dt_bias': dt_bias((L, GDN_HEADS)),
        'gdn_norm': gain((L, GDN_DV)),
        'ssm_conv_w': nrm((L, SSM_CONV, SSM_CONV_DIM), SSM_CONV ** -0.5),
        'ssm_conv_b': nrm((L, SSM_CONV_DIM), 0.02),
        'ssm_A_log': a_log((L, SSM_HEADS)),
        'ssm_dt_bias': dt_bias((L, SSM_HEADS)),
        'ssm_D': 1.0 + nrm((L, SSM_HEADS), 0.1),
        'ssm_norm': gain((L, SSM_D_INNER)),
        'mla_q_norm': gain((L, MLA_Q_RANK)),
        'mla_w_uq': nrm((L, MLA_Q_RANK, MLA_HEADS, MLA_NOPE + MLA_ROPE), MLA_Q_RANK ** -0.5),
        'mla_kv_norm': gain((L, MLA_KV_RANK)),
        'mla_w_uk': nrm((L, MLA_KV_RANK, MLA_HEADS, MLA_NOPE), MLA_KV_RANK ** -0.5),
        'mla_w_uv': nrm((L, MLA_KV_RANK, MLA_HEADS, MLA_V), MLA_KV_RANK ** -0.5),
        'w_branch': nrm((L, N_BRANCH, BRANCH_WIDTH, D_MODEL), BRANCH_WIDTH ** -0.5),
        'w_out': nrm((L, D_MODEL, D_MODEL), D_MODEL ** -0.5 * res),
        'ffn_norm': gain((L, D_MODEL)),
        'ffn_w_up': nrm((L, D_MODEL, 2 * D_FF), D_MODEL ** -0.5),
        'ffn_conv_w': nrm((L, FFN_CONV, 2 * D_FF), FFN_CONV ** -0.5),
        'ffn_conv_b': nrm((L, 2 * D_FF), 0.02),
        'ffn_w_down': nrm((L, D_FF, D_MODEL), D_FF ** -0.5 * res),
        'final_norm': gain((D_MODEL,)),
    }


def reference(x_prompt, x_sample, cache_mla_latent, cache_mla_krope, state_gdn_conv, state_gdn_ssm,
              state_ssm_conv, state_ssm, state_ffn_conv, attn_norm, w_in, gdn_conv_w, gdn_A_log, gdn_dt_bias,
              gdn_norm, ssm_conv_w, ssm_conv_b, ssm_A_log, ssm_dt_bias, ssm_D, ssm_norm, mla_q_norm, mla_w_uq,
              mla_kv_norm, mla_w_uk, mla_w_uv, w_branch, w_out, ffn_norm, ffn_w_up, ffn_conv_w, ffn_conv_b,
              ffn_w_down, final_norm):
    bp, sp = x_prompt.shape[0], x_prompt.shape[1]
    n_past = cache_mla_latent.shape[2]
    pos_p = jnp.arange(sp, dtype=jnp.int32)
    pos_s = n_past + jnp.arange(x_sample.shape[1], dtype=jnp.int32)
    zero_state = (
        jnp.zeros((bp, 0, MLA_KV_RANK), cache_mla_latent.dtype),
        jnp.zeros((bp, 0, MLA_ROPE), cache_mla_krope.dtype),
        jnp.zeros((bp, GDN_CONV - 1, GDN_QKV_DIM), state_gdn_conv.dtype),
        jnp.zeros((bp, GDN_HEADS, GDN_DK, GDN_DV), state_gdn_ssm.dtype),
        jnp.zeros((bp, SSM_CONV - 1, SSM_CONV_DIM), state_ssm_conv.dtype),
        jnp.zeros((bp, SSM_HEADS, SSM_HEAD_DIM, SSM_STATE), state_ssm.dtype),
        jnp.zeros((bp, FFN_CONV - 1, 2 * D_FF), state_ffn_conv.dtype),
    )
    xp, xs = x_prompt, x_sample
    new_p, new_s = [], []
    for l in range(DEPTH):
        lp = (attn_norm[l], w_in[l], gdn_conv_w[l], gdn_A_log[l], gdn_dt_bias[l], gdn_norm[l],
              ssm_conv_w[l], ssm_conv_b[l], ssm_A_log[l], ssm_dt_bias[l], ssm_D[l], ssm_norm[l],
              mla_q_norm[l], mla_w_uq[l], mla_kv_norm[l], mla_w_uk[l], mla_w_uv[l], w_branch[l], w_out[l],
              ffn_norm[l], ffn_w_up[l], ffn_conv_w[l], ffn_conv_b[l], ffn_w_down[l])
        xp, st_p = _layer(xp, pos_p, zero_state, lp)
        st_in = (cache_mla_latent[l], cache_mla_krope[l], state_gdn_conv[l], state_gdn_ssm[l],
                 state_ssm_conv[l], state_ssm[l], state_ffn_conv[l])
        xs, st_s = _layer(xs, pos_s, st_in, lp)
        new_p.append(st_p)
        new_s.append(st_s)
    y_prompt = _rmsnorm(xp, final_norm)
    y_sample = _rmsnorm(xs, final_norm)
    lat_p, kr_p, gconv_p, gssm_p, mconv_p, mssm_p, fconv_p = [jnp.stack(t) for t in zip(*new_p)]
    lat_s, kr_s, gconv_s, gssm_s, mconv_s, mssm_s, fconv_s = [jnp.stack(t) for t in zip(*new_s)]
    return (y_prompt, y_sample, lat_p, kr_p, gconv_p, gssm_p, mconv_p, mssm_p, fconv_p,
            lat_s, kr_s, gconv_s, gssm_s, mconv_s, mssm_s, fconv_s)
```

```python
import functools
import math

import jax
import jax.numpy as jnp
from jax import lax
from jax.experimental import pallas as pl
from jax.experimental.pallas import tpu as pltpu

F32 = jnp.float32
BF16 = jnp.bfloat16
HI = lax.Precision.HIGHEST

EPS = 1e-6
CHUNK = 64
D_MODEL = 4096
BRANCH = 1024
GDN_H, GDN_D = 8, 128
GDN_QKV = 3072
SSM_H, SSM_P, SSM_G, SSM_N = 16, 64, 2, 128
SSM_CONV_DIM = 1536
MLA_H, MLA_NOPE, MLA_ROPE, MLA_V = 8, 128, 64, 128
MLA_Q_RANK, MLA_KV_RANK = 768, 512
ROPE_THETA = 10000.0
D_FF = 11008
LATKR = 640
QSLOT = 256

P_GATE = 0
P_MLA = 12288
P_SMALL = P_MLA + 1280
P_XBC = 13824
P_QKV = 15360
P_ZA = 18432
P_ZB = 19456
P_TOT = 20480
L_BETA, L_ALPHA, L_DT = 64, 72, 80

VMEM_LIMIT = 56 * 1024 * 1024


def _cp(sem):
    return pltpu.CompilerParams(dimension_semantics=sem, vmem_limit_bytes=VMEM_LIMIT)


def _softplus(x):
    return jnp.maximum(x, 0.0) + jnp.log1p(jnp.exp(-jnp.abs(x)))


def _silu(x):
    return x * jax.nn.sigmoid(x)


def _nt(a, b):
    return lax.dot_general(a, b, (((1,), (1,)), ((), ())), preferred_element_type=F32)


def _tn(a, b):
    return lax.dot_general(a, b, (((0,), (0,)), ((), ())), preferred_element_type=F32)


def _rms_kernel(x_ref, w_ref, o_ref):
    x = x_ref[...]
    ms = jnp.mean(x * x, axis=-1, keepdims=True)
    o_ref[...] = (x * lax.rsqrt(ms + EPS) * w_ref[...]).astype(o_ref.dtype)


def rmsnorm(x, w, out_dtype):
    m, d = x.shape
    tm = min(256, m)
    return pl.pallas_call(
        _rms_kernel,
        out_shape=jax.ShapeDtypeStruct((m, d), out_dtype),
        grid=(m // tm,),
        in_specs=[pl.BlockSpec((tm, d), lambda i: (i, 0)),
                  pl.BlockSpec((1, d), lambda i: (0, 0))],
        out_specs=pl.BlockSpec((tm, d), lambda i: (i, 0)),
        compiler_params=_cp(("parallel",)),
        name="rmsnorm",
    )(x, w.reshape(1, d))


def _mm_kernel(*refs, nk, n_extra, epilogue):
    a_ref, b_ref = refs[0], refs[1]
    extras = refs[2:2 + n_extra]
    o_ref = refs[2 + n_extra]
    part = jnp.dot(a_ref[...], b_ref[...], preferred_element_type=F32)
    if nk == 1:
        o_ref[...] = epilogue(part, *extras).astype(o_ref.dtype)
        return
    acc_ref = refs[3 + n_extra]
    k = pl.program_id(2)

    @pl.when(k == 0)
    def _():
        acc_ref[...] = part

    @pl.when(k > 0)
    def _():
        acc_ref[...] += part

    @pl.when(k == nk - 1)
    def _():
        o_ref[...] = epilogue(acc_ref[...], *extras).astype(o_ref.dtype)


def matmul(a, b, out_dtype, *, tm, tn, tk=None, extras=(), epilogue=None, name="matmul"):
    m, kd = a.shape
    n = b.shape[1]
    tm, tn = min(tm, m), min(tn, n)
    while m % tm:
        tm //= 2
    tk = kd if tk is None else tk
    nk = kd // tk
    assert m % tm == 0 and n % tn == 0 and kd % tk == 0
    if epilogue is None:
        epilogue = lambda acc: acc
    in_specs = [pl.BlockSpec((tm, tk), lambda i, j, k: (i, k)),
                pl.BlockSpec((tk, tn), lambda i, j, k: (k, j))]
    args = [a, b]
    for arr, blk, imap in extras:
        in_specs.append(pl.BlockSpec(blk, functools.partial(lambda i, j, k, f: f(i, j), f=imap)))
        args.append(arr)
    scratch = [pltpu.VMEM((tm, tn), F32)] if nk > 1 else []
    return pl.pallas_call(
        functools.partial(_mm_kernel, nk=nk, n_extra=len(extras), epilogue=epilogue),
        out_shape=jax.ShapeDtypeStruct((m, n), out_dtype),
        grid=(m // tm, n // tn, nk),
        in_specs=in_specs,
        out_specs=pl.BlockSpec((tm, tn), lambda i, j, k: (i, j)),
        scratch_shapes=scratch,
        compiler_params=_cp(("parallel", "parallel", "arbitrary")),
        name=name,
    )(*args)


def _residual_epilogue(acc, r_ref):
    return acc + r_ref[...]


def matmul_residual(a, b, res, *, tm, tn, tk=None, name="matmul_residual"):
    tm_, tn_ = min(tm, a.shape[0]), min(tn, b.shape[1])
    return matmul(a, b, F32, tm=tm, tn=tn, tk=tk, name=name,
                  extras=[(res, (tm_, tn_), lambda i, j: (i, j))], epilogue=_residual_epilogue)


def _q_rope_epilogue(acc, c_ref, sa_ref, sb_ref):
    c, sa, sb = c_ref[...], sa_ref[...], sb_ref[...]
    scale = (MLA_NOPE + MLA_ROPE) ** -0.5
    outs = []
    for h in range(MLA_H):
        xs = acc[:, h * QSLOT:(h + 1) * QSLOT]
        y = xs * c + pltpu.roll(xs, 32, 1) * sa + pltpu.roll(xs, QSLOT - 32, 1) * sb
        outs.append(y * scale)
    return jnp.concatenate(outs, axis=1)


def _conv_kernel(x_ref, st_ref, w_ref, b_ref, y_ref, ns_ref, xp_ref, *, width, ts, act):
    t = pl.program_id(2)
    nt = pl.num_programs(2)

    @pl.when(t == 0)
    def _():
        xp_ref[0:8, :] = st_ref[0]

    @pl.when(t > 0)
    def _():
        xp_ref[0:8, :] = xp_ref[ts:ts + 8, :]

    xp_ref[8:8 + ts, :] = x_ref[0]
    off = 8 - (width - 1)
    y = xp_ref[off:off + ts, :] * w_ref[0:1, :]
    for j in range(1, width):
        y = y + xp_ref[off + j:off + j + ts, :] * w_ref[j:j + 1, :]
    y = y + b_ref[...]
    if act:
        y = _silu(y)
    y_ref[0] = y.astype(y_ref.dtype)

    @pl.when(t == nt - 1)
    def _():
        ns_ref[0] = xp_ref[ts:ts + 8, :]


def causal_conv(x3, col_off, chans, state8, w, b, *, act, ct=512):
    bsz, s, _ = x3.shape
    width = w.shape[0]
    ts = min(512, s)
    assert chans % ct == 0 and col_off % ct == 0 and s % ts == 0
    cb0 = col_off // ct
    return pl.pallas_call(
        functools.partial(_conv_kernel, width=width, ts=ts, act=act),
        out_shape=(jax.ShapeDtypeStruct((bsz, s, chans), F32),
                   jax.ShapeDtypeStruct((bsz, 8, chans), F32)),
        grid=(bsz, chans // ct, s // ts),
        in_specs=[pl.BlockSpec((1, ts, ct), lambda bi, c, t: (bi, t, cb0 + c)),
                  pl.BlockSpec((1, 8, ct), lambda bi, c, t: (bi, 0, c)),
                  pl.BlockSpec((width, ct), lambda bi, c, t: (0, c)),
                  pl.BlockSpec((1, ct), lambda bi, c, t: (0, c))],
        out_specs=(pl.BlockSpec((1, ts, ct), lambda bi, c, t: (bi, t, c)),
                   pl.BlockSpec((1, 8, ct), lambda bi, c, t: (bi, 0, c))),
        scratch_shapes=[pltpu.VMEM((ts + 8, ct), F32)],
        compiler_params=_cp(("parallel", "parallel", "arbitrary")),
        name="causal_conv",
    )(x3, state8, w, b.reshape(1, chans))


def _ffn_conv_kernel(xa_ref, xb_ref, sa_ref, sb_ref, wa_ref, wb_ref, ba_ref, bb_ref,
                     y_ref, nsa_ref, nsb_ref, xpa_ref, xpb_ref, *, width, ts):
    t = pl.program_id(2)
    nt = pl.num_programs(2)

    def conv(x_ref, st_ref, w_ref, b_ref, xp_ref):
        @pl.when(t == 0)
        def _():
            xp_ref[0:8, :] = st_ref[0]

        @pl.when(t > 0)
        def _():
            xp_ref[0:8, :] = xp_ref[ts:ts + 8, :]

        xp_ref[8:8 + ts, :] = x_ref[0]
        off = 8 - (width - 1)
        y = xp_ref[off:off + ts, :] * w_ref[0:1, :]
        for j in range(1, width):
            y = y + xp_ref[off + j:off + j + ts, :] * w_ref[j:j + 1, :]
        return y + b_ref[...]

    ya = conv(xa_ref, sa_ref, wa_ref, ba_ref, xpa_ref)
    yb = conv(xb_ref, sb_ref, wb_ref, bb_ref, xpb_ref)
    y_ref[0] = (_silu(ya) * yb).astype(y_ref.dtype)

    @pl.when(t == nt - 1)
    def _():
        nsa_ref[0] = xpa_ref[ts:ts + 8, :]
        nsb_ref[0] = xpb_ref[ts:ts + 8, :]


def ffn_conv_gate(up3, state8, w, b, *, ct=256):
    bsz, s, _ = up3.shape
    width = w.shape[0]
    ts = min(512, s)
    nc = D_FF // ct
    b2 = b.reshape(1, 2 * D_FF)
    y, nsa, nsb = pl.pallas_call(
        functools.partial(_ffn_conv_kernel, width=width, ts=ts),
        out_shape=(jax.ShapeDtypeStruct((bsz, s, D_FF), BF16),
                   jax.ShapeDtypeStruct((bsz, 8, D_FF), F32),
                   jax.ShapeDtypeStruct((bsz, 8, D_FF), F32)),
        grid=(bsz, nc, s // ts),
        in_specs=[pl.BlockSpec((1, ts, ct), lambda bi, c, t: (bi, t, c)),
                  pl.BlockSpec((1, ts, ct), lambda bi, c, t: (bi, t, nc + c)),
                  pl.BlockSpec((1, 8, ct), lambda bi, c, t: (bi, 0, c)),
                  pl.BlockSpec((1, 8, ct), lambda bi, c, t: (bi, 0, nc + c)),
                  pl.BlockSpec((width, ct), lambda bi, c, t: (0, c)),
                  pl.BlockSpec((width, ct), lambda bi, c, t: (0, nc + c)),
                  pl.BlockSpec((1, ct), lambda bi, c, t: (0, c)),
                  pl.BlockSpec((1, ct), lambda bi, c, t: (0, nc + c))],
        out_specs=(pl.BlockSpec((1, ts, ct), lambda bi, c, t: (bi, t, c)),
                   pl.BlockSpec((1, 8, ct), lambda bi, c, t: (bi, 0, c)),
                   pl.BlockSpec((1, 8, ct), lambda bi, c, t: (bi, 0, c))),
        scratch_shapes=[pltpu.VMEM((ts + 8, ct), F32), pltpu.VMEM((ts + 8, ct), F32)],
        compiler_params=_cp(("parallel", "parallel", "arbitrary")),
        name="ffn_conv_gate",
    )(up3, up3, state8, state8, w, w, b2, b2)
    return y, jnp.concatenate([nsa, nsb], axis=-1)


def _chunk_masks(lt, c):
    sh = int(math.log2(c))
    ri = lax.broadcasted_iota(jnp.int32, (lt, lt), 0)
    ci = lax.broadcasted_iota(jnp.int32, (lt, lt), 1)
    same = (ri >> sh) == (ci >> sh)
    return ri, ci, same


def _unit_lower_inverse(m, eye, c):
    p = -m
    t = eye + p
    for _ in range(int(math.log2(c)) - 1):
        p = jnp.dot(p, p, precision=HI, preferred_element_type=F32)
        t = t + jnp.dot(p, t, precision=HI, preferred_element_type=F32)
    return t


def _gdn_kernel(q_ref, k_ref, v_ref, sm_ref, z_ref, s0_ref, alog_ref, dtb_ref, nw_ref,
                o_ref, sout_ref, s_sc, vn_sc, *, lt, c):
    h = pl.program_id(1)
    t = pl.program_id(2)
    nt = pl.num_programs(2)

    @pl.when(t == 0)
    def _():
        s_sc[...] = s0_ref[0, 0]

    q = q_ref[0]
    k = k_ref[0]
    v = v_ref[0]
    q = q * lax.rsqrt(jnp.sum(q * q, axis=-1, keepdims=True) + EPS) * (GDN_D ** -0.5)
    k = k * lax.rsqrt(jnp.sum(k * k, axis=-1, keepdims=True) + EPS)
    sm = sm_ref[0]
    lane = lax.broadcasted_iota(jnp.int32, (1, 128), 1)
    beta = jnp.sum(jnp.where(lane == L_BETA + h, jax.nn.sigmoid(sm), 0.0), axis=-1, keepdims=True)
    g_full = -jnp.exp(alog_ref[...]) * _softplus(sm + dtb_ref[...])

    ri, ci, same = _chunk_masks(lt, c)
    tri = same & (ci <= ri)
    strict = same & (ci < ri)
    eye_b = ri == ci
    eye = jnp.where(eye_b, 1.0, 0.0)
    gc_full = jnp.dot(jnp.where(tri, 1.0, 0.0), g_full, precision=HI, preferred_element_type=F32)
    gc = jnp.sum(jnp.where(lane == L_ALPHA + h, gc_full, 0.0), axis=-1, keepdims=True)
    gr = jnp.sum(jnp.where(eye_b, gc, 0.0), axis=0, keepdims=True)
    decay = jnp.exp(jnp.where(tri, gc - gr, 0.0))

    kb = k * beta
    k16 = k.astype(BF16)
    kk = _nt(kb.astype(BF16), k16)
    qk = _nt(q.astype(BF16), k16)
    m = jnp.where(strict, kk * decay, 0.0)
    a = jnp.where(tri, qk * decay, 0.0).astype(BF16)
    tinv = _unit_lower_inverse(m, eye, c)
    eg = jnp.exp(gc)
    rhs = jnp.concatenate([v * beta, kb * eg], axis=1).astype(BF16)
    uw = jnp.dot(tinv.astype(BF16), rhs, preferred_element_type=F32)
    u = uw[:, :GDN_D]
    w = uw[:, GDN_D:].astype(BF16)
    qd = (q * eg).astype(BF16)
    z = z_ref[0]
    nw = nw_ref[...]

    s = s_sc[...]
    vn_sc[...] = u.astype(BF16)
    for cc in range(lt // c):
        r0 = cc * c
        s16 = s.astype(BF16)
        vnew = u[r0:r0 + c] - jnp.dot(w[r0:r0 + c], s16, preferred_element_type=F32)
        vn16 = vnew.astype(BF16)
        vn_sc[r0:r0 + c, :] = vn16
        o = (jnp.dot(qd[r0:r0 + c], s16, preferred_element_type=F32)
             + jnp.dot(a[r0:r0 + c, :], vn_sc[...], preferred_element_type=F32))
        glast = gc[r0 + c - 1:r0 + c, :]
        kd = (k[r0:r0 + c] * jnp.exp(glast - gc[r0:r0 + c])).astype(BF16)
        s = s * jnp.exp(glast) + _tn(kd, vn16)
        ms = jnp.mean(o * o, axis=-1, keepdims=True)
        o = o * lax.rsqrt(ms + EPS) * nw * _silu(z[r0:r0 + c])
        o_ref[0, r0:r0 + c, :] = o.astype(o_ref.dtype)
    s_sc[...] = s

    @pl.when(t == nt - 1)
    def _():
        sout_ref[0, 0] = s


def gdn(qkv3, p3, s0, alog_row, dtb_row, norm_w):
    bsz, s, _ = qkv3.shape
    c = min(CHUNK, s)
    lt = min(256, s)
    small_blk = P_SMALL // 128
    za_blk = P_ZA // 128
    blk = lambda off: pl.BlockSpec((1, lt, 128), lambda b, h, t: (b, t, off + h))
    return pl.pallas_call(
        functools.partial(_gdn_kernel, lt=lt, c=c),
        out_shape=(jax.ShapeDtypeStruct((bsz, s, BRANCH), BF16),
                   jax.ShapeDtypeStruct((bsz, GDN_H, GDN_D, GDN_D), F32)),
        grid=(bsz, GDN_H, s // lt),
        in_specs=[blk(0), blk(GDN_H), blk(2 * GDN_H),
                  pl.BlockSpec((1, lt, 128), lambda b, h, t: (b, t, small_blk)),
                  blk(za_blk),
                  pl.BlockSpec((1, 1, GDN_D, GDN_D), lambda b, h, t: (b, h, 0, 0)),
                  pl.BlockSpec((1, 128), lambda b, h, t: (0, 0)),
                  pl.BlockSpec((1, 128), lambda b, h, t: (0, 0)),
                  pl.BlockSpec((1, 128), lambda b, h, t: (0, 0))],
        out_specs=(pl.BlockSpec((1, lt, 128), lambda b, h, t: (b, t, h)),
                   pl.BlockSpec((1, 1, GDN_D, GDN_D), lambda b, h, t: (b, h, 0, 0))),
        scratch_shapes=[pltpu.VMEM((GDN_D, GDN_D), F32), pltpu.VMEM((lt, GDN_D), BF16)],
        compiler_params=_cp(("parallel", "parallel", "arbitrary")),
        name="gdn",
    )(qkv3, qkv3, qkv3, p3, p3, s0, alog_row, dtb_row, norm_w.reshape(1, GDN_D))


def _ssd_kernel(x_ref, b_ref, c_ref, sm_ref, z_ref, h0_ref, alog_ref, dtb_ref, d_ref, nw_ref,
                y_ref, hout_ref, hs_sc, y_sc, *, lt, c):
    g = pl.program_id(1)
    t = pl.program_id(2)
    nt = pl.num_programs(2)
    npair = SSM_H // SSM_G // 2

    @pl.when(t == 0)
    def _():
        hs_sc[...] = h0_ref[0]

    sm = sm_ref[0]
    dt_full = _softplus(sm + dtb_ref[...])
    da = dt_full * (-jnp.exp(alog_ref[...]))
    ri, ci, same = _chunk_masks(lt, c)
    tri = same & (ci <= ri)
    eye_b = ri == ci
    acum_full = jnp.dot(jnp.where(tri, 1.0, 0.0), da, precision=HI, preferred_element_type=F32)
    nch = lt // c
    lasts = [acum_full[cc * c + c - 1:cc * c + c, :] for cc in range(nch)]
    alast_full = jnp.concatenate([jnp.broadcast_to(l, (c, 128)) for l in lasts], axis=0)
    ea_full = jnp.exp(acum_full)
    dtw_full = dt_full * jnp.exp(alast_full - acum_full)

    bm = b_ref[0]
    cm = c_ref[0]
    bm16 = bm.astype(BF16)
    cm16 = cm.astype(BF16)
    cb = jnp.where(tri, _nt(cm16, bm16), 0.0)
    lane = lax.broadcasted_iota(jnp.int32, (1, 128), 1)
    lo = lane < SSM_P
    rowlo = lax.broadcasted_iota(jnp.int32, (128, 1), 0) < SSM_P
    x = x_ref[0]
    z = z_ref[0]
    d_row = d_ref[...]

    def ext(full, l):
        return jnp.sum(jnp.where(lane == l, full, 0.0), axis=-1, keepdims=True)

    for j in range(npair):
        la = L_DT + g * (SSM_H // SSM_G) + 2 * j
        lb = la + 1
        ac_a, ac_b = ext(acum_full, la), ext(acum_full, lb)
        row_a = jnp.sum(jnp.where(eye_b, ac_a, 0.0), axis=0, keepdims=True)
        row_b = jnp.sum(jnp.where(eye_b, ac_b, 0.0), axis=0, keepdims=True)
        l_a = (jnp.exp(jnp.where(tri, ac_a - row_a, 0.0)) * cb).astype(BF16)
        l_b = (jnp.exp(jnp.where(tri, ac_b - row_b, 0.0)) * cb).astype(BF16)
        dt_pair = jnp.where(lo, ext(dt_full, la), ext(dt_full, lb))
        ea_pair = jnp.where(lo, ext(ea_full, la), ext(ea_full, lb))
        dtw_pair = jnp.where(lo, ext(dtw_full, la), ext(dtw_full, lb))
        xp = x[:, j * 128:(j + 1) * 128]
        xdt = xp * dt_pair
        y = (jnp.dot(l_a, jnp.where(lo, xdt, 0.0).astype(BF16), preferred_element_type=F32)
             + jnp.dot(l_b, jnp.where(lo, 0.0, xdt).astype(BF16), preferred_element_type=F32))
        xw = (xp * dtw_pair).astype(BF16)
        hs = hs_sc[j]
        yoffs = []
        for cc in range(nch):
            r0 = cc * c
            yoffs.append(_nt(cm16[r0:r0 + c], hs.astype(BF16)) * ea_pair[r0:r0 + c])
            cs = _tn(xw[r0:r0 + c], bm16[r0:r0 + c])
            al = jnp.where(rowlo, jnp.exp(ac_a[r0 + c - 1:r0 + c, :]), jnp.exp(ac_b[r0 + c - 1:r0 + c, :]))
            hs = hs * al + cs
        hs_sc[j] = hs
        yoff = yoffs[0] if nch == 1 else jnp.concatenate(yoffs, axis=0)
        ytot = (y + yoff + d_row[:, j * 128:(j + 1) * 128] * xp) * _silu(z[:, j * 128:(j + 1) * 128])
        y_sc[:, j * 128:(j + 1) * 128] = ytot

    yy = y_sc[...]
    ms = jnp.mean(yy * yy, axis=-1, keepdims=True)
    y_ref[0] = (yy * lax.rsqrt(ms + EPS) * nw_ref[...]).astype(y_ref.dtype)

    @pl.when(t == nt - 1)
    def _():
        hout_ref[0] = hs_sc[...]


def ssd(xbc3, p3, h0, alog_row, dtb_row, d_row, norm_w):
    bsz, s, _ = xbc3.shape
    c = min(CHUNK, s)
    lt = min(256, s)
    gw = BRANCH // SSM_G
    npair = SSM_H // SSM_G // 2
    small_blk = P_SMALL // 128
    zb_blk = P_ZB // gw
    return pl.pallas_call(
        functools.partial(_ssd_kernel, lt=lt, c=c),
        out_shape=(jax.ShapeDtypeStruct((bsz, s, BRANCH), BF16),
                   jax.ShapeDtypeStruct((bsz, SSM_H // 2, 128, 128), F32)),
        grid=(bsz, SSM_G, s // lt),
        in_specs=[pl.BlockSpec((1, lt, gw), lambda b, g, t: (b, t, g)),
                  pl.BlockSpec((1, lt, 128), lambda b, g, t: (b, t, BRANCH // 128 + g)),
                  pl.BlockSpec((1, lt, 128), lambda b, g, t: (b, t, BRANCH // 128 + SSM_G + g)),
                  pl.BlockSpec((1, lt, 128), lambda b, g, t: (b, t, small_blk)),
                  pl.BlockSpec((1, lt, gw), lambda b, g, t: (b, t, zb_blk + g)),
                  pl.BlockSpec((1, npair, 128, 128), lambda b, g, t: (b, g, 0, 0)),
                  pl.BlockSpec((1, 128), lambda b, g, t: (0, 0)),
                  pl.BlockSpec((1, 128), lambda b, g, t: (0, 0)),
                  pl.BlockSpec((1, gw), lambda b, g, t: (0, g)),
                  pl.BlockSpec((1, gw), lambda b, g, t: (0, g))],
        out_specs=(pl.BlockSpec((1, lt, gw), lambda b, g, t: (b, t, g)),
                   pl.BlockSpec((1, npair, 128, 128), lambda b, g, t: (b, g, 0, 0))),
        scratch_shapes=[pltpu.VMEM((npair, 128, 128), F32), pltpu.VMEM((lt, gw), F32)],
        compiler_params=_cp(("parallel", "parallel", "arbitrary")),
        name="ssd",
    )(xbc3, xbc3, xbc3, p3, p3, h0, alog_row, dtb_row, d_row, norm_w.reshape(1, BRANCH))


def _mla_prep_kernel(p_ref, qn_ref, kvn_ref, ck_ref, sak_ref, sbk_ref,
                     cq_ref, lat_ref, kr_ref, latkr_ref):
    blk = p_ref[...]
    cq = blk[:, 0:MLA_Q_RANK]
    cq = cq * lax.rsqrt(jnp.mean(cq * cq, axis=-1, keepdims=True) + EPS) * qn_ref[...]
    cq_ref[...] = cq.astype(cq_ref.dtype)
    ckv = blk[:, MLA_Q_RANK:MLA_Q_RANK + MLA_KV_RANK]
    lat = ckv * lax.rsqrt(jnp.mean(ckv * ckv, axis=-1, keepdims=True) + EPS) * kvn_ref[...]
    lat_ref[...] = lat
    sm = blk[:, 1280:1408]
    y = sm * ck_ref[...] + pltpu.roll(sm, 32, 1) * sak_ref[...] + pltpu.roll(sm, 96, 1) * sbk_ref[...]
    kr_ref[...] = y[:, 0:MLA_ROPE]
    latkr_ref[:, 0:MLA_KV_RANK] = lat.astype(latkr_ref.dtype)
    latkr_ref[:, MLA_KV_RANK:LATKR] = y.astype(latkr_ref.dtype)


def mla_prep(p, q_norm, kv_norm, ck, sak, sbk):
    m = p.shape[0]
    tm = min(256, m)
    mla_blk = P_MLA // 1536
    row = lambda n: pl.BlockSpec((tm, n), lambda i: (i, 0))
    return pl.pallas_call(
        _mla_prep_kernel,
        out_shape=(jax.ShapeDtypeStruct((m, MLA_Q_RANK), BF16),
                   jax.ShapeDtypeStruct((m, MLA_KV_RANK), F32),
                   jax.ShapeDtypeStruct((m, MLA_ROPE), F32),
                   jax.ShapeDtypeStruct((m, LATKR), BF16)),
        grid=(m // tm,),
        in_specs=[pl.BlockSpec((tm, 1536), lambda i: (i, mla_blk)),
                  pl.BlockSpec((1, MLA_Q_RANK), lambda i: (0, 0)),
                  pl.BlockSpec((1, MLA_KV_RANK), lambda i: (0, 0)),
                  row(128), row(128), row(128)],
        out_specs=(row(MLA_Q_RANK), row(MLA_KV_RANK), row(MLA_ROPE), row(LATKR)),
        compiler_params=_cp(("parallel",)),
        name="mla_prep",
    )(p, q_norm.reshape(1, -1), kv_norm.reshape(1, -1), ck, sak, sbk)


def _flash_kernel(q_ref, k_ref, v_ref, o_ref, m_sc, l_sc, acc_sc, *, tq, tk, q_off, nk_real):
    i = pl.program_id(2)
    j = pl.program_id(3)
    nj = pl.num_programs(3)

    @pl.when(j == 0)
    def _():
        m_sc[...] = jnp.full(m_sc.shape, -1e30, F32)
        l_sc[...] = jnp.zeros(l_sc.shape, F32)
        acc_sc[...] = jnp.zeros(acc_sc.shape, F32)

    q_last = q_off + i * tq + tq - 1

    @pl.when(((j * tk) >> 6) <= (q_last >> 6))
    def _():
        s = _nt(q_ref[0], k_ref[0])
        qpos = q_off + i * tq + lax.broadcasted_iota(jnp.int32, (tq, 1), 0)
        kidx = j * tk + lax.broadcasted_iota(jnp.int32, (1, tk), 1)
        vis = ((kidx >> 6) <= (qpos >> 6)) & (kidx < nk_real)
        s = jnp.where(vis, s, -1e30)
        m_prev = m_sc[...]
        m_new = jnp.maximum(m_prev, jnp.max(s, axis=-1, keepdims=True))
        alpha = jnp.exp(m_prev - m_new)
        p = jnp.where(vis, jnp.exp(s - m_new), 0.0)
        l_sc[...] = alpha * l_sc[...] + jnp.sum(p, axis=-1, keepdims=True)
        acc_sc[...] = alpha * acc_sc[...] + jnp.dot(p.astype(BF16), v_ref[0], preferred_element_type=F32)
        m_sc[...] = m_new

    @pl.when(j == nj - 1)
    def _():
        o_ref[0] = (acc_sc[...] / l_sc[...]).astype(o_ref.dtype)


def flash_attention(q3, kv3, *, q_off, nk_real):
    bsz, sq, _ = q3.shape
    tk_tot = kv3.shape[1]
    tq = min(512, sq)
    tk = tk_tot if tk_tot <= 4096 else 512
    assert sq % tq == 0 and tk_tot % tk == 0
    v_blk0 = MLA_H * QSLOT // MLA_V

    def last_blk(i):
        return ((q_off + i * tq + tq - 1) >> 6 << 6) // tk

    return pl.pallas_call(
        functools.partial(_flash_kernel, tq=tq, tk=tk, q_off=q_off, nk_real=nk_real),
        out_shape=jax.ShapeDtypeStruct((bsz, sq, BRANCH), BF16),
        grid=(bsz, MLA_H, sq // tq, tk_tot // tk),
        in_specs=[pl.BlockSpec((1, tq, QSLOT), lambda b, h, i, j: (b, i, h)),
                  pl.BlockSpec((1, tk, QSLOT), lambda b, h, i, j: (b, jnp.minimum(j, last_blk(i)), h)),
                  pl.BlockSpec((1, tk, MLA_V), lambda b, h, i, j: (b, jnp.minimum(j, last_blk(i)), v_blk0 + h))],
        out_specs=pl.BlockSpec((1, tq, MLA_V), lambda b, h, i, j: (b, i, h)),
        scratch_shapes=[pltpu.VMEM((tq, 1), F32), pltpu.VMEM((tq, 1), F32), pltpu.VMEM((tq, MLA_V), F32)],
        compiler_params=_cp(("parallel", "parallel", "parallel", "arbitrary")),
        name="flash_attention",
    )(q3, kv3, kv3)


def _branch_kernel(oa_ref, ob_ref, oc_ref, w_ref, g0_ref, g1_ref, g2_ref, out_ref):
    acc = None
    for n, (o_ref, g_ref) in enumerate(((oa_ref, g0_ref), (ob_ref, g1_ref), (oc_ref, g2_ref))):
        u = jnp.dot(o_ref[...], w_ref[n], preferred_element_type=F32)
        term = jax.nn.sigmoid(g_ref[...]) * u
        acc = term if acc is None else acc + term
    out_ref[...] = acc.astype(out_ref.dtype)


def branch_merge(oa, ob, oc, w_branch, p):
    m = oa.shape[0]
    tm = min(512, m)
    tn = 512
    nj = D_MODEL // tn
    o_spec = pl.BlockSpec((tm, BRANCH), lambda i, j: (i, 0))
    gate = lambda n: pl.BlockSpec((tm, tn), lambda i, j: (i, P_GATE // tn + n * nj + j))
    return pl.pallas_call(
        _branch_kernel,
        out_shape=jax.ShapeDtypeStruct((m, D_MODEL), BF16),
        grid=(m // tm, nj),
        in_specs=[o_spec, o_spec, o_spec,
                  pl.BlockSpec((3, BRANCH, tn), lambda i, j: (0, 0, j)),
                  gate(0), gate(1), gate(2)],
        out_specs=pl.BlockSpec((tm, tn), lambda i, j: (i, j)),
        compiler_params=_cp(("parallel", "parallel")),
        name="branch_merge",
    )(oa, ob, oc, w_branch, p, p, p)


def _pack_layer(l, w):
    w_in = w['w_in'][l]
    seg = lambda a, n: w_in[:, a:a + n]
    zeros = lambda n: jnp.zeros((D_MODEL, n), w_in.dtype)
    w_in_p = jnp.concatenate([
        seg(8032, 12288),
        seg(6688, 768), seg(7456, 512),
        seg(7968, 64), seg(4096, 8), seg(4104, 8), seg(6672, 16), zeros(32), zeros(128),
        seg(5136, 1536), seg(0, 3072), seg(3072, 1024), seg(4112, 1024)], axis=1).astype(BF16)
    w_uq = jnp.pad(w['mla_w_uq'][l], ((0, 0), (0, 0), (0, QSLOT - MLA_NOPE - MLA_ROPE)))
    w_uq = w_uq.reshape(MLA_Q_RANK, MLA_H * QSLOT).astype(BF16)
    wk = jnp.pad(w['mla_w_uk'][l], ((0, 0), (0, 0), (0, QSLOT - MLA_NOPE))).reshape(MLA_KV_RANK, MLA_H * QSLOT)
    eye = jnp.broadcast_to(jnp.eye(MLA_ROPE, dtype=F32)[:, None, :], (MLA_ROPE, MLA_H, MLA_ROPE))
    wkr = jnp.pad(eye, ((0, 0), (0, 0), (MLA_NOPE, QSLOT - MLA_NOPE - MLA_ROPE))).reshape(MLA_ROPE, MLA_H * QSLOT)
    wk = jnp.concatenate([wk, wkr, jnp.zeros((LATKR - MLA_KV_RANK - MLA_ROPE, MLA_H * QSLOT), F32)], axis=0)
    wv = jnp.pad(w['mla_w_uv'][l].reshape(MLA_KV_RANK, MLA_H * MLA_V), ((0, LATKR - MLA_KV_RANK), (0, 0)))
    w_kv = jnp.concatenate([wk, wv], axis=1).astype(BF16)

    def lane_row(vals, off):
        return jnp.zeros((1, 128), F32).at[0, off:off + vals.shape[0]].set(vals.astype(F32))

    return dict(
        attn_norm=w['attn_norm'][l], w_in=w_in_p,
        gdn_conv_w=w['gdn_conv_w'][l], gdn_alog=lane_row(w['gdn_A_log'][l], L_ALPHA),
        gdn_dtb=lane_row(w['gdn_dt_bias'][l], L_ALPHA), gdn_norm=w['gdn_norm'][l],
        ssm_conv_w=w['ssm_conv_w'][l], ssm_conv_b=w['ssm_conv_b'][l],
        ssm_alog=lane_row(w['ssm_A_log'][l], L_DT), ssm_dtb=lane_row(w['ssm_dt_bias'][l], L_DT),
        ssm_d=jnp.repeat(w['ssm_D'][l].astype(F32), SSM_P).reshape(1, BRANCH), ssm_norm=w['ssm_norm'][l],
        mla_q_norm=w['mla_q_norm'][l], w_uq=w_uq, mla_kv_norm=w['mla_kv_norm'][l], w_kv=w_kv,
        w_branch=w['w_branch'][l].astype(BF16), w_out=w['w_out'][l].astype(BF16),
        ffn_norm=w['ffn_norm'][l], w_up=w['ffn_w_up'][l].astype(BF16),
        ffn_conv_w=w['ffn_conv_w'][l], ffn_conv_b=w['ffn_conv_b'][l],
        w_down=w['ffn_w_down'][l].astype(BF16))


def _rope_tables(pos):
    half = MLA_ROPE // 2
    inv = ROPE_THETA ** (-jnp.arange(half, dtype=F32) / half)
    ang = pos.astype(F32)[:, None] * inv[None, :]
    cos, sin = jnp.cos(ang), jnp.sin(ang)
    n = pos.shape[0]
    z = lambda k: jnp.zeros((n, k), F32)
    cq = jnp.concatenate([jnp.ones((n, MLA_NOPE), F32), cos, cos, z(64)], axis=1)
    saq = jnp.concatenate([z(MLA_NOPE + half), sin, z(64)], axis=1)
    sbq = jnp.concatenate([z(MLA_NOPE), -sin, z(half + 64)], axis=1)
    ck = jnp.concatenate([cos, cos, z(64)], axis=1)
    sak = jnp.concatenate([z(half), sin, z(64)], axis=1)
    sbk = jnp.concatenate([-sin, z(half + 64)], axis=1)
    return cq, saq, sbq, ck, sak, sbk


def _pad_state(st):
    return jnp.pad(st, ((0, 0), (8 - st.shape[1], 0), (0, 0)))


def _layer(x, tabs, q_off, state, lw):
    latkr_past, gconv, gssm, mconv, mssm, fconv = state
    bsz, s, _ = x.shape
    m = bsz * s
    cq_t, saq_t, sbq_t, ck_t, sak_t, sbk_t = tabs
    x2 = x.reshape(m, D_MODEL)
    mm_tm = 512

    h = rmsnorm(x2, lw['attn_norm'], BF16)
    p = matmul(h, lw['w_in'], F32, tm=mm_tm, tn=1024, name="in_proj")
    p3 = p.reshape(bsz, s, P_TOT)

    qkv_act, gconv8 = causal_conv(p3, P_QKV, GDN_QKV, _pad_state(gconv), lw['gdn_conv_w'],
                                  jnp.zeros((GDN_QKV,), F32), act=True)
    o_a, gssm_new = gdn(qkv_act, p3, gssm, lw['gdn_alog'], lw['gdn_dtb'], lw['gdn_norm'])

    xbc_act, mconv8 = causal_conv(p3, P_XBC, SSM_CONV_DIM, _pad_state(mconv), lw['ssm_conv_w'],
                                  lw['ssm_conv_b'], act=True)
    o_b, mssm_new = ssd(xbc_act, p3, mssm.reshape(bsz, SSM_H // 2, 128, 128), lw['ssm_alog'], lw['ssm_dtb'],
                        lw['ssm_d'], lw['ssm_norm'])
    mssm_new = mssm_new.reshape(bsz, SSM_H, SSM_P, SSM_N)

    cqn, lat, kr, latkr = mla_prep(p, lw['mla_q_norm'], lw['mla_kv_norm'], ck_t, sak_t, sbk_t)
    tq_tab = min(mm_tm, m)
    q_full = matmul(cqn, lw['w_uq'], BF16, tm=mm_tm, tn=MLA_H * QSLOT, name="q_up",
                    extras=[(cq_t, (tq_tab, QSLOT), lambda i, j: (i, 0)),
                            (saq_t, (tq_tab, QSLOT), lambda i, j: (i, 0)),
                            (sbq_t, (tq_tab, QSLOT), lambda i, j: (i, 0))],
                    epilogue=_q_rope_epilogue)
    latkr3 = latkr.reshape(bsz, s, LATKR)
    if latkr_past is None:
        latkr_all, nk_real = latkr3, s
    else:
        nk_real = latkr_past.shape[1] + s
        nk_pad = -(-nk_real // 128) * 128
        latkr_all = jnp.concatenate(
            [latkr_past, latkr3, jnp.zeros((bsz, nk_pad - nk_real, LATKR), BF16)], axis=1)
    tk_tot = latkr_all.shape[1]
    kv = matmul(latkr_all.reshape(bsz * tk_tot, LATKR), lw['w_kv'], BF16, tm=mm_tm, tn=1024, name="kv_up")
    o_c = flash_attention(q_full.reshape(bsz, s, MLA_H * QSLOT), kv.reshape(bsz, tk_tot, -1),
                          q_off=q_off, nk_real=nk_real)

    merged = branch_merge(o_a.reshape(m, BRANCH), o_b.reshape(m, BRANCH), o_c.reshape(m, BRANCH),
                          lw['w_branch'], p)
    x2 = matmul_residual(merged, lw['w_out'], x2, tm=mm_tm, tn=1024, name="out_proj")

    h2 = rmsnorm(x2, lw['ffn_norm'], BF16)
    up = matmul(h2, lw['w_up'], F32, tm=mm_tm, tn=512, name="ffn_up")
    act, fconv8 = ffn_conv_gate(up.reshape(bsz, s, 2 * D_FF), _pad_state(fconv), lw['ffn_conv_w'],
                                lw['ffn_conv_b'])
    x2 = matmul_residual(act.reshape(m, D_FF), lw['w_down'], x2, tm=mm_tm, tn=1024, tk=D_FF // 2,
                         name="ffn_down")

    new = (lat.reshape(bsz, s, MLA_KV_RANK), kr.reshape(bsz, s, MLA_ROPE),
           gconv8[:, 8 - gconv.shape[1]:], gssm_new, mconv8[:, 8 - mconv.shape[1]:], mssm_new,
           fconv8[:, 8 - fconv.shape[1]:])
    return x2.reshape(bsz, s, D_MODEL), new


def kernel(x_prompt, x_sample, cache_mla_latent, cache_mla_krope, state_gdn_conv, state_gdn_ssm, state_ssm_conv, state_ssm, state_ffn_conv, attn_norm, w_in, gdn_conv_w, gdn_A_log, gdn_dt_bias, gdn_norm, ssm_conv_w, ssm_conv_b, ssm_A_log, ssm_dt_bias, ssm_D, ssm_norm, mla_q_norm, mla_w_uq, mla_kv_norm, mla_w_uk, mla_w_uv, w_branch, w_out, ffn_norm, ffn_w_up, ffn_conv_w, ffn_conv_b, ffn_w_down, final_norm):
    weights = dict(attn_norm=attn_norm, w_in=w_in, gdn_conv_w=gdn_conv_w, gdn_A_log=gdn_A_log,
                   gdn_dt_bias=gdn_dt_bias, gdn_norm=gdn_norm, ssm_conv_w=ssm_conv_w, ssm_conv_b=ssm_conv_b,
                   ssm_A_log=ssm_A_log, ssm_dt_bias=ssm_dt_bias, ssm_D=ssm_D, ssm_norm=ssm_norm,
                   mla_q_norm=mla_q_norm, mla_w_uq=mla_w_uq, mla_kv_norm=mla_kv_norm, mla_w_uk=mla_w_uk,
                   mla_w_uv=mla_w_uv, w_branch=w_branch, w_out=w_out, ffn_norm=ffn_norm, ffn_w_up=ffn_w_up,
                   ffn_conv_w=ffn_conv_w, ffn_conv_b=ffn_conv_b, ffn_w_down=ffn_w_down)
    depth = w_in.shape[0]
    bp, sp, _ = x_prompt.shape
    bs, ss, _ = x_sample.shape
    n_past = cache_mla_latent.shape[2]
    pos_p = jnp.arange(sp, dtype=jnp.int32)
    pos_s = n_past + jnp.arange(ss, dtype=jnp.int32)
    tabs_p = _rope_tables(pos_p)
    tabs_s = tuple(jnp.tile(t, (bs, 1)) for t in _rope_tables(pos_s))
    zero_state = (
        None,
        jnp.zeros((bp,) + state_gdn_conv.shape[2:], F32),
        jnp.zeros((bp,) + state_gdn_ssm.shape[2:], F32),
        jnp.zeros((bp,) + state_ssm_conv.shape[2:], F32),
        jnp.zeros((bp,) + state_ssm.shape[2:], F32),
        jnp.zeros((bp,) + state_ffn_conv.shape[2:], F32),
    )
    xp, xs = x_prompt, x_sample
    new_p, new_s = [], []
    for l in range(depth):
        lw = _pack_layer(l, weights)
        xp, st_p = _layer(xp, tabs_p, 0, zero_state, lw)
        latkr_past = jnp.concatenate(
            [cache_mla_latent[l], cache_mla_krope[l],
             jnp.zeros((bs, n_past, LATKR - MLA_KV_RANK - MLA_ROPE), F32)], axis=-1).astype(BF16)
        st_in = (latkr_past, state_gdn_conv[l], state_gdn_ssm[l], state_ssm_conv[l], state_ssm[l],
                 state_ffn_conv[l])
        xs, st_s = _layer(xs, tabs_s, n_past, st_in, lw)
        new_p.append(st_p)
        new_s.append(st_s)
    y_prompt = rmsnorm(xp.reshape(bp * sp, D_MODEL), final_norm, F32).reshape(bp, sp, D_MODEL)
    y_sample = rmsnorm(xs.reshape(bs * ss, D_MODEL), final_norm, F32).reshape(bs, ss, D_MODEL)
    outs_p = [jnp.stack(t) for t in zip(*new_p)]
    outs_s = [jnp.stack(t) for t in zip(*new_s)]
    return (y_prompt, y_sample, *outs_p, *outs_s)
```

```python
import functools
import math

import jax
import jax.numpy as jnp
from jax import lax
from jax.experimental import pallas as pl
from jax.experimental.pallas import tpu as pltpu

F32 = jnp.float32
BF16 = jnp.bfloat16
HI = lax.Precision.HIGHEST

EPS = 1e-6
CHUNK = 64
D_MODEL = 4096
BRANCH = 1024
GDN_H, GDN_D = 8, 128
GDN_QKV = 3072
SSM_H, SSM_P, SSM_G, SSM_N = 16, 64, 2, 128
SSM_CONV_DIM = 1536
MLA_H, MLA_NOPE, MLA_ROPE, MLA_V = 8, 128, 64, 128
MLA_Q_RANK, MLA_KV_RANK = 768, 512
ROPE_THETA = 10000.0
D_FF = 11008
LATKR = 640
QSLOT = 256

P_GATE = 0
P_MLA = 12288
P_SMALL = P_MLA + 1280
P_XBC = 13824
P_QKV = 15360
P_ZA = 18432
P_ZB = 19456
P_TOT = 20480
L_BETA, L_ALPHA, L_DT = 64, 72, 80

VMEM_LIMIT = 56 * 1024 * 1024


def _cp(sem):
    return pltpu.CompilerParams(dimension_semantics=sem, vmem_limit_bytes=VMEM_LIMIT)


def _softplus(x):
    return jnp.maximum(x, 0.0) + jnp.log1p(jnp.exp(-jnp.abs(x)))


def _silu(x):
    return x * jax.nn.sigmoid(x)


def _nt(a, b):
    return lax.dot_general(a, b, (((1,), (1,)), ((), ())), preferred_element_type=F32)


def _tn(a, b):
    return lax.dot_general(a, b, (((0,), (0,)), ((), ())), preferred_element_type=F32)


def _rms_kernel(x_ref, w_ref, o_ref):
    x = x_ref[...]
    ms = jnp.mean(x * x, axis=-1, keepdims=True)
    o_ref[...] = (x * lax.rsqrt(ms + EPS) * w_ref[...]).astype(o_ref.dtype)


def rmsnorm(x, w, out_dtype):
    m, d = x.shape
    tm = min(256, m)
    return pl.pallas_call(
        _rms_kernel,
        out_shape=jax.ShapeDtypeStruct((m, d), out_dtype),
        grid=(m // tm,),
        in_specs=[pl.BlockSpec((tm, d), lambda i: (i, 0)),
                  pl.BlockSpec((1, d), lambda i: (0, 0))],
        out_specs=pl.BlockSpec((tm, d), lambda i: (i, 0)),
        compiler_params=_cp(("parallel",)),
        name="rmsnorm",
    )(x, w.reshape(1, d))


def _mm_kernel(*refs, nk, n_extra, epilogue):
    a_ref, b_ref = refs[0], refs[1]
    extras = refs[2:2 + n_extra]
    o_ref = refs[2 + n_extra]
    part = jnp.dot(a_ref[...], b_ref[...], preferred_element_type=F32)
    if nk == 1:
        o_ref[...] = epilogue(part, *extras).astype(o_ref.dtype)
        return
    acc_ref = refs[3 + n_extra]
    k = pl.program_id(2)

    @pl.when(k == 0)
    def _():
        acc_ref[...] = part

    @pl.when(k > 0)
    def _():
        acc_ref[...] += part

    @pl.when(k == nk - 1)
    def _():
        o_ref[...] = epilogue(acc_ref[...], *extras).astype(o_ref.dtype)


def matmul(a, b, out_dtype, *, tm, tn, tk=None, extras=(), epilogue=None, name="matmul"):
    m, kd = a.shape
    n = b.shape[1]
    tm, tn = min(tm, m), min(tn, n)
    while m % tm:
        tm //= 2
    tk = kd if tk is None else tk
    nk = kd // tk
    assert m % tm == 0 and n % tn == 0 and kd % tk == 0
    if epilogue is None:
        epilogue = lambda acc: acc
    in_specs = [pl.BlockSpec((tm, tk), lambda i, j, k: (i, k)),
                pl.BlockSpec((tk, tn), lambda i, j, k: (k, j))]
    args = [a, b]
    for arr, blk, imap in extras:
        in_specs.append(pl.BlockSpec(blk, functools.partial(lambda i, j, k, f: f(i, j), f=imap)))
        args.append(arr)
    scratch = [pltpu.VMEM((tm, tn), F32)] if nk > 1 else []
    return pl.pallas_call(
        functools.partial(_mm_kernel, nk=nk, n_extra=len(extras), epilogue=epilogue),
        out_shape=jax.ShapeDtypeStruct((m, n), out_dtype),
        grid=(m // tm, n // tn, nk),
        in_specs=in_specs,
        out_specs=pl.BlockSpec((tm, tn), lambda i, j, k: (i, j)),
        scratch_shapes=scratch,
        compiler_params=_cp(("parallel", "parallel", "arbitrary")),
        name=name,
    )(*args)


def _residual_epilogue(acc, r_ref):
    return acc + r_ref[...]


def matmul_residual(a, b, res, *, tm, tn, tk=None, name="matmul_residual"):
    tm_, tn_ = min(tm, a.shape[0]), min(tn, b.shape[1])
    return matmul(a, b, F32, tm=tm, tn=tn, tk=tk, name=name,
                  extras=[(res, (tm_, tn_), lambda i, j: (i, j))], epilogue=_residual_epilogue)


def _q_rope_epilogue(acc, c_ref, sa_ref, sb_ref):
    c, sa, sb = c_ref[...], sa_ref[...], sb_ref[...]
    scale = (MLA_NOPE + MLA_ROPE) ** -0.5 * math.log2(math.e)
    outs = []
    for h in range(MLA_H):
        xs = acc[:, h * QSLOT:(h + 1) * QSLOT]
        y = xs * c + pltpu.roll(xs, 32, 1) * sa + pltpu.roll(xs, QSLOT - 32, 1) * sb
        outs.append(y * scale)
    return jnp.concatenate(outs, axis=1)


def _conv_kernel(x_ref, st_ref, w_ref, b_ref, y_ref, ns_ref, xp_ref, *, width, ts, act):
    t = pl.program_id(2)
    nt = pl.num_programs(2)

    @pl.when(t == 0)
    def _():
        xp_ref[0:8, :] = st_ref[0]

    @pl.when(t > 0)
    def _():
        xp_ref[0:8, :] = xp_ref[ts:ts + 8, :]

    xp_ref[8:8 + ts, :] = x_ref[0]
    off = 8 - (width - 1)
    y = xp_ref[off:off + ts, :] * w_ref[0:1, :]
    for j in range(1, width):
        y = y + xp_ref[off + j:off + j + ts, :] * w_ref[j:j + 1, :]
    y = y + b_ref[...]
    if act:
        y = _silu(y)
    y_ref[0] = y.astype(y_ref.dtype)

    @pl.when(t == nt - 1)
    def _():
        ns_ref[0] = xp_ref[ts:ts + 8, :]


def causal_conv(x3, col_off, chans, state8, w, b, *, act, ct=512):
    bsz, s, _ = x3.shape
    width = w.shape[0]
    ts = min(512, s)
    assert chans % ct == 0 and col_off % ct == 0 and s % ts == 0
    cb0 = col_off // ct
    return pl.pallas_call(
        functools.partial(_conv_kernel, width=width, ts=ts, act=act),
        out_shape=(jax.ShapeDtypeStruct((bsz, s, chans), F32),
                   jax.ShapeDtypeStruct((bsz, 8, chans), F32)),
        grid=(bsz, chans // ct, s // ts),
        in_specs=[pl.BlockSpec((1, ts, ct), lambda bi, c, t: (bi, t, cb0 + c)),
                  pl.BlockSpec((1, 8, ct), lambda bi, c, t: (bi, 0, c)),
                  pl.BlockSpec((width, ct), lambda bi, c, t: (0, c)),
                  pl.BlockSpec((1, ct), lambda bi, c, t: (0, c))],
        out_specs=(pl.BlockSpec((1, ts, ct), lambda bi, c, t: (bi, t, c)),
                   pl.BlockSpec((1, 8, ct), lambda bi, c, t: (bi, 0, c))),
        scratch_shapes=[pltpu.VMEM((ts + 8, ct), F32)],
        compiler_params=_cp(("parallel", "parallel", "arbitrary")),
        name="causal_conv",
    )(x3, state8, w, b.reshape(1, chans))


def _ffn_conv_kernel(xa_ref, xb_ref, sa_ref, sb_ref, wa_ref, wb_ref, ba_ref, bb_ref,
                     y_ref, nsa_ref, nsb_ref, xpa_ref, xpb_ref, *, width, ts):
    t = pl.program_id(2)
    nt = pl.num_programs(2)

    def conv(x_ref, st_ref, w_ref, b_ref, xp_ref):
        @pl.when(t == 0)
        def _():
            xp_ref[0:8, :] = st_ref[0]

        @pl.when(t > 0)
        def _():
            xp_ref[0:8, :] = xp_ref[ts:ts + 8, :]

        xp_ref[8:8 + ts, :] = x_ref[0]
        off = 8 - (width - 1)
        y = xp_ref[off:off + ts, :] * w_ref[0:1, :]
        for j in range(1, width):
            y = y + xp_ref[off + j:off + j + ts, :] * w_ref[j:j + 1, :]
        return y + b_ref[...]

    ya = conv(xa_ref, sa_ref, wa_ref, ba_ref, xpa_ref)
    yb = conv(xb_ref, sb_ref, wb_ref, bb_ref, xpb_ref)
    y_ref[0] = (_silu(ya) * yb).astype(y_ref.dtype)

    @pl.when(t == nt - 1)
    def _():
        nsa_ref[0] = xpa_ref[ts:ts + 8, :]
        nsb_ref[0] = xpb_ref[ts:ts + 8, :]


def ffn_conv_gate(up3, state8, w, b, *, ct=256):
    bsz, s, _ = up3.shape
    width = w.shape[0]
    ts = min(512, s)
    nc = D_FF // ct
    b2 = b.reshape(1, 2 * D_FF)
    y, nsa, nsb = pl.pallas_call(
        functools.partial(_ffn_conv_kernel, width=width, ts=ts),
        out_shape=(jax.ShapeDtypeStruct((bsz, s, D_FF), BF16),
                   jax.ShapeDtypeStruct((bsz, 8, D_FF), F32),
                   jax.ShapeDtypeStruct((bsz, 8, D_FF), F32)),
        grid=(bsz, nc, s // ts),
        in_specs=[pl.BlockSpec((1, ts, ct), lambda bi, c, t: (bi, t, c)),
                  pl.BlockSpec((1, ts, ct), lambda bi, c, t: (bi, t, nc + c)),
                  pl.BlockSpec((1, 8, ct), lambda bi, c, t: (bi, 0, c)),
                  pl.BlockSpec((1, 8, ct), lambda bi, c, t: (bi, 0, nc + c)),
                  pl.BlockSpec((width, ct), lambda bi, c, t: (0, c)),
                  pl.BlockSpec((width, ct), lambda bi, c, t: (0, nc + c)),
                  pl.BlockSpec((1, ct), lambda bi, c, t: (0, c)),
                  pl.BlockSpec((1, ct), lambda bi, c, t: (0, nc + c))],
        out_specs=(pl.BlockSpec((1, ts, ct), lambda bi, c, t: (bi, t, c)),
                   pl.BlockSpec((1, 8, ct), lambda bi, c, t: (bi, 0, c)),
                   pl.BlockSpec((1, 8, ct), lambda bi, c, t: (bi, 0, c))),
        scratch_shapes=[pltpu.VMEM((ts + 8, ct), F32), pltpu.VMEM((ts + 8, ct), F32)],
        compiler_params=_cp(("parallel", "parallel", "arbitrary")),
        name="ffn_conv_gate",
    )(up3, up3, state8, state8, w, w, b2, b2)
    return y, jnp.concatenate([nsa, nsb], axis=-1)


HALO = 16


FFN_TN = 256


def _interleave_ffn(x):
    lead = x.shape[:-1]
    x = x.reshape(*lead, 2, D_FF // FFN_TN, FFN_TN)
    return jnp.swapaxes(x, -3, -2).reshape(*lead, 2 * D_FF)


def _deinterleave_ffn(x):
    lead = x.shape[:-1]
    x = x.reshape(*lead, D_FF // FFN_TN, 2, FFN_TN)
    return jnp.swapaxes(x, -3, -2).reshape(*lead, 2 * D_FF)


def _ffn_up_fused_kernel(a_ref, halo_ref, w_ref, st_ref, cw_ref, cb_ref, y_ref, ns_ref, aext_ref, xp_ref,
                         *, tm, tiles_per_seq, width):
    i = pl.program_id(0)
    j = pl.program_id(1)

    @pl.when(j == 0)
    def _():
        aext_ref[0:HALO, :] = halo_ref[...]
        aext_ref[HALO:HALO + tm, :] = a_ref[...]

    xp_ref[...] = jnp.dot(aext_ref[...], w_ref[...], preferred_element_type=F32)

    @pl.when((i % tiles_per_seq) == 0)
    def _():
        xp_ref[HALO - 8:HALO, :] = st_ref[0]

    off = HALO - (width - 1)
    y = xp_ref[off:off + tm, :] * cw_ref[0:1, :]
    for t in range(1, width):
        y = y + xp_ref[off + t:off + t + tm, :] * cw_ref[t:t + 1, :]
    y = y + cb_ref[...]
    ns_ref[0] = xp_ref[HALO + tm - 8:HALO + tm, :]
    y_ref[...] = (_silu(y[:, :FFN_TN]) * y[:, FFN_TN:]).astype(y_ref.dtype)


def ffn_up_fused(h2, w_up_il, state8, cw, cb, *, bsz, tm=512):
    m, kd = h2.shape
    s = m // bsz
    assert s % tm == 0 and tm % HALO == 0
    tiles_per_seq = s // tm
    width = cw.shape[0]
    nc = D_FF // FFN_TN
    tn2 = 2 * FFN_TN
    y, ns = pl.pallas_call(
        functools.partial(_ffn_up_fused_kernel, tm=tm, tiles_per_seq=tiles_per_seq, width=width),
        out_shape=(jax.ShapeDtypeStruct((m, D_FF), BF16),
                   jax.ShapeDtypeStruct((m // tm, 8, 2 * D_FF), F32)),
        grid=(m // tm, nc),
        in_specs=[pl.BlockSpec((tm, kd), lambda i, j: (i, 0)),
                  pl.BlockSpec((HALO, kd), lambda i, j: (jnp.maximum(i * (tm // HALO) - 1, 0), 0)),
                  pl.BlockSpec((kd, tn2), lambda i, j: (0, j)),
                  pl.BlockSpec((1, 8, tn2), lambda i, j: (i // tiles_per_seq, 0, j)),
                  pl.BlockSpec((width, tn2), lambda i, j: (0, j)),
                  pl.BlockSpec((1, tn2), lambda i, j: (0, j))],
        out_specs=(pl.BlockSpec((tm, FFN_TN), lambda i, j: (i, j)),
                   pl.BlockSpec((1, 8, tn2), lambda i, j: (i, 0, j))),
        scratch_shapes=[pltpu.VMEM((tm + HALO, kd), BF16), pltpu.VMEM((tm + HALO, tn2), F32)],
        compiler_params=_cp(("arbitrary", "arbitrary")),
        name="ffn_up_fused",
    )(h2, h2, w_up_il, _interleave_ffn(state8), _interleave_ffn(cw), _interleave_ffn(cb.reshape(1, 2 * D_FF)))
    return y, _deinterleave_ffn(ns[tiles_per_seq - 1::tiles_per_seq])


def _chunk_masks(lt, c):
    sh = int(math.log2(c))
    ri = lax.broadcasted_iota(jnp.int32, (lt, lt), 0)
    ci = lax.broadcasted_iota(jnp.int32, (lt, lt), 1)
    same = (ri >> sh) == (ci >> sh)
    return ri, ci, same


def _split3(x):
    hi = x.astype(BF16)
    r = x - hi.astype(F32)
    mid = r.astype(BF16)
    lo = (r - mid.astype(F32)).astype(BF16)
    return hi, mid, lo


def _dot_exact_lhs(l16, x):
    hi, mid, lo = _split3(x)
    return (jnp.dot(l16, hi, preferred_element_type=F32) + jnp.dot(l16, mid, preferred_element_type=F32)
            + jnp.dot(l16, lo, preferred_element_type=F32))


def _dot_hi(a, b):
    a_hi = a.astype(BF16)
    a_lo = (a - a_hi.astype(F32)).astype(BF16)
    b_hi = b.astype(BF16)
    b_lo = (b - b_hi.astype(F32)).astype(BF16)
    return (jnp.dot(a_hi, b_hi, preferred_element_type=F32) + jnp.dot(a_hi, b_lo, preferred_element_type=F32)
            + jnp.dot(a_lo, b_hi, preferred_element_type=F32))


def _unit_lower_inverse(m, eye, c):
    p = -m
    t = eye + p
    for _ in range(int(math.log2(c)) - 1):
        p16 = p.astype(BF16)
        p = jnp.dot(p16, p16, preferred_element_type=F32)
        t = t + jnp.dot(p.astype(BF16), t.astype(BF16), preferred_element_type=F32)
    r = eye - _dot_hi(eye + m, t)
    return t + jnp.dot(t.astype(BF16), r.astype(BF16), preferred_element_type=F32)


def _gdn_kernel(q_ref, k_ref, v_ref, sm_ref, z_ref, s0_ref, alog_ref, dtb_ref, nw_ref,
                o_ref, sout_ref, s_sc, vn_sc, *, lt, c, hps):
    hg = pl.program_id(1)
    t = pl.program_id(2)
    nt = pl.num_programs(2)

    @pl.when(t == 0)
    def _():
        s_sc[...] = s0_ref[0]

    sm = sm_ref[0]
    lane = lax.broadcasted_iota(jnp.int32, (1, 128), 1)
    beta_full = jax.nn.sigmoid(sm)
    g_full = -jnp.exp(alog_ref[...]) * _softplus(sm + dtb_ref[...])
    ri, ci, same = _chunk_masks(lt, c)
    tri = same & (ci <= ri)
    strict = same & (ci < ri)
    eye_b = ri == ci
    eye = jnp.where(eye_b, 1.0, 0.0)
    gc_full = _dot_exact_lhs(jnp.where(tri, 1.0, 0.0).astype(BF16), g_full)
    nw = nw_ref[...]

    for hh in range(hps):
        cols = slice(hh * GDN_D, (hh + 1) * GDN_D)
        h = hg * hps + hh
        q = q_ref[0, :, cols]
        k = k_ref[0, :, cols]
        v = v_ref[0, :, cols]
        q = q * lax.rsqrt(jnp.sum(q * q, axis=-1, keepdims=True) + EPS) * (GDN_D ** -0.5)
        k = k * lax.rsqrt(jnp.sum(k * k, axis=-1, keepdims=True) + EPS)
        beta = jnp.sum(jnp.where(lane == L_BETA + h, beta_full, 0.0), axis=-1, keepdims=True)
        gc = jnp.sum(jnp.where(lane == L_ALPHA + h, gc_full, 0.0), axis=-1, keepdims=True)
        gr = jnp.sum(jnp.where(eye_b, gc, 0.0), axis=0, keepdims=True)
        decay = jnp.exp(jnp.where(tri, gc - gr, 0.0))

        kb = k * beta
        k16 = k.astype(BF16)
        kk = _nt(kb.astype(BF16), k16)
        qk = _nt(q.astype(BF16), k16)
        m = jnp.where(strict, kk * decay, 0.0)
        a = jnp.where(tri, qk * decay, 0.0).astype(BF16)
        tinv = _unit_lower_inverse(m, eye, c)
        eg = jnp.exp(gc)
        rhs = jnp.concatenate([v * beta, kb * eg], axis=1).astype(BF16)
        uw = jnp.dot(tinv.astype(BF16), rhs, preferred_element_type=F32)
        u = uw[:, :GDN_D]
        w = uw[:, GDN_D:].astype(BF16)
        qd = (q * eg).astype(BF16)
        z = z_ref[0, :, cols]

        s = s_sc[hh]
        vn_sc[hh] = u.astype(BF16)
        for cc in range(lt // c):
            r0 = cc * c
            s16 = s.astype(BF16)
            vnew = u[r0:r0 + c] - jnp.dot(w[r0:r0 + c], s16, preferred_element_type=F32)
            vn16 = vnew.astype(BF16)
            vn_sc[hh, r0:r0 + c, :] = vn16
            o = (jnp.dot(qd[r0:r0 + c], s16, preferred_element_type=F32)
                 + jnp.dot(a[r0:r0 + c, :], vn_sc[hh], preferred_element_type=F32))
            glast = gc[r0 + c - 1:r0 + c, :]
            kd = (k[r0:r0 + c] * jnp.exp(glast - gc[r0:r0 + c])).astype(BF16)
            s = s * jnp.exp(glast) + _tn(kd, vn16)
            ms = jnp.mean(o * o, axis=-1, keepdims=True)
            o = o * lax.rsqrt(ms + EPS) * nw * _silu(z[r0:r0 + c])
            o_ref[0, r0:r0 + c, cols] = o.astype(o_ref.dtype)
        s_sc[hh] = s

    @pl.when(t == nt - 1)
    def _():
        sout_ref[0] = s_sc[...]


def gdn(qkv3, p3, s0, alog_row, dtb_row, norm_w, *, hps=4):
    bsz, s, _ = qkv3.shape
    c = min(CHUNK, s)
    lt = min(256, s)
    hw = hps * GDN_D
    ng = GDN_H // hps
    small_blk = P_SMALL // 128
    za_blk = P_ZA // hw
    blk = lambda off: pl.BlockSpec((1, lt, hw), lambda b, h, t: (b, t, off + h))
    return pl.pallas_call(
        functools.partial(_gdn_kernel, lt=lt, c=c, hps=hps),
        out_shape=(jax.ShapeDtypeStruct((bsz, s, BRANCH), BF16),
                   jax.ShapeDtypeStruct((bsz, GDN_H, GDN_D, GDN_D), F32)),
        grid=(bsz, ng, s // lt),
        in_specs=[blk(0), blk(ng), blk(2 * ng),
                  pl.BlockSpec((1, lt, 128), lambda b, h, t: (b, t, small_blk)),
                  blk(za_blk),
                  pl.BlockSpec((1, hps, GDN_D, GDN_D), lambda b, h, t: (b, h, 0, 0)),
                  pl.BlockSpec((1, 128), lambda b, h, t: (0, 0)),
                  pl.BlockSpec((1, 128), lambda b, h, t: (0, 0)),
                  pl.BlockSpec((1, 128), lambda b, h, t: (0, 0))],
        out_specs=(pl.BlockSpec((1, lt, hw), lambda b, h, t: (b, t, h)),
                   pl.BlockSpec((1, hps, GDN_D, GDN_D), lambda b, h, t: (b, h, 0, 0))),
        scratch_shapes=[pltpu.VMEM((hps, GDN_D, GDN_D), F32), pltpu.VMEM((hps, lt, GDN_D), BF16)],
        compiler_params=_cp(("parallel", "parallel", "arbitrary")),
        name="gdn",
    )(qkv3, qkv3, qkv3, p3, p3, s0, alog_row, dtb_row, norm_w.reshape(1, GDN_D))


def _ssd_kernel(x_ref, b_ref, c_ref, sm_ref, z_ref, h0_ref, alog_ref, dtb_ref, d_ref, nw_ref,
                y_ref, hout_ref, hs_sc, y_sc, *, lt, c):
    g = pl.program_id(1)
    t = pl.program_id(2)
    nt = pl.num_programs(2)
    npair = SSM_H // SSM_G // 2

    @pl.when(t == 0)
    def _():
        hs_sc[...] = h0_ref[0]

    sm = sm_ref[0]
    dt_full = _softplus(sm + dtb_ref[...])
    da = dt_full * (-jnp.exp(alog_ref[...]))
    ri, ci, same = _chunk_masks(lt, c)
    tri = same & (ci <= ri)
    eye_b = ri == ci
    acum_full = _dot_exact_lhs(jnp.where(tri, 1.0, 0.0).astype(BF16), da)
    nch = lt // c
    lasts = [acum_full[cc * c + c - 1:cc * c + c, :] for cc in range(nch)]
    alast_full = jnp.concatenate([jnp.broadcast_to(l, (c, 128)) for l in lasts], axis=0)
    ea_full = jnp.exp(acum_full)
    dtw_full = dt_full * jnp.exp(alast_full - acum_full)

    bm = b_ref[0]
    cm = c_ref[0]
    bm16 = bm.astype(BF16)
    cm16 = cm.astype(BF16)
    cb = jnp.where(tri, _nt(cm16, bm16), 0.0)
    lane = lax.broadcasted_iota(jnp.int32, (1, 128), 1)
    lo = lane < SSM_P
    rowlo = lax.broadcasted_iota(jnp.int32, (128, 1), 0) < SSM_P
    x = x_ref[0]
    z = z_ref[0]
    d_row = d_ref[...]

    def ext(full, l):
        return jnp.sum(jnp.where(lane == l, full, 0.0), axis=-1, keepdims=True)

    for j in range(npair):
        la = L_DT + g * (SSM_H // SSM_G) + 2 * j
        lb = la + 1
        ac_a, ac_b = ext(acum_full, la), ext(acum_full, lb)
        row_a = jnp.sum(jnp.where(eye_b, ac_a, 0.0), axis=0, keepdims=True)
        row_b = jnp.sum(jnp.where(eye_b, ac_b, 0.0), axis=0, keepdims=True)
        l_a = (jnp.exp(jnp.where(tri, ac_a - row_a, 0.0)) * cb).astype(BF16)
        l_b = (jnp.exp(jnp.where(tri, ac_b - row_b, 0.0)) * cb).astype(BF16)
        dt_pair = jnp.where(lo, ext(dt_full, la), ext(dt_full, lb))
        ea_pair = jnp.where(lo, ext(ea_full, la), ext(ea_full, lb))
        dtw_pair = jnp.where(lo, ext(dtw_full, la), ext(dtw_full, lb))
        xp = x[:, j * 128:(j + 1) * 128]
        xdt = xp * dt_pair
        y = (jnp.dot(l_a, jnp.where(lo, xdt, 0.0).astype(BF16), preferred_element_type=F32)
             + jnp.dot(l_b, jnp.where(lo, 0.0, xdt).astype(BF16), preferred_element_type=F32))
        xw = (xp * dtw_pair).astype(BF16)
        hs = hs_sc[j]
        yoffs = []
        for cc in range(nch):
            r0 = cc * c
            yoffs.append(_nt(cm16[r0:r0 + c], hs.astype(BF16)) * ea_pair[r0:r0 + c])
            cs = _tn(xw[r0:r0 + c], bm16[r0:r0 + c])
            al = jnp.where(rowlo, jnp.exp(ac_a[r0 + c - 1:r0 + c, :]), jnp.exp(ac_b[r0 + c - 1:r0 + c, :]))
            hs = hs * al + cs
        hs_sc[j] = hs
        yoff = yoffs[0] if nch == 1 else jnp.concatenate(yoffs, axis=0)
        ytot = (y + yoff + d_row[:, j * 128:(j + 1) * 128] * xp) * _silu(z[:, j * 128:(j + 1) * 128])
        y_sc[:, j * 128:(j + 1) * 128] = ytot

    yy = y_sc[...]
    ms = jnp.mean(yy * yy, axis=-1, keepdims=True)
    y_ref[0] = (yy * lax.rsqrt(ms + EPS) * nw_ref[...]).astype(y_ref.dtype)

    @pl.when(t == nt - 1)
    def _():
        hout_ref[0] = hs_sc[...]


def ssd(xbc3, p3, h0, alog_row, dtb_row, d_row, norm_w):
    bsz, s, _ = xbc3.shape
    c = min(CHUNK, s)
    lt = min(256, s)
    gw = BRANCH // SSM_G
    npair = SSM_H // SSM_G // 2
    small_blk = P_SMALL // 128
    zb_blk = P_ZB // gw
    return pl.pallas_call(
        functools.partial(_ssd_kernel, lt=lt, c=c),
        out_shape=(jax.ShapeDtypeStruct((bsz, s, BRANCH), BF16),
                   jax.ShapeDtypeStruct((bsz, SSM_H // 2, 128, 128), F32)),
        grid=(bsz, SSM_G, s // lt),
        in_specs=[pl.BlockSpec((1, lt, gw), lambda b, g, t: (b, t, g)),
                  pl.BlockSpec((1, lt, 128), lambda b, g, t: (b, t, BRANCH // 128 + g)),
                  pl.BlockSpec((1, lt, 128), lambda b, g, t: (b, t, BRANCH // 128 + SSM_G + g)),
                  pl.BlockSpec((1, lt, 128), lambda b, g, t: (b, t, small_blk)),
                  pl.BlockSpec((1, lt, gw), lambda b, g, t: (b, t, zb_blk + g)),
                  pl.BlockSpec((1, npair, 128, 128), lambda b, g, t: (b, g, 0, 0)),
                  pl.BlockSpec((1, 128), lambda b, g, t: (0, 0)),
                  pl.BlockSpec((1, 128), lambda b, g, t: (0, 0)),
                  pl.BlockSpec((1, gw), lambda b, g, t: (0, g)),
                  pl.BlockSpec((1, gw), lambda b, g, t: (0, g))],
        out_specs=(pl.BlockSpec((1, lt, gw), lambda b, g, t: (b, t, g)),
                   pl.BlockSpec((1, npair, 128, 128), lambda b, g, t: (b, g, 0, 0))),
        scratch_shapes=[pltpu.VMEM((npair, 128, 128), F32), pltpu.VMEM((lt, gw), F32)],
        compiler_params=_cp(("parallel", "parallel", "arbitrary")),
        name="ssd",
    )(xbc3, xbc3, xbc3, p3, p3, h0, alog_row, dtb_row, d_row, norm_w.reshape(1, BRANCH))


def _mla_prep_kernel(p_ref, qn_ref, kvn_ref, ck_ref, sak_ref, sbk_ref,
                     cq_ref, lat_ref, kr_ref, latkr_ref):
    blk = p_ref[...]
    cq = blk[:, 0:MLA_Q_RANK]
    cq = cq * lax.rsqrt(jnp.mean(cq * cq, axis=-1, keepdims=True) + EPS) * qn_ref[...]
    cq_ref[...] = cq.astype(cq_ref.dtype)
    ckv = blk[:, MLA_Q_RANK:MLA_Q_RANK + MLA_KV_RANK]
    lat = ckv * lax.rsqrt(jnp.mean(ckv * ckv, axis=-1, keepdims=True) + EPS) * kvn_ref[...]
    lat_ref[...] = lat
    sm = blk[:, 1280:1408]
    y = sm * ck_ref[...] + pltpu.roll(sm, 32, 1) * sak_ref[...] + pltpu.roll(sm, 96, 1) * sbk_ref[...]
    kr_ref[...] = y[:, 0:MLA_ROPE]
    latkr_ref[:, 0:MLA_KV_RANK] = lat.astype(latkr_ref.dtype)
    latkr_ref[:, MLA_KV_RANK:LATKR] = y.astype(latkr_ref.dtype)


def mla_prep(p, q_norm, kv_norm, ck, sak, sbk):
    m = p.shape[0]
    tm = min(256, m)
    mla_blk = P_MLA // 1536
    row = lambda n: pl.BlockSpec((tm, n), lambda i: (i, 0))
    return pl.pallas_call(
        _mla_prep_kernel,
        out_shape=(jax.ShapeDtypeStruct((m, MLA_Q_RANK), BF16),
                   jax.ShapeDtypeStruct((m, MLA_KV_RANK), F32),
                   jax.ShapeDtypeStruct((m, MLA_ROPE), F32),
                   jax.ShapeDtypeStruct((m, LATKR), BF16)),
        grid=(m // tm,),
        in_specs=[pl.BlockSpec((tm, 1536), lambda i: (i, mla_blk)),
                  pl.BlockSpec((1, MLA_Q_RANK), lambda i: (0, 0)),
                  pl.BlockSpec((1, MLA_KV_RANK), lambda i: (0, 0)),
                  row(128), row(128), row(128)],
        out_specs=(row(MLA_Q_RANK), row(MLA_KV_RANK), row(MLA_ROPE), row(LATKR)),
        compiler_params=_cp(("parallel",)),
        name="mla_prep",
    )(p, q_norm.reshape(1, -1), kv_norm.reshape(1, -1), ck, sak, sbk)


def _flash_kernel(q_ref, k_ref, v_ref, o_ref, m_sc, l_sc, acc_sc, *, tq, tk, q_off, nk_real):
    i = pl.program_id(2)
    j = pl.program_id(3)
    nj = pl.num_programs(3)

    @pl.when(j == 0)
    def _():
        m_sc[...] = jnp.full(m_sc.shape, -1e30, F32)
        l_sc[...] = jnp.zeros(l_sc.shape, F32)
        acc_sc[...] = jnp.zeros(acc_sc.shape, F32)

    q_last = q_off + i * tq + tq - 1

    @pl.when(((j * tk) >> 6) <= (q_last >> 6))
    def _():
        s = _nt(q_ref[0], k_ref[0])
        qpos = q_off + i * tq + lax.broadcasted_iota(jnp.int32, (tq, 1), 0)
        kidx = j * tk + lax.broadcasted_iota(jnp.int32, (1, tk), 1)
        vis = ((kidx >> 6) <= (qpos >> 6)) & (kidx < nk_real)
        s = jnp.where(vis, s, -1e30)
        m_prev = m_sc[...]
        m_new = jnp.maximum(m_prev, jnp.max(s, axis=-1, keepdims=True))
        alpha = jnp.exp2(m_prev - m_new)
        p = jnp.where(vis, jnp.exp2(s - m_new), 0.0)
        l_sc[...] = alpha * l_sc[...] + jnp.sum(p, axis=-1, keepdims=True)
        acc_sc[...] = alpha * acc_sc[...] + jnp.dot(p.astype(BF16), v_ref[0], preferred_element_type=F32)
        m_sc[...] = m_new

    @pl.when(j == nj - 1)
    def _():
        o_ref[0] = (acc_sc[...] / l_sc[...]).astype(o_ref.dtype)


def flash_attention(q3, kv3, *, q_off, nk_real):
    bsz, sq, _ = q3.shape
    tk_tot = kv3.shape[1]
    tq = min(512, sq)
    tk = tk_tot if tk_tot <= 4096 else 512
    assert sq % tq == 0 and tk_tot % tk == 0
    v_blk0 = MLA_H * QSLOT // MLA_V

    def last_blk(i):
        return ((q_off + i * tq + tq - 1) >> 6 << 6) // tk

    return pl.pallas_call(
        functools.partial(_flash_kernel, tq=tq, tk=tk, q_off=q_off, nk_real=nk_real),
        out_shape=jax.ShapeDtypeStruct((bsz, sq, BRANCH), BF16),
        grid=(bsz, MLA_H, sq // tq, tk_tot // tk),
        in_specs=[pl.BlockSpec((1, tq, QSLOT), lambda b, h, i, j: (b, i, h)),
                  pl.BlockSpec((1, tk, QSLOT), lambda b, h, i, j: (b, jnp.minimum(j, last_blk(i)), h)),
                  pl.BlockSpec((1, tk, MLA_V), lambda b, h, i, j: (b, jnp.minimum(j, last_blk(i)), v_blk0 + h))],
        out_specs=pl.BlockSpec((1, tq, MLA_V), lambda b, h, i, j: (b, i, h)),
        scratch_shapes=[pltpu.VMEM((tq, 1), F32), pltpu.VMEM((tq, 1), F32), pltpu.VMEM((tq, MLA_V), F32)],
        compiler_params=_cp(("parallel", "parallel", "parallel", "arbitrary")),
        name="flash_attention",
    )(q3, kv3, kv3)


def _flash_t_kernel(qi_ref, kj_ref, q_ref, k_ref, vt_ref, o_ref, m_sc, l_sc, acc_sc, *, tq, tk):
    pidx = pl.program_id(2)
    i = qi_ref[pidx]
    j = kj_ref[pidx]

    @pl.when(j == 0)
    def _():
        m_sc[...] = jnp.full(m_sc.shape, -1e30, F32)
        l_sc[...] = jnp.zeros(l_sc.shape, F32)
        acc_sc[...] = jnp.zeros(acc_sc.shape, F32)

    s = _nt(k_ref[0], q_ref[0])

    def update(s, vis):
        m_prev = m_sc[...]
        m_new = jnp.maximum(m_prev, jnp.max(s, axis=0, keepdims=True))
        alpha = jnp.exp2(m_prev - m_new)
        p = jnp.exp2(s - m_new)
        if vis is not None:
            p = jnp.where(vis, p, 0.0)
        l_sc[...] = alpha * l_sc[...] + jnp.sum(p, axis=0, keepdims=True)
        acc_sc[...] = alpha * acc_sc[...] + jnp.dot(vt_ref[0], p.astype(BF16), preferred_element_type=F32)
        m_sc[...] = m_new

    full = ((j * tk + tk - 1) >> 6) <= ((i * tq) >> 6)

    @pl.when(full)
    def _():
        update(s, None)

    @pl.when(jnp.logical_not(full))
    def _():
        kidx = j * tk + lax.broadcasted_iota(jnp.int32, (tk, 1), 0)
        qpos = i * tq + lax.broadcasted_iota(jnp.int32, (1, tq), 1)
        vis = (kidx >> 6) <= (qpos >> 6)
        update(jnp.where(vis, s, -1e30), vis)

    @pl.when(j == (i * tq + tq - 1) // tk)
    def _():
        o_ref[0] = (acc_sc[...] / l_sc[...]).T.astype(o_ref.dtype)


def flash_attention_causal(q3, k3, vt3, *, tq=1024, tk=512):
    bsz, s, _ = q3.shape
    tq, tk = min(tq, s), min(tk, s)
    assert s % tq == 0 and s % tk == 0 and tq % CHUNK == 0 and tk % CHUNK == 0
    pairs = [(i, j) for i in range(s // tq) for j in range((i * tq + tq - 1) // tk + 1)]
    qi = jnp.asarray([p[0] for p in pairs], jnp.int32)
    kj = jnp.asarray([p[1] for p in pairs], jnp.int32)
    grid_spec = pltpu.PrefetchScalarGridSpec(
        num_scalar_prefetch=2,
        grid=(bsz, MLA_H, len(pairs)),
        in_specs=[pl.BlockSpec((1, tq, QSLOT), lambda b, h, p, qi, kj: (b, qi[p], h)),
                  pl.BlockSpec((1, tk, QSLOT), lambda b, h, p, qi, kj: (b, kj[p], h)),
                  pl.BlockSpec((1, MLA_V, tk), lambda b, h, p, qi, kj: (b, h, kj[p]))],
        out_specs=pl.BlockSpec((1, tq, MLA_V), lambda b, h, p, qi, kj: (b, qi[p], h)),
        scratch_shapes=[pltpu.VMEM((1, tq), F32), pltpu.VMEM((1, tq), F32), pltpu.VMEM((MLA_V, tq), F32)])
    return pl.pallas_call(
        functools.partial(_flash_t_kernel, tq=tq, tk=tk),
        out_shape=jax.ShapeDtypeStruct((bsz, s, BRANCH), BF16),
        grid_spec=grid_spec,
        compiler_params=_cp(("parallel", "parallel", "arbitrary")),
        name="flash_attention_causal",
    )(qi, kj, q3, k3, vt3)


def _vt_kernel(w_ref, x_ref, o_ref):
    o_ref[0] = _nt(w_ref[...], x_ref[0]).astype(o_ref.dtype)


def v_up_transposed(w_vt, latkr3, *, tn=512):
    bsz, s, _ = latkr3.shape
    tn = min(tn, s)
    nv = w_vt.shape[0]
    return pl.pallas_call(
        _vt_kernel,
        out_shape=jax.ShapeDtypeStruct((bsz, nv, s), BF16),
        grid=(bsz, s // tn),
        in_specs=[pl.BlockSpec((nv, LATKR), lambda b, t: (0, 0)),
                  pl.BlockSpec((1, tn, LATKR), lambda b, t: (b, t, 0))],
        out_specs=pl.BlockSpec((1, nv, tn), lambda b, t: (b, 0, t)),
        compiler_params=_cp(("parallel", "parallel")),
        name="v_up_transposed",
    )(w_vt, latkr3)


def _branch_kernel(oa_ref, ob_ref, oc_ref, w_ref, g0_ref, g1_ref, g2_ref, out_ref):
    acc = None
    for n, (o_ref, g_ref) in enumerate(((oa_ref, g0_ref), (ob_ref, g1_ref), (oc_ref, g2_ref))):
        u = jnp.dot(o_ref[...], w_ref[n], preferred_element_type=F32)
        term = jax.nn.sigmoid(g_ref[...]) * u
        acc = term if acc is None else acc + term
    out_ref[...] = acc.astype(out_ref.dtype)


def branch_merge(oa, ob, oc, w_branch, p):
    m = oa.shape[0]
    tm = min(512, m)
    tn = 512
    nj = D_MODEL // tn
    o_spec = pl.BlockSpec((tm, BRANCH), lambda i, j: (i, 0))
    gate = lambda n: pl.BlockSpec((tm, tn), lambda i, j: (i, P_GATE // tn + n * nj + j))
    return pl.pallas_call(
        _branch_kernel,
        out_shape=jax.ShapeDtypeStruct((m, D_MODEL), BF16),
        grid=(m // tm, nj),
        in_specs=[o_spec, o_spec, o_spec,
                  pl.BlockSpec((3, BRANCH, tn), lambda i, j: (0, 0, j)),
                  gate(0), gate(1), gate(2)],
        out_specs=pl.BlockSpec((tm, tn), lambda i, j: (i, j)),
        compiler_params=_cp(("parallel", "parallel")),
        name="branch_merge",
    )(oa, ob, oc, w_branch, p, p, p)


def _pack_layer(l, w):
    w_in = w['w_in'][l]
    seg = lambda a, n: w_in[:, a:a + n]
    zeros = lambda n: jnp.zeros((D_MODEL, n), w_in.dtype)
    w_in_p = jnp.concatenate([
        seg(8032, 12288),
        seg(6688, 768), seg(7456, 512),
        seg(7968, 64), seg(4096, 8), seg(4104, 8), seg(6672, 16), zeros(32), zeros(128),
        seg(5136, 1536), seg(0, 3072), seg(3072, 1024), seg(4112, 1024)], axis=1).astype(BF16)
    w_uq = jnp.pad(w['mla_w_uq'][l], ((0, 0), (0, 0), (0, QSLOT - MLA_NOPE - MLA_ROPE)))
    w_uq = w_uq.reshape(MLA_Q_RANK, MLA_H * QSLOT).astype(BF16)
    wk = jnp.pad(w['mla_w_uk'][l], ((0, 0), (0, 0), (0, QSLOT - MLA_NOPE))).reshape(MLA_KV_RANK, MLA_H * QSLOT)
    eye = jnp.broadcast_to(jnp.eye(MLA_ROPE, dtype=F32)[:, None, :], (MLA_ROPE, MLA_H, MLA_ROPE))
    wkr = jnp.pad(eye, ((0, 0), (0, 0), (MLA_NOPE, QSLOT - MLA_NOPE - MLA_ROPE))).reshape(MLA_ROPE, MLA_H * QSLOT)
    wk = jnp.concatenate([wk, wkr, jnp.zeros((LATKR - MLA_KV_RANK - MLA_ROPE, MLA_H * QSLOT), F32)], axis=0)
    wv = jnp.pad(w['mla_w_uv'][l].reshape(MLA_KV_RANK, MLA_H * MLA_V), ((0, LATKR - MLA_KV_RANK), (0, 0)))
    w_kv = jnp.concatenate([wk, wv], axis=1).astype(BF16)
    w_k = wk.astype(BF16)
    w_vt = wv.T.astype(BF16)

    def lane_row(vals, off):
        return jnp.zeros((1, 128), F32).at[0, off:off + vals.shape[0]].set(vals.astype(F32))

    return dict(
        attn_norm=w['attn_norm'][l], w_in=w_in_p,
        gdn_conv_w=w['gdn_conv_w'][l], gdn_alog=lane_row(w['gdn_A_log'][l], L_ALPHA),
        gdn_dtb=lane_row(w['gdn_dt_bias'][l], L_ALPHA), gdn_norm=w['gdn_norm'][l],
        ssm_conv_w=w['ssm_conv_w'][l], ssm_conv_b=w['ssm_conv_b'][l],
        ssm_alog=lane_row(w['ssm_A_log'][l], L_DT), ssm_dtb=lane_row(w['ssm_dt_bias'][l], L_DT),
        ssm_d=jnp.repeat(w['ssm_D'][l].astype(F32), SSM_P).reshape(1, BRANCH), ssm_norm=w['ssm_norm'][l],
        mla_q_norm=w['mla_q_norm'][l], w_uq=w_uq, mla_kv_norm=w['mla_kv_norm'][l], w_kv=w_kv, w_k=w_k, w_vt=w_vt,
        w_branch=w['w_branch'][l].astype(BF16), w_out=w['w_out'][l].astype(BF16),
        ffn_norm=w['ffn_norm'][l], w_up=w['ffn_w_up'][l].astype(BF16),
        w_up_il=_interleave_ffn(w['ffn_w_up'][l]).astype(BF16),
        ffn_conv_w=w['ffn_conv_w'][l], ffn_conv_b=w['ffn_conv_b'][l],
        w_down=w['ffn_w_down'][l].astype(BF16))


def _rope_tables(pos):
    half = MLA_ROPE // 2
    inv = ROPE_THETA ** (-jnp.arange(half, dtype=F32) / half)
    ang = pos.astype(F32)[:, None] * inv[None, :]
    cos, sin = jnp.cos(ang), jnp.sin(ang)
    n = pos.shape[0]
    z = lambda k: jnp.zeros((n, k), F32)
    cq = jnp.concatenate([jnp.ones((n, MLA_NOPE), F32), cos, cos, z(64)], axis=1)
    saq = jnp.concatenate([z(MLA_NOPE + half), sin, z(64)], axis=1)
    sbq = jnp.concatenate([z(MLA_NOPE), -sin, z(half + 64)], axis=1)
    ck = jnp.concatenate([cos, cos, z(64)], axis=1)
    sak = jnp.concatenate([z(half), sin, z(64)], axis=1)
    sbk = jnp.concatenate([-sin, z(half + 64)], axis=1)
    return cq, saq, sbq, ck, sak, sbk


def _pad_state(st):
    return jnp.pad(st, ((0, 0), (8 - st.shape[1], 0), (0, 0)))


def _layer(x, tabs, q_off, state, lw):
    latkr_past, gconv, gssm, mconv, mssm, fconv = state
    bsz, s, _ = x.shape
    m = bsz * s
    cq_t, saq_t, sbq_t, ck_t, sak_t, sbk_t = tabs
    x2 = x.reshape(m, D_MODEL)
    mm_tm = 512

    h = rmsnorm(x2, lw['attn_norm'], BF16)
    p = matmul(h, lw['w_in'], F32, tm=mm_tm, tn=1024, name="in_proj")
    p3 = p.reshape(bsz, s, P_TOT)

    qkv_act, gconv8 = causal_conv(p3, P_QKV, GDN_QKV, _pad_state(gconv), lw['gdn_conv_w'],
                                  jnp.zeros((GDN_QKV,), F32), act=True)
    o_a, gssm_new = gdn(qkv_act, p3, gssm, lw['gdn_alog'], lw['gdn_dtb'], lw['gdn_norm'])

    xbc_act, mconv8 = causal_conv(p3, P_XBC, SSM_CONV_DIM, _pad_state(mconv), lw['ssm_conv_w'],
                                  lw['ssm_conv_b'], act=True)
    o_b, mssm_new = ssd(xbc_act, p3, mssm.reshape(bsz, SSM_H // 2, 128, 128), lw['ssm_alog'], lw['ssm_dtb'],
                        lw['ssm_d'], lw['ssm_norm'])
    mssm_new = mssm_new.reshape(bsz, SSM_H, SSM_P, SSM_N)

    cqn, lat, kr, latkr = mla_prep(p, lw['mla_q_norm'], lw['mla_kv_norm'], ck_t, sak_t, sbk_t)
    tq_tab = min(mm_tm, m)
    q_full = matmul(cqn, lw['w_uq'], BF16, tm=mm_tm, tn=MLA_H * QSLOT, name="q_up",
                    extras=[(cq_t, (tq_tab, QSLOT), lambda i, j: (i, 0)),
                            (saq_t, (tq_tab, QSLOT), lambda i, j: (i, 0)),
                            (sbq_t, (tq_tab, QSLOT), lambda i, j: (i, 0))],
                    epilogue=_q_rope_epilogue)
    latkr3 = latkr.reshape(bsz, s, LATKR)
    q3 = q_full.reshape(bsz, s, MLA_H * QSLOT)
    if latkr_past is None and q_off == 0 and s % 512 == 0:
        k_full = matmul(latkr, lw['w_k'], BF16, tm=mm_tm, tn=1024, name="k_up")
        vt = v_up_transposed(lw['w_vt'], latkr3)
        o_c = flash_attention_causal(q3, k_full.reshape(bsz, s, MLA_H * QSLOT), vt)
    else:
        past = jnp.zeros((bsz, 0, LATKR), BF16) if latkr_past is None else latkr_past
        nk_real = past.shape[1] + s
        nk_pad = -(-nk_real // 128) * 128
        latkr_all = jnp.concatenate([past, latkr3, jnp.zeros((bsz, nk_pad - nk_real, LATKR), BF16)], axis=1)
        kv = matmul(latkr_all.reshape(bsz * nk_pad, LATKR), lw['w_kv'], BF16, tm=mm_tm, tn=1024, name="kv_up")
        o_c = flash_attention(q3, kv.reshape(bsz, nk_pad, -1), q_off=q_off, nk_real=nk_real)

    merged = branch_merge(o_a.reshape(m, BRANCH), o_b.reshape(m, BRANCH), o_c.reshape(m, BRANCH),
                          lw['w_branch'], p)
    x2 = matmul_residual(merged, lw['w_out'], x2, tm=mm_tm, tn=1024, name="out_proj")

    h2 = rmsnorm(x2, lw['ffn_norm'], BF16)
    if s % 512 == 0:
        act, fconv8 = ffn_up_fused(h2, lw['w_up_il'], _pad_state(fconv), lw['ffn_conv_w'], lw['ffn_conv_b'],
                                   bsz=bsz)
    else:
        up = matmul(h2, lw['w_up'], F32, tm=mm_tm, tn=512, name="ffn_up")
        act, fconv8 = ffn_conv_gate(up.reshape(bsz, s, 2 * D_FF), _pad_state(fconv), lw['ffn_conv_w'],
                                    lw['ffn_conv_b'], ct=D_FF // 2)
    x2 = matmul_residual(act.reshape(m, D_FF), lw['w_down'], x2, tm=mm_tm, tn=1024, tk=D_FF // 2,
                         name="ffn_down")

    new = (lat.reshape(bsz, s, MLA_KV_RANK), kr.reshape(bsz, s, MLA_ROPE),
           gconv8[:, 8 - gconv.shape[1]:], gssm_new, mconv8[:, 8 - mconv.shape[1]:], mssm_new,
           fconv8[:, 8 - fconv.shape[1]:])
    return x2.reshape(bsz, s, D_MODEL), new


def kernel(x_prompt, x_sample, cache_mla_latent, cache_mla_krope, state_gdn_conv, state_gdn_ssm, state_ssm_conv, state_ssm, state_ffn_conv, attn_norm, w_in, gdn_conv_w, gdn_A_log, gdn_dt_bias, gdn_norm, ssm_conv_w, ssm_conv_b, ssm_A_log, ssm_dt_bias, ssm_D, ssm_norm, mla_q_norm, mla_w_uq, mla_kv_norm, mla_w_uk, mla_w_uv, w_branch, w_out, ffn_norm, ffn_w_up, ffn_conv_w, ffn_conv_b, ffn_w_down, final_norm):
    weights = dict(attn_norm=attn_norm, w_in=w_in, gdn_conv_w=gdn_conv_w, gdn_A_log=gdn_A_log,
                   gdn_dt_bias=gdn_dt_bias, gdn_norm=gdn_norm, ssm_conv_w=ssm_conv_w, ssm_conv_b=ssm_conv_b,
                   ssm_A_log=ssm_A_log, ssm_dt_bias=ssm_dt_bias, ssm_D=ssm_D, ssm_norm=ssm_norm,
                   mla_q_norm=mla_q_norm, mla_w_uq=mla_w_uq, mla_kv_norm=mla_kv_norm, mla_w_uk=mla_w_uk,
                   mla_w_uv=mla_w_uv, w_branch=w_branch, w_out=w_out, ffn_norm=ffn_norm, ffn_w_up=ffn_w_up,
                   ffn_conv_w=ffn_conv_w, ffn_conv_b=ffn_conv_b, ffn_w_down=ffn_w_down)
    depth = w_in.shape[0]
    bp, sp, _ = x_prompt.shape
    bs, ss, _ = x_sample.shape
    n_past = cache_mla_latent.shape[2]
    pos_p = jnp.arange(sp, dtype=jnp.int32)
    pos_s = n_past + jnp.arange(ss, dtype=jnp.int32)
    tabs_p = _rope_tables(pos_p)
    tabs_s = tuple(jnp.tile(t, (bs, 1)) for t in _rope_tables(pos_s))
    zero_state = (
        None,
        jnp.zeros((bp,) + state_gdn_conv.shape[2:], F32),
        jnp.zeros((bp,) + state_gdn_ssm.shape[2:], F32),
        jnp.zeros((bp,) + state_ssm_conv.shape[2:], F32),
        jnp.zeros((bp,) + state_ssm.shape[2:], F32),
        jnp.zeros((bp,) + state_ffn_conv.shape[2:], F32),
    )
    xp, xs = x_prompt, x_sample
    new_p, new_s = [], []
    for l in range(depth):
        lw = _pack_layer(l, weights)
        xp, st_p = _layer(xp, tabs_p, 0, zero_state, lw)
        latkr_past = jnp.concatenate(
            [cache_mla_latent[l], cache_mla_krope[l],
             jnp.zeros((bs, n_past, LATKR - MLA_KV_RANK - MLA_ROPE), F32)], axis=-1).astype(BF16)
        st_in = (latkr_past, state_gdn_conv[l], state_gdn_ssm[l], state_ssm_conv[l], state_ssm[l],
                 state_ffn_conv[l])
        xs, st_s = _layer(xs, tabs_s, n_past, st_in, lw)
        new_p.append(st_p)
        new_s.append(st_s)
    y_prompt = rmsnorm(xp.reshape(bp * sp, D_MODEL), final_norm, F32).reshape(bp, sp, D_MODEL)
    y_sample = rmsnorm(xs.reshape(bs * ss, D_MODEL), final_norm, F32).reshape(bs, ss, D_MODEL)
    outs_p = [jnp.stack(t) for t in zip(*new_p)]
    outs_s = [jnp.stack(t) for t in zip(*new_s)]
    return (y_prompt, y_sample, *outs_p, *outs_s)
```

```python
import functools
import math

import jax
import jax.numpy as jnp
from jax import lax
from jax.experimental import pallas as pl
from jax.experimental.pallas import tpu as pltpu

F32 = jnp.float32
BF16 = jnp.bfloat16
HI = lax.Precision.HIGHEST

EPS = 1e-6
CHUNK = 64
D_MODEL = 4096
BRANCH = 1024
GDN_H, GDN_D = 8, 128
GDN_QKV = 3072
SSM_H, SSM_P, SSM_G, SSM_N = 16, 64, 2, 128
SSM_CONV_DIM = 1536
MLA_H, MLA_NOPE, MLA_ROPE, MLA_V = 8, 128, 64, 128
MLA_Q_RANK, MLA_KV_RANK = 768, 512
ROPE_THETA = 10000.0
D_FF = 11008
LATKR = 640
QSLOT = 256

P_GATE = 0
P_MLA = 12288
P_SMALL = P_MLA + 1280
P_XBC = 13824
P_QKV = 15360
P_ZA = 18432
P_ZB = 19456
P_TOT = 20480
L_BETA, L_ALPHA, L_DT = 64, 72, 80

VMEM_LIMIT = 56 * 1024 * 1024


def _cp(sem):
    return pltpu.CompilerParams(dimension_semantics=sem, vmem_limit_bytes=VMEM_LIMIT)


def _softplus(x):
    return jnp.maximum(x, 0.0) + jnp.log1p(jnp.exp(-jnp.abs(x)))


def _silu(x):
    return x * jax.nn.sigmoid(x)


def _nt(a, b):
    return lax.dot_general(a, b, (((1,), (1,)), ((), ())), preferred_element_type=F32)


def _tn(a, b):
    return lax.dot_general(a, b, (((0,), (0,)), ((), ())), preferred_element_type=F32)


def _rms_kernel(x_ref, w_ref, o_ref):
    x = x_ref[...]
    ms = jnp.mean(x * x, axis=-1, keepdims=True)
    o_ref[...] = (x * lax.rsqrt(ms + EPS) * w_ref[...]).astype(o_ref.dtype)


def rmsnorm(x, w, out_dtype):
    m, d = x.shape
    tm = min(256, m)
    return pl.pallas_call(
        _rms_kernel,
        out_shape=jax.ShapeDtypeStruct((m, d), out_dtype),
        grid=(m // tm,),
        in_specs=[pl.BlockSpec((tm, d), lambda i: (i, 0)),
                  pl.BlockSpec((1, d), lambda i: (0, 0))],
        out_specs=pl.BlockSpec((tm, d), lambda i: (i, 0)),
        compiler_params=_cp(("parallel",)),
        name="rmsnorm",
    )(x, w.reshape(1, d))


def _mm_kernel(*refs, nk, n_extra, epilogue):
    a_ref, b_ref = refs[0], refs[1]
    extras = refs[2:2 + n_extra]
    o_ref = refs[2 + n_extra]
    part = jnp.dot(a_ref[...], b_ref[...], preferred_element_type=F32)
    if nk == 1:
        o_ref[...] = epilogue(part, *extras).astype(o_ref.dtype)
        return
    acc_ref = refs[3 + n_extra]
    k = pl.program_id(2)

    @pl.when(k == 0)
    def _():
        acc_ref[...] = part

    @pl.when(k > 0)
    def _():
        acc_ref[...] += part

    @pl.when(k == nk - 1)
    def _():
        o_ref[...] = epilogue(acc_ref[...], *extras).astype(o_ref.dtype)


def matmul(a, b, out_dtype, *, tm, tn, tk=None, extras=(), epilogue=None, name="matmul"):
    m, kd = a.shape
    n = b.shape[1]
    tm, tn = min(tm, m), min(tn, n)
    while m % tm:
        tm //= 2
    tk = kd if tk is None else tk
    nk = kd // tk
    assert m % tm == 0 and n % tn == 0 and kd % tk == 0
    if epilogue is None:
        epilogue = lambda acc: acc
    in_specs = [pl.BlockSpec((tm, tk), lambda i, j, k: (i, k)),
                pl.BlockSpec((tk, tn), lambda i, j, k: (k, j))]
    args = [a, b]
    for arr, blk, imap in extras:
        in_specs.append(pl.BlockSpec(blk, functools.partial(lambda i, j, k, f: f(i, j), f=imap)))
        args.append(arr)
    scratch = [pltpu.VMEM((tm, tn), F32)] if nk > 1 else []
    return pl.pallas_call(
        functools.partial(_mm_kernel, nk=nk, n_extra=len(extras), epilogue=epilogue),
        out_shape=jax.ShapeDtypeStruct((m, n), out_dtype),
        grid=(m // tm, n // tn, nk),
        in_specs=in_specs,
        out_specs=pl.BlockSpec((tm, tn), lambda i, j, k: (i, j)),
        scratch_shapes=scratch,
        compiler_params=_cp(("parallel", "parallel", "arbitrary")),
        name=name,
    )(*args)


def _residual_epilogue(acc, r_ref):
    return acc + r_ref[...]


def matmul_residual(a, b, res, *, tm, tn, tk=None, name="matmul_residual"):
    tm_, tn_ = min(tm, a.shape[0]), min(tn, b.shape[1])
    return matmul(a, b, F32, tm=tm, tn=tn, tk=tk, name=name,
                  extras=[(res, (tm_, tn_), lambda i, j: (i, j))], epilogue=_residual_epilogue)


def _q_rope_epilogue(acc, c_ref, sa_ref, sb_ref):
    c, sa, sb = c_ref[...], sa_ref[...], sb_ref[...]
    scale = (MLA_NOPE + MLA_ROPE) ** -0.5 * math.log2(math.e)
    outs = []
    for h in range(MLA_H):
        xs = acc[:, h * QSLOT:(h + 1) * QSLOT]
        y = xs * c + pltpu.roll(xs, 32, 1) * sa + pltpu.roll(xs, QSLOT - 32, 1) * sb
        outs.append(y * scale)
    return jnp.concatenate(outs, axis=1)


def _conv_kernel(x_ref, st_ref, w_ref, b_ref, y_ref, ns_ref, xp_ref, *, width, ts, act):
    t = pl.program_id(2)
    nt = pl.num_programs(2)

    @pl.when(t == 0)
    def _():
        xp_ref[0:8, :] = st_ref[0]

    @pl.when(t > 0)
    def _():
        xp_ref[0:8, :] = xp_ref[ts:ts + 8, :]

    xp_ref[8:8 + ts, :] = x_ref[0]
    off = 8 - (width - 1)
    y = xp_ref[off:off + ts, :] * w_ref[0:1, :]
    for j in range(1, width):
        y = y + xp_ref[off + j:off + j + ts, :] * w_ref[j:j + 1, :]
    y = y + b_ref[...]
    if act:
        y = _silu(y)
    y_ref[0] = y.astype(y_ref.dtype)

    @pl.when(t == nt - 1)
    def _():
        ns_ref[0] = xp_ref[ts:ts + 8, :]


def causal_conv(x3, col_off, chans, state8, w, b, *, act, ct=512):
    bsz, s, _ = x3.shape
    width = w.shape[0]
    ts = min(512, s)
    assert chans % ct == 0 and col_off % ct == 0 and s % ts == 0
    cb0 = col_off // ct
    return pl.pallas_call(
        functools.partial(_conv_kernel, width=width, ts=ts, act=act),
        out_shape=(jax.ShapeDtypeStruct((bsz, s, chans), F32),
                   jax.ShapeDtypeStruct((bsz, 8, chans), F32)),
        grid=(bsz, chans // ct, s // ts),
        in_specs=[pl.BlockSpec((1, ts, ct), lambda bi, c, t: (bi, t, cb0 + c)),
                  pl.BlockSpec((1, 8, ct), lambda bi, c, t: (bi, 0, c)),
                  pl.BlockSpec((width, ct), lambda bi, c, t: (0, c)),
                  pl.BlockSpec((1, ct), lambda bi, c, t: (0, c))],
        out_specs=(pl.BlockSpec((1, ts, ct), lambda bi, c, t: (bi, t, c)),
                   pl.BlockSpec((1, 8, ct), lambda bi, c, t: (bi, 0, c))),
        scratch_shapes=[pltpu.VMEM((ts + 8, ct), F32)],
        compiler_params=_cp(("parallel", "parallel", "arbitrary")),
        name="causal_conv",
    )(x3, state8, w, b.reshape(1, chans))


HALO = 16


FFN_TN = 256


def _interleave_ffn(x):
    lead = x.shape[:-1]
    x = x.reshape(*lead, 2, D_FF // FFN_TN, FFN_TN)
    return jnp.swapaxes(x, -3, -2).reshape(*lead, 2 * D_FF)


def _deinterleave_ffn(x):
    lead = x.shape[:-1]
    x = x.reshape(*lead, D_FF // FFN_TN, 2, FFN_TN)
    return jnp.swapaxes(x, -3, -2).reshape(*lead, 2 * D_FF)


def _il_cast_kernel(a_ref, b_ref, o_ref):
    o_ref[:, :FFN_TN] = a_ref[...].astype(o_ref.dtype)
    o_ref[:, FFN_TN:] = b_ref[...].astype(o_ref.dtype)


def interleave_cast_w_up(w, *, tk=1024):
    kd = w.shape[0]
    nc = D_FF // FFN_TN
    return pl.pallas_call(
        _il_cast_kernel,
        out_shape=jax.ShapeDtypeStruct((kd, 2 * D_FF), BF16),
        grid=(kd // tk, nc),
        in_specs=[pl.BlockSpec((tk, FFN_TN), lambda k, j: (k, j)),
                  pl.BlockSpec((tk, FFN_TN), lambda k, j: (k, nc + j))],
        out_specs=pl.BlockSpec((tk, 2 * FFN_TN), lambda k, j: (k, j)),
        compiler_params=_cp(("parallel", "parallel")),
        name="interleave_cast_w_up",
    )(w, w)


def _ffn_conv_gate_il_kernel(x_ref, st_ref, cw_ref, cb_ref, y_ref, ns_ref, xp_ref, *, width, s):
    xp_ref[0:8, :] = st_ref[0]
    xp_ref[8:8 + s, :] = x_ref[0]
    off = 8 - (width - 1)
    y = xp_ref[off:off + s, :] * cw_ref[0:1, :]
    for t in range(1, width):
        y = y + xp_ref[off + t:off + t + s, :] * cw_ref[t:t + 1, :]
    y = y + cb_ref[...]
    ns_ref[0] = xp_ref[s:s + 8, :]
    for c in range(D_FF // FFN_TN):
        g = y[:, 2 * c * FFN_TN:(2 * c + 1) * FFN_TN]
        v = y[:, (2 * c + 1) * FFN_TN:(2 * c + 2) * FFN_TN]
        y_ref[0, :, c * FFN_TN:(c + 1) * FFN_TN] = (_silu(g) * v).astype(y_ref.dtype)


def ffn_conv_gate_il(up3, state8, cw, cb):
    bsz, s, n2 = up3.shape
    width = cw.shape[0]
    y, ns = pl.pallas_call(
        functools.partial(_ffn_conv_gate_il_kernel, width=width, s=s),
        out_shape=(jax.ShapeDtypeStruct((bsz, s, D_FF), BF16),
                   jax.ShapeDtypeStruct((bsz, 8, n2), F32)),
        grid=(bsz,),
        in_specs=[pl.BlockSpec((1, s, n2), lambda b: (b, 0, 0)),
                  pl.BlockSpec((1, 8, n2), lambda b: (b, 0, 0)),
                  pl.BlockSpec((width, n2), lambda b: (0, 0)),
                  pl.BlockSpec((1, n2), lambda b: (0, 0))],
        out_specs=(pl.BlockSpec((1, s, D_FF), lambda b: (b, 0, 0)),
                   pl.BlockSpec((1, 8, n2), lambda b: (b, 0, 0))),
        scratch_shapes=[pltpu.VMEM((s + 8, n2), F32)],
        compiler_params=_cp(("parallel",)),
        name="ffn_conv_gate_il",
    )(up3, _interleave_ffn(state8), _interleave_ffn(cw), _interleave_ffn(cb.reshape(1, n2)))
    return y, _deinterleave_ffn(ns)


FFN_SUB = 256


def _ffn_up_fused_kernel(a_ref, halo_ref, w_ref, st_ref, cw_ref, cb_ref, y_ref, ns_ref, aext_ref, *xp_refs,
                         tm, tiles_per_seq, width):
    i = pl.program_id(0)
    j = pl.program_id(1)

    @pl.when(j == 0)
    def _():
        aext_ref[0:HALO, :] = halo_ref[...]
        aext_ref[HALO:HALO + tm, :] = a_ref[...]

    nsub = tm // FFN_SUB
    for r in range(nsub):
        xp_refs[r][...] = jnp.dot(aext_ref[r * FFN_SUB:(r + 1) * FFN_SUB + HALO, :], w_ref[...],
                                  preferred_element_type=F32)
    first = (i % tiles_per_seq) == 0
    xp_refs[0][HALO - 8:HALO, :] = jnp.where(first, st_ref[0], xp_refs[0][HALO - 8:HALO, :])
    off = HALO - (width - 1)
    cw = cw_ref[...]
    cb = cb_ref[...]
    for r in range(nsub):
        xp_ref = xp_refs[r]
        y = xp_ref[off:off + FFN_SUB, :] * cw[0:1, :]
        for t in range(1, width):
            y = y + xp_ref[off + t:off + t + FFN_SUB, :] * cw[t:t + 1, :]
        y = y + cb
        y_ref[r * FFN_SUB:(r + 1) * FFN_SUB, :] = (_silu(y[:, :FFN_TN]) * y[:, FFN_TN:]).astype(y_ref.dtype)
    ns_ref[0] = xp_refs[nsub - 1][HALO + FFN_SUB - 8:HALO + FFN_SUB, :]


def ffn_up_fused(h2, w_up_il, state8, cw, cb, *, bsz, tm=1024):
    m, kd = h2.shape
    s = m // bsz
    tm = min(tm, s)
    assert s % tm == 0 and tm % FFN_SUB == 0
    tiles_per_seq = s // tm
    width = cw.shape[0]
    nc = D_FF // FFN_TN
    tn2 = 2 * FFN_TN
    y, ns = pl.pallas_call(
        functools.partial(_ffn_up_fused_kernel, tm=tm, tiles_per_seq=tiles_per_seq, width=width),
        out_shape=(jax.ShapeDtypeStruct((m, D_FF), BF16),
                   jax.ShapeDtypeStruct((m // tm, 8, 2 * D_FF), F32)),
        grid=(m // tm, nc),
        in_specs=[pl.BlockSpec((tm, kd), lambda i, j: (i, 0)),
                  pl.BlockSpec((HALO, kd), lambda i, j: (jnp.maximum(i * (tm // HALO) - 1, 0), 0)),
                  pl.BlockSpec((kd, tn2), lambda i, j: (0, j)),
                  pl.BlockSpec((1, 8, tn2), lambda i, j: (i // tiles_per_seq, 0, j)),
                  pl.BlockSpec((width, tn2), lambda i, j: (0, j)),
                  pl.BlockSpec((1, tn2), lambda i, j: (0, j))],
        out_specs=(pl.BlockSpec((tm, FFN_TN), lambda i, j: (i, j)),
                   pl.BlockSpec((1, 8, tn2), lambda i, j: (i, 0, j))),
        scratch_shapes=[pltpu.VMEM((tm + HALO, kd), BF16)]
        + [pltpu.VMEM((FFN_SUB + HALO, tn2), F32) for _ in range(tm // FFN_SUB)],
        compiler_params=_cp(("arbitrary", "arbitrary")),
        name="ffn_up_fused",
    )(h2, h2, w_up_il, _interleave_ffn(state8), _interleave_ffn(cw), _interleave_ffn(cb.reshape(1, 2 * D_FF)))
    return y, _deinterleave_ffn(ns[tiles_per_seq - 1::tiles_per_seq])


def _chunk_masks(lt, c):
    sh = int(math.log2(c))
    ri = lax.broadcasted_iota(jnp.int32, (lt, lt), 0)
    ci = lax.broadcasted_iota(jnp.int32, (lt, lt), 1)
    same = (ri >> sh) == (ci >> sh)
    return ri, ci, same


def _split3(x):
    hi = x.astype(BF16)
    r = x - hi.astype(F32)
    mid = r.astype(BF16)
    lo = (r - mid.astype(F32)).astype(BF16)
    return hi, mid, lo


def _dot_exact_lhs(l16, x):
    hi, mid, lo = _split3(x)
    return (jnp.dot(l16, hi, preferred_element_type=F32) + jnp.dot(l16, mid, preferred_element_type=F32)
            + jnp.dot(l16, lo, preferred_element_type=F32))


def _dot_hi(a, b):
    a_hi = a.astype(BF16)
    a_lo = (a - a_hi.astype(F32)).astype(BF16)
    b_hi = b.astype(BF16)
    b_lo = (b - b_hi.astype(F32)).astype(BF16)
    return (jnp.dot(a_hi, b_hi, preferred_element_type=F32) + jnp.dot(a_hi, b_lo, preferred_element_type=F32)
            + jnp.dot(a_lo, b_hi, preferred_element_type=F32))


def _gdn_kernel(q_ref, k_ref, v_ref, sm_ref, z_ref, s0_ref, alog_ref, dtb_ref, nw_ref,
                o_ref, sout_ref, s_sc, vn_sc, *, lt, c, hps):
    hg = pl.program_id(1)
    t = pl.program_id(2)
    nt = pl.num_programs(2)

    @pl.when(t == 0)
    def _():
        s_sc[...] = s0_ref[0]

    sm = sm_ref[0]
    lane = lax.broadcasted_iota(jnp.int32, (1, 128), 1)
    beta_full = jax.nn.sigmoid(sm)
    g_full = -jnp.exp(alog_ref[...]) * _softplus(sm + dtb_ref[...])
    ri, ci, same = _chunk_masks(lt, c)
    tri = same & (ci <= ri)
    strict = same & (ci < ri)
    eye_b = ri == ci
    eye = jnp.where(eye_b, 1.0, 0.0)
    gc_full = _dot_exact_lhs(jnp.where(tri, 1.0, 0.0).astype(BF16), g_full)
    nw = nw_ref[...]

    hs = range(hps)
    cols = [slice(hh * GDN_D, (hh + 1) * GDN_D) for hh in hs]
    dot16 = lambda x, y: jnp.dot(x.astype(BF16), y.astype(BF16), preferred_element_type=F32)
    ks, k16s, kbs, gcs, ms, avs, qds, rhss = [], [], [], [], [], [], [], []
    for hh in hs:
        h = hg * hps + hh
        q = q_ref[0, :, cols[hh]]
        k = k_ref[0, :, cols[hh]]
        q = q * lax.rsqrt(jnp.sum(q * q, axis=-1, keepdims=True) + EPS) * (GDN_D ** -0.5)
        k = k * lax.rsqrt(jnp.sum(k * k, axis=-1, keepdims=True) + EPS)
        beta = jnp.sum(jnp.where(lane == L_BETA + h, beta_full, 0.0), axis=-1, keepdims=True)
        gc = jnp.sum(jnp.where(lane == L_ALPHA + h, gc_full, 0.0), axis=-1, keepdims=True)
        gr = jnp.sum(jnp.where(eye_b, gc, 0.0), axis=0, keepdims=True)
        decay = jnp.exp(jnp.where(tri, gc - gr, 0.0))
        kb = k * beta
        k16 = k.astype(BF16)
        ms.append(jnp.where(strict, _nt(kb.astype(BF16), k16) * decay, 0.0))
        avs.append(jnp.where(tri, _nt(q.astype(BF16), k16) * decay, 0.0).astype(BF16))
        eg = jnp.exp(gc)
        rhss.append(jnp.concatenate([v_ref[0, :, cols[hh]] * beta, kb * eg], axis=1).astype(BF16))
        qds.append((q * eg).astype(BF16))
        ks.append(k)
        gcs.append(gc)

    ps = [-m for m in ms]
    ts = [eye + p for p in ps]
    for _ in range(int(math.log2(c)) - 1):
        ps = [dot16(p, p) for p in ps]
        ts = [tt + dot16(p, tt) for p, tt in zip(ps, ts)]
    rs = [eye - _dot_hi(eye + m, tt) for m, tt in zip(ms, ts)]
    ts = [tt + dot16(tt, r) for tt, r in zip(ts, rs)]
    uws = [jnp.dot(tt.astype(BF16), rhs, preferred_element_type=F32) for tt, rhs in zip(ts, rhss)]
    us = [uw[:, :GDN_D] for uw in uws]
    ws = [uw[:, GDN_D:].astype(BF16) for uw in uws]

    ss = [s_sc[hh] for hh in hs]
    for hh in hs:
        vn_sc[hh] = us[hh].astype(BF16)
    for cc in range(lt // c):
        r0 = cc * c
        rows = slice(r0, r0 + c)
        for hh in hs:
            s16 = ss[hh].astype(BF16)
            vnew = us[hh][rows] - jnp.dot(ws[hh][rows], s16, preferred_element_type=F32)
            vn16 = vnew.astype(BF16)
            vn_sc[hh, rows, :] = vn16
            o = (jnp.dot(qds[hh][rows], s16, preferred_element_type=F32)
                 + jnp.dot(avs[hh][rows, :], vn_sc[hh], preferred_element_type=F32))
            glast = gcs[hh][r0 + c - 1:r0 + c, :]
            kd = (ks[hh][rows] * jnp.exp(glast - gcs[hh][rows])).astype(BF16)
            ss[hh] = ss[hh] * jnp.exp(glast) + _tn(kd, vn16)
            mean_sq = jnp.mean(o * o, axis=-1, keepdims=True)
            o = o * lax.rsqrt(mean_sq + EPS) * nw * _silu(z_ref[0, rows, cols[hh]])
            o_ref[0, rows, cols[hh]] = o.astype(o_ref.dtype)
    for hh in hs:
        s_sc[hh] = ss[hh]

    @pl.when(t == nt - 1)
    def _():
        sout_ref[0] = s_sc[...]


def gdn(qkv3, p3, s0, alog_row, dtb_row, norm_w, *, hps=4):
    bsz, s, _ = qkv3.shape
    c = min(CHUNK, s)
    lt = min(256, s)
    hw = hps * GDN_D
    ng = GDN_H // hps
    small_blk = P_SMALL // 128
    za_blk = P_ZA // hw
    blk = lambda off: pl.BlockSpec((1, lt, hw), lambda b, h, t: (b, t, off + h))
    return pl.pallas_call(
        functools.partial(_gdn_kernel, lt=lt, c=c, hps=hps),
        out_shape=(jax.ShapeDtypeStruct((bsz, s, BRANCH), BF16),
                   jax.ShapeDtypeStruct((bsz, GDN_H, GDN_D, GDN_D), F32)),
        grid=(bsz, ng, s // lt),
        in_specs=[blk(0), blk(ng), blk(2 * ng),
                  pl.BlockSpec((1, lt, 128), lambda b, h, t: (b, t, small_blk)),
                  blk(za_blk),
                  pl.BlockSpec((1, hps, GDN_D, GDN_D), lambda b, h, t: (b, h, 0, 0)),
                  pl.BlockSpec((1, 128), lambda b, h, t: (0, 0)),
                  pl.BlockSpec((1, 128), lambda b, h, t: (0, 0)),
                  pl.BlockSpec((1, 128), lambda b, h, t: (0, 0))],
        out_specs=(pl.BlockSpec((1, lt, hw), lambda b, h, t: (b, t, h)),
                   pl.BlockSpec((1, hps, GDN_D, GDN_D), lambda b, h, t: (b, h, 0, 0))),
        scratch_shapes=[pltpu.VMEM((hps, GDN_D, GDN_D), F32), pltpu.VMEM((hps, lt, GDN_D), BF16)],
        compiler_params=_cp(("parallel", "parallel", "arbitrary")),
        name="gdn",
    )(qkv3, qkv3, qkv3, p3, p3, s0, alog_row, dtb_row, norm_w.reshape(1, GDN_D))


def _ssd_kernel(x_ref, b_ref, c_ref, sm_ref, z_ref, h0_ref, alog_ref, dtb_ref, d_ref, nw_ref,
                y_ref, hout_ref, hs_sc, y_sc, *, lt, c):
    g = pl.program_id(1)
    t = pl.program_id(2)
    nt = pl.num_programs(2)
    npair = SSM_H // SSM_G // 2

    @pl.when(t == 0)
    def _():
        hs_sc[...] = h0_ref[0]

    sm = sm_ref[0]
    dt_full = _softplus(sm + dtb_ref[...])
    da = dt_full * (-jnp.exp(alog_ref[...]))
    ri, ci, same = _chunk_masks(lt, c)
    tri = same & (ci <= ri)
    eye_b = ri == ci
    acum_full = _dot_exact_lhs(jnp.where(tri, 1.0, 0.0).astype(BF16), da)
    nch = lt // c
    lasts = [acum_full[cc * c + c - 1:cc * c + c, :] for cc in range(nch)]
    alast_full = jnp.concatenate([jnp.broadcast_to(l, (c, 128)) for l in lasts], axis=0)
    ea_full = jnp.exp(acum_full)
    dtw_full = dt_full * jnp.exp(alast_full - acum_full)

    bm = b_ref[0]
    cm = c_ref[0]
    bm16 = bm.astype(BF16)
    cm16 = cm.astype(BF16)
    cb = jnp.where(tri, _nt(cm16, bm16), 0.0)
    lane = lax.broadcasted_iota(jnp.int32, (1, 128), 1)
    lo = lane < SSM_P
    rowlo = lax.broadcasted_iota(jnp.int32, (128, 1), 0) < SSM_P
    x = x_ref[0]
    z = z_ref[0]
    d_row = d_ref[...]

    def ext(full, l):
        return jnp.sum(jnp.where(lane == l, full, 0.0), axis=-1, keepdims=True)

    for j in range(npair):
        la = L_DT + g * (SSM_H // SSM_G) + 2 * j
        lb = la + 1
        ac_a, ac_b = ext(acum_full, la), ext(acum_full, lb)
        row_a = jnp.sum(jnp.where(eye_b, ac_a, 0.0), axis=0, keepdims=True)
        row_b = jnp.sum(jnp.where(eye_b, ac_b, 0.0), axis=0, keepdims=True)
        l_a = (jnp.exp(jnp.where(tri, ac_a - row_a, 0.0)) * cb).astype(BF16)
        l_b = (jnp.exp(jnp.where(tri, ac_b - row_b, 0.0)) * cb).astype(BF16)
        dt_pair = jnp.where(lo, ext(dt_full, la), ext(dt_full, lb))
        ea_pair = jnp.where(lo, ext(ea_full, la), ext(ea_full, lb))
        dtw_pair = jnp.where(lo, ext(dtw_full, la), ext(dtw_full, lb))
        xp = x[:, j * 128:(j + 1) * 128]
        xdt = xp * dt_pair
        y = (jnp.dot(l_a, jnp.where(lo, xdt, 0.0).astype(BF16), preferred_element_type=F32)
             + jnp.dot(l_b, jnp.where(lo, 0.0, xdt).astype(BF16), preferred_element_type=F32))
        xw = (xp * dtw_pair).astype(BF16)
        hs = hs_sc[j]
        yoffs = []
        for cc in range(nch):
            r0 = cc * c
            yoffs.append(_nt(cm16[r0:r0 + c], hs.astype(BF16)) * ea_pair[r0:r0 + c])
            cs = _tn(xw[r0:r0 + c], bm16[r0:r0 + c])
            al = jnp.where(rowlo, jnp.exp(ac_a[r0 + c - 1:r0 + c, :]), jnp.exp(ac_b[r0 + c - 1:r0 + c, :]))
            hs = hs * al + cs
        hs_sc[j] = hs
        yoff = yoffs[0] if nch == 1 else jnp.concatenate(yoffs, axis=0)
        ytot = (y + yoff + d_row[:, j * 128:(j + 1) * 128] * xp) * _silu(z[:, j * 128:(j + 1) * 128])
        y_sc[:, j * 128:(j + 1) * 128] = ytot

    yy = y_sc[...]
    ms = jnp.mean(yy * yy, axis=-1, keepdims=True)
    y_ref[0] = (yy * lax.rsqrt(ms + EPS) * nw_ref[...]).astype(y_ref.dtype)

    @pl.when(t == nt - 1)
    def _():
        hout_ref[0] = hs_sc[...]


def ssd(xbc3, p3, h0, alog_row, dtb_row, d_row, norm_w):
    bsz, s, _ = xbc3.shape
    c = min(CHUNK, s)
    lt = min(256, s)
    gw = BRANCH // SSM_G
    npair = SSM_H // SSM_G // 2
    small_blk = P_SMALL // 128
    zb_blk = P_ZB // gw
    return pl.pallas_call(
        functools.partial(_ssd_kernel, lt=lt, c=c),
        out_shape=(jax.ShapeDtypeStruct((bsz, s, BRANCH), BF16),
                   jax.ShapeDtypeStruct((bsz, SSM_H // 2, 128, 128), F32)),
        grid=(bsz, SSM_G, s // lt),
        in_specs=[pl.BlockSpec((1, lt, gw), lambda b, g, t: (b, t, g)),
                  pl.BlockSpec((1, lt, 128), lambda b, g, t: (b, t, BRANCH // 128 + g)),
                  pl.BlockSpec((1, lt, 128), lambda b, g, t: (b, t, BRANCH // 128 + SSM_G + g)),
                  pl.BlockSpec((1, lt, 128), lambda b, g, t: (b, t, small_blk)),
                  pl.BlockSpec((1, lt, gw), lambda b, g, t: (b, t, zb_blk + g)),
                  pl.BlockSpec((1, npair, 128, 128), lambda b, g, t: (b, g, 0, 0)),
                  pl.BlockSpec((1, 128), lambda b, g, t: (0, 0)),
                  pl.BlockSpec((1, 128), lambda b, g, t: (0, 0)),
                  pl.BlockSpec((1, gw), lambda b, g, t: (0, g)),
                  pl.BlockSpec((1, gw), lambda b, g, t: (0, g))],
        out_specs=(pl.BlockSpec((1, lt, gw), lambda b, g, t: (b, t, g)),
                   pl.BlockSpec((1, npair, 128, 128), lambda b, g, t: (b, g, 0, 0))),
        scratch_shapes=[pltpu.VMEM((npair, 128, 128), F32), pltpu.VMEM((lt, gw), F32)],
        compiler_params=_cp(("parallel", "parallel", "arbitrary")),
        name="ssd",
    )(xbc3, xbc3, xbc3, p3, p3, h0, alog_row, dtb_row, d_row, norm_w.reshape(1, BRANCH))


def _mla_prep_kernel(p_ref, qn_ref, kvn_ref, ck_ref, sak_ref, sbk_ref,
                     cq_ref, lat_ref, kr_ref, latkr_ref):
    blk = p_ref[...]
    cq = blk[:, 0:MLA_Q_RANK]
    cq = cq * lax.rsqrt(jnp.mean(cq * cq, axis=-1, keepdims=True) + EPS) * qn_ref[...]
    cq_ref[...] = cq.astype(cq_ref.dtype)
    ckv = blk[:, MLA_Q_RANK:MLA_Q_RANK + MLA_KV_RANK]
    lat = ckv * lax.rsqrt(jnp.mean(ckv * ckv, axis=-1, keepdims=True) + EPS) * kvn_ref[...]
    lat_ref[...] = lat
    sm = blk[:, 1280:1408]
    y = sm * ck_ref[...] + pltpu.roll(sm, 32, 1) * sak_ref[...] + pltpu.roll(sm, 96, 1) * sbk_ref[...]
    kr_ref[...] = y[:, 0:MLA_ROPE]
    latkr_ref[:, 0:MLA_KV_RANK] = lat.astype(latkr_ref.dtype)
    latkr_ref[:, MLA_KV_RANK:LATKR] = y.astype(latkr_ref.dtype)


def mla_prep(p, q_norm, kv_norm, ck, sak, sbk):
    m = p.shape[0]
    tm = min(256, m)
    mla_blk = P_MLA // 1536
    row = lambda n: pl.BlockSpec((tm, n), lambda i: (i, 0))
    return pl.pallas_call(
        _mla_prep_kernel,
        out_shape=(jax.ShapeDtypeStruct((m, MLA_Q_RANK), BF16),
                   jax.ShapeDtypeStruct((m, MLA_KV_RANK), F32),
                   jax.ShapeDtypeStruct((m, MLA_ROPE), F32),
                   jax.ShapeDtypeStruct((m, LATKR), BF16)),
        grid=(m // tm,),
        in_specs=[pl.BlockSpec((tm, 1536), lambda i: (i, mla_blk)),
                  pl.BlockSpec((1, MLA_Q_RANK), lambda i: (0, 0)),
                  pl.BlockSpec((1, MLA_KV_RANK), lambda i: (0, 0)),
                  row(128), row(128), row(128)],
        out_specs=(row(MLA_Q_RANK), row(MLA_KV_RANK), row(MLA_ROPE), row(LATKR)),
        compiler_params=_cp(("parallel",)),
        name="mla_prep",
    )(p, q_norm.reshape(1, -1), kv_norm.reshape(1, -1), ck, sak, sbk)


def _flash_kernel(q_ref, k_ref, v_ref, o_ref, m_sc, l_sc, acc_sc, *, tq, tk, q_off, nk_real):
    i = pl.program_id(2)
    j = pl.program_id(3)
    nj = pl.num_programs(3)

    @pl.when(j == 0)
    def _():
        m_sc[...] = jnp.full(m_sc.shape, -1e30, F32)
        l_sc[...] = jnp.zeros(l_sc.shape, F32)
        acc_sc[...] = jnp.zeros(acc_sc.shape, F32)

    q_last = q_off + i * tq + tq - 1

    @pl.when(((j * tk) >> 6) <= (q_last >> 6))
    def _():
        s = _nt(q_ref[0], k_ref[0])
        qpos = q_off + i * tq + lax.broadcasted_iota(jnp.int32, (tq, 1), 0)
        kidx = j * tk + lax.broadcasted_iota(jnp.int32, (1, tk), 1)
        vis = ((kidx >> 6) <= (qpos >> 6)) & (kidx < nk_real)
        s = jnp.where(vis, s, -1e30)
        m_prev = m_sc[...]
        m_new = jnp.maximum(m_prev, jnp.max(s, axis=-1, keepdims=True))
        alpha = jnp.exp2(m_prev - m_new)
        p = jnp.where(vis, jnp.exp2(s - m_new), 0.0)
        l_sc[...] = alpha * l_sc[...] + jnp.sum(p, axis=-1, keepdims=True)
        acc_sc[...] = alpha * acc_sc[...] + jnp.dot(p.astype(BF16), v_ref[0], preferred_element_type=F32)
        m_sc[...] = m_new

    @pl.when(j == nj - 1)
    def _():
        o_ref[0] = (acc_sc[...] / l_sc[...]).astype(o_ref.dtype)


def flash_attention(q3, kv3, *, q_off, nk_real):
    bsz, sq, _ = q3.shape
    tk_tot = kv3.shape[1]
    tq = min(512, sq)
    tk = tk_tot if tk_tot <= 4096 else 512
    assert sq % tq == 0 and tk_tot % tk == 0
    v_blk0 = MLA_H * QSLOT // MLA_V

    def last_blk(i):
        return ((q_off + i * tq + tq - 1) >> 6 << 6) // tk

    return pl.pallas_call(
        functools.partial(_flash_kernel, tq=tq, tk=tk, q_off=q_off, nk_real=nk_real),
        out_shape=jax.ShapeDtypeStruct((bsz, sq, BRANCH), BF16),
        grid=(bsz, MLA_H, sq // tq, tk_tot // tk),
        in_specs=[pl.BlockSpec((1, tq, QSLOT), lambda b, h, i, j: (b, i, h)),
                  pl.BlockSpec((1, tk, QSLOT), lambda b, h, i, j: (b, jnp.minimum(j, last_blk(i)), h)),
                  pl.BlockSpec((1, tk, MLA_V), lambda b, h, i, j: (b, jnp.minimum(j, last_blk(i)), v_blk0 + h))],
        out_specs=pl.BlockSpec((1, tq, MLA_V), lambda b, h, i, j: (b, i, h)),
        scratch_shapes=[pltpu.VMEM((tq, 1), F32), pltpu.VMEM((tq, 1), F32), pltpu.VMEM((tq, MLA_V), F32)],
        compiler_params=_cp(("parallel", "parallel", "parallel", "arbitrary")),
        name="flash_attention",
    )(q3, kv3, kv3)


def _flash_t_kernel(qi_ref, kj_ref, q_ref, k_ref, vt_ref, o_ref, m_sc, l_sc, acc_sc, *, tq, tk):
    pidx = pl.program_id(2)
    i = qi_ref[pidx]
    j = kj_ref[pidx]

    @pl.when(j == 0)
    def _():
        m_sc[...] = jnp.full(m_sc.shape, -1e30, F32)
        l_sc[...] = jnp.zeros(l_sc.shape, F32)
        acc_sc[...] = jnp.zeros(acc_sc.shape, F32)

    s = _nt(k_ref[0], q_ref[0])

    def update(s, vis):
        m_prev = m_sc[...]
        m_new = jnp.maximum(m_prev, jnp.max(s, axis=0, keepdims=True))
        alpha = jnp.exp2(m_prev - m_new)
        p = jnp.exp2(s - m_new)
        if vis is not None:
            p = jnp.where(vis, p, 0.0)
        l_sc[...] = alpha * l_sc[...] + jnp.sum(p, axis=0, keepdims=True)
        acc_sc[...] = alpha * acc_sc[...] + jnp.dot(vt_ref[0], p.astype(BF16), preferred_element_type=F32)
        m_sc[...] = m_new

    full = ((j * tk + tk - 1) >> 6) <= ((i * tq) >> 6)

    @pl.when(full)
    def _():
        update(s, None)

    @pl.when(jnp.logical_not(full))
    def _():
        kidx = j * tk + lax.broadcasted_iota(jnp.int32, (tk, 1), 0)
        qpos = i * tq + lax.broadcasted_iota(jnp.int32, (1, tq), 1)
        vis = (kidx >> 6) <= (qpos >> 6)
        update(jnp.where(vis, s, -1e30), vis)

    @pl.when(j == (i * tq + tq - 1) // tk)
    def _():
        o_ref[0] = (acc_sc[...] / l_sc[...]).T.astype(o_ref.dtype)


def flash_attention_causal(q3, k3, vt3, *, tq=1024, tk=1024):
    bsz, s, _ = q3.shape
    tq, tk = min(tq, s), min(tk, s)
    assert s % tq == 0 and s % tk == 0 and tq % CHUNK == 0 and tk % CHUNK == 0
    pairs = [(i, j) for i in range(s // tq) for j in range((i * tq + tq - 1) // tk + 1)]
    qi = jnp.asarray([p[0] for p in pairs], jnp.int32)
    kj = jnp.asarray([p[1] for p in pairs], jnp.int32)
    grid_spec = pltpu.PrefetchScalarGridSpec(
        num_scalar_prefetch=2,
        grid=(bsz, MLA_H, len(pairs)),
        in_specs=[pl.BlockSpec((1, tq, QSLOT), lambda b, h, p, qi, kj: (b, qi[p], h)),
                  pl.BlockSpec((1, tk, QSLOT), lambda b, h, p, qi, kj: (b, kj[p], h)),
                  pl.BlockSpec((1, MLA_V, tk), lambda b, h, p, qi, kj: (b, h, kj[p]))],
        out_specs=pl.BlockSpec((1, tq, MLA_V), lambda b, h, p, qi, kj: (b, qi[p], h)),
        scratch_shapes=[pltpu.VMEM((1, tq), F32), pltpu.VMEM((1, tq), F32), pltpu.VMEM((MLA_V, tq), F32)])
    return pl.pallas_call(
        functools.partial(_flash_t_kernel, tq=tq, tk=tk),
        out_shape=jax.ShapeDtypeStruct((bsz, s, BRANCH), BF16),
        grid_spec=grid_spec,
        compiler_params=_cp(("parallel", "parallel", "arbitrary")),
        name="flash_attention_causal",
    )(qi, kj, q3, k3, vt3)


def _vt_kernel(w_ref, x_ref, o_ref):
    o_ref[0] = _nt(w_ref[...], x_ref[0]).astype(o_ref.dtype)


def v_up_transposed(w_vt, latkr3, *, tn=512):
    bsz, s, _ = latkr3.shape
    tn = min(tn, s)
    nv = w_vt.shape[0]
    return pl.pallas_call(
        _vt_kernel,
        out_shape=jax.ShapeDtypeStruct((bsz, nv, s), BF16),
        grid=(bsz, s // tn),
        in_specs=[pl.BlockSpec((nv, LATKR), lambda b, t: (0, 0)),
                  pl.BlockSpec((1, tn, LATKR), lambda b, t: (b, t, 0))],
        out_specs=pl.BlockSpec((1, nv, tn), lambda b, t: (b, 0, t)),
        compiler_params=_cp(("parallel", "parallel")),
        name="v_up_transposed",
    )(w_vt, latkr3)


def _branch_kernel(oa_ref, ob_ref, oc_ref, w_ref, g0_ref, g1_ref, g2_ref, out_ref):
    acc = None
    for n, (o_ref, g_ref) in enumerate(((oa_ref, g0_ref), (ob_ref, g1_ref), (oc_ref, g2_ref))):
        u = jnp.dot(o_ref[...], w_ref[n], preferred_element_type=F32)
        term = jax.nn.sigmoid(g_ref[...]) * u
        acc = term if acc is None else acc + term
    out_ref[...] = acc.astype(out_ref.dtype)


def branch_merge(oa, ob, oc, w_branch, p):
    m = oa.shape[0]
    tm = min(512, m)
    tn = 512
    nj = D_MODEL // tn
    o_spec = pl.BlockSpec((tm, BRANCH), lambda i, j: (i, 0))
    gate = lambda n: pl.BlockSpec((tm, tn), lambda i, j: (i, P_GATE // tn + n * nj + j))
    return pl.pallas_call(
        _branch_kernel,
        out_shape=jax.ShapeDtypeStruct((m, D_MODEL), BF16),
        grid=(m // tm, nj),
        in_specs=[o_spec, o_spec, o_spec,
                  pl.BlockSpec((3, BRANCH, tn), lambda i, j: (0, 0, j)),
                  gate(0), gate(1), gate(2)],
        out_specs=pl.BlockSpec((tm, tn), lambda i, j: (i, j)),
        compiler_params=_cp(("parallel", "parallel")),
        name="branch_merge",
    )(oa, ob, oc, w_branch, p, p, p)


def _pack_layer(l, w):
    w_in = w['w_in'][l]
    seg = lambda a, n: w_in[:, a:a + n]
    zeros = lambda n: jnp.zeros((D_MODEL, n), w_in.dtype)
    w_in_p = jnp.concatenate([
        seg(8032, 12288),
        seg(6688, 768), seg(7456, 512),
        seg(7968, 64), seg(4096, 8), seg(4104, 8), seg(6672, 16), zeros(32), zeros(128),
        seg(5136, 1536), seg(0, 3072), seg(3072, 1024), seg(4112, 1024)], axis=1).astype(BF16)
    w_uq = jnp.pad(w['mla_w_uq'][l], ((0, 0), (0, 0), (0, QSLOT - MLA_NOPE - MLA_ROPE)))
    w_uq = w_uq.reshape(MLA_Q_RANK, MLA_H * QSLOT).astype(BF16)
    wk = jnp.pad(w['mla_w_uk'][l], ((0, 0), (0, 0), (0, QSLOT - MLA_NOPE))).reshape(MLA_KV_RANK, MLA_H * QSLOT)
    eye = jnp.broadcast_to(jnp.eye(MLA_ROPE, dtype=F32)[:, None, :], (MLA_ROPE, MLA_H, MLA_ROPE))
    wkr = jnp.pad(eye, ((0, 0), (0, 0), (MLA_NOPE, QSLOT - MLA_NOPE - MLA_ROPE))).reshape(MLA_ROPE, MLA_H * QSLOT)
    wk = jnp.concatenate([wk, wkr, jnp.zeros((LATKR - MLA_KV_RANK - MLA_ROPE, MLA_H * QSLOT), F32)], axis=0)
    wv = jnp.pad(w['mla_w_uv'][l].reshape(MLA_KV_RANK, MLA_H * MLA_V), ((0, LATKR - MLA_KV_RANK), (0, 0)))
    w_kv = jnp.concatenate([wk, wv], axis=1).astype(BF16)
    w_k = wk.astype(BF16)
    w_vt = wv.T.astype(BF16)

    def lane_row(vals, off):
        return jnp.zeros((1, 128), F32).at[0, off:off + vals.shape[0]].set(vals.astype(F32))

    return dict(
        attn_norm=w['attn_norm'][l], w_in=w_in_p,
        gdn_conv_w=w['gdn_conv_w'][l], gdn_alog=lane_row(w['gdn_A_log'][l], L_ALPHA),
        gdn_dtb=lane_row(w['gdn_dt_bias'][l], L_ALPHA), gdn_norm=w['gdn_norm'][l],
        ssm_conv_w=w['ssm_conv_w'][l], ssm_conv_b=w['ssm_conv_b'][l],
        ssm_alog=lane_row(w['ssm_A_log'][l], L_DT), ssm_dtb=lane_row(w['ssm_dt_bias'][l], L_DT),
        ssm_d=jnp.repeat(w['ssm_D'][l].astype(F32), SSM_P).reshape(1, BRANCH), ssm_norm=w['ssm_norm'][l],
        mla_q_norm=w['mla_q_norm'][l], w_uq=w_uq, mla_kv_norm=w['mla_kv_norm'][l], w_kv=w_kv, w_k=w_k, w_vt=w_vt,
        w_branch=w['w_branch'][l].astype(BF16), w_out=w['w_out'][l].astype(BF16),
        ffn_norm=w['ffn_norm'][l], w_up_il=interleave_cast_w_up(w['ffn_w_up'][l]),
        ffn_conv_w=w['ffn_conv_w'][l], ffn_conv_b=w['ffn_conv_b'][l],
        w_down=w['ffn_w_down'][l].astype(BF16))


def _rope_tables(pos):
    half = MLA_ROPE // 2
    inv = ROPE_THETA ** (-jnp.arange(half, dtype=F32) / half)
    ang = pos.astype(F32)[:, None] * inv[None, :]
    cos, sin = jnp.cos(ang), jnp.sin(ang)
    n = pos.shape[0]
    z = lambda k: jnp.zeros((n, k), F32)
    cq = jnp.concatenate([jnp.ones((n, MLA_NOPE), F32), cos, cos, z(64)], axis=1)
    saq = jnp.concatenate([z(MLA_NOPE + half), sin, z(64)], axis=1)
    sbq = jnp.concatenate([z(MLA_NOPE), -sin, z(half + 64)], axis=1)
    ck = jnp.concatenate([cos, cos, z(64)], axis=1)
    sak = jnp.concatenate([z(half), sin, z(64)], axis=1)
    sbk = jnp.concatenate([-sin, z(half + 64)], axis=1)
    return cq, saq, sbq, ck, sak, sbk


def _pad_state(st):
    return jnp.pad(st, ((0, 0), (8 - st.shape[1], 0), (0, 0)))


def _layer(x, tabs, q_off, state, lw):
    latkr_past, gconv, gssm, mconv, mssm, fconv = state
    bsz, s, _ = x.shape
    m = bsz * s
    cq_t, saq_t, sbq_t, ck_t, sak_t, sbk_t = tabs
    x2 = x.reshape(m, D_MODEL)
    mm_tm = 512

    h = rmsnorm(x2, lw['attn_norm'], BF16)
    p = matmul(h, lw['w_in'], F32, tm=mm_tm, tn=1024, name="in_proj")
    p3 = p.reshape(bsz, s, P_TOT)

    qkv_act, gconv8 = causal_conv(p3, P_QKV, GDN_QKV, _pad_state(gconv), lw['gdn_conv_w'],
                                  jnp.zeros((GDN_QKV,), F32), act=True)
    o_a, gssm_new = gdn(qkv_act, p3, gssm, lw['gdn_alog'], lw['gdn_dtb'], lw['gdn_norm'])

    xbc_act, mconv8 = causal_conv(p3, P_XBC, SSM_CONV_DIM, _pad_state(mconv), lw['ssm_conv_w'],
                                  lw['ssm_conv_b'], act=True)
    o_b, mssm_new = ssd(xbc_act, p3, mssm.reshape(bsz, SSM_H // 2, 128, 128), lw['ssm_alog'], lw['ssm_dtb'],
                        lw['ssm_d'], lw['ssm_norm'])
    mssm_new = mssm_new.reshape(bsz, SSM_H, SSM_P, SSM_N)

    cqn, lat, kr, latkr = mla_prep(p, lw['mla_q_norm'], lw['mla_kv_norm'], ck_t, sak_t, sbk_t)
    tq_tab = min(mm_tm, m)
    q_full = matmul(cqn, lw['w_uq'], BF16, tm=mm_tm, tn=MLA_H * QSLOT, name="q_up",
                    extras=[(cq_t, (tq_tab, QSLOT), lambda i, j: (i, 0)),
                            (saq_t, (tq_tab, QSLOT), lambda i, j: (i, 0)),
                            (sbq_t, (tq_tab, QSLOT), lambda i, j: (i, 0))],
                    epilogue=_q_rope_epilogue)
    latkr3 = latkr.reshape(bsz, s, LATKR)
    q3 = q_full.reshape(bsz, s, MLA_H * QSLOT)
    if latkr_past is None and q_off == 0 and s % 512 == 0:
        k_full = matmul(latkr, lw['w_k'], BF16, tm=mm_tm, tn=1024, name="k_up")
        vt = v_up_transposed(lw['w_vt'], latkr3)
        o_c = flash_attention_causal(q3, k_full.reshape(bsz, s, MLA_H * QSLOT), vt)
    else:
        past = jnp.zeros((bsz, 0, LATKR), BF16) if latkr_past is None else latkr_past
        nk_real = past.shape[1] + s
        nk_pad = -(-nk_real // 128) * 128
        latkr_all = jnp.concatenate([past, latkr3, jnp.zeros((bsz, nk_pad - nk_real, LATKR), BF16)], axis=1)
        kv = matmul(latkr_all.reshape(bsz * nk_pad, LATKR), lw['w_kv'], BF16, tm=mm_tm, tn=1024, name="kv_up")
        o_c = flash_attention(q3, kv.reshape(bsz, nk_pad, -1), q_off=q_off, nk_real=nk_real)

    merged = branch_merge(o_a.reshape(m, BRANCH), o_b.reshape(m, BRANCH), o_c.reshape(m, BRANCH),
                          lw['w_branch'], p)
    x2 = matmul_residual(merged, lw['w_out'], x2, tm=mm_tm, tn=1024, name="out_proj")

    h2 = rmsnorm(x2, lw['ffn_norm'], BF16)
    if s % 512 == 0:
        act, fconv8 = ffn_up_fused(h2, lw['w_up_il'], _pad_state(fconv), lw['ffn_conv_w'], lw['ffn_conv_b'],
                                   bsz=bsz, tm=1024 if s % 1024 == 0 else 512)
    else:
        up = matmul(h2, lw['w_up_il'], F32, tm=mm_tm, tn=512, name="ffn_up")
        act, fconv8 = ffn_conv_gate_il(up.reshape(bsz, s, 2 * D_FF), _pad_state(fconv), lw['ffn_conv_w'],
                                       lw['ffn_conv_b'])
    x2 = matmul_residual(act.reshape(m, D_FF), lw['w_down'], x2, tm=mm_tm, tn=1024, tk=D_FF // 2,
                         name="ffn_down")

    new = (lat.reshape(bsz, s, MLA_KV_RANK), kr.reshape(bsz, s, MLA_ROPE),
           gconv8[:, 8 - gconv.shape[1]:], gssm_new, mconv8[:, 8 - mconv.shape[1]:], mssm_new,
           fconv8[:, 8 - fconv.shape[1]:])
    return x2.reshape(bsz, s, D_MODEL), new


def kernel(x_prompt, x_sample, cache_mla_latent, cache_mla_krope, state_gdn_conv, state_gdn_ssm, state_ssm_conv, state_ssm, state_ffn_conv, attn_norm, w_in, gdn_conv_w, gdn_A_log, gdn_dt_bias, gdn_norm, ssm_conv_w, ssm_conv_b, ssm_A_log, ssm_dt_bias, ssm_D, ssm_norm, mla_q_norm, mla_w_uq, mla_kv_norm, mla_w_uk, mla_w_uv, w_branch, w_out, ffn_norm, ffn_w_up, ffn_conv_w, ffn_conv_b, ffn_w_down, final_norm):
    weights = dict(attn_norm=attn_norm, w_in=w_in, gdn_conv_w=gdn_conv_w, gdn_A_log=gdn_A_log,
                   gdn_dt_bias=gdn_dt_bias, gdn_norm=gdn_norm, ssm_conv_w=ssm_conv_w, ssm_conv_b=ssm_conv_b,
                   ssm_A_log=ssm_A_log, ssm_dt_bias=ssm_dt_bias, ssm_D=ssm_D, ssm_norm=ssm_norm,
                   mla_q_norm=mla_q_norm, mla_w_uq=mla_w_uq, mla_kv_norm=mla_kv_norm, mla_w_uk=mla_w_uk,
                   mla_w_uv=mla_w_uv, w_branch=w_branch, w_out=w_out, ffn_norm=ffn_norm, ffn_w_up=ffn_w_up,
                   ffn_conv_w=ffn_conv_w, ffn_conv_b=ffn_conv_b, ffn_w_down=ffn_w_down)
    depth = w_in.shape[0]
    bp, sp, _ = x_prompt.shape
    bs, ss, _ = x_sample.shape
    n_past = cache_mla_latent.shape[2]
    pos_p = jnp.arange(sp, dtype=jnp.int32)
    pos_s = n_past + jnp.arange(ss, dtype=jnp.int32)
    tabs_p = _rope_tables(pos_p)
    tabs_s = tuple(jnp.tile(t, (bs, 1)) for t in _rope_tables(pos_s))
    zero_state = (
        None,
        jnp.zeros((bp,) + state_gdn_conv.shape[2:], F32),
        jnp.zeros((bp,) + state_gdn_ssm.shape[2:], F32),
        jnp.zeros((bp,) + state_ssm_conv.shape[2:], F32),
        jnp.zeros((bp,) + state_ssm.shape[2:], F32),
        jnp.zeros((bp,) + state_ffn_conv.shape[2:], F32),
    )
    xp, xs = x_prompt, x_sample
    new_p, new_s = [], []
    for l in range(depth):
        lw = _pack_layer(l, weights)
        xp, st_p = _layer(xp, tabs_p, 0, zero_state, lw)
        latkr_past = jnp.concatenate(
            [cache_mla_latent[l], cache_mla_krope[l],
             jnp.zeros((bs, n_past, LATKR - MLA_KV_RANK - MLA_ROPE), F32)], axis=-1).astype(BF16)
        st_in = (latkr_past, state_gdn_conv[l], state_gdn_ssm[l], state_ssm_conv[l], state_ssm[l],
                 state_ffn_conv[l])
        xs, st_s = _layer(xs, tabs_s, n_past, st_in, lw)
        new_p.append(st_p)
        new_s.append(st_s)
    y_prompt = rmsnorm(xp.reshape(bp * sp, D_MODEL), final_norm, F32).reshape(bp, sp, D_MODEL)
    y_sample = rmsnorm(xs.reshape(bs * ss, D_MODEL), final_norm, F32).reshape(bs, ss, D_MODEL)
    outs_p = [jnp.stack(t) for t in zip(*new_p)]
    outs_s = [jnp.stack(t) for t in zip(*new_s)]
    return (y_prompt, y_sample, *outs_p, *outs_s)
```

```python
import functools
import math

import jax
import jax.numpy as jnp
from jax import lax
from jax.experimental import pallas as pl
from jax.experimental.pallas import tpu as pltpu

F32 = jnp.float32
BF16 = jnp.bfloat16
HI = lax.Precision.HIGHEST

EPS = 1e-6
CHUNK = 64
D_MODEL = 4096
BRANCH = 1024
GDN_H, GDN_D = 8, 128
GDN_QKV = 3072
SSM_H, SSM_P, SSM_G, SSM_N = 16, 64, 2, 128
SSM_CONV_DIM = 1536
MLA_H, MLA_NOPE, MLA_ROPE, MLA_V = 8, 128, 64, 128
MLA_Q_RANK, MLA_KV_RANK = 768, 512
ROPE_THETA = 10000.0
D_FF = 11008
LATKR = 640
QSLOT = 256

P_GATE = 0
P_MLA = 12288
P_SMALL = P_MLA + 1280
P_XBC = 13824
P_QKV = 15360
P_ZA = 18432
P_ZB = 19456
P_TOT = 20480
L_BETA, L_ALPHA, L_DT = 64, 72, 80

VMEM_LIMIT = 56 * 1024 * 1024


def _cp(sem):
    return pltpu.CompilerParams(dimension_semantics=sem, vmem_limit_bytes=VMEM_LIMIT)


def _softplus(x):
    return jnp.maximum(x, 0.0) + jnp.log1p(jnp.exp(-jnp.abs(x)))


def _silu(x):
    return x * jax.nn.sigmoid(x)


def _nt(a, b):
    return lax.dot_general(a, b, (((1,), (1,)), ((), ())), preferred_element_type=F32)


def _tn(a, b):
    return lax.dot_general(a, b, (((0,), (0,)), ((), ())), preferred_element_type=F32)


def _rms_kernel(x_ref, w_ref, o_ref):
    x = x_ref[...]
    ms = jnp.mean(x * x, axis=-1, keepdims=True)
    o_ref[...] = (x * lax.rsqrt(ms + EPS) * w_ref[...]).astype(o_ref.dtype)


def rmsnorm(x, w, out_dtype):
    m, d = x.shape
    tm = min(256, m)
    return pl.pallas_call(
        _rms_kernel,
        out_shape=jax.ShapeDtypeStruct((m, d), out_dtype),
        grid=(m // tm,),
        in_specs=[pl.BlockSpec((tm, d), lambda i: (i, 0)),
                  pl.BlockSpec((1, d), lambda i: (0, 0))],
        out_specs=pl.BlockSpec((tm, d), lambda i: (i, 0)),
        compiler_params=_cp(("parallel",)),
        name="rmsnorm",
    )(x, w.reshape(1, d))


def _mm_kernel(*refs, nk, n_extra, epilogue):
    a_ref, b_ref = refs[0], refs[1]
    extras = refs[2:2 + n_extra]
    o_ref = refs[2 + n_extra]
    part = jnp.dot(a_ref[...], b_ref[...], preferred_element_type=F32)
    if nk == 1:
        o_ref[...] = epilogue(part, *extras).astype(o_ref.dtype)
        return
    acc_ref = refs[3 + n_extra]
    k = pl.program_id(2)

    @pl.when(k == 0)
    def _():
        acc_ref[...] = part

    @pl.when(k > 0)
    def _():
        acc_ref[...] += part

    @pl.when(k == nk - 1)
    def _():
        o_ref[...] = epilogue(acc_ref[...], *extras).astype(o_ref.dtype)


def matmul(a, b, out_dtype, *, tm, tn, tk=None, extras=(), epilogue=None, name="matmul"):
    m, kd = a.shape
    n = b.shape[1]
    tm, tn = min(tm, m), min(tn, n)
    while m % tm:
        tm //= 2
    tk = kd if tk is None else tk
    nk = kd // tk
    assert m % tm == 0 and n % tn == 0 and kd % tk == 0
    if epilogue is None:
        epilogue = lambda acc: acc
    in_specs = [pl.BlockSpec((tm, tk), lambda i, j, k: (i, k)),
                pl.BlockSpec((tk, tn), lambda i, j, k: (k, j))]
    args = [a, b]
    for arr, blk, imap in extras:
        in_specs.append(pl.BlockSpec(blk, functools.partial(lambda i, j, k, f: f(i, j), f=imap)))
        args.append(arr)
    scratch = [pltpu.VMEM((tm, tn), F32)] if nk > 1 else []
    return pl.pallas_call(
        functools.partial(_mm_kernel, nk=nk, n_extra=len(extras), epilogue=epilogue),
        out_shape=jax.ShapeDtypeStruct((m, n), out_dtype),
        grid=(m // tm, n // tn, nk),
        in_specs=in_specs,
        out_specs=pl.BlockSpec((tm, tn), lambda i, j, k: (i, j)),
        scratch_shapes=scratch,
        compiler_params=_cp(("parallel", "parallel", "arbitrary")),
        name=name,
    )(*args)


def _residual_epilogue(acc, r_ref):
    return acc + r_ref[...]


def matmul_residual(a, b, res, *, tm, tn, tk=None, name="matmul_residual"):
    tm_, tn_ = min(tm, a.shape[0]), min(tn, b.shape[1])
    return matmul(a, b, F32, tm=tm, tn=tn, tk=tk, name=name,
                  extras=[(res, (tm_, tn_), lambda i, j: (i, j))], epilogue=_residual_epilogue)


def _q_rope_epilogue(acc, c_ref, sa_ref, sb_ref):
    c, sa, sb = c_ref[...], sa_ref[...], sb_ref[...]
    scale = (MLA_NOPE + MLA_ROPE) ** -0.5 * math.log2(math.e)
    outs = []
    for h in range(MLA_H):
        xs = acc[:, h * QSLOT:(h + 1) * QSLOT]
        y = xs * c + pltpu.roll(xs, 32, 1) * sa + pltpu.roll(xs, QSLOT - 32, 1) * sb
        outs.append(y * scale)
    return jnp.concatenate(outs, axis=1)


def _conv_kernel(x_ref, st_ref, w_ref, b_ref, y_ref, ns_ref, xp_ref, *, width, ts, act):
    t = pl.program_id(2)
    nt = pl.num_programs(2)

    @pl.when(t == 0)
    def _():
        xp_ref[0:8, :] = st_ref[0]

    @pl.when(t > 0)
    def _():
        xp_ref[0:8, :] = xp_ref[ts:ts + 8, :]

    xp_ref[8:8 + ts, :] = x_ref[0]
    off = 8 - (width - 1)
    y = xp_ref[off:off + ts, :] * w_ref[0:1, :]
    for j in range(1, width):
        y = y + xp_ref[off + j:off + j + ts, :] * w_ref[j:j + 1, :]
    y = y + b_ref[...]
    if act:
        y = _silu(y)
    y_ref[0] = y.astype(y_ref.dtype)

    @pl.when(t == nt - 1)
    def _():
        ns_ref[0] = xp_ref[ts:ts + 8, :]


def causal_conv(x3, col_off, chans, state8, w, b, *, act, ct=512):
    bsz, s, _ = x3.shape
    width = w.shape[0]
    ts = min(512, s)
    assert chans % ct == 0 and col_off % ct == 0 and s % ts == 0
    cb0 = col_off // ct
    return pl.pallas_call(
        functools.partial(_conv_kernel, width=width, ts=ts, act=act),
        out_shape=(jax.ShapeDtypeStruct((bsz, s, chans), F32),
                   jax.ShapeDtypeStruct((bsz, 8, chans), F32)),
        grid=(bsz, chans // ct, s // ts),
        in_specs=[pl.BlockSpec((1, ts, ct), lambda bi, c, t: (bi, t, cb0 + c)),
                  pl.BlockSpec((1, 8, ct), lambda bi, c, t: (bi, 0, c)),
                  pl.BlockSpec((width, ct), lambda bi, c, t: (0, c)),
                  pl.BlockSpec((1, ct), lambda bi, c, t: (0, c))],
        out_specs=(pl.BlockSpec((1, ts, ct), lambda bi, c, t: (bi, t, c)),
                   pl.BlockSpec((1, 8, ct), lambda bi, c, t: (bi, 0, c))),
        scratch_shapes=[pltpu.VMEM((ts + 8, ct), F32)],
        compiler_params=_cp(("parallel", "parallel", "arbitrary")),
        name="causal_conv",
    )(x3, state8, w, b.reshape(1, chans))


HALO = 16


FFN_TN = 256


def _interleave_ffn(x):
    lead = x.shape[:-1]
    x = x.reshape(*lead, 2, D_FF // FFN_TN, FFN_TN)
    return jnp.swapaxes(x, -3, -2).reshape(*lead, 2 * D_FF)


def _deinterleave_ffn(x):
    lead = x.shape[:-1]
    x = x.reshape(*lead, D_FF // FFN_TN, 2, FFN_TN)
    return jnp.swapaxes(x, -3, -2).reshape(*lead, 2 * D_FF)


def _il_cast_kernel(a_ref, b_ref, o_ref):
    o_ref[0, :, :FFN_TN] = a_ref[0].astype(o_ref.dtype)
    o_ref[0, :, FFN_TN:] = b_ref[0].astype(o_ref.dtype)


def interleave_cast_w_up(w_all, l, *, tk=1024):
    kd = w_all.shape[1]
    nc = D_FF // FFN_TN
    return pl.pallas_call(
        _il_cast_kernel,
        out_shape=jax.ShapeDtypeStruct((nc, kd, 2 * FFN_TN), BF16),
        grid=(kd // tk, nc),
        in_specs=[pl.BlockSpec((1, tk, FFN_TN), lambda k, j: (l, k, j)),
                  pl.BlockSpec((1, tk, FFN_TN), lambda k, j: (l, k, nc + j))],
        out_specs=pl.BlockSpec((1, tk, 2 * FFN_TN), lambda k, j: (j, k, 0)),
        compiler_params=_cp(("parallel", "parallel")),
        name="interleave_cast_w_up",
    )(w_all, w_all)


def _mm_tiles_kernel(a_ref, w_ref, o_ref):
    o_ref[...] = jnp.dot(a_ref[...], w_ref[0], preferred_element_type=F32)


def matmul_tile_major(a, w_tiles):
    m, kd = a.shape
    nt, _, tn = w_tiles.shape
    return pl.pallas_call(
        _mm_tiles_kernel,
        out_shape=jax.ShapeDtypeStruct((m, nt * tn), F32),
        grid=(nt,),
        in_specs=[pl.BlockSpec((m, kd), lambda j: (0, 0)),
                  pl.BlockSpec((1, kd, tn), lambda j: (j, 0, 0))],
        out_specs=pl.BlockSpec((m, tn), lambda j: (0, j)),
        compiler_params=_cp(("parallel",)),
        name="ffn_up",
    )(a, w_tiles)


def _ffn_conv_gate_il_kernel(x_ref, st_ref, cw_ref, cb_ref, y_ref, ns_ref, xp_ref, *, width, s):
    xp_ref[0:8, :] = st_ref[0]
    xp_ref[8:8 + s, :] = x_ref[0]
    off = 8 - (width - 1)
    y = xp_ref[off:off + s, :] * cw_ref[0:1, :]
    for t in range(1, width):
        y = y + xp_ref[off + t:off + t + s, :] * cw_ref[t:t + 1, :]
    y = y + cb_ref[...]
    ns_ref[0] = xp_ref[s:s + 8, :]
    for c in range(D_FF // FFN_TN):
        g = y[:, 2 * c * FFN_TN:(2 * c + 1) * FFN_TN]
        v = y[:, (2 * c + 1) * FFN_TN:(2 * c + 2) * FFN_TN]
        y_ref[0, :, c * FFN_TN:(c + 1) * FFN_TN] = (_silu(g) * v).astype(y_ref.dtype)


def ffn_conv_gate_il(up3, state8, cw, cb):
    bsz, s, n2 = up3.shape
    width = cw.shape[0]
    y, ns = pl.pallas_call(
        functools.partial(_ffn_conv_gate_il_kernel, width=width, s=s),
        out_shape=(jax.ShapeDtypeStruct((bsz, s, D_FF), BF16),
                   jax.ShapeDtypeStruct((bsz, 8, n2), F32)),
        grid=(bsz,),
        in_specs=[pl.BlockSpec((1, s, n2), lambda b: (b, 0, 0)),
                  pl.BlockSpec((1, 8, n2), lambda b: (b, 0, 0)),
                  pl.BlockSpec((width, n2), lambda b: (0, 0)),
                  pl.BlockSpec((1, n2), lambda b: (0, 0))],
        out_specs=(pl.BlockSpec((1, s, D_FF), lambda b: (b, 0, 0)),
                   pl.BlockSpec((1, 8, n2), lambda b: (b, 0, 0))),
        scratch_shapes=[pltpu.VMEM((s + 8, n2), F32)],
        compiler_params=_cp(("parallel",)),
        name="ffn_conv_gate_il",
    )(up3, _interleave_ffn(state8), _interleave_ffn(cw), _interleave_ffn(cb.reshape(1, n2)))
    return y, _deinterleave_ffn(ns)


FFN_SUB = 512


def _ffn_up_fused_kernel(a_ref, halo_ref, w_ref, st_ref, cw_ref, cb_ref, y_ref, ns_ref, aext_ref, *xp_refs,
                         tm, tiles_per_seq, width):
    i = pl.program_id(0)
    j = pl.program_id(1)

    @pl.when(j == 0)
    def _():
        aext_ref[0:HALO, :] = halo_ref[...]
        aext_ref[HALO:HALO + tm, :] = a_ref[...]

    nsub = tm // FFN_SUB
    for r in range(nsub):
        xp_refs[r][...] = jnp.dot(aext_ref[r * FFN_SUB:(r + 1) * FFN_SUB + HALO, :], w_ref[0],
                                  preferred_element_type=F32)
    first = (i % tiles_per_seq) == 0
    xp_refs[0][HALO - 8:HALO, :] = jnp.where(first, st_ref[0], xp_refs[0][HALO - 8:HALO, :])
    off = HALO - (width - 1)
    cw = cw_ref[...]
    cb = cb_ref[...]
    for r in range(nsub):
        xp_ref = xp_refs[r]
        y = xp_ref[off:off + FFN_SUB, :] * cw[0:1, :]
        for t in range(1, width):
            y = y + xp_ref[off + t:off + t + FFN_SUB, :] * cw[t:t + 1, :]
        y = y + cb
        y_ref[r * FFN_SUB:(r + 1) * FFN_SUB, :] = (_silu(y[:, :FFN_TN]) * y[:, FFN_TN:]).astype(y_ref.dtype)
    ns_ref[0] = xp_refs[nsub - 1][HALO + FFN_SUB - 8:HALO + FFN_SUB, :]


def ffn_up_fused(h2, w_up_il, state8, cw, cb, *, bsz, tm=1024):
    m, kd = h2.shape
    s = m // bsz
    tm = min(tm, s)
    assert s % tm == 0 and tm % FFN_SUB == 0
    tiles_per_seq = s // tm
    width = cw.shape[0]
    nc = D_FF // FFN_TN
    tn2 = 2 * FFN_TN
    y, ns = pl.pallas_call(
        functools.partial(_ffn_up_fused_kernel, tm=tm, tiles_per_seq=tiles_per_seq, width=width),
        out_shape=(jax.ShapeDtypeStruct((m, D_FF), BF16),
                   jax.ShapeDtypeStruct((m // tm, 8, 2 * D_FF), F32)),
        grid=(m // tm, nc),
        in_specs=[pl.BlockSpec((tm, kd), lambda i, j: (i, 0)),
                  pl.BlockSpec((HALO, kd), lambda i, j: (jnp.maximum(i * (tm // HALO) - 1, 0), 0)),
                  pl.BlockSpec((1, kd, tn2), lambda i, j: (j, 0, 0)),
                  pl.BlockSpec((1, 8, tn2), lambda i, j: (i // tiles_per_seq, 0, j)),
                  pl.BlockSpec((width, tn2), lambda i, j: (0, j)),
                  pl.BlockSpec((1, tn2), lambda i, j: (0, j))],
        out_specs=(pl.BlockSpec((tm, FFN_TN), lambda i, j: (i, j)),
                   pl.BlockSpec((1, 8, tn2), lambda i, j: (i, 0, j))),
        scratch_shapes=[pltpu.VMEM((tm + HALO, kd), BF16)]
        + [pltpu.VMEM((FFN_SUB + HALO, tn2), F32) for _ in range(tm // FFN_SUB)],
        compiler_params=_cp(("arbitrary", "arbitrary")),
        name="ffn_up_fused",
    )(h2, h2, w_up_il, _interleave_ffn(state8), _interleave_ffn(cw), _interleave_ffn(cb.reshape(1, 2 * D_FF)))
    return y, _deinterleave_ffn(ns[tiles_per_seq - 1::tiles_per_seq])


def _chunk_masks(lt, c):
    sh = int(math.log2(c))
    ri = lax.broadcasted_iota(jnp.int32, (lt, lt), 0)
    ci = lax.broadcasted_iota(jnp.int32, (lt, lt), 1)
    same = (ri >> sh) == (ci >> sh)
    return ri, ci, same


def _split3(x):
    hi = x.astype(BF16)
    r = x - hi.astype(F32)
    mid = r.astype(BF16)
    lo = (r - mid.astype(F32)).astype(BF16)
    return hi, mid, lo


def _dot_exact_lhs(l16, x):
    hi, mid, lo = _split3(x)
    return (jnp.dot(l16, hi, preferred_element_type=F32) + jnp.dot(l16, mid, preferred_element_type=F32)
            + jnp.dot(l16, lo, preferred_element_type=F32))


def _dot_hi(a, b):
    a_hi = a.astype(BF16)
    a_lo = (a - a_hi.astype(F32)).astype(BF16)
    b_hi = b.astype(BF16)
    b_lo = (b - b_hi.astype(F32)).astype(BF16)
    return (jnp.dot(a_hi, b_hi, preferred_element_type=F32) + jnp.dot(a_hi, b_lo, preferred_element_type=F32)
            + jnp.dot(a_lo, b_hi, preferred_element_type=F32))


def _gdn_kernel(q_ref, k_ref, v_ref, sm_ref, z_ref, s0_ref, alog_ref, dtb_ref, nw_ref,
                o_ref, sout_ref, s_sc, vn_sc, *, lt, c, hps):
    hg = pl.program_id(1)
    t = pl.program_id(2)
    nt = pl.num_programs(2)

    @pl.when(t == 0)
    def _():
        s_sc[...] = s0_ref[0]

    sm = sm_ref[0]
    lane = lax.broadcasted_iota(jnp.int32, (1, 128), 1)
    beta_full = jax.nn.sigmoid(sm)
    g_full = -jnp.exp(alog_ref[...]) * _softplus(sm + dtb_ref[...])
    ri, ci, same = _chunk_masks(lt, c)
    tri = same & (ci <= ri)
    strict = same & (ci < ri)
    eye_b = ri == ci
    eye = jnp.where(eye_b, 1.0, 0.0)
    gc_full = _dot_exact_lhs(jnp.where(tri, 1.0, 0.0).astype(BF16), g_full)
    nw = nw_ref[...]

    hs = range(hps)
    cols = [slice(hh * GDN_D, (hh + 1) * GDN_D) for hh in hs]
    dot16 = lambda x, y: jnp.dot(x.astype(BF16), y.astype(BF16), preferred_element_type=F32)
    ks, k16s, kbs, gcs, ms, avs, qds, rhss = [], [], [], [], [], [], [], []
    for hh in hs:
        h = hg * hps + hh
        q = q_ref[0, :, cols[hh]]
        k = k_ref[0, :, cols[hh]]
        q = q * lax.rsqrt(jnp.sum(q * q, axis=-1, keepdims=True) + EPS) * (GDN_D ** -0.5)
        k = k * lax.rsqrt(jnp.sum(k * k, axis=-1, keepdims=True) + EPS)
        beta = jnp.sum(jnp.where(lane == L_BETA + h, beta_full, 0.0), axis=-1, keepdims=True)
        gc = jnp.sum(jnp.where(lane == L_ALPHA + h, gc_full, 0.0), axis=-1, keepdims=True)
        gr = jnp.sum(jnp.where(eye_b, gc, 0.0), axis=0, keepdims=True)
        decay = jnp.exp(jnp.where(tri, gc - gr, 0.0))
        kb = k * beta
        k16 = k.astype(BF16)
        ms.append(jnp.where(strict, _nt(kb.astype(BF16), k16) * decay, 0.0))
        avs.append(jnp.where(tri, _nt(q.astype(BF16), k16) * decay, 0.0).astype(BF16))
        eg = jnp.exp(gc)
        rhss.append(jnp.concatenate([v_ref[0, :, cols[hh]] * beta, kb * eg], axis=1).astype(BF16))
        qds.append((q * eg).astype(BF16))
        ks.append(k)
        gcs.append(gc)

    ps = [-m for m in ms]
    ts = [eye + p for p in ps]
    for _ in range(int(math.log2(c)) - 1):
        ps = [dot16(p, p) for p in ps]
        ts = [tt + dot16(p, tt) for p, tt in zip(ps, ts)]
    rs = [eye - _dot_hi(eye + m, tt) for m, tt in zip(ms, ts)]
    ts = [tt + dot16(tt, r) for tt, r in zip(ts, rs)]
    uws = [jnp.dot(tt.astype(BF16), rhs, preferred_element_type=F32) for tt, rhs in zip(ts, rhss)]
    us = [uw[:, :GDN_D] for uw in uws]
    ws = [uw[:, GDN_D:].astype(BF16) for uw in uws]

    ss = [s_sc[hh] for hh in hs]
    for hh in hs:
        vn_sc[hh] = us[hh].astype(BF16)
    for cc in range(lt // c):
        r0 = cc * c
        rows = slice(r0, r0 + c)
        for hh in hs:
            s16 = ss[hh].astype(BF16)
            vnew = us[hh][rows] - jnp.dot(ws[hh][rows], s16, preferred_element_type=F32)
            vn16 = vnew.astype(BF16)
            vn_sc[hh, rows, :] = vn16
            o = (jnp.dot(qds[hh][rows], s16, preferred_element_type=F32)
                 + jnp.dot(avs[hh][rows, :], vn_sc[hh], preferred_element_type=F32))
            glast = gcs[hh][r0 + c - 1:r0 + c, :]
            kd = (ks[hh][rows] * jnp.exp(glast - gcs[hh][rows])).astype(BF16)
            ss[hh] = ss[hh] * jnp.exp(glast) + _tn(kd, vn16)
            mean_sq = jnp.mean(o * o, axis=-1, keepdims=True)
            o = o * lax.rsqrt(mean_sq + EPS) * nw * _silu(z_ref[0, rows, cols[hh]])
            o_ref[0, rows, cols[hh]] = o.astype(o_ref.dtype)
    for hh in hs:
        s_sc[hh] = ss[hh]

    @pl.when(t == nt - 1)
    def _():
        sout_ref[0] = s_sc[...]


def gdn(qkv3, p3, s0, alog_row, dtb_row, norm_w, *, hps=4):
    bsz, s, _ = qkv3.shape
    c = min(CHUNK, s)
    lt = min(256, s)
    hw = hps * GDN_D
    ng = GDN_H // hps
    small_blk = P_SMALL // 128
    za_blk = P_ZA // hw
    blk = lambda off: pl.BlockSpec((1, lt, hw), lambda b, h, t: (b, t, off + h))
    return pl.pallas_call(
        functools.partial(_gdn_kernel, lt=lt, c=c, hps=hps),
        out_shape=(jax.ShapeDtypeStruct((bsz, s, BRANCH), BF16),
                   jax.ShapeDtypeStruct((bsz, GDN_H, GDN_D, GDN_D), F32)),
        grid=(bsz, ng, s // lt),
        in_specs=[blk(0), blk(ng), blk(2 * ng),
                  pl.BlockSpec((1, lt, 128), lambda b, h, t: (b, t, small_blk)),
                  blk(za_blk),
                  pl.BlockSpec((1, hps, GDN_D, GDN_D), lambda b, h, t: (b, h, 0, 0)),
                  pl.BlockSpec((1, 128), lambda b, h, t: (0, 0)),
                  pl.BlockSpec((1, 128), lambda b, h, t: (0, 0)),
                  pl.BlockSpec((1, 128), lambda b, h, t: (0, 0))],
        out_specs=(pl.BlockSpec((1, lt, hw), lambda b, h, t: (b, t, h)),
                   pl.BlockSpec((1, hps, GDN_D, GDN_D), lambda b, h, t: (b, h, 0, 0))),
        scratch_shapes=[pltpu.VMEM((hps, GDN_D, GDN_D), F32), pltpu.VMEM((hps, lt, GDN_D), BF16)],
        compiler_params=_cp(("parallel", "parallel", "arbitrary")),
        name="gdn",
    )(qkv3, qkv3, qkv3, p3, p3, s0, alog_row, dtb_row, norm_w.reshape(1, GDN_D))


def _ssd_kernel(x_ref, b_ref, c_ref, sm_ref, z_ref, h0_ref, alog_ref, dtb_ref, d_ref, nw_ref,
                y_ref, hout_ref, hs_sc, y_sc, *, lt, c):
    g = pl.program_id(1)
    t = pl.program_id(2)
    nt = pl.num_programs(2)
    npair = SSM_H // SSM_G // 2

    @pl.when(t == 0)
    def _():
        hs_sc[...] = h0_ref[0]

    sm = sm_ref[0]
    dt_full = _softplus(sm + dtb_ref[...])
    da = dt_full * (-jnp.exp(alog_ref[...]))
    ri, ci, same = _chunk_masks(lt, c)
    tri = same & (ci <= ri)
    eye_b = ri == ci
    acum_full = _dot_exact_lhs(jnp.where(tri, 1.0, 0.0).astype(BF16), da)
    nch = lt // c
    lasts = [acum_full[cc * c + c - 1:cc * c + c, :] for cc in range(nch)]
    alast_full = jnp.concatenate([jnp.broadcast_to(l, (c, 128)) for l in lasts], axis=0)
    ea_full = jnp.exp(acum_full)
    dtw_full = dt_full * jnp.exp(alast_full - acum_full)

    bm = b_ref[0]
    cm = c_ref[0]
    bm16 = bm.astype(BF16)
    cm16 = cm.astype(BF16)
    cb = jnp.where(tri, _nt(cm16, bm16), 0.0)
    lane = lax.broadcasted_iota(jnp.int32, (1, 128), 1)
    lo = lane < SSM_P
    rowlo = lax.broadcasted_iota(jnp.int32, (128, 1), 0) < SSM_P
    x = x_ref[0]
    z = z_ref[0]
    d_row = d_ref[...]

    def ext(full, l):
        return jnp.sum(jnp.where(lane == l, full, 0.0), axis=-1, keepdims=True)

    for j in range(npair):
        la = L_DT + g * (SSM_H // SSM_G) + 2 * j
        lb = la + 1
        ac_a, ac_b = ext(acum_full, la), ext(acum_full, lb)
        row_a = jnp.sum(jnp.where(eye_b, ac_a, 0.0), axis=0, keepdims=True)
        row_b = jnp.sum(jnp.where(eye_b, ac_b, 0.0), axis=0, keepdims=True)
        l_a = (jnp.exp(jnp.where(tri, ac_a - row_a, 0.0)) * cb).astype(BF16)
        l_b = (jnp.exp(jnp.where(tri, ac_b - row_b, 0.0)) * cb).astype(BF16)
        dt_pair = jnp.where(lo, ext(dt_full, la), ext(dt_full, lb))
        ea_pair = jnp.where(lo, ext(ea_full, la), ext(ea_full, lb))
        dtw_pair = jnp.where(lo, ext(dtw_full, la), ext(dtw_full, lb))
        xp = x[:, j * 128:(j + 1) * 128]
        xdt = xp * dt_pair
        y = (jnp.dot(l_a, jnp.where(lo, xdt, 0.0).astype(BF16), preferred_element_type=F32)
             + jnp.dot(l_b, jnp.where(lo, 0.0, xdt).astype(BF16), preferred_element_type=F32))
        xw = (xp * dtw_pair).astype(BF16)
        hs = hs_sc[j]
        yoffs = []
        for cc in range(nch):
            r0 = cc * c
            yoffs.append(_nt(cm16[r0:r0 + c], hs.astype(BF16)) * ea_pair[r0:r0 + c])
            cs = _tn(xw[r0:r0 + c], bm16[r0:r0 + c])
            al = jnp.where(rowlo, jnp.exp(ac_a[r0 + c - 1:r0 + c, :]), jnp.exp(ac_b[r0 + c - 1:r0 + c, :]))
            hs = hs * al + cs
        hs_sc[j] = hs
        yoff = yoffs[0] if nch == 1 else jnp.concatenate(yoffs, axis=0)
        ytot = (y + yoff + d_row[:, j * 128:(j + 1) * 128] * xp) * _silu(z[:, j * 128:(j + 1) * 128])
        y_sc[:, j * 128:(j + 1) * 128] = ytot

    yy = y_sc[...]
    ms = jnp.mean(yy * yy, axis=-1, keepdims=True)
    y_ref[0] = (yy * lax.rsqrt(ms + EPS) * nw_ref[...]).astype(y_ref.dtype)

    @pl.when(t == nt - 1)
    def _():
        hout_ref[0] = hs_sc[...]


def ssd(xbc3, p3, h0, alog_row, dtb_row, d_row, norm_w):
    bsz, s, _ = xbc3.shape
    c = min(CHUNK, s)
    lt = min(256, s)
    gw = BRANCH // SSM_G
    npair = SSM_H // SSM_G // 2
    small_blk = P_SMALL // 128
    zb_blk = P_ZB // gw
    return pl.pallas_call(
        functools.partial(_ssd_kernel, lt=lt, c=c),
        out_shape=(jax.ShapeDtypeStruct((bsz, s, BRANCH), BF16),
                   jax.ShapeDtypeStruct((bsz, SSM_H // 2, 128, 128), F32)),
        grid=(bsz, SSM_G, s // lt),
        in_specs=[pl.BlockSpec((1, lt, gw), lambda b, g, t: (b, t, g)),
                  pl.BlockSpec((1, lt, 128), lambda b, g, t: (b, t, BRANCH // 128 + g)),
                  pl.BlockSpec((1, lt, 128), lambda b, g, t: (b, t, BRANCH // 128 + SSM_G + g)),
                  pl.BlockSpec((1, lt, 128), lambda b, g, t: (b, t, small_blk)),
                  pl.BlockSpec((1, lt, gw), lambda b, g, t: (b, t, zb_blk + g)),
                  pl.BlockSpec((1, npair, 128, 128), lambda b, g, t: (b, g, 0, 0)),
                  pl.BlockSpec((1, 128), lambda b, g, t: (0, 0)),
                  pl.BlockSpec((1, 128), lambda b, g, t: (0, 0)),
                  pl.BlockSpec((1, gw), lambda b, g, t: (0, g)),
                  pl.BlockSpec((1, gw), lambda b, g, t: (0, g))],
        out_specs=(pl.BlockSpec((1, lt, gw), lambda b, g, t: (b, t, g)),
                   pl.BlockSpec((1, npair, 128, 128), lambda b, g, t: (b, g, 0, 0))),
        scratch_shapes=[pltpu.VMEM((npair, 128, 128), F32), pltpu.VMEM((lt, gw), F32)],
        compiler_params=_cp(("parallel", "parallel", "arbitrary")),
        name="ssd",
    )(xbc3, xbc3, xbc3, p3, p3, h0, alog_row, dtb_row, d_row, norm_w.reshape(1, BRANCH))


def _mla_prep_kernel(p_ref, qn_ref, kvn_ref, ck_ref, sak_ref, sbk_ref,
                     cq_ref, lat_ref, kr_ref, latkr_ref):
    blk = p_ref[...]
    cq = blk[:, 0:MLA_Q_RANK]
    cq = cq * lax.rsqrt(jnp.mean(cq * cq, axis=-1, keepdims=True) + EPS) * qn_ref[...]
    cq_ref[...] = cq.astype(cq_ref.dtype)
    ckv = blk[:, MLA_Q_RANK:MLA_Q_RANK + MLA_KV_RANK]
    lat = ckv * lax.rsqrt(jnp.mean(ckv * ckv, axis=-1, keepdims=True) + EPS) * kvn_ref[...]
    lat_ref[...] = lat
    sm = blk[:, 1280:1408]
    y = sm * ck_ref[...] + pltpu.roll(sm, 32, 1) * sak_ref[...] + pltpu.roll(sm, 96, 1) * sbk_ref[...]
    kr_ref[...] = y[:, 0:MLA_ROPE]
    latkr_ref[:, 0:MLA_KV_RANK] = lat.astype(latkr_ref.dtype)
    latkr_ref[:, MLA_KV_RANK:LATKR] = y.astype(latkr_ref.dtype)


def mla_prep(p, q_norm, kv_norm, ck, sak, sbk):
    m = p.shape[0]
    tm = min(256, m)
    mla_blk = P_MLA // 1536
    row = lambda n: pl.BlockSpec((tm, n), lambda i: (i, 0))
    return pl.pallas_call(
        _mla_prep_kernel,
        out_shape=(jax.ShapeDtypeStruct((m, MLA_Q_RANK), BF16),
                   jax.ShapeDtypeStruct((m, MLA_KV_RANK), F32),
                   jax.ShapeDtypeStruct((m, MLA_ROPE), F32),
                   jax.ShapeDtypeStruct((m, LATKR), BF16)),
        grid=(m // tm,),
        in_specs=[pl.BlockSpec((tm, 1536), lambda i: (i, mla_blk)),
                  pl.BlockSpec((1, MLA_Q_RANK), lambda i: (0, 0)),
                  pl.BlockSpec((1, MLA_KV_RANK), lambda i: (0, 0)),
                  row(128), row(128), row(128)],
        out_specs=(row(MLA_Q_RANK), row(MLA_KV_RANK), row(MLA_ROPE), row(LATKR)),
        compiler_params=_cp(("parallel",)),
        name="mla_prep",
    )(p, q_norm.reshape(1, -1), kv_norm.reshape(1, -1), ck, sak, sbk)


def _flash_kernel(q_ref, k_ref, v_ref, o_ref, m_sc, l_sc, acc_sc, *, tq, tk, q_off, nk_real):
    i = pl.program_id(2)
    j = pl.program_id(3)
    nj = pl.num_programs(3)

    @pl.when(j == 0)
    def _():
        m_sc[...] = jnp.full(m_sc.shape, -1e30, F32)
        l_sc[...] = jnp.zeros(l_sc.shape, F32)
        acc_sc[...] = jnp.zeros(acc_sc.shape, F32)

    q_last = q_off + i * tq + tq - 1

    @pl.when(((j * tk) >> 6) <= (q_last >> 6))
    def _():
        s = _nt(q_ref[0], k_ref[0])
        qpos = q_off + i * tq + lax.broadcasted_iota(jnp.int32, (tq, 1), 0)
        kidx = j * tk + lax.broadcasted_iota(jnp.int32, (1, tk), 1)
        vis = ((kidx >> 6) <= (qpos >> 6)) & (kidx < nk_real)
        s = jnp.where(vis, s, -1e30)
        m_prev = m_sc[...]
        m_new = jnp.maximum(m_prev, jnp.max(s, axis=-1, keepdims=True))
        alpha = jnp.exp2(m_prev - m_new)
        p = jnp.where(vis, jnp.exp2(s - m_new), 0.0)
        l_sc[...] = alpha * l_sc[...] + jnp.sum(p, axis=-1, keepdims=True)
        acc_sc[...] = alpha * acc_sc[...] + jnp.dot(p.astype(BF16), v_ref[0], preferred_element_type=F32)
        m_sc[...] = m_new

    @pl.when(j == nj - 1)
    def _():
        o_ref[0] = (acc_sc[...] / l_sc[...]).astype(o_ref.dtype)


def flash_attention(q3, kv3, *, q_off, nk_real):
    bsz, sq, _ = q3.shape
    tk_tot = kv3.shape[1]
    tq = min(512, sq)
    tk = tk_tot if tk_tot <= 4096 else 512
    assert sq % tq == 0 and tk_tot % tk == 0
    v_blk0 = MLA_H * QSLOT // MLA_V

    def last_blk(i):
        return ((q_off + i * tq + tq - 1) >> 6 << 6) // tk

    return pl.pallas_call(
        functools.partial(_flash_kernel, tq=tq, tk=tk, q_off=q_off, nk_real=nk_real),
        out_shape=jax.ShapeDtypeStruct((bsz, sq, BRANCH), BF16),
        grid=(bsz, MLA_H, sq // tq, tk_tot // tk),
        in_specs=[pl.BlockSpec((1, tq, QSLOT), lambda b, h, i, j: (b, i, h)),
                  pl.BlockSpec((1, tk, QSLOT), lambda b, h, i, j: (b, jnp.minimum(j, last_blk(i)), h)),
                  pl.BlockSpec((1, tk, MLA_V), lambda b, h, i, j: (b, jnp.minimum(j, last_blk(i)), v_blk0 + h))],
        out_specs=pl.BlockSpec((1, tq, MLA_V), lambda b, h, i, j: (b, i, h)),
        scratch_shapes=[pltpu.VMEM((tq, 1), F32), pltpu.VMEM((tq, 1), F32), pltpu.VMEM((tq, MLA_V), F32)],
        compiler_params=_cp(("parallel", "parallel", "parallel", "arbitrary")),
        name="flash_attention",
    )(q3, kv3, kv3)


FLASH_STRIP = 256


def _flash_t_kernel(qi_ref, kj_ref, q_ref, k_ref, vt_ref, o_ref, m_sc, l_sc, acc_sc, *, tq, tk):
    pidx = pl.program_id(2)
    i = qi_ref[pidx]
    j = kj_ref[pidx]

    @pl.when(j == 0)
    def _():
        m_sc[...] = jnp.full(m_sc.shape, -1e30, F32)
        l_sc[...] = jnp.zeros(l_sc.shape, F32)
        acc_sc[...] = jnp.zeros(acc_sc.shape, F32)

    nstrip = tq // FLASH_STRIP

    def tile(masked):
        k = k_ref[0]
        vt = vt_ref[0]
        kchunk = (j * tk + lax.broadcasted_iota(jnp.int32, (tk, 1), 0)) >> 6 if masked else None
        s_v, p_v = {}, {}

        def scores(c):
            rows = slice(c * FLASH_STRIP, (c + 1) * FLASH_STRIP)
            s = _nt(k, q_ref[0, rows, :])
            vis = None
            if masked:
                qchunk = (i * tq + c * FLASH_STRIP + lax.broadcasted_iota(jnp.int32, (1, FLASH_STRIP), 1)) >> 6
                vis = kchunk <= qchunk
                s = jnp.where(vis, s, -1e30)
            s_v[c] = (s, vis)

        def softmax(c):
            lanes = slice(c * FLASH_STRIP, (c + 1) * FLASH_STRIP)
            s, vis = s_v.pop(c)
            m_prev = m_sc[:, lanes]
            m_new = jnp.maximum(m_prev, jnp.max(s, axis=0, keepdims=True))
            alpha = jnp.exp2(m_prev - m_new)
            p = jnp.exp2(s - m_new)
            if masked:
                p = jnp.where(vis, p, 0.0)
            l_sc[:, lanes] = alpha * l_sc[:, lanes] + jnp.sum(p, axis=0, keepdims=True)
            m_sc[:, lanes] = m_new
            p_v[c] = (p.astype(BF16), alpha)

        def values(c):
            lanes = slice(c * FLASH_STRIP, (c + 1) * FLASH_STRIP)
            p16, alpha = p_v.pop(c)
            acc_sc[:, lanes] = alpha * acc_sc[:, lanes] + jnp.dot(vt, p16, preferred_element_type=F32)

        for step in range(nstrip + 2):
            if step < nstrip:
                scores(step)
            if 0 <= step - 1 < nstrip:
                softmax(step - 1)
            if 0 <= step - 2 < nstrip:
                values(step - 2)

    full = ((j * tk + tk - 1) >> 6) <= ((i * tq) >> 6)

    @pl.when(full)
    def _():
        tile(False)

    @pl.when(jnp.logical_not(full))
    def _():
        tile(True)

    @pl.when(j == (i * tq + tq - 1) // tk)
    def _():
        o_ref[0] = (acc_sc[...] / l_sc[...]).T.astype(o_ref.dtype)


def flash_attention_causal(q3, k3, vt3, *, tq=1024, tk=1024):
    bsz, s, _ = q3.shape
    tq, tk = min(tq, s), min(tk, s)
    assert s % tq == 0 and s % tk == 0 and tq % FLASH_STRIP == 0 and tk % CHUNK == 0
    pairs = [(i, j) for i in range(s // tq) for j in range((i * tq + tq - 1) // tk + 1)]
    qi = jnp.asarray([p[0] for p in pairs], jnp.int32)
    kj = jnp.asarray([p[1] for p in pairs], jnp.int32)
    grid_spec = pltpu.PrefetchScalarGridSpec(
        num_scalar_prefetch=2,
        grid=(bsz, MLA_H, len(pairs)),
        in_specs=[pl.BlockSpec((1, tq, QSLOT), lambda b, h, p, qi, kj: (b, qi[p], h)),
                  pl.BlockSpec((1, tk, QSLOT), lambda b, h, p, qi, kj: (b, kj[p], h)),
                  pl.BlockSpec((1, MLA_V, tk), lambda b, h, p, qi, kj: (b, h, kj[p]))],
        out_specs=pl.BlockSpec((1, tq, MLA_V), lambda b, h, p, qi, kj: (b, qi[p], h)),
        scratch_shapes=[pltpu.VMEM((1, tq), F32), pltpu.VMEM((1, tq), F32), pltpu.VMEM((MLA_V, tq), F32)])
    return pl.pallas_call(
        functools.partial(_flash_t_kernel, tq=tq, tk=tk),
        out_shape=jax.ShapeDtypeStruct((bsz, s, BRANCH), BF16),
        grid_spec=grid_spec,
        compiler_params=_cp(("parallel", "parallel", "arbitrary")),
        name="flash_attention_causal",
    )(qi, kj, q3, k3, vt3)


def _vt_kernel(w_ref, x_ref, o_ref):
    o_ref[0] = _nt(w_ref[...], x_ref[0]).astype(o_ref.dtype)


def v_up_transposed(w_vt, latkr3, *, tn=512):
    bsz, s, _ = latkr3.shape
    tn = min(tn, s)
    nv = w_vt.shape[0]
    return pl.pallas_call(
        _vt_kernel,
        out_shape=jax.ShapeDtypeStruct((bsz, nv, s), BF16),
        grid=(bsz, s // tn),
        in_specs=[pl.BlockSpec((nv, LATKR), lambda b, t: (0, 0)),
                  pl.BlockSpec((1, tn, LATKR), lambda b, t: (b, t, 0))],
        out_specs=pl.BlockSpec((1, nv, tn), lambda b, t: (b, 0, t)),
        compiler_params=_cp(("parallel", "parallel")),
        name="v_up_transposed",
    )(w_vt, latkr3)


def _branch_kernel(oa_ref, ob_ref, oc_ref, w_ref, g0_ref, g1_ref, g2_ref, out_ref):
    gates = [jax.nn.sigmoid(g_ref[...]) for g_ref in (g0_ref, g1_ref, g2_ref)]
    us = [jnp.dot(o_ref[...], w_ref[n], preferred_element_type=F32)
          for n, o_ref in enumerate((oa_ref, ob_ref, oc_ref))]
    out_ref[...] = (gates[0] * us[0] + gates[1] * us[1] + gates[2] * us[2]).astype(out_ref.dtype)


def branch_merge(oa, ob, oc, w_branch, p):
    m = oa.shape[0]
    tm = min(512, m)
    tn = 512
    nj = D_MODEL // tn
    o_spec = pl.BlockSpec((tm, BRANCH), lambda i, j: (i, 0))
    gate = lambda n: pl.BlockSpec((tm, tn), lambda i, j: (i, P_GATE // tn + n * nj + j))
    return pl.pallas_call(
        _branch_kernel,
        out_shape=jax.ShapeDtypeStruct((m, D_MODEL), BF16),
        grid=(m // tm, nj),
        in_specs=[o_spec, o_spec, o_spec,
                  pl.BlockSpec((3, BRANCH, tn), lambda i, j: (0, 0, j)),
                  gate(0), gate(1), gate(2)],
        out_specs=pl.BlockSpec((tm, tn), lambda i, j: (i, j)),
        compiler_params=_cp(("parallel", "parallel")),
        name="branch_merge",
    )(oa, ob, oc, w_branch, p, p, p)


def _pack_layer(l, w):
    w_in = w['w_in'][l]
    seg = lambda a, n: w_in[:, a:a + n]
    zeros = lambda n: jnp.zeros((D_MODEL, n), w_in.dtype)
    w_in_p = jnp.concatenate([
        seg(8032, 12288),
        seg(6688, 768), seg(7456, 512),
        seg(7968, 64), seg(4096, 8), seg(4104, 8), seg(6672, 16), zeros(32), zeros(128),
        seg(5136, 1536), seg(0, 3072), seg(3072, 1024), seg(4112, 1024)], axis=1).astype(BF16)
    w_uq = jnp.pad(w['mla_w_uq'][l], ((0, 0), (0, 0), (0, QSLOT - MLA_NOPE - MLA_ROPE)))
    w_uq = w_uq.reshape(MLA_Q_RANK, MLA_H * QSLOT).astype(BF16)
    wk = jnp.pad(w['mla_w_uk'][l], ((0, 0), (0, 0), (0, QSLOT - MLA_NOPE))).reshape(MLA_KV_RANK, MLA_H * QSLOT)
    eye = jnp.broadcast_to(jnp.eye(MLA_ROPE, dtype=F32)[:, None, :], (MLA_ROPE, MLA_H, MLA_ROPE))
    wkr = jnp.pad(eye, ((0, 0), (0, 0), (MLA_NOPE, QSLOT - MLA_NOPE - MLA_ROPE))).reshape(MLA_ROPE, MLA_H * QSLOT)
    wk = jnp.concatenate([wk, wkr, jnp.zeros((LATKR - MLA_KV_RANK - MLA_ROPE, MLA_H * QSLOT), F32)], axis=0)
    wv = jnp.pad(w['mla_w_uv'][l].reshape(MLA_KV_RANK, MLA_H * MLA_V), ((0, LATKR - MLA_KV_RANK), (0, 0)))
    w_kv = jnp.concatenate([wk, wv], axis=1).astype(BF16)
    w_k = wk.astype(BF16)
    w_vt = wv.T.astype(BF16)

    def lane_row(vals, off):
        return jnp.zeros((1, 128), F32).at[0, off:off + vals.shape[0]].set(vals.astype(F32))

    return dict(
        attn_norm=w['attn_norm'][l], w_in=w_in_p,
        gdn_conv_w=w['gdn_conv_w'][l], gdn_alog=lane_row(w['gdn_A_log'][l], L_ALPHA),
        gdn_dtb=lane_row(w['gdn_dt_bias'][l], L_ALPHA), gdn_norm=w['gdn_norm'][l],
        ssm_conv_w=w['ssm_conv_w'][l], ssm_conv_b=w['ssm_conv_b'][l],
        ssm_alog=lane_row(w['ssm_A_log'][l], L_DT), ssm_dtb=lane_row(w['ssm_dt_bias'][l], L_DT),
        ssm_d=jnp.repeat(w['ssm_D'][l].astype(F32), SSM_P).reshape(1, BRANCH), ssm_norm=w['ssm_norm'][l],
        mla_q_norm=w['mla_q_norm'][l], w_uq=w_uq, mla_kv_norm=w['mla_kv_norm'][l], w_kv=w_kv, w_k=w_k, w_vt=w_vt,
        w_branch=w['w_branch'][l].astype(BF16), w_out=w['w_out'][l].astype(BF16),
        ffn_norm=w['ffn_norm'][l], w_up_il=interleave_cast_w_up(w['ffn_w_up'], l),
        ffn_conv_w=w['ffn_conv_w'][l], ffn_conv_b=w['ffn_conv_b'][l],
        w_down=w['ffn_w_down'][l].astype(BF16))


def _rope_tables(pos):
    half = MLA_ROPE // 2
    inv = ROPE_THETA ** (-jnp.arange(half, dtype=F32) / half)
    ang = pos.astype(F32)[:, None] * inv[None, :]
    cos, sin = jnp.cos(ang), jnp.sin(ang)
    n = pos.shape[0]
    z = lambda k: jnp.zeros((n, k), F32)
    cq = jnp.concatenate([jnp.ones((n, MLA_NOPE), F32), cos, cos, z(64)], axis=1)
    saq = jnp.concatenate([z(MLA_NOPE + half), sin, z(64)], axis=1)
    sbq = jnp.concatenate([z(MLA_NOPE), -sin, z(half + 64)], axis=1)
    ck = jnp.concatenate([cos, cos, z(64)], axis=1)
    sak = jnp.concatenate([z(half), sin, z(64)], axis=1)
    sbk = jnp.concatenate([-sin, z(half + 64)], axis=1)
    return cq, saq, sbq, ck, sak, sbk


def _pad_state(st):
    return jnp.pad(st, ((0, 0), (8 - st.shape[1], 0), (0, 0)))


def _layer(x, tabs, q_off, state, lw):
    latkr_past, gconv, gssm, mconv, mssm, fconv = state
    bsz, s, _ = x.shape
    m = bsz * s
    cq_t, saq_t, sbq_t, ck_t, sak_t, sbk_t = tabs
    x2 = x.reshape(m, D_MODEL)
    mm_tm = 512

    h = rmsnorm(x2, lw['attn_norm'], BF16)
    p = matmul(h, lw['w_in'], F32, tm=mm_tm, tn=1024, name="in_proj")
    p3 = p.reshape(bsz, s, P_TOT)

    qkv_act, gconv8 = causal_conv(p3, P_QKV, GDN_QKV, _pad_state(gconv), lw['gdn_conv_w'],
                                  jnp.zeros((GDN_QKV,), F32), act=True)
    o_a, gssm_new = gdn(qkv_act, p3, gssm, lw['gdn_alog'], lw['gdn_dtb'], lw['gdn_norm'])

    xbc_act, mconv8 = causal_conv(p3, P_XBC, SSM_CONV_DIM, _pad_state(mconv), lw['ssm_conv_w'],
                                  lw['ssm_conv_b'], act=True)
    o_b, mssm_new = ssd(xbc_act, p3, mssm.reshape(bsz, SSM_H // 2, 128, 128), lw['ssm_alog'], lw['ssm_dtb'],
                        lw['ssm_d'], lw['ssm_norm'])
    mssm_new = mssm_new.reshape(bsz, SSM_H, SSM_P, SSM_N)

    cqn, lat, kr, latkr = mla_prep(p, lw['mla_q_norm'], lw['mla_kv_norm'], ck_t, sak_t, sbk_t)
    tq_tab = min(mm_tm, m)
    q_full = matmul(cqn, lw['w_uq'], BF16, tm=mm_tm, tn=MLA_H * QSLOT, name="q_up",
                    extras=[(cq_t, (tq_tab, QSLOT), lambda i, j: (i, 0)),
                            (saq_t, (tq_tab, QSLOT), lambda i, j: (i, 0)),
                            (sbq_t, (tq_tab, QSLOT), lambda i, j: (i, 0))],
                    epilogue=_q_rope_epilogue)
    latkr3 = latkr.reshape(bsz, s, LATKR)
    q3 = q_full.reshape(bsz, s, MLA_H * QSLOT)
    if latkr_past is None and q_off == 0 and s % 512 == 0:
        k_full = matmul(latkr, lw['w_k'], BF16, tm=mm_tm, tn=1024, name="k_up")
        vt = v_up_transposed(lw['w_vt'], latkr3)
        o_c = flash_attention_causal(q3, k_full.reshape(bsz, s, MLA_H * QSLOT), vt)
    else:
        past = jnp.zeros((bsz, 0, LATKR), BF16) if latkr_past is None else latkr_past
        nk_real = past.shape[1] + s
        nk_pad = -(-nk_real // 128) * 128
        latkr_all = jnp.concatenate([past, latkr3, jnp.zeros((bsz, nk_pad - nk_real, LATKR), BF16)], axis=1)
        kv = matmul(latkr_all.reshape(bsz * nk_pad, LATKR), lw['w_kv'], BF16, tm=2048, tn=1024, name="kv_up")
        o_c = flash_attention(q3, kv.reshape(bsz, nk_pad, -1), q_off=q_off, nk_real=nk_real)

    merged = branch_merge(o_a.reshape(m, BRANCH), o_b.reshape(m, BRANCH), o_c.reshape(m, BRANCH),
                          lw['w_branch'], p)
    x2 = matmul_residual(merged, lw['w_out'], x2, tm=mm_tm, tn=1024, name="out_proj")

    h2 = rmsnorm(x2, lw['ffn_norm'], BF16)
    if s % 512 == 0:
        act, fconv8 = ffn_up_fused(h2, lw['w_up_il'], _pad_state(fconv), lw['ffn_conv_w'], lw['ffn_conv_b'],
                                   bsz=bsz, tm=1024 if s % 1024 == 0 else 512)
    else:
        up = matmul_tile_major(h2, lw['w_up_il'])
        act, fconv8 = ffn_conv_gate_il(up.reshape(bsz, s, 2 * D_FF), _pad_state(fconv), lw['ffn_conv_w'],
                                       lw['ffn_conv_b'])
    x2 = matmul_residual(act.reshape(m, D_FF), lw['w_down'], x2, tm=mm_tm, tn=1024, tk=D_FF // 2,
                         name="ffn_down")

    new = (lat.reshape(bsz, s, MLA_KV_RANK), kr.reshape(bsz, s, MLA_ROPE),
           gconv8[:, 8 - gconv.shape[1]:], gssm_new, mconv8[:, 8 - mconv.shape[1]:], mssm_new,
           fconv8[:, 8 - fconv.shape[1]:])
    return x2.reshape(bsz, s, D_MODEL), new


def kernel(x_prompt, x_sample, cache_mla_latent, cache_mla_krope, state_gdn_conv, state_gdn_ssm, state_ssm_conv, state_ssm, state_ffn_conv, attn_norm, w_in, gdn_conv_w, gdn_A_log, gdn_dt_bias, gdn_norm, ssm_conv_w, ssm_conv_b, ssm_A_log, ssm_dt_bias, ssm_D, ssm_norm, mla_q_norm, mla_w_uq, mla_kv_norm, mla_w_uk, mla_w_uv, w_branch, w_out, ffn_norm, ffn_w_up, ffn_conv_w, ffn_conv_b, ffn_w_down, final_norm):
    weights = dict(attn_norm=attn_norm, w_in=w_in, gdn_conv_w=gdn_conv_w, gdn_A_log=gdn_A_log,
                   gdn_dt_bias=gdn_dt_bias, gdn_norm=gdn_norm, ssm_conv_w=ssm_conv_w, ssm_conv_b=ssm_conv_b,
                   ssm_A_log=ssm_A_log, ssm_dt_bias=ssm_dt_bias, ssm_D=ssm_D, ssm_norm=ssm_norm,
                   mla_q_norm=mla_q_norm, mla_w_uq=mla_w_uq, mla_kv_norm=mla_kv_norm, mla_w_uk=mla_w_uk,
                   mla_w_uv=mla_w_uv, w_branch=w_branch, w_out=w_out, ffn_norm=ffn_norm, ffn_w_up=ffn_w_up,
                   ffn_conv_w=ffn_conv_w, ffn_conv_b=ffn_conv_b, ffn_w_down=ffn_w_down)
    depth = w_in.shape[0]
    bp, sp, _ = x_prompt.shape
    bs, ss, _ = x_sample.shape
    n_past = cache_mla_latent.shape[2]
    pos_p = jnp.arange(sp, dtype=jnp.int32)
    pos_s = n_past + jnp.arange(ss, dtype=jnp.int32)
    tabs_p = _rope_tables(pos_p)
    tabs_s = tuple(jnp.tile(t, (bs, 1)) for t in _rope_tables(pos_s))
    zero_state = (
        None,
        jnp.zeros((bp,) + state_gdn_conv.shape[2:], F32),
        jnp.zeros((bp,) + state_gdn_ssm.shape[2:], F32),
        jnp.zeros((bp,) + state_ssm_conv.shape[2:], F32),
        jnp.zeros((bp,) + state_ssm.shape[2:], F32),
        jnp.zeros((bp,) + state_ffn_conv.shape[2:], F32),
    )
    xp, xs = x_prompt, x_sample
    new_p, new_s = [], []
    for l in range(depth):
        lw = _pack_layer(l, weights)
        xp, st_p = _layer(xp, tabs_p, 0, zero_state, lw)
        latkr_past = jnp.concatenate(
            [cache_mla_latent[l], cache_mla_krope[l],
             jnp.zeros((bs, n_past, LATKR - MLA_KV_RANK - MLA_ROPE), F32)], axis=-1).astype(BF16)
        st_in = (latkr_past, state_gdn_conv[l], state_gdn_ssm[l], state_ssm_conv[l], state_ssm[l],
                 state_ffn_conv[l])
        xs, st_s = _layer(xs, tabs_s, n_past, st_in, lw)
        new_p.append(st_p)
        new_s.append(st_s)
    y_prompt = rmsnorm(xp.reshape(bp * sp, D_MODEL), final_norm, F32).reshape(bp, sp, D_MODEL)
    y_sample = rmsnorm(xs.reshape(bs * ss, D_MODEL), final_norm, F32).reshape(bs, ss, D_MODEL)
    outs_p = [jnp.stack(t) for t in zip(*new_p)]
    outs_s = [jnp.stack(t) for t in zip(*new_s)]
    return (y_prompt, y_sample, *outs_p, *outs_s)
```

```python
import functools
import math

import jax
import jax.numpy as jnp
from jax import lax
from jax.experimental import pallas as pl
from jax.experimental.pallas import tpu as pltpu

F32 = jnp.float32
BF16 = jnp.bfloat16
HI = lax.Precision.HIGHEST

EPS = 1e-6
CHUNK = 64
D_MODEL = 4096
BRANCH = 1024
GDN_H, GDN_D = 8, 128
GDN_QKV = 3072
SSM_H, SSM_P, SSM_G, SSM_N = 16, 64, 2, 128
SSM_CONV_DIM = 1536
MLA_H, MLA_NOPE, MLA_ROPE, MLA_V = 8, 128, 64, 128
MLA_Q_RANK, MLA_KV_RANK = 768, 512
ROPE_THETA = 10000.0
D_FF = 11008
LATKR = 640
QSLOT = 256

P_GATE = 0
P_MLA = 12288
P_SMALL = P_MLA + 1280
P_XBC = 13824
P_QKV = 15360
P_ZA = 18432
P_ZB = 19456
P_TOT = 20480
L_BETA, L_ALPHA, L_DT = 64, 72, 80

VMEM_LIMIT = 56 * 1024 * 1024


def _cp(sem):
    return pltpu.CompilerParams(dimension_semantics=sem, vmem_limit_bytes=VMEM_LIMIT)


def _softplus(x):
    return jnp.maximum(x, 0.0) + jnp.log1p(jnp.exp(-jnp.abs(x)))


def _silu(x):
    return x * jax.nn.sigmoid(x)


def _nt(a, b):
    return lax.dot_general(a, b, (((1,), (1,)), ((), ())), preferred_element_type=F32)


def _tn(a, b):
    return lax.dot_general(a, b, (((0,), (0,)), ((), ())), preferred_element_type=F32)


def _rms_kernel(x_ref, w_ref, o_ref):
    x = x_ref[...]
    ms = jnp.mean(x * x, axis=-1, keepdims=True)
    o_ref[...] = (x * lax.rsqrt(ms + EPS) * w_ref[...]).astype(o_ref.dtype)


def rmsnorm(x, w, out_dtype):
    m, d = x.shape
    tm = min(256, m)
    return pl.pallas_call(
        _rms_kernel,
        out_shape=jax.ShapeDtypeStruct((m, d), out_dtype),
        grid=(m // tm,),
        in_specs=[pl.BlockSpec((tm, d), lambda i: (i, 0)),
                  pl.BlockSpec((1, d), lambda i: (0, 0))],
        out_specs=pl.BlockSpec((tm, d), lambda i: (i, 0)),
        compiler_params=_cp(("parallel",)),
        name="rmsnorm",
    )(x, w.reshape(1, d))


def _mm_kernel(*refs, nk, n_extra, epilogue):
    a_ref, b_ref = refs[0], refs[1]
    extras = refs[2:2 + n_extra]
    o_ref = refs[2 + n_extra]
    part = jnp.dot(a_ref[...], b_ref[...], preferred_element_type=F32)
    if nk == 1:
        o_ref[...] = epilogue(part, *extras).astype(o_ref.dtype)
        return
    acc_ref = refs[3 + n_extra]
    k = pl.program_id(2)

    @pl.when(k == 0)
    def _():
        acc_ref[...] = part

    @pl.when(k > 0)
    def _():
        acc_ref[...] += part

    @pl.when(k == nk - 1)
    def _():
        o_ref[...] = epilogue(acc_ref[...], *extras).astype(o_ref.dtype)


def matmul(a, b, out_dtype, *, tm, tn, tk=None, extras=(), epilogue=None, name="matmul"):
    m, kd = a.shape
    n = b.shape[1]
    tm, tn = min(tm, m), min(tn, n)
    while m % tm:
        tm //= 2
    tk = kd if tk is None else tk
    nk = kd // tk
    assert m % tm == 0 and n % tn == 0 and kd % tk == 0
    if epilogue is None:
        epilogue = lambda acc: acc
    in_specs = [pl.BlockSpec((tm, tk), lambda i, j, k: (i, k)),
                pl.BlockSpec((tk, tn), lambda i, j, k: (k, j))]
    args = [a, b]
    for arr, blk, imap in extras:
        in_specs.append(pl.BlockSpec(blk, functools.partial(lambda i, j, k, f: f(i, j), f=imap)))
        args.append(arr)
    scratch = [pltpu.VMEM((tm, tn), F32)] if nk > 1 else []
    return pl.pallas_call(
        functools.partial(_mm_kernel, nk=nk, n_extra=len(extras), epilogue=epilogue),
        out_shape=jax.ShapeDtypeStruct((m, n), out_dtype),
        grid=(m // tm, n // tn, nk),
        in_specs=in_specs,
        out_specs=pl.BlockSpec((tm, tn), lambda i, j, k: (i, j)),
        scratch_shapes=scratch,
        compiler_params=_cp(("parallel", "parallel", "arbitrary")),
        name=name,
    )(*args)


def _norm_mm_kernel(x_ref, nw_ref, b_ref, o_ref, a16_ref):
    @pl.when(pl.program_id(1) == 0)
    def _():
        x = x_ref[...]
        ms = jnp.mean(x * x, axis=-1, keepdims=True)
        a16_ref[...] = (x * lax.rsqrt(ms + EPS) * nw_ref[...]).astype(a16_ref.dtype)

    o_ref[...] = jnp.dot(a16_ref[...], b_ref[...], preferred_element_type=F32).astype(o_ref.dtype)


def rmsnorm_matmul(x, nw, b, out_dtype, *, tm, tn, name):
    m, kd = x.shape
    n = b.shape[1]
    tm = min(tm, m)
    assert m % tm == 0 and n % tn == 0
    return pl.pallas_call(
        _norm_mm_kernel,
        out_shape=jax.ShapeDtypeStruct((m, n), out_dtype),
        grid=(m // tm, n // tn),
        in_specs=[pl.BlockSpec((tm, kd), lambda i, j: (i, 0)),
                  pl.BlockSpec((1, kd), lambda i, j: (0, 0)),
                  pl.BlockSpec((kd, tn), lambda i, j: (0, j))],
        out_specs=pl.BlockSpec((tm, tn), lambda i, j: (i, j)),
        scratch_shapes=[pltpu.VMEM((tm, kd), BF16)],
        compiler_params=_cp(("parallel", "arbitrary")),
        name=name,
    )(x, nw.reshape(1, kd), b)


def _residual_epilogue(acc, r_ref):
    return acc + r_ref[...]


def matmul_residual(a, b, res, *, tm, tn, tk=None, name="matmul_residual"):
    tm_, tn_ = min(tm, a.shape[0]), min(tn, b.shape[1])
    return matmul(a, b, F32, tm=tm, tn=tn, tk=tk, name=name,
                  extras=[(res, (tm_, tn_), lambda i, j: (i, j))], epilogue=_residual_epilogue)


def _q_rope_epilogue(acc, c_ref, sa_ref, sb_ref):
    c, sa, sb = c_ref[...], sa_ref[...], sb_ref[...]
    scale = (MLA_NOPE + MLA_ROPE) ** -0.5 * math.log2(math.e)
    outs = []
    for h in range(MLA_H):
        xs = acc[:, h * QSLOT:(h + 1) * QSLOT]
        y = xs * c + pltpu.roll(xs, 32, 1) * sa + pltpu.roll(xs, QSLOT - 32, 1) * sb
        outs.append(y * scale)
    return jnp.concatenate(outs, axis=1)


def _conv_kernel(x_ref, st_ref, w_ref, b_ref, y_ref, ns_ref, xp_ref, *, width, ts, act):
    t = pl.program_id(2)
    nt = pl.num_programs(2)

    @pl.when(t == 0)
    def _():
        xp_ref[0:8, :] = st_ref[0]

    @pl.when(t > 0)
    def _():
        xp_ref[0:8, :] = xp_ref[ts:ts + 8, :]

    xp_ref[8:8 + ts, :] = x_ref[0]
    off = 8 - (width - 1)
    y = xp_ref[off:off + ts, :] * w_ref[0:1, :]
    for j in range(1, width):
        y = y + xp_ref[off + j:off + j + ts, :] * w_ref[j:j + 1, :]
    y = y + b_ref[...]
    if act:
        y = _silu(y)
    y_ref[0] = y.astype(y_ref.dtype)

    @pl.when(t == nt - 1)
    def _():
        ns_ref[0] = xp_ref[ts:ts + 8, :]


def causal_conv(x3, col_off, chans, state8, w, b, *, act, ct=512):
    bsz, s, _ = x3.shape
    width = w.shape[0]
    ts = min(512, s)
    assert chans % ct == 0 and col_off % ct == 0 and s % ts == 0
    cb0 = col_off // ct
    return pl.pallas_call(
        functools.partial(_conv_kernel, width=width, ts=ts, act=act),
        out_shape=(jax.ShapeDtypeStruct((bsz, s, chans), F32),
                   jax.ShapeDtypeStruct((bsz, 8, chans), F32)),
        grid=(bsz, chans // ct, s // ts),
        in_specs=[pl.BlockSpec((1, ts, ct), lambda bi, c, t: (bi, t, cb0 + c)),
                  pl.BlockSpec((1, 8, ct), lambda bi, c, t: (bi, 0, c)),
                  pl.BlockSpec((width, ct), lambda bi, c, t: (0, c)),
                  pl.BlockSpec((1, ct), lambda bi, c, t: (0, c))],
        out_specs=(pl.BlockSpec((1, ts, ct), lambda bi, c, t: (bi, t, c)),
                   pl.BlockSpec((1, 8, ct), lambda bi, c, t: (bi, 0, c))),
        scratch_shapes=[pltpu.VMEM((ts + 8, ct), F32)],
        compiler_params=_cp(("parallel", "parallel", "arbitrary")),
        name="causal_conv",
    )(x3, state8, w, b.reshape(1, chans))


HALO = 16


FFN_TN = 256


def _interleave_ffn(x):
    lead = x.shape[:-1]
    x = x.reshape(*lead, 2, D_FF // FFN_TN, FFN_TN)
    return jnp.swapaxes(x, -3, -2).reshape(*lead, 2 * D_FF)


def _deinterleave_ffn(x):
    lead = x.shape[:-1]
    x = x.reshape(*lead, D_FF // FFN_TN, 2, FFN_TN)
    return jnp.swapaxes(x, -3, -2).reshape(*lead, 2 * D_FF)


def _il_cast_kernel(a_ref, b_ref, o_ref):
    o_ref[0, :, :FFN_TN] = a_ref[0].astype(o_ref.dtype)
    o_ref[0, :, FFN_TN:] = b_ref[0].astype(o_ref.dtype)


def interleave_cast_w_up(w_all, l, *, tk=1024):
    kd = w_all.shape[1]
    nc = D_FF // FFN_TN
    return pl.pallas_call(
        _il_cast_kernel,
        out_shape=jax.ShapeDtypeStruct((nc, kd, 2 * FFN_TN), BF16),
        grid=(kd // tk, nc),
        in_specs=[pl.BlockSpec((1, tk, FFN_TN), lambda k, j: (l, k, j)),
                  pl.BlockSpec((1, tk, FFN_TN), lambda k, j: (l, k, nc + j))],
        out_specs=pl.BlockSpec((1, tk, 2 * FFN_TN), lambda k, j: (j, k, 0)),
        compiler_params=_cp(("parallel", "parallel")),
        name="interleave_cast_w_up",
    )(w_all, w_all)


def _mm_tiles_kernel(a_ref, w_ref, o_ref):
    o_ref[...] = jnp.dot(a_ref[...], w_ref[0], preferred_element_type=F32)


def matmul_tile_major(a, w_tiles):
    m, kd = a.shape
    nt, _, tn = w_tiles.shape
    return pl.pallas_call(
        _mm_tiles_kernel,
        out_shape=jax.ShapeDtypeStruct((m, nt * tn), F32),
        grid=(nt,),
        in_specs=[pl.BlockSpec((m, kd), lambda j: (0, 0)),
                  pl.BlockSpec((1, kd, tn), lambda j: (j, 0, 0))],
        out_specs=pl.BlockSpec((m, tn), lambda j: (0, j)),
        compiler_params=_cp(("parallel",)),
        name="ffn_up",
    )(a, w_tiles)


def _ffn_conv_gate_il_kernel(x_ref, st_ref, cw_ref, cb_ref, y_ref, ns_ref, xp_ref, *, width, s):
    xp_ref[0:8, :] = st_ref[0]
    xp_ref[8:8 + s, :] = x_ref[0]
    off = 8 - (width - 1)
    y = xp_ref[off:off + s, :] * cw_ref[0:1, :]
    for t in range(1, width):
        y = y + xp_ref[off + t:off + t + s, :] * cw_ref[t:t + 1, :]
    y = y + cb_ref[...]
    ns_ref[0] = xp_ref[s:s + 8, :]
    for c in range(D_FF // FFN_TN):
        g = y[:, 2 * c * FFN_TN:(2 * c + 1) * FFN_TN]
        v = y[:, (2 * c + 1) * FFN_TN:(2 * c + 2) * FFN_TN]
        y_ref[0, :, c * FFN_TN:(c + 1) * FFN_TN] = (_silu(g) * v).astype(y_ref.dtype)


def ffn_conv_gate_il(up3, state8, cw, cb):
    bsz, s, n2 = up3.shape
    width = cw.shape[0]
    y, ns = pl.pallas_call(
        functools.partial(_ffn_conv_gate_il_kernel, width=width, s=s),
        out_shape=(jax.ShapeDtypeStruct((bsz, s, D_FF), BF16),
                   jax.ShapeDtypeStruct((bsz, 8, n2), F32)),
        grid=(bsz,),
        in_specs=[pl.BlockSpec((1, s, n2), lambda b: (b, 0, 0)),
                  pl.BlockSpec((1, 8, n2), lambda b: (b, 0, 0)),
                  pl.BlockSpec((width, n2), lambda b: (0, 0)),
                  pl.BlockSpec((1, n2), lambda b: (0, 0))],
        out_specs=(pl.BlockSpec((1, s, D_FF), lambda b: (b, 0, 0)),
                   pl.BlockSpec((1, 8, n2), lambda b: (b, 0, 0))),
        scratch_shapes=[pltpu.VMEM((s + 8, n2), F32)],
        compiler_params=_cp(("parallel",)),
        name="ffn_conv_gate_il",
    )(up3, _interleave_ffn(state8), _interleave_ffn(cw), _interleave_ffn(cb.reshape(1, n2)))
    return y, _deinterleave_ffn(ns)


FFN_SUB = 512


def _ffn_up_fused_kernel(a_ref, halo_ref, w_ref, st_ref, cw_ref, cb_ref, y_ref, ns_ref, aext_ref, *xp_refs,
                         tm, tiles_per_seq, width):
    i = pl.program_id(0)
    j = pl.program_id(1)

    @pl.when(j == 0)
    def _():
        aext_ref[0:HALO, :] = halo_ref[...]
        aext_ref[HALO:HALO + tm, :] = a_ref[...]

    nsub = tm // FFN_SUB
    for r in range(nsub):
        xp_refs[r][...] = jnp.dot(aext_ref[r * FFN_SUB:(r + 1) * FFN_SUB + HALO, :], w_ref[0],
                                  preferred_element_type=F32)
    first = (i % tiles_per_seq) == 0
    xp_refs[0][HALO - 8:HALO, :] = jnp.where(first, st_ref[0], xp_refs[0][HALO - 8:HALO, :])
    off = HALO - (width - 1)
    cw = cw_ref[...]
    cb = cb_ref[...]
    for r in range(nsub):
        xp_ref = xp_refs[r]
        y = xp_ref[off:off + FFN_SUB, :] * cw[0:1, :]
        for t in range(1, width):
            y = y + xp_ref[off + t:off + t + FFN_SUB, :] * cw[t:t + 1, :]
        y = y + cb
        y_ref[r * FFN_SUB:(r + 1) * FFN_SUB, :] = (_silu(y[:, :FFN_TN]) * y[:, FFN_TN:]).astype(y_ref.dtype)
    ns_ref[0] = xp_refs[nsub - 1][HALO + FFN_SUB - 8:HALO + FFN_SUB, :]


def ffn_up_fused(h2, w_up_il, state8, cw, cb, *, bsz, tm=1024):
    m, kd = h2.shape
    s = m // bsz
    tm = min(tm, s)
    assert s % tm == 0 and tm % FFN_SUB == 0
    tiles_per_seq = s // tm
    width = cw.shape[0]
    nc = D_FF // FFN_TN
    tn2 = 2 * FFN_TN
    y, ns = pl.pallas_call(
        functools.partial(_ffn_up_fused_kernel, tm=tm, tiles_per_seq=tiles_per_seq, width=width),
        out_shape=(jax.ShapeDtypeStruct((m, D_FF), BF16),
                   jax.ShapeDtypeStruct((m // tm, 8, 2 * D_FF), F32)),
        grid=(m // tm, nc),
        in_specs=[pl.BlockSpec((tm, kd), lambda i, j: (i, 0)),
                  pl.BlockSpec((HALO, kd), lambda i, j: (jnp.maximum(i * (tm // HALO) - 1, 0), 0)),
                  pl.BlockSpec((1, kd, tn2), lambda i, j: (j, 0, 0)),
                  pl.BlockSpec((1, 8, tn2), lambda i, j: (i // tiles_per_seq, 0, j)),
                  pl.BlockSpec((width, tn2), lambda i, j: (0, j)),
                  pl.BlockSpec((1, tn2), lambda i, j: (0, j))],
        out_specs=(pl.BlockSpec((tm, FFN_TN), lambda i, j: (i, j)),
                   pl.BlockSpec((1, 8, tn2), lambda i, j: (i, 0, j))),
        scratch_shapes=[pltpu.VMEM((tm + HALO, kd), BF16)]
        + [pltpu.VMEM((FFN_SUB + HALO, tn2), F32) for _ in range(tm // FFN_SUB)],
        compiler_params=_cp(("arbitrary", "arbitrary")),
        name="ffn_up_fused",
    )(h2, h2, w_up_il, _interleave_ffn(state8), _interleave_ffn(cw), _interleave_ffn(cb.reshape(1, 2 * D_FF)))
    return y, _deinterleave_ffn(ns[tiles_per_seq - 1::tiles_per_seq])


def _chunk_masks(lt, c):
    sh = int(math.log2(c))
    ri = lax.broadcasted_iota(jnp.int32, (lt, lt), 0)
    ci = lax.broadcasted_iota(jnp.int32, (lt, lt), 1)
    same = (ri >> sh) == (ci >> sh)
    return ri, ci, same


def _split3(x):
    hi = x.astype(BF16)
    r = x - hi.astype(F32)
    mid = r.astype(BF16)
    lo = (r - mid.astype(F32)).astype(BF16)
    return hi, mid, lo


def _dot_exact_lhs(l16, x):
    hi, mid, lo = _split3(x)
    return (jnp.dot(l16, hi, preferred_element_type=F32) + jnp.dot(l16, mid, preferred_element_type=F32)
            + jnp.dot(l16, lo, preferred_element_type=F32))


def _dot_hi(a, b):
    a_hi = a.astype(BF16)
    a_lo = (a - a_hi.astype(F32)).astype(BF16)
    b_hi = b.astype(BF16)
    b_lo = (b - b_hi.astype(F32)).astype(BF16)
    return (jnp.dot(a_hi, b_hi, preferred_element_type=F32) + jnp.dot(a_hi, b_lo, preferred_element_type=F32)
            + jnp.dot(a_lo, b_hi, preferred_element_type=F32))


def _gdn_kernel(q_ref, k_ref, v_ref, sm_ref, z_ref, s0_ref, alog_ref, dtb_ref, nw_ref,
                o_ref, sout_ref, s_sc, vn_sc, *, lt, c, hps):
    hg = pl.program_id(1)
    t = pl.program_id(2)
    nt = pl.num_programs(2)

    @pl.when(t == 0)
    def _():
        s_sc[...] = s0_ref[0]

    sm = sm_ref[0]
    lane = lax.broadcasted_iota(jnp.int32, (1, 128), 1)
    beta_full = jax.nn.sigmoid(sm)
    g_full = -jnp.exp(alog_ref[...]) * _softplus(sm + dtb_ref[...])
    ri, ci, same = _chunk_masks(lt, c)
    tri = same & (ci <= ri)
    strict = same & (ci < ri)
    eye_b = ri == ci
    eye = jnp.where(eye_b, 1.0, 0.0)
    gc_full = _dot_exact_lhs(jnp.where(tri, 1.0, 0.0).astype(BF16), g_full)
    nw = nw_ref[...]

    hs = range(hps)
    cols = [slice(hh * GDN_D, (hh + 1) * GDN_D) for hh in hs]
    dot16 = lambda x, y: jnp.dot(x.astype(BF16), y.astype(BF16), preferred_element_type=F32)
    ks, k16s, kbs, gcs, ms, avs, qds, rhss = [], [], [], [], [], [], [], []
    for hh in hs:
        h = hg * hps + hh
        q = q_ref[0, :, cols[hh]]
        k = k_ref[0, :, cols[hh]]
        q = q * lax.rsqrt(jnp.sum(q * q, axis=-1, keepdims=True) + EPS) * (GDN_D ** -0.5)
        k = k * lax.rsqrt(jnp.sum(k * k, axis=-1, keepdims=True) + EPS)
        beta = jnp.sum(jnp.where(lane == L_BETA + h, beta_full, 0.0), axis=-1, keepdims=True)
        gc = jnp.sum(jnp.where(lane == L_ALPHA + h, gc_full, 0.0), axis=-1, keepdims=True)
        gr = jnp.sum(jnp.where(eye_b, gc, 0.0), axis=0, keepdims=True)
        decay = jnp.exp(jnp.where(tri, gc - gr, 0.0))
        kb = k * beta
        k16 = k.astype(BF16)
        ms.append(jnp.where(strict, _nt(kb.astype(BF16), k16) * decay, 0.0))
        avs.append(jnp.where(tri, _nt(q.astype(BF16), k16) * decay, 0.0).astype(BF16))
        eg = jnp.exp(gc)
        rhss.append(jnp.concatenate([v_ref[0, :, cols[hh]] * beta, kb * eg], axis=1).astype(BF16))
        qds.append((q * eg).astype(BF16))
        ks.append(k)
        gcs.append(gc)

    ps = [-m for m in ms]
    ts = [eye + p for p in ps]
    for _ in range(int(math.log2(c)) - 1):
        ps = [dot16(p, p) for p in ps]
        ts = [tt + dot16(p, tt) for p, tt in zip(ps, ts)]
    rs = [eye - _dot_hi(eye + m, tt) for m, tt in zip(ms, ts)]
    ts = [tt + dot16(tt, r) for tt, r in zip(ts, rs)]
    uws = [jnp.dot(tt.astype(BF16), rhs, preferred_element_type=F32) for tt, rhs in zip(ts, rhss)]
    us = [uw[:, :GDN_D] for uw in uws]
    ws = [uw[:, GDN_D:].astype(BF16) for uw in uws]

    ss = [s_sc[hh] for hh in hs]
    for hh in hs:
        vn_sc[hh] = us[hh].astype(BF16)
    for cc in range(lt // c):
        r0 = cc * c
        rows = slice(r0, r0 + c)
        for hh in hs:
            s16 = ss[hh].astype(BF16)
            vnew = us[hh][rows] - jnp.dot(ws[hh][rows], s16, preferred_element_type=F32)
            vn16 = vnew.astype(BF16)
            vn_sc[hh, rows, :] = vn16
            o = (jnp.dot(qds[hh][rows], s16, preferred_element_type=F32)
                 + jnp.dot(avs[hh][rows, :], vn_sc[hh], preferred_element_type=F32))
            glast = gcs[hh][r0 + c - 1:r0 + c, :]
            kd = (ks[hh][rows] * jnp.exp(glast - gcs[hh][rows])).astype(BF16)
            ss[hh] = ss[hh] * jnp.exp(glast) + _tn(kd, vn16)
            mean_sq = jnp.mean(o * o, axis=-1, keepdims=True)
            o = o * lax.rsqrt(mean_sq + EPS) * nw * _silu(z_ref[0, rows, cols[hh]])
            o_ref[0, rows, cols[hh]] = o.astype(o_ref.dtype)
    for hh in hs:
        s_sc[hh] = ss[hh]

    @pl.when(t == nt - 1)
    def _():
        sout_ref[0] = s_sc[...]


def gdn(qkv3, p3, s0, alog_row, dtb_row, norm_w, *, hps=4):
    bsz, s, _ = qkv3.shape
    c = min(CHUNK, s)
    lt = min(256, s)
    hw = hps * GDN_D
    ng = GDN_H // hps
    small_blk = P_SMALL // 128
    za_blk = P_ZA // hw
    blk = lambda off: pl.BlockSpec((1, lt, hw), lambda b, h, t: (b, t, off + h))
    return pl.pallas_call(
        functools.partial(_gdn_kernel, lt=lt, c=c, hps=hps),
        out_shape=(jax.ShapeDtypeStruct((bsz, s, BRANCH), BF16),
                   jax.ShapeDtypeStruct((bsz, GDN_H, GDN_D, GDN_D), F32)),
        grid=(bsz, ng, s // lt),
        in_specs=[blk(0), blk(ng), blk(2 * ng),
                  pl.BlockSpec((1, lt, 128), lambda b, h, t: (b, t, small_blk)),
                  blk(za_blk),
                  pl.BlockSpec((1, hps, GDN_D, GDN_D), lambda b, h, t: (b, h, 0, 0)),
                  pl.BlockSpec((1, 128), lambda b, h, t: (0, 0)),
                  pl.BlockSpec((1, 128), lambda b, h, t: (0, 0)),
                  pl.BlockSpec((1, 128), lambda b, h, t: (0, 0))],
        out_specs=(pl.BlockSpec((1, lt, hw), lambda b, h, t: (b, t, h)),
                   pl.BlockSpec((1, hps, GDN_D, GDN_D), lambda b, h, t: (b, h, 0, 0))),
        scratch_shapes=[pltpu.VMEM((hps, GDN_D, GDN_D), F32), pltpu.VMEM((hps, lt, GDN_D), BF16)],
        compiler_params=_cp(("parallel", "parallel", "arbitrary")),
        name="gdn",
    )(qkv3, qkv3, qkv3, p3, p3, s0, alog_row, dtb_row, norm_w.reshape(1, GDN_D))


def _ssd_kernel(x_ref, b_ref, c_ref, sm_ref, z_ref, h0_ref, alog_ref, dtb_ref, d_ref, nw_ref,
                y_ref, hout_ref, hs_sc, y_sc, *, lt, c):
    g = pl.program_id(1)
    t = pl.program_id(2)
    nt = pl.num_programs(2)
    npair = SSM_H // SSM_G // 2

    @pl.when(t == 0)
    def _():
        hs_sc[...] = h0_ref[0]

    sm = sm_ref[0]
    dt_full = _softplus(sm + dtb_ref[...])
    da = dt_full * (-jnp.exp(alog_ref[...]))
    ri, ci, same = _chunk_masks(lt, c)
    tri = same & (ci <= ri)
    eye_b = ri == ci
    acum_full = _dot_exact_lhs(jnp.where(tri, 1.0, 0.0).astype(BF16), da)
    nch = lt // c
    lasts = [acum_full[cc * c + c - 1:cc * c + c, :] for cc in range(nch)]
    alast_full = jnp.concatenate([jnp.broadcast_to(l, (c, 128)) for l in lasts], axis=0)
    ea_full = jnp.exp(acum_full)
    dtw_full = dt_full * jnp.exp(alast_full - acum_full)

    bm = b_ref[0]
    cm = c_ref[0]
    bm16 = bm.astype(BF16)
    cm16 = cm.astype(BF16)
    cb = jnp.where(tri, _nt(cm16, bm16), 0.0)
    lane = lax.broadcasted_iota(jnp.int32, (1, 128), 1)
    lo = lane < SSM_P
    rowlo = lax.broadcasted_iota(jnp.int32, (128, 1), 0) < SSM_P
    x = x_ref[0]
    z = z_ref[0]
    d_row = d_ref[...]

    def ext(full, l):
        return jnp.sum(jnp.where(lane == l, full, 0.0), axis=-1, keepdims=True)

    for j in range(npair):
        la = L_DT + g * (SSM_H // SSM_G) + 2 * j
        lb = la + 1
        ac_a, ac_b = ext(acum_full, la), ext(acum_full, lb)
        row_a = jnp.sum(jnp.where(eye_b, ac_a, 0.0), axis=0, keepdims=True)
        row_b = jnp.sum(jnp.where(eye_b, ac_b, 0.0), axis=0, keepdims=True)
        l_a = (jnp.exp(jnp.where(tri, ac_a - row_a, 0.0)) * cb).astype(BF16)
        l_b = (jnp.exp(jnp.where(tri, ac_b - row_b, 0.0)) * cb).astype(BF16)
        dt_pair = jnp.where(lo, ext(dt_full, la), ext(dt_full, lb))
        ea_pair = jnp.where(lo, ext(ea_full, la), ext(ea_full, lb))
        dtw_pair = jnp.where(lo, ext(dtw_full, la), ext(dtw_full, lb))
        xp = x[:, j * 128:(j + 1) * 128]
        xdt = xp * dt_pair
        y = (jnp.dot(l_a, jnp.where(lo, xdt, 0.0).astype(BF16), preferred_element_type=F32)
             + jnp.dot(l_b, jnp.where(lo, 0.0, xdt).astype(BF16), preferred_element_type=F32))
        xw = (xp * dtw_pair).astype(BF16)
        hs = hs_sc[j]
        yoffs = []
        for cc in range(nch):
            r0 = cc * c
            yoffs.append(_nt(cm16[r0:r0 + c], hs.astype(BF16)) * ea_pair[r0:r0 + c])
            cs = _tn(xw[r0:r0 + c], bm16[r0:r0 + c])
            al = jnp.where(rowlo, jnp.exp(ac_a[r0 + c - 1:r0 + c, :]), jnp.exp(ac_b[r0 + c - 1:r0 + c, :]))
            hs = hs * al + cs
        hs_sc[j] = hs
        yoff = yoffs[0] if nch == 1 else jnp.concatenate(yoffs, axis=0)
        ytot = (y + yoff + d_row[:, j * 128:(j + 1) * 128] * xp) * _silu(z[:, j * 128:(j + 1) * 128])
        y_sc[:, j * 128:(j + 1) * 128] = ytot

    yy = y_sc[...]
    ms = jnp.mean(yy * yy, axis=-1, keepdims=True)
    y_ref[0] = (yy * lax.rsqrt(ms + EPS) * nw_ref[...]).astype(y_ref.dtype)

    @pl.when(t == nt - 1)
    def _():
        hout_ref[0] = hs_sc[...]


def ssd(xbc3, p3, h0, alog_row, dtb_row, d_row, norm_w):
    bsz, s, _ = xbc3.shape
    c = min(CHUNK, s)
    lt = min(256, s)
    gw = BRANCH // SSM_G
    npair = SSM_H // SSM_G // 2
    small_blk = P_SMALL // 128
    zb_blk = P_ZB // gw
    return pl.pallas_call(
        functools.partial(_ssd_kernel, lt=lt, c=c),
        out_shape=(jax.ShapeDtypeStruct((bsz, s, BRANCH), BF16),
                   jax.ShapeDtypeStruct((bsz, SSM_H // 2, 128, 128), F32)),
        grid=(bsz, SSM_G, s // lt),
        in_specs=[pl.BlockSpec((1, lt, gw), lambda b, g, t: (b, t, g)),
                  pl.BlockSpec((1, lt, 128), lambda b, g, t: (b, t, BRANCH // 128 + g)),
                  pl.BlockSpec((1, lt, 128), lambda b, g, t: (b, t, BRANCH // 128 + SSM_G + g)),
                  pl.BlockSpec((1, lt, 128), lambda b, g, t: (b, t, small_blk)),
                  pl.BlockSpec((1, lt, gw), lambda b, g, t: (b, t, zb_blk + g)),
                  pl.BlockSpec((1, npair, 128, 128), lambda b, g, t: (b, g, 0, 0)),
                  pl.BlockSpec((1, 128), lambda b, g, t: (0, 0)),
                  pl.BlockSpec((1, 128), lambda b, g, t: (0, 0)),
                  pl.BlockSpec((1, gw), lambda b, g, t: (0, g)),
                  pl.BlockSpec((1, gw), lambda b, g, t: (0, g))],
        out_specs=(pl.BlockSpec((1, lt, gw), lambda b, g, t: (b, t, g)),
                   pl.BlockSpec((1, npair, 128, 128), lambda b, g, t: (b, g, 0, 0))),
        scratch_shapes=[pltpu.VMEM((npair, 128, 128), F32), pltpu.VMEM((lt, gw), F32)],
        compiler_params=_cp(("parallel", "parallel", "arbitrary")),
        name="ssd",
    )(xbc3, xbc3, xbc3, p3, p3, h0, alog_row, dtb_row, d_row, norm_w.reshape(1, BRANCH))


def _mla_prep_kernel(p_ref, qn_ref, kvn_ref, ck_ref, sak_ref, sbk_ref,
                     cq_ref, lat_ref, kr_ref, latkr_ref):
    blk = p_ref[...]
    cq = blk[:, 0:MLA_Q_RANK]
    cq = cq * lax.rsqrt(jnp.mean(cq * cq, axis=-1, keepdims=True) + EPS) * qn_ref[...]
    cq_ref[...] = cq.astype(cq_ref.dtype)
    ckv = blk[:, MLA_Q_RANK:MLA_Q_RANK + MLA_KV_RANK]
    lat = ckv * lax.rsqrt(jnp.mean(ckv * ckv, axis=-1, keepdims=True) + EPS) * kvn_ref[...]
    lat_ref[...] = lat
    sm = blk[:, 1280:1408]
    y = sm * ck_ref[...] + pltpu.roll(sm, 32, 1) * sak_ref[...] + pltpu.roll(sm, 96, 1) * sbk_ref[...]
    kr_ref[...] = y[:, 0:MLA_ROPE]
    latkr_ref[:, 0:MLA_KV_RANK] = lat.astype(latkr_ref.dtype)
    latkr_ref[:, MLA_KV_RANK:LATKR] = y.astype(latkr_ref.dtype)


def mla_prep(p, q_norm, kv_norm, ck, sak, sbk):
    m = p.shape[0]
    tm = min(256, m)
    mla_blk = P_MLA // 1536
    row = lambda n: pl.BlockSpec((tm, n), lambda i: (i, 0))
    return pl.pallas_call(
        _mla_prep_kernel,
        out_shape=(jax.ShapeDtypeStruct((m, MLA_Q_RANK), BF16),
                   jax.ShapeDtypeStruct((m, MLA_KV_RANK), F32),
                   jax.ShapeDtypeStruct((m, MLA_ROPE), F32),
                   jax.ShapeDtypeStruct((m, LATKR), BF16)),
        grid=(m // tm,),
        in_specs=[pl.BlockSpec((tm, 1536), lambda i: (i, mla_blk)),
                  pl.BlockSpec((1, MLA_Q_RANK), lambda i: (0, 0)),
                  pl.BlockSpec((1, MLA_KV_RANK), lambda i: (0, 0)),
                  row(128), row(128), row(128)],
        out_specs=(row(MLA_Q_RANK), row(MLA_KV_RANK), row(MLA_ROPE), row(LATKR)),
        compiler_params=_cp(("parallel",)),
        name="mla_prep",
    )(p, q_norm.reshape(1, -1), kv_norm.reshape(1, -1), ck, sak, sbk)


def _flash_kernel(q_ref, k_ref, v_ref, o_ref, m_sc, l_sc, acc_sc, *, tq, tk, q_off, nk_real):
    i = pl.program_id(2)
    j = pl.program_id(3)
    nj = pl.num_programs(3)

    @pl.when(j == 0)
    def _():
        m_sc[...] = jnp.full(m_sc.shape, -1e30, F32)
        l_sc[...] = jnp.zeros(l_sc.shape, F32)
        acc_sc[...] = jnp.zeros(acc_sc.shape, F32)

    q_last = q_off + i * tq + tq - 1

    @pl.when(((j * tk) >> 6) <= (q_last >> 6))
    def _():
        s = _nt(q_ref[0], k_ref[0])
        qpos = q_off + i * tq + lax.broadcasted_iota(jnp.int32, (tq, 1), 0)
        kidx = j * tk + lax.broadcasted_iota(jnp.int32, (1, tk), 1)
        vis = ((kidx >> 6) <= (qpos >> 6)) & (kidx < nk_real)
        s = jnp.where(vis, s, -1e30)
        m_prev = m_sc[...]
        m_new = jnp.maximum(m_prev, jnp.max(s, axis=-1, keepdims=True))
        alpha = jnp.exp2(m_prev - m_new)
        p = jnp.where(vis, jnp.exp2(s - m_new), 0.0)
        l_sc[...] = alpha * l_sc[...] + jnp.sum(p, axis=-1, keepdims=True)
        acc_sc[...] = alpha * acc_sc[...] + jnp.dot(p.astype(BF16), v_ref[0], preferred_element_type=F32)
        m_sc[...] = m_new

    @pl.when(j == nj - 1)
    def _():
        o_ref[0] = (acc_sc[...] / l_sc[...]).astype(o_ref.dtype)


def flash_attention(q3, kv3, *, q_off, nk_real):
    bsz, sq, _ = q3.shape
    tk_tot = kv3.shape[1]
    tq = min(512, sq)
    tk = tk_tot if tk_tot <= 4096 else 512
    assert sq % tq == 0 and tk_tot % tk == 0
    v_blk0 = MLA_H * QSLOT // MLA_V

    def last_blk(i):
        return ((q_off + i * tq + tq - 1) >> 6 << 6) // tk

    return pl.pallas_call(
        functools.partial(_flash_kernel, tq=tq, tk=tk, q_off=q_off, nk_real=nk_real),
        out_shape=jax.ShapeDtypeStruct((bsz, sq, BRANCH), BF16),
        grid=(bsz, MLA_H, sq // tq, tk_tot // tk),
        in_specs=[pl.BlockSpec((1, tq, QSLOT), lambda b, h, i, j: (b, i, h)),
                  pl.BlockSpec((1, tk, QSLOT), lambda b, h, i, j: (b, jnp.minimum(j, last_blk(i)), h)),
                  pl.BlockSpec((1, tk, MLA_V), lambda b, h, i, j: (b, jnp.minimum(j, last_blk(i)), v_blk0 + h))],
        out_specs=pl.BlockSpec((1, tq, MLA_V), lambda b, h, i, j: (b, i, h)),
        scratch_shapes=[pltpu.VMEM((tq, 1), F32), pltpu.VMEM((tq, 1), F32), pltpu.VMEM((tq, MLA_V), F32)],
        compiler_params=_cp(("parallel", "parallel", "parallel", "arbitrary")),
        name="flash_attention",
    )(q3, kv3, kv3)


FLASH_STRIP = 256


def _flash_t_kernel(qi_ref, kj_ref, q_ref, k_ref, vt_ref, o_ref, m_sc, l_sc, acc_sc, *, tq, tk):
    pidx = pl.program_id(2)
    i = qi_ref[pidx]
    j = kj_ref[pidx]

    @pl.when(j == 0)
    def _():
        m_sc[...] = jnp.full(m_sc.shape, -1e30, F32)
        l_sc[...] = jnp.zeros(l_sc.shape, F32)
        acc_sc[...] = jnp.zeros(acc_sc.shape, F32)

    nstrip = tq // FLASH_STRIP

    def strip_mode(c, key_off):
        q_lo, q_hi = (c * FLASH_STRIP) // CHUNK, (c * FLASH_STRIP + FLASH_STRIP - 1) // CHUNK
        k_lo, k_hi = key_off // CHUNK, (key_off + tk - 1) // CHUNK
        if k_lo > q_hi:
            return "hidden"
        return "visible" if k_hi <= q_lo else "masked"

    def tile(key_off):
        modes = ["visible" if key_off is None else strip_mode(c, key_off) for c in range(nstrip)]
        live = [c for c in range(nstrip) if modes[c] != "hidden"]
        k = k_ref[0]
        vt = vt_ref[0]
        kchunk = (j * tk + lax.broadcasted_iota(jnp.int32, (tk, 1), 0)) >> 6
        s_v, p_v = {}, {}

        def scores(c):
            rows = slice(c * FLASH_STRIP, (c + 1) * FLASH_STRIP)
            s = _nt(k, q_ref[0, rows, :])
            vis = None
            if modes[c] == "masked":
                qchunk = (i * tq + c * FLASH_STRIP + lax.broadcasted_iota(jnp.int32, (1, FLASH_STRIP), 1)) >> 6
                vis = kchunk <= qchunk
                s = jnp.where(vis, s, -1e30)
            s_v[c] = (s, vis)

        def softmax(c):
            lanes = slice(c * FLASH_STRIP, (c + 1) * FLASH_STRIP)
            s, vis = s_v.pop(c)
            m_prev = m_sc[:, lanes]
            m_new = jnp.maximum(m_prev, jnp.max(s, axis=0, keepdims=True))
            alpha = jnp.exp2(m_prev - m_new)
            p = jnp.exp2(s - m_new)
            if vis is not None:
                p = jnp.where(vis, p, 0.0)
            l_sc[:, lanes] = alpha * l_sc[:, lanes] + jnp.sum(p, axis=0, keepdims=True)
            m_sc[:, lanes] = m_new
            p_v[c] = (p.astype(BF16), alpha)

        def values(c):
            lanes = slice(c * FLASH_STRIP, (c + 1) * FLASH_STRIP)
            p16, alpha = p_v.pop(c)
            acc_sc[:, lanes] = alpha * acc_sc[:, lanes] + jnp.dot(vt, p16, preferred_element_type=F32)

        for step in range(len(live) + 2):
            if step < len(live):
                scores(live[step])
            if 0 <= step - 1 < len(live):
                softmax(live[step - 1])
            if 0 <= step - 2 < len(live):
                values(live[step - 2])

    ratio = tq // tk

    @pl.when(j < i * ratio)
    def _():
        tile(None)

    for r in range(ratio):
        @pl.when(j == i * ratio + r)
        def _(r=r):
            tile(r * tk)

    @pl.when(j == (i * tq + tq - 1) // tk)
    def _():
        o_ref[0] = (acc_sc[...] / l_sc[...]).T.astype(o_ref.dtype)


def flash_attention_causal(q3, k3, vt3, *, tq=2048, tk=1024):
    bsz, s, _ = q3.shape
    tq, tk = min(tq, s), min(tk, s)
    assert s % tq == 0 and tq % tk == 0 and tq % FLASH_STRIP == 0 and tk % CHUNK == 0
    pairs = [(i, j) for i in range(s // tq) for j in range((i * tq + tq - 1) // tk + 1)]
    qi = jnp.asarray([p[0] for p in pairs], jnp.int32)
    kj = jnp.asarray([p[1] for p in pairs], jnp.int32)
    grid_spec = pltpu.PrefetchScalarGridSpec(
        num_scalar_prefetch=2,
        grid=(bsz, MLA_H, len(pairs)),
        in_specs=[pl.BlockSpec((1, tq, QSLOT), lambda b, h, p, qi, kj: (b, qi[p], h)),
                  pl.BlockSpec((1, tk, QSLOT), lambda b, h, p, qi, kj: (b, kj[p], h)),
                  pl.BlockSpec((1, MLA_V, tk), lambda b, h, p, qi, kj: (b, h, kj[p]))],
        out_specs=pl.BlockSpec((1, tq, MLA_V), lambda b, h, p, qi, kj: (b, qi[p], h)),
        scratch_shapes=[pltpu.VMEM((1, tq), F32), pltpu.VMEM((1, tq), F32), pltpu.VMEM((MLA_V, tq), F32)])
    return pl.pallas_call(
        functools.partial(_flash_t_kernel, tq=tq, tk=tk),
        out_shape=jax.ShapeDtypeStruct((bsz, s, BRANCH), BF16),
        grid_spec=grid_spec,
        compiler_params=_cp(("parallel", "parallel", "arbitrary")),
        name="flash_attention_causal",
    )(qi, kj, q3, k3, vt3)


def _vt_kernel(w_ref, x_ref, o_ref):
    o_ref[0] = _nt(w_ref[...], x_ref[0]).astype(o_ref.dtype)


def v_up_transposed(w_vt, latkr3, *, tn=512):
    bsz, s, _ = latkr3.shape
    tn = min(tn, s)
    nv = w_vt.shape[0]
    return pl.pallas_call(
        _vt_kernel,
        out_shape=jax.ShapeDtypeStruct((bsz, nv, s), BF16),
        grid=(bsz, s // tn),
        in_specs=[pl.BlockSpec((nv, LATKR), lambda b, t: (0, 0)),
                  pl.BlockSpec((1, tn, LATKR), lambda b, t: (b, t, 0))],
        out_specs=pl.BlockSpec((1, nv, tn), lambda b, t: (b, 0, t)),
        compiler_params=_cp(("parallel", "parallel")),
        name="v_up_transposed",
    )(w_vt, latkr3)


def _branch_kernel(oa_ref, ob_ref, oc_ref, w_ref, g0_ref, g1_ref, g2_ref, out_ref):
    gates = [jax.nn.sigmoid(g_ref[...]) for g_ref in (g0_ref, g1_ref, g2_ref)]
    us = [jnp.dot(o_ref[...], w_ref[n], preferred_element_type=F32)
          for n, o_ref in enumerate((oa_ref, ob_ref, oc_ref))]
    out_ref[...] = (gates[0] * us[0] + gates[1] * us[1] + gates[2] * us[2]).astype(out_ref.dtype)


def branch_merge(oa, ob, oc, w_branch, p):
    m = oa.shape[0]
    tm = min(1024, m)
    tn = 512
    nj = D_MODEL // tn
    o_spec = pl.BlockSpec((tm, BRANCH), lambda i, j: (i, 0))
    gate = lambda n: pl.BlockSpec((tm, tn), lambda i, j: (i, P_GATE // tn + n * nj + j))
    return pl.pallas_call(
        _branch_kernel,
        out_shape=jax.ShapeDtypeStruct((m, D_MODEL), BF16),
        grid=(m // tm, nj),
        in_specs=[o_spec, o_spec, o_spec,
                  pl.BlockSpec((3, BRANCH, tn), lambda i, j: (0, 0, j)),
                  gate(0), gate(1), gate(2)],
        out_specs=pl.BlockSpec((tm, tn), lambda i, j: (i, j)),
        compiler_params=_cp(("parallel", "parallel")),
        name="branch_merge",
    )(oa, ob, oc, w_branch, p, p, p)


def _pack_layer(l, w):
    w_in = w['w_in'][l]
    seg = lambda a, n: w_in[:, a:a + n]
    zeros = lambda n: jnp.zeros((D_MODEL, n), w_in.dtype)
    w_in_p = jnp.concatenate([
        seg(8032, 12288),
        seg(6688, 768), seg(7456, 512),
        seg(7968, 64), seg(4096, 8), seg(4104, 8), seg(6672, 16), zeros(32), zeros(128),
        seg(5136, 1536), seg(0, 3072), seg(3072, 1024), seg(4112, 1024)], axis=1).astype(BF16)
    w_uq = jnp.pad(w['mla_w_uq'][l], ((0, 0), (0, 0), (0, QSLOT - MLA_NOPE - MLA_ROPE)))
    w_uq = w_uq.reshape(MLA_Q_RANK, MLA_H * QSLOT).astype(BF16)
    wk = jnp.pad(w['mla_w_uk'][l], ((0, 0), (0, 0), (0, QSLOT - MLA_NOPE))).reshape(MLA_KV_RANK, MLA_H * QSLOT)
    eye = jnp.broadcast_to(jnp.eye(MLA_ROPE, dtype=F32)[:, None, :], (MLA_ROPE, MLA_H, MLA_ROPE))
    wkr = jnp.pad(eye, ((0, 0), (0, 0), (MLA_NOPE, QSLOT - MLA_NOPE - MLA_ROPE))).reshape(MLA_ROPE, MLA_H * QSLOT)
    wk = jnp.concatenate([wk, wkr, jnp.zeros((LATKR - MLA_KV_RANK - MLA_ROPE, MLA_H * QSLOT), F32)], axis=0)
    wv = jnp.pad(w['mla_w_uv'][l].reshape(MLA_KV_RANK, MLA_H * MLA_V), ((0, LATKR - MLA_KV_RANK), (0, 0)))
    w_kv = jnp.concatenate([wk, wv], axis=1).astype(BF16)
    w_k = wk.astype(BF16)
    w_vt = wv.T.astype(BF16)

    def lane_row(vals, off):
        return jnp.zeros((1, 128), F32).at[0, off:off + vals.shape[0]].set(vals.astype(F32))

    return dict(
        attn_norm=w['attn_norm'][l], w_in=w_in_p,
        gdn_conv_w=w['gdn_conv_w'][l], gdn_alog=lane_row(w['gdn_A_log'][l], L_ALPHA),
        gdn_dtb=lane_row(w['gdn_dt_bias'][l], L_ALPHA), gdn_norm=w['gdn_norm'][l],
        ssm_conv_w=w['ssm_conv_w'][l], ssm_conv_b=w['ssm_conv_b'][l],
        ssm_alog=lane_row(w['ssm_A_log'][l], L_DT), ssm_dtb=lane_row(w['ssm_dt_bias'][l], L_DT),
        ssm_d=jnp.repeat(w['ssm_D'][l].astype(F32), SSM_P).reshape(1, BRANCH), ssm_norm=w['ssm_norm'][l],
        mla_q_norm=w['mla_q_norm'][l], w_uq=w_uq, mla_kv_norm=w['mla_kv_norm'][l], w_kv=w_kv, w_k=w_k, w_vt=w_vt,
        w_branch=w['w_branch'][l].astype(BF16), w_out=w['w_out'][l].astype(BF16),
        ffn_norm=w['ffn_norm'][l], w_up_il=interleave_cast_w_up(w['ffn_w_up'], l),
        ffn_conv_w=w['ffn_conv_w'][l], ffn_conv_b=w['ffn_conv_b'][l],
        w_down=w['ffn_w_down'][l].astype(BF16))


def _rope_tables(pos):
    half = MLA_ROPE // 2
    inv = ROPE_THETA ** (-jnp.arange(half, dtype=F32) / half)
    ang = pos.astype(F32)[:, None] * inv[None, :]
    cos, sin = jnp.cos(ang), jnp.sin(ang)
    n = pos.shape[0]
    z = lambda k: jnp.zeros((n, k), F32)
    cq = jnp.concatenate([jnp.ones((n, MLA_NOPE), F32), cos, cos, z(64)], axis=1)
    saq = jnp.concatenate([z(MLA_NOPE + half), sin, z(64)], axis=1)
    sbq = jnp.concatenate([z(MLA_NOPE), -sin, z(half + 64)], axis=1)
    ck = jnp.concatenate([cos, cos, z(64)], axis=1)
    sak = jnp.concatenate([z(half), sin, z(64)], axis=1)
    sbk = jnp.concatenate([-sin, z(half + 64)], axis=1)
    return cq, saq, sbq, ck, sak, sbk


def _pad_state(st):
    return jnp.pad(st, ((0, 0), (8 - st.shape[1], 0), (0, 0)))


def _layer(x, tabs, q_off, state, lw):
    latkr_past, gconv, gssm, mconv, mssm, fconv = state
    bsz, s, _ = x.shape
    m = bsz * s
    cq_t, saq_t, sbq_t, ck_t, sak_t, sbk_t = tabs
    x2 = x.reshape(m, D_MODEL)
    mm_tm = 512

    p = rmsnorm_matmul(x2, lw['attn_norm'], lw['w_in'], F32, tm=mm_tm, tn=1024, name="in_proj")
    p3 = p.reshape(bsz, s, P_TOT)

    qkv_act, gconv8 = causal_conv(p3, P_QKV, GDN_QKV, _pad_state(gconv), lw['gdn_conv_w'],
                                  jnp.zeros((GDN_QKV,), F32), act=True)
    o_a, gssm_new = gdn(qkv_act, p3, gssm, lw['gdn_alog'], lw['gdn_dtb'], lw['gdn_norm'])

    xbc_act, mconv8 = causal_conv(p3, P_XBC, SSM_CONV_DIM, _pad_state(mconv), lw['ssm_conv_w'],
                                  lw['ssm_conv_b'], act=True)
    o_b, mssm_new = ssd(xbc_act, p3, mssm.reshape(bsz, SSM_H // 2, 128, 128), lw['ssm_alog'], lw['ssm_dtb'],
                        lw['ssm_d'], lw['ssm_norm'])
    mssm_new = mssm_new.reshape(bsz, SSM_H, SSM_P, SSM_N)

    cqn, lat, kr, latkr = mla_prep(p, lw['mla_q_norm'], lw['mla_kv_norm'], ck_t, sak_t, sbk_t)
    tq_tab = min(mm_tm, m)
    q_full = matmul(cqn, lw['w_uq'], BF16, tm=mm_tm, tn=MLA_H * QSLOT, name="q_up",
                    extras=[(cq_t, (tq_tab, QSLOT), lambda i, j: (i, 0)),
                            (saq_t, (tq_tab, QSLOT), lambda i, j: (i, 0)),
                            (sbq_t, (tq_tab, QSLOT), lambda i, j: (i, 0))],
                    epilogue=_q_rope_epilogue)
    latkr3 = latkr.reshape(bsz, s, LATKR)
    q3 = q_full.reshape(bsz, s, MLA_H * QSLOT)
    if latkr_past is None and q_off == 0 and s % 512 == 0:
        k_full = matmul(latkr, lw['w_k'], BF16, tm=mm_tm, tn=1024, name="k_up")
        vt = v_up_transposed(lw['w_vt'], latkr3)
        o_c = flash_attention_causal(q3, k_full.reshape(bsz, s, MLA_H * QSLOT), vt)
    else:
        past = jnp.zeros((bsz, 0, LATKR), BF16) if latkr_past is None else latkr_past
        nk_real = past.shape[1] + s
        nk_pad = -(-nk_real // 128) * 128
        latkr_all = jnp.concatenate([past, latkr3, jnp.zeros((bsz, nk_pad - nk_real, LATKR), BF16)], axis=1)
        kv = matmul(latkr_all.reshape(bsz * nk_pad, LATKR), lw['w_kv'], BF16, tm=2048, tn=1024, name="kv_up")
        o_c = flash_attention(q3, kv.reshape(bsz, nk_pad, -1), q_off=q_off, nk_real=nk_real)

    merged = branch_merge(o_a.reshape(m, BRANCH), o_b.reshape(m, BRANCH), o_c.reshape(m, BRANCH),
                          lw['w_branch'], p)
    x2 = matmul_residual(merged, lw['w_out'], x2, tm=mm_tm, tn=1024, name="out_proj")

    h2 = rmsnorm(x2, lw['ffn_norm'], BF16)
    if s % 512 == 0:
        act, fconv8 = ffn_up_fused(h2, lw['w_up_il'], _pad_state(fconv), lw['ffn_conv_w'], lw['ffn_conv_b'],
                                   bsz=bsz, tm=1024 if s % 1024 == 0 else 512)
    else:
        up = matmul_tile_major(h2, lw['w_up_il'])
        act, fconv8 = ffn_conv_gate_il(up.reshape(bsz, s, 2 * D_FF), _pad_state(fconv), lw['ffn_conv_w'],
                                       lw['ffn_conv_b'])
    x2 = matmul_residual(act.reshape(m, D_FF), lw['w_down'], x2, tm=mm_tm, tn=1024, tk=D_FF // 2,
                         name="ffn_down")

    new = (lat.reshape(bsz, s, MLA_KV_RANK), kr.reshape(bsz, s, MLA_ROPE),
           gconv8[:, 8 - gconv.shape[1]:], gssm_new, mconv8[:, 8 - mconv.shape[1]:], mssm_new,
           fconv8[:, 8 - fconv.shape[1]:])
    return x2.reshape(bsz, s, D_MODEL), new


def kernel(x_prompt, x_sample, cache_mla_latent, cache_mla_krope, state_gdn_conv, state_gdn_ssm, state_ssm_conv, state_ssm, state_ffn_conv, attn_norm, w_in, gdn_conv_w, gdn_A_log, gdn_dt_bias, gdn_norm, ssm_conv_w, ssm_conv_b, ssm_A_log, ssm_dt_bias, ssm_D, ssm_norm, mla_q_norm, mla_w_uq, mla_kv_norm, mla_w_uk, mla_w_uv, w_branch, w_out, ffn_norm, ffn_w_up, ffn_conv_w, ffn_conv_b, ffn_w_down, final_norm):
    weights = dict(attn_norm=attn_norm, w_in=w_in, gdn_conv_w=gdn_conv_w, gdn_A_log=gdn_A_log,
                   gdn_dt_bias=gdn_dt_bias, gdn_norm=gdn_norm, ssm_conv_w=ssm_conv_w, ssm_conv_b=ssm_conv_b,
                   ssm_A_log=ssm_A_log, ssm_dt_bias=ssm_dt_bias, ssm_D=ssm_D, ssm_norm=ssm_norm,
                   mla_q_norm=mla_q_norm, mla_w_uq=mla_w_uq, mla_kv_norm=mla_kv_norm, mla_w_uk=mla_w_uk,
                   mla_w_uv=mla_w_uv, w_branch=w_branch, w_out=w_out, ffn_norm=ffn_norm, ffn_w_up=ffn_w_up,
                   ffn_conv_w=ffn_conv_w, ffn_conv_b=ffn_conv_b, ffn_w_down=ffn_w_down)
    depth = w_in.shape[0]
    bp, sp, _ = x_prompt.shape
    bs, ss, _ = x_sample.shape
    n_past = cache_mla_latent.shape[2]
    pos_p = jnp.arange(sp, dtype=jnp.int32)
    pos_s = n_past + jnp.arange(ss, dtype=jnp.int32)
    tabs_p = _rope_tables(pos_p)
    tabs_s = tuple(jnp.tile(t, (bs, 1)) for t in _rope_tables(pos_s))
    zero_state = (
        None,
        jnp.zeros((bp,) + state_gdn_conv.shape[2:], F32),
        jnp.zeros((bp,) + state_gdn_ssm.shape[2:], F32),
        jnp.zeros((bp,) + state_ssm_conv.shape[2:], F32),
        jnp.zeros((bp,) + state_ssm.shape[2:], F32),
        jnp.zeros((bp,) + state_ffn_conv.shape[2:], F32),
    )
    xp, xs = x_prompt, x_sample
    new_p, new_s = [], []
    for l in range(depth):
        lw = _pack_layer(l, weights)
        xp, st_p = _layer(xp, tabs_p, 0, zero_state, lw)
        latkr_past = jnp.concatenate(
            [cache_mla_latent[l], cache_mla_krope[l],
             jnp.zeros((bs, n_past, LATKR - MLA_KV_RANK - MLA_ROPE), F32)], axis=-1).astype(BF16)
        st_in = (latkr_past, state_gdn_conv[l], state_gdn_ssm[l], state_ssm_conv[l], state_ssm[l],
                 state_ffn_conv[l])
        xs, st_s = _layer(xs, tabs_s, n_past, st_in, lw)
        new_p.append(st_p)
        new_s.append(st_s)
    y_prompt = rmsnorm(xp.reshape(bp * sp, D_MODEL), final_norm, F32).reshape(bp, sp, D_MODEL)
    y_sample = rmsnorm(xs.reshape(bs * ss, D_MODEL), final_norm, F32).reshape(bs, ss, D_MODEL)
    outs_p = [jnp.stack(t) for t in zip(*new_p)]
    outs_s = [jnp.stack(t) for t in zip(*new_s)]
    return (y_prompt, y_sample, *outs_p, *outs_s)
```

```python
import functools
import math

import jax
import jax.numpy as jnp
from jax import lax
from jax.experimental import pallas as pl
from jax.experimental.pallas import tpu as pltpu

F32 = jnp.float32
BF16 = jnp.bfloat16
HI = lax.Precision.HIGHEST

EPS = 1e-6
CHUNK = 64
D_MODEL = 4096
BRANCH = 1024
GDN_H, GDN_D = 8, 128
GDN_QKV = 3072
SSM_H, SSM_P, SSM_G, SSM_N = 16, 64, 2, 128
SSM_CONV_DIM = 1536
MLA_H, MLA_NOPE, MLA_ROPE, MLA_V = 8, 128, 64, 128
MLA_Q_RANK, MLA_KV_RANK = 768, 512
ROPE_THETA = 10000.0
D_FF = 11008
LATKR = 640
QSLOT = 256

P_GATE = 0
P_MLA = 12288
P_SMALL = P_MLA + 1280
P_XBC = 13824
P_QKV = 15360
P_ZA = 18432
P_ZB = 19456
P_TOT = 20480
L_BETA, L_ALPHA, L_DT = 64, 72, 80

VMEM_LIMIT = 56 * 1024 * 1024


def _cp(sem):
    return pltpu.CompilerParams(dimension_semantics=sem, vmem_limit_bytes=VMEM_LIMIT)


def _softplus(x):
    return jnp.maximum(x, 0.0) + jnp.log1p(jnp.exp(-jnp.abs(x)))


def _silu(x):
    return x * jax.nn.sigmoid(x)


def _nt(a, b):
    return lax.dot_general(a, b, (((1,), (1,)), ((), ())), preferred_element_type=F32)


def _tn(a, b):
    return lax.dot_general(a, b, (((0,), (0,)), ((), ())), preferred_element_type=F32)


def _rms_kernel(x_ref, w_ref, o_ref):
    x = x_ref[...]
    ms = jnp.mean(x * x, axis=-1, keepdims=True)
    o_ref[...] = (x * lax.rsqrt(ms + EPS) * w_ref[...]).astype(o_ref.dtype)


def rmsnorm(x, w, out_dtype):
    m, d = x.shape
    tm = min(256, m)
    return pl.pallas_call(
        _rms_kernel,
        out_shape=jax.ShapeDtypeStruct((m, d), out_dtype),
        grid=(m // tm,),
        in_specs=[pl.BlockSpec((tm, d), lambda i: (i, 0)),
                  pl.BlockSpec((1, d), lambda i: (0, 0))],
        out_specs=pl.BlockSpec((tm, d), lambda i: (i, 0)),
        compiler_params=_cp(("parallel",)),
        name="rmsnorm",
    )(x, w.reshape(1, d))


def _mm_kernel(*refs, nk, n_extra, epilogue):
    a_ref, b_ref = refs[0], refs[1]
    extras = refs[2:2 + n_extra]
    o_ref = refs[2 + n_extra]
    part = jnp.dot(a_ref[...], b_ref[...], preferred_element_type=F32)
    if nk == 1:
        o_ref[...] = epilogue(part, *extras).astype(o_ref.dtype)
        return
    acc_ref = refs[3 + n_extra]
    k = pl.program_id(2)

    @pl.when(k == 0)
    def _():
        acc_ref[...] = part

    @pl.when(k > 0)
    def _():
        acc_ref[...] += part

    @pl.when(k == nk - 1)
    def _():
        o_ref[...] = epilogue(acc_ref[...], *extras).astype(o_ref.dtype)


def matmul(a, b, out_dtype, *, tm, tn, tk=None, extras=(), epilogue=None, name="matmul"):
    m, kd = a.shape
    n = b.shape[1]
    tm, tn = min(tm, m), min(tn, n)
    while m % tm:
        tm //= 2
    tk = kd if tk is None else tk
    nk = kd // tk
    assert m % tm == 0 and n % tn == 0 and kd % tk == 0
    if epilogue is None:
        epilogue = lambda acc: acc
    in_specs = [pl.BlockSpec((tm, tk), lambda i, j, k: (i, k)),
                pl.BlockSpec((tk, tn), lambda i, j, k: (k, j))]
    args = [a, b]
    for arr, blk, imap in extras:
        in_specs.append(pl.BlockSpec(blk, functools.partial(lambda i, j, k, f: f(i, j), f=imap)))
        args.append(arr)
    scratch = [pltpu.VMEM((tm, tn), F32)] if nk > 1 else []
    return pl.pallas_call(
        functools.partial(_mm_kernel, nk=nk, n_extra=len(extras), epilogue=epilogue),
        out_shape=jax.ShapeDtypeStruct((m, n), out_dtype),
        grid=(m // tm, n // tn, nk),
        in_specs=in_specs,
        out_specs=pl.BlockSpec((tm, tn), lambda i, j, k: (i, j)),
        scratch_shapes=scratch,
        compiler_params=_cp(("parallel", "parallel", "arbitrary")),
        name=name,
    )(*args)


def _norm_mm_kernel(x_ref, nw_ref, b_ref, o_ref, a16_ref):
    @pl.when(pl.program_id(1) == 0)
    def _():
        x = x_ref[...]
        ms = jnp.mean(x * x, axis=-1, keepdims=True)
        a16_ref[...] = (x * lax.rsqrt(ms + EPS) * nw_ref[...]).astype(a16_ref.dtype)

    o_ref[...] = jnp.dot(a16_ref[...], b_ref[...], preferred_element_type=F32).astype(o_ref.dtype)


def rmsnorm_matmul(x, nw, b, out_dtype, *, tm, tn, name):
    m, kd = x.shape
    n = b.shape[1]
    tm = min(tm, m)
    assert m % tm == 0 and n % tn == 0
    return pl.pallas_call(
        _norm_mm_kernel,
        out_shape=jax.ShapeDtypeStruct((m, n), out_dtype),
        grid=(m // tm, n // tn),
        in_specs=[pl.BlockSpec((tm, kd), lambda i, j: (i, 0)),
                  pl.BlockSpec((1, kd), lambda i, j: (0, 0)),
                  pl.BlockSpec((kd, tn), lambda i, j: (0, j))],
        out_specs=pl.BlockSpec((tm, tn), lambda i, j: (i, j)),
        scratch_shapes=[pltpu.VMEM((tm, kd), BF16)],
        compiler_params=_cp(("parallel", "arbitrary")),
        name=name,
    )(x, nw.reshape(1, kd), b)


def _residual_epilogue(acc, r_ref):
    return acc + r_ref[...]


def matmul_residual(a, b, res, *, tm, tn, tk=None, name="matmul_residual"):
    tm_, tn_ = min(tm, a.shape[0]), min(tn, b.shape[1])
    return matmul(a, b, F32, tm=tm, tn=tn, tk=tk, name=name,
                  extras=[(res, (tm_, tn_), lambda i, j: (i, j))], epilogue=_residual_epilogue)


def _q_rope_epilogue(acc, c_ref, sa_ref, sb_ref):
    c, sa, sb = c_ref[...], sa_ref[...], sb_ref[...]
    scale = (MLA_NOPE + MLA_ROPE) ** -0.5 * math.log2(math.e)
    outs = []
    for h in range(MLA_H):
        xs = acc[:, h * QSLOT:(h + 1) * QSLOT]
        y = xs * c + pltpu.roll(xs, 32, 1) * sa + pltpu.roll(xs, QSLOT - 32, 1) * sb
        outs.append(y * scale)
    return jnp.concatenate(outs, axis=1)


def _conv_kernel(x_ref, st_ref, w_ref, b_ref, y_ref, ns_ref, xp_ref, *, width, ts, act):
    t = pl.program_id(2)
    nt = pl.num_programs(2)

    @pl.when(t == 0)
    def _():
        xp_ref[0:8, :] = st_ref[0]

    @pl.when(t > 0)
    def _():
        xp_ref[0:8, :] = xp_ref[ts:ts + 8, :]

    xp_ref[8:8 + ts, :] = x_ref[0]
    off = 8 - (width - 1)
    y = xp_ref[off:off + ts, :] * w_ref[0:1, :]
    for j in range(1, width):
        y = y + xp_ref[off + j:off + j + ts, :] * w_ref[j:j + 1, :]
    y = y + b_ref[...]
    if act:
        y = _silu(y)
    y_ref[0] = y.astype(y_ref.dtype)

    @pl.when(t == nt - 1)
    def _():
        ns_ref[0] = xp_ref[ts:ts + 8, :]


def causal_conv(x3, col_off, chans, state8, w, b, *, act, ct=512):
    bsz, s, _ = x3.shape
    width = w.shape[0]
    ts = min(512, s)
    assert chans % ct == 0 and col_off % ct == 0 and s % ts == 0
    cb0 = col_off // ct
    return pl.pallas_call(
        functools.partial(_conv_kernel, width=width, ts=ts, act=act),
        out_shape=(jax.ShapeDtypeStruct((bsz, s, chans), F32),
                   jax.ShapeDtypeStruct((bsz, 8, chans), F32)),
        grid=(bsz, chans // ct, s // ts),
        in_specs=[pl.BlockSpec((1, ts, ct), lambda bi, c, t: (bi, t, cb0 + c)),
                  pl.BlockSpec((1, 8, ct), lambda bi, c, t: (bi, 0, c)),
                  pl.BlockSpec((width, ct), lambda bi, c, t: (0, c)),
                  pl.BlockSpec((1, ct), lambda bi, c, t: (0, c))],
        out_specs=(pl.BlockSpec((1, ts, ct), lambda bi, c, t: (bi, t, c)),
                   pl.BlockSpec((1, 8, ct), lambda bi, c, t: (bi, 0, c))),
        scratch_shapes=[pltpu.VMEM((ts + 8, ct), F32)],
        compiler_params=_cp(("parallel", "parallel", "arbitrary")),
        name="causal_conv",
    )(x3, state8, w, b.reshape(1, chans))


HALO = 16


FFN_TN = 256


def _interleave_ffn(x):
    lead = x.shape[:-1]
    x = x.reshape(*lead, 2, D_FF // FFN_TN, FFN_TN)
    return jnp.swapaxes(x, -3, -2).reshape(*lead, 2 * D_FF)


def _deinterleave_ffn(x):
    lead = x.shape[:-1]
    x = x.reshape(*lead, D_FF // FFN_TN, 2, FFN_TN)
    return jnp.swapaxes(x, -3, -2).reshape(*lead, 2 * D_FF)


def _il_cast_kernel(a_ref, b_ref, o_ref):
    o_ref[0, :, :FFN_TN] = a_ref[0].astype(o_ref.dtype)
    o_ref[0, :, FFN_TN:] = b_ref[0].astype(o_ref.dtype)


def interleave_cast_w_up(w_all, l, *, tk=1024):
    kd = w_all.shape[1]
    nc = D_FF // FFN_TN
    return pl.pallas_call(
        _il_cast_kernel,
        out_shape=jax.ShapeDtypeStruct((nc, kd, 2 * FFN_TN), BF16),
        grid=(kd // tk, nc),
        in_specs=[pl.BlockSpec((1, tk, FFN_TN), lambda k, j: (l, k, j)),
                  pl.BlockSpec((1, tk, FFN_TN), lambda k, j: (l, k, nc + j))],
        out_specs=pl.BlockSpec((1, tk, 2 * FFN_TN), lambda k, j: (j, k, 0)),
        compiler_params=_cp(("parallel", "parallel")),
        name="interleave_cast_w_up",
    )(w_all, w_all)


def _mm_tiles_kernel(a_ref, w_ref, o_ref):
    o_ref[...] = jnp.dot(a_ref[...], w_ref[0], preferred_element_type=F32)


def matmul_tile_major(a, w_tiles):
    m, kd = a.shape
    nt, _, tn = w_tiles.shape
    return pl.pallas_call(
        _mm_tiles_kernel,
        out_shape=jax.ShapeDtypeStruct((m, nt * tn), F32),
        grid=(nt,),
        in_specs=[pl.BlockSpec((m, kd), lambda j: (0, 0)),
                  pl.BlockSpec((1, kd, tn), lambda j: (j, 0, 0))],
        out_specs=pl.BlockSpec((m, tn), lambda j: (0, j)),
        compiler_params=_cp(("parallel",)),
        name="ffn_up",
    )(a, w_tiles)


def _ffn_conv_gate_il_kernel(x_ref, st_ref, cw_ref, cb_ref, y_ref, ns_ref, xp_ref, *, width, s):
    xp_ref[0:8, :] = st_ref[0]
    xp_ref[8:8 + s, :] = x_ref[0]
    off = 8 - (width - 1)
    y = xp_ref[off:off + s, :] * cw_ref[0:1, :]
    for t in range(1, width):
        y = y + xp_ref[off + t:off + t + s, :] * cw_ref[t:t + 1, :]
    y = y + cb_ref[...]
    ns_ref[0] = xp_ref[s:s + 8, :]
    for c in range(D_FF // FFN_TN):
        g = y[:, 2 * c * FFN_TN:(2 * c + 1) * FFN_TN]
        v = y[:, (2 * c + 1) * FFN_TN:(2 * c + 2) * FFN_TN]
        y_ref[0, :, c * FFN_TN:(c + 1) * FFN_TN] = (_silu(g) * v).astype(y_ref.dtype)


def ffn_conv_gate_il(up3, state8, cw, cb):
    bsz, s, n2 = up3.shape
    width = cw.shape[0]
    y, ns = pl.pallas_call(
        functools.partial(_ffn_conv_gate_il_kernel, width=width, s=s),
        out_shape=(jax.ShapeDtypeStruct((bsz, s, D_FF), BF16),
                   jax.ShapeDtypeStruct((bsz, 8, n2), F32)),
        grid=(bsz,),
        in_specs=[pl.BlockSpec((1, s, n2), lambda b: (b, 0, 0)),
                  pl.BlockSpec((1, 8, n2), lambda b: (b, 0, 0)),
                  pl.BlockSpec((width, n2), lambda b: (0, 0)),
                  pl.BlockSpec((1, n2), lambda b: (0, 0))],
        out_specs=(pl.BlockSpec((1, s, D_FF), lambda b: (b, 0, 0)),
                   pl.BlockSpec((1, 8, n2), lambda b: (b, 0, 0))),
        scratch_shapes=[pltpu.VMEM((s + 8, n2), F32)],
        compiler_params=_cp(("parallel",)),
        name="ffn_conv_gate_il",
    )(up3, _interleave_ffn(state8), _interleave_ffn(cw), _interleave_ffn(cb.reshape(1, n2)))
    return y, _deinterleave_ffn(ns)


FFN_SUB = 512


def _ffn_up_fused_kernel(a_ref, halo_ref, w_ref, st_ref, cw_ref, cb_ref, y_ref, ns_ref, aext_ref, *xp_refs,
                         tm, tiles_per_seq, width):
    i = pl.program_id(0)
    j = pl.program_id(1)

    @pl.when(j == 0)
    def _():
        aext_ref[0:HALO, :] = halo_ref[...]
        aext_ref[HALO:HALO + FFN_SUB, :] = a_ref[0:FFN_SUB, :]

    nsub = tm // FFN_SUB
    xp_refs[0][...] = jnp.dot(aext_ref[...], w_ref[0], preferred_element_type=F32)
    for r in range(1, nsub):
        xp_refs[r][...] = jnp.dot(a_ref[r * FFN_SUB - HALO:(r + 1) * FFN_SUB, :], w_ref[0],
                                  preferred_element_type=F32)
    first = (i % tiles_per_seq) == 0
    xp_refs[0][HALO - 8:HALO, :] = jnp.where(first, st_ref[0], xp_refs[0][HALO - 8:HALO, :])
    off = HALO - (width - 1)
    cw = cw_ref[...]
    cb = cb_ref[...]
    for r in range(nsub):
        xp_ref = xp_refs[r]
        y = xp_ref[off:off + FFN_SUB, :] * cw[0:1, :]
        for t in range(1, width):
            y = y + xp_ref[off + t:off + t + FFN_SUB, :] * cw[t:t + 1, :]
        y = y + cb
        y_ref[r * FFN_SUB:(r + 1) * FFN_SUB, :] = (_silu(y[:, :FFN_TN]) * y[:, FFN_TN:]).astype(y_ref.dtype)
    ns_ref[0] = xp_refs[nsub - 1][HALO + FFN_SUB - 8:HALO + FFN_SUB, :]


def ffn_up_fused(h2, w_up_il, state8, cw, cb, *, bsz, tm=1024):
    m, kd = h2.shape
    s = m // bsz
    tm = min(tm, s)
    assert s % tm == 0 and tm % FFN_SUB == 0
    tiles_per_seq = s // tm
    width = cw.shape[0]
    nc = D_FF // FFN_TN
    tn2 = 2 * FFN_TN
    y, ns = pl.pallas_call(
        functools.partial(_ffn_up_fused_kernel, tm=tm, tiles_per_seq=tiles_per_seq, width=width),
        out_shape=(jax.ShapeDtypeStruct((m, D_FF), BF16),
                   jax.ShapeDtypeStruct((m // tm, 8, 2 * D_FF), F32)),
        grid=(m // tm, nc),
        in_specs=[pl.BlockSpec((tm, kd), lambda i, j: (i, 0), pipeline_mode=pl.Buffered(1)),
                  pl.BlockSpec((HALO, kd), lambda i, j: (jnp.maximum(i * (tm // HALO) - 1, 0), 0)),
                  pl.BlockSpec((1, kd, tn2), lambda i, j: (j, 0, 0)),
                  pl.BlockSpec((1, 8, tn2), lambda i, j: (i // tiles_per_seq, 0, j)),
                  pl.BlockSpec((width, tn2), lambda i, j: (0, j)),
                  pl.BlockSpec((1, tn2), lambda i, j: (0, j))],
        out_specs=(pl.BlockSpec((tm, FFN_TN), lambda i, j: (i, j)),
                   pl.BlockSpec((1, 8, tn2), lambda i, j: (i, 0, j))),
        scratch_shapes=[pltpu.VMEM((FFN_SUB + HALO, kd), BF16)]
        + [pltpu.VMEM((FFN_SUB + HALO, tn2), F32) for _ in range(tm // FFN_SUB)],
        compiler_params=_cp(("arbitrary", "arbitrary")),
        name="ffn_up_fused",
    )(h2, h2, w_up_il, _interleave_ffn(state8), _interleave_ffn(cw), _interleave_ffn(cb.reshape(1, 2 * D_FF)))
    return y, _deinterleave_ffn(ns[tiles_per_seq - 1::tiles_per_seq])


def _chunk_masks(lt, c):
    sh = int(math.log2(c))
    ri = lax.broadcasted_iota(jnp.int32, (lt, lt), 0)
    ci = lax.broadcasted_iota(jnp.int32, (lt, lt), 1)
    same = (ri >> sh) == (ci >> sh)
    return ri, ci, same


def _split3(x):
    hi = x.astype(BF16)
    r = x - hi.astype(F32)
    mid = r.astype(BF16)
    lo = (r - mid.astype(F32)).astype(BF16)
    return hi, mid, lo


def _dot_exact_lhs(l16, x):
    hi, mid, lo = _split3(x)
    return (jnp.dot(l16, hi, preferred_element_type=F32) + jnp.dot(l16, mid, preferred_element_type=F32)
            + jnp.dot(l16, lo, preferred_element_type=F32))


def _dot_hi(a, b):
    a_hi = a.astype(BF16)
    a_lo = (a - a_hi.astype(F32)).astype(BF16)
    b_hi = b.astype(BF16)
    b_lo = (b - b_hi.astype(F32)).astype(BF16)
    return (jnp.dot(a_hi, b_hi, preferred_element_type=F32) + jnp.dot(a_hi, b_lo, preferred_element_type=F32)
            + jnp.dot(a_lo, b_hi, preferred_element_type=F32))


def _gdn_kernel(q_ref, k_ref, v_ref, sm_ref, z_ref, s0_ref, alog_ref, dtb_ref, nw_ref,
                o_ref, sout_ref, s_sc, vn_sc, *, lt, c, hps):
    hg = pl.program_id(1)
    t = pl.program_id(2)
    nt = pl.num_programs(2)

    @pl.when(t == 0)
    def _():
        s_sc[...] = s0_ref[0]

    sm = sm_ref[0]
    lane = lax.broadcasted_iota(jnp.int32, (1, 128), 1)
    beta_full = jax.nn.sigmoid(sm)
    g_full = -jnp.exp(alog_ref[...]) * _softplus(sm + dtb_ref[...])
    ri, ci, same = _chunk_masks(lt, c)
    tri = same & (ci <= ri)
    strict = same & (ci < ri)
    eye_b = ri == ci
    eye = jnp.where(eye_b, 1.0, 0.0)
    gc_full = _dot_exact_lhs(jnp.where(tri, 1.0, 0.0).astype(BF16), g_full)
    nw = nw_ref[...]

    hs = range(hps)
    cols = [slice(hh * GDN_D, (hh + 1) * GDN_D) for hh in hs]
    dot16 = lambda x, y: jnp.dot(x.astype(BF16), y.astype(BF16), preferred_element_type=F32)
    ks, k16s, kbs, gcs, ms, avs, qds, rhss = [], [], [], [], [], [], [], []
    for hh in hs:
        h = hg * hps + hh
        q = q_ref[0, :, cols[hh]]
        k = k_ref[0, :, cols[hh]]
        q = q * lax.rsqrt(jnp.sum(q * q, axis=-1, keepdims=True) + EPS) * (GDN_D ** -0.5)
        k = k * lax.rsqrt(jnp.sum(k * k, axis=-1, keepdims=True) + EPS)
        beta = jnp.sum(jnp.where(lane == L_BETA + h, beta_full, 0.0), axis=-1, keepdims=True)
        gc = jnp.sum(jnp.where(lane == L_ALPHA + h, gc_full, 0.0), axis=-1, keepdims=True)
        gr = jnp.sum(jnp.where(eye_b, gc, 0.0), axis=0, keepdims=True)
        decay = jnp.exp(jnp.where(tri, gc - gr, 0.0))
        kb = k * beta
        k16 = k.astype(BF16)
        ms.append(jnp.where(strict, _nt(kb.astype(BF16), k16) * decay, 0.0))
        avs.append(jnp.where(tri, _nt(q.astype(BF16), k16) * decay, 0.0).astype(BF16))
        eg = jnp.exp(gc)
        rhss.append(jnp.concatenate([v_ref[0, :, cols[hh]] * beta, kb * eg], axis=1).astype(BF16))
        qds.append((q * eg).astype(BF16))
        ks.append(k)
        gcs.append(gc)

    ps = [-m for m in ms]
    ts = [eye + p for p in ps]
    for _ in range(int(math.log2(c)) - 1):
        ps = [dot16(p, p) for p in ps]
        ts = [tt + dot16(p, tt) for p, tt in zip(ps, ts)]
    rs = [eye - _dot_hi(eye + m, tt) for m, tt in zip(ms, ts)]
    ts = [tt + dot16(tt, r) for tt, r in zip(ts, rs)]
    uws = [jnp.dot(tt.astype(BF16), rhs, preferred_element_type=F32) for tt, rhs in zip(ts, rhss)]
    us = [uw[:, :GDN_D] for uw in uws]
    ws = [uw[:, GDN_D:].astype(BF16) for uw in uws]

    ss = [s_sc[hh] for hh in hs]
    for hh in hs:
        vn_sc[hh] = us[hh].astype(BF16)
    for cc in range(lt // c):
        r0 = cc * c
        rows = slice(r0, r0 + c)
        for hh in hs:
            s16 = ss[hh].astype(BF16)
            vnew = us[hh][rows] - jnp.dot(ws[hh][rows], s16, preferred_element_type=F32)
            vn16 = vnew.astype(BF16)
            vn_sc[hh, rows, :] = vn16
            o = (jnp.dot(qds[hh][rows], s16, preferred_element_type=F32)
                 + jnp.dot(avs[hh][rows, :], vn_sc[hh], preferred_element_type=F32))
            glast = gcs[hh][r0 + c - 1:r0 + c, :]
            kd = (ks[hh][rows] * jnp.exp(glast - gcs[hh][rows])).astype(BF16)
            ss[hh] = ss[hh] * jnp.exp(glast) + _tn(kd, vn16)
            mean_sq = jnp.mean(o * o, axis=-1, keepdims=True)
            o = o * lax.rsqrt(mean_sq + EPS) * nw * _silu(z_ref[0, rows, cols[hh]])
            o_ref[0, rows, cols[hh]] = o.astype(o_ref.dtype)
    for hh in hs:
        s_sc[hh] = ss[hh]

    @pl.when(t == nt - 1)
    def _():
        sout_ref[0] = s_sc[...]


def gdn(qkv3, p3, s0, alog_row, dtb_row, norm_w, *, hps=4):
    bsz, s, _ = qkv3.shape
    c = min(CHUNK, s)
    lt = min(256, s)
    hw = hps * GDN_D
    ng = GDN_H // hps
    small_blk = P_SMALL // 128
    za_blk = P_ZA // hw
    blk = lambda off: pl.BlockSpec((1, lt, hw), lambda b, h, t: (b, t, off + h))
    return pl.pallas_call(
        functools.partial(_gdn_kernel, lt=lt, c=c, hps=hps),
        out_shape=(jax.ShapeDtypeStruct((bsz, s, BRANCH), BF16),
                   jax.ShapeDtypeStruct((bsz, GDN_H, GDN_D, GDN_D), F32)),
        grid=(bsz, ng, s // lt),
        in_specs=[blk(0), blk(ng), blk(2 * ng),
                  pl.BlockSpec((1, lt, 128), lambda b, h, t: (b, t, small_blk)),
                  blk(za_blk),
                  pl.BlockSpec((1, hps, GDN_D, GDN_D), lambda b, h, t: (b, h, 0, 0)),
                  pl.BlockSpec((1, 128), lambda b, h, t: (0, 0)),
                  pl.BlockSpec((1, 128), lambda b, h, t: (0, 0)),
                  pl.BlockSpec((1, 128), lambda b, h, t: (0, 0))],
        out_specs=(pl.BlockSpec((1, lt, hw), lambda b, h, t: (b, t, h)),
                   pl.BlockSpec((1, hps, GDN_D, GDN_D), lambda b, h, t: (b, h, 0, 0))),
        scratch_shapes=[pltpu.VMEM((hps, GDN_D, GDN_D), F32), pltpu.VMEM((hps, lt, GDN_D), BF16)],
        compiler_params=_cp(("parallel", "parallel", "arbitrary")),
        name="gdn",
    )(qkv3, qkv3, qkv3, p3, p3, s0, alog_row, dtb_row, norm_w.reshape(1, GDN_D))


def _ssd_kernel(x_ref, b_ref, c_ref, sm_ref, z_ref, h0_ref, alog_ref, dtb_ref, d_ref, nw_ref,
                y_ref, hout_ref, hs_sc, y_sc, *, lt, c):
    g = pl.program_id(1)
    t = pl.program_id(2)
    nt = pl.num_programs(2)
    npair = SSM_H // SSM_G // 2

    @pl.when(t == 0)
    def _():
        hs_sc[...] = h0_ref[0]

    sm = sm_ref[0]
    dt_full = _softplus(sm + dtb_ref[...])
    da = dt_full * (-jnp.exp(alog_ref[...]))
    ri, ci, same = _chunk_masks(lt, c)
    tri = same & (ci <= ri)
    eye_b = ri == ci
    acum_full = _dot_exact_lhs(jnp.where(tri, 1.0, 0.0).astype(BF16), da)
    nch = lt // c
    lasts = [acum_full[cc * c + c - 1:cc * c + c, :] for cc in range(nch)]
    alast_full = jnp.concatenate([jnp.broadcast_to(l, (c, 128)) for l in lasts], axis=0)
    ea_full = jnp.exp(acum_full)
    dtw_full = dt_full * jnp.exp(alast_full - acum_full)

    bm = b_ref[0]
    cm = c_ref[0]
    bm16 = bm.astype(BF16)
    cm16 = cm.astype(BF16)
    cb = jnp.where(tri, _nt(cm16, bm16), 0.0)
    lane = lax.broadcasted_iota(jnp.int32, (1, 128), 1)
    lo = lane < SSM_P
    rowlo = lax.broadcasted_iota(jnp.int32, (128, 1), 0) < SSM_P
    x = x_ref[0]
    z = z_ref[0]
    d_row = d_ref[...]

    def ext(full, l):
        return jnp.sum(jnp.where(lane == l, full, 0.0), axis=-1, keepdims=True)

    for j in range(npair):
        la = L_DT + g * (SSM_H // SSM_G) + 2 * j
        lb = la + 1
        ac_a, ac_b = ext(acum_full, la), ext(acum_full, lb)
        row_a = jnp.sum(jnp.where(eye_b, ac_a, 0.0), axis=0, keepdims=True)
        row_b = jnp.sum(jnp.where(eye_b, ac_b, 0.0), axis=0, keepdims=True)
        l_a = (jnp.exp(jnp.where(tri, ac_a - row_a, 0.0)) * cb).astype(BF16)
        l_b = (jnp.exp(jnp.where(tri, ac_b - row_b, 0.0)) * cb).astype(BF16)
        dt_pair = jnp.where(lo, ext(dt_full, la), ext(dt_full, lb))
        ea_pair = jnp.where(lo, ext(ea_full, la), ext(ea_full, lb))
        dtw_pair = jnp.where(lo, ext(dtw_full, la), ext(dtw_full, lb))
        xp = x[:, j * 128:(j + 1) * 128]
        xdt = xp * dt_pair
        y = (jnp.dot(l_a, jnp.where(lo, xdt, 0.0).astype(BF16), preferred_element_type=F32)
             + jnp.dot(l_b, jnp.where(lo, 0.0, xdt).astype(BF16), preferred_element_type=F32))
        xw = (xp * dtw_pair).astype(BF16)
        hs = hs_sc[j]
        yoffs = []
        for cc in range(nch):
            r0 = cc * c
            yoffs.append(_nt(cm16[r0:r0 + c], hs.astype(BF16)) * ea_pair[r0:r0 + c])
            cs = _tn(xw[r0:r0 + c], bm16[r0:r0 + c])
            al = jnp.where(rowlo, jnp.exp(ac_a[r0 + c - 1:r0 + c, :]), jnp.exp(ac_b[r0 + c - 1:r0 + c, :]))
            hs = hs * al + cs
        hs_sc[j] = hs
        yoff = yoffs[0] if nch == 1 else jnp.concatenate(yoffs, axis=0)
        ytot = (y + yoff + d_row[:, j * 128:(j + 1) * 128] * xp) * _silu(z[:, j * 128:(j + 1) * 128])
        y_sc[:, j * 128:(j + 1) * 128] = ytot

    yy = y_sc[...]
    ms = jnp.mean(yy * yy, axis=-1, keepdims=True)
    y_ref[0] = (yy * lax.rsqrt(ms + EPS) * nw_ref[...]).astype(y_ref.dtype)

    @pl.when(t == nt - 1)
    def _():
        hout_ref[0] = hs_sc[...]


def ssd(xbc3, p3, h0, alog_row, dtb_row, d_row, norm_w):
    bsz, s, _ = xbc3.shape
    c = min(CHUNK, s)
    lt = min(256, s)
    gw = BRANCH // SSM_G
    npair = SSM_H // SSM_G // 2
    small_blk = P_SMALL // 128
    zb_blk = P_ZB // gw
    return pl.pallas_call(
        functools.partial(_ssd_kernel, lt=lt, c=c),
        out_shape=(jax.ShapeDtypeStruct((bsz, s, BRANCH), BF16),
                   jax.ShapeDtypeStruct((bsz, SSM_H // 2, 128, 128), F32)),
        grid=(bsz, SSM_G, s // lt),
        in_specs=[pl.BlockSpec((1, lt, gw), lambda b, g, t: (b, t, g)),
                  pl.BlockSpec((1, lt, 128), lambda b, g, t: (b, t, BRANCH // 128 + g)),
                  pl.BlockSpec((1, lt, 128), lambda b, g, t: (b, t, BRANCH // 128 + SSM_G + g)),
                  pl.BlockSpec((1, lt, 128), lambda b, g, t: (b, t, small_blk)),
                  pl.BlockSpec((1, lt, gw), lambda b, g, t: (b, t, zb_blk + g)),
                  pl.BlockSpec((1, npair, 128, 128), lambda b, g, t: (b, g, 0, 0)),
                  pl.BlockSpec((1, 128), lambda b, g, t: (0, 0)),
                  pl.BlockSpec((1, 128), lambda b, g, t: (0, 0)),
                  pl.BlockSpec((1, gw), lambda b, g, t: (0, g)),
                  pl.BlockSpec((1, gw), lambda b, g, t: (0, g))],
        out_specs=(pl.BlockSpec((1, lt, gw), lambda b, g, t: (b, t, g)),
                   pl.BlockSpec((1, npair, 128, 128), lambda b, g, t: (b, g, 0, 0))),
        scratch_shapes=[pltpu.VMEM((npair, 128, 128), F32), pltpu.VMEM((lt, gw), F32)],
        compiler_params=_cp(("parallel", "parallel", "arbitrary")),
        name="ssd",
    )(xbc3, xbc3, xbc3, p3, p3, h0, alog_row, dtb_row, d_row, norm_w.reshape(1, BRANCH))


def _mla_prep_kernel(p_ref, qn_ref, kvn_ref, ck_ref, sak_ref, sbk_ref,
                     cq_ref, lat_ref, kr_ref, latkr_ref):
    blk = p_ref[...]
    cq = blk[:, 0:MLA_Q_RANK]
    cq = cq * lax.rsqrt(jnp.mean(cq * cq, axis=-1, keepdims=True) + EPS) * qn_ref[...]
    cq_ref[...] = cq.astype(cq_ref.dtype)
    ckv = blk[:, MLA_Q_RANK:MLA_Q_RANK + MLA_KV_RANK]
    lat = ckv * lax.rsqrt(jnp.mean(ckv * ckv, axis=-1, keepdims=True) + EPS) * kvn_ref[...]
    lat_ref[...] = lat
    sm = blk[:, 1280:1408]
    y = sm * ck_ref[...] + pltpu.roll(sm, 32, 1) * sak_ref[...] + pltpu.roll(sm, 96, 1) * sbk_ref[...]
    kr_ref[...] = y[:, 0:MLA_ROPE]
    latkr_ref[:, 0:MLA_KV_RANK] = lat.astype(latkr_ref.dtype)
    latkr_ref[:, MLA_KV_RANK:LATKR] = y.astype(latkr_ref.dtype)


def mla_prep(p, q_norm, kv_norm, ck, sak, sbk):
    m = p.shape[0]
    tm = min(256, m)
    mla_blk = P_MLA // 1536
    row = lambda n: pl.BlockSpec((tm, n), lambda i: (i, 0))
    return pl.pallas_call(
        _mla_prep_kernel,
        out_shape=(jax.ShapeDtypeStruct((m, MLA_Q_RANK), BF16),
                   jax.ShapeDtypeStruct((m, MLA_KV_RANK), F32),
                   jax.ShapeDtypeStruct((m, MLA_ROPE), F32),
                   jax.ShapeDtypeStruct((m, LATKR), BF16)),
        grid=(m // tm,),
        in_specs=[pl.BlockSpec((tm, 1536), lambda i: (i, mla_blk)),
                  pl.BlockSpec((1, MLA_Q_RANK), lambda i: (0, 0)),
                  pl.BlockSpec((1, MLA_KV_RANK), lambda i: (0, 0)),
                  row(128), row(128), row(128)],
        out_specs=(row(MLA_Q_RANK), row(MLA_KV_RANK), row(MLA_ROPE), row(LATKR)),
        compiler_params=_cp(("parallel",)),
        name="mla_prep",
    )(p, q_norm.reshape(1, -1), kv_norm.reshape(1, -1), ck, sak, sbk)


def _q_absorb_kernel(q_ref, wk_ref, o_ref):
    q = q_ref[...]
    o_ref[0, :, 0:MLA_KV_RANK] = jnp.dot(q[:, 0:MLA_NOPE], wk_ref[0],
                                         preferred_element_type=F32).astype(o_ref.dtype)
    o_ref[0, :, MLA_KV_RANK:LATKR] = q[:, MLA_NOPE:QSLOT]


def q_absorb(q_full, w_ukt):
    m = q_full.shape[0]
    return pl.pallas_call(
        _q_absorb_kernel,
        out_shape=jax.ShapeDtypeStruct((MLA_H, m, LATKR), BF16),
        grid=(MLA_H,),
        in_specs=[pl.BlockSpec((m, QSLOT), lambda h: (0, h)),
                  pl.BlockSpec((1, MLA_NOPE, MLA_KV_RANK), lambda h: (h, 0, 0))],
        out_specs=pl.BlockSpec((1, m, LATKR), lambda h: (h, 0, 0)),
        compiler_params=_cp(("parallel",)),
        name="q_absorb",
    )(q_full, w_ukt)


def _latent_attn_kernel(q_ref, kv_ref, wv_ref, o_ref, *, s, q_off, nk_real):
    kv = kv_ref[0]
    nk = kv.shape[0]
    q = q_ref[...].reshape(MLA_H * s, LATKR)
    sc = _nt(q, kv)
    tok = lax.broadcasted_iota(jnp.int32, (MLA_H * s, 1), 0) & (s - 1)
    kidx = lax.broadcasted_iota(jnp.int32, (1, nk), 1)
    vis = ((kidx >> 6) <= ((q_off + tok) >> 6)) & (kidx < nk_real)
    sc = jnp.where(vis, sc, -1e30)
    p = jnp.where(vis, jnp.exp2(sc - jnp.max(sc, axis=-1, keepdims=True)), 0.0)
    l = jnp.sum(p, axis=-1, keepdims=True)
    ctx = (jnp.dot(p.astype(BF16), kv, preferred_element_type=F32) / l).astype(BF16)
    for h in range(MLA_H):
        o_ref[0, :, h * MLA_V:(h + 1) * MLA_V] = jnp.dot(
            ctx[h * s:(h + 1) * s, 0:MLA_KV_RANK], wv_ref[h], preferred_element_type=F32).astype(o_ref.dtype)


def latent_attention(q_lat, latkr_all, w_uv_h, *, bsz, s, q_off, nk_real):
    tk_tot = latkr_all.shape[1]
    return pl.pallas_call(
        functools.partial(_latent_attn_kernel, s=s, q_off=q_off, nk_real=nk_real),
        out_shape=jax.ShapeDtypeStruct((bsz, s, BRANCH), BF16),
        grid=(bsz,),
        in_specs=[pl.BlockSpec((MLA_H, s, LATKR), lambda b: (0, b, 0)),
                  pl.BlockSpec((1, tk_tot, LATKR), lambda b: (b, 0, 0)),
                  pl.BlockSpec((MLA_H, MLA_KV_RANK, MLA_V), lambda b: (0, 0, 0))],
        out_specs=pl.BlockSpec((1, s, BRANCH), lambda b: (b, 0, 0)),
        compiler_params=_cp(("parallel",)),
        name="latent_attention",
    )(q_lat, latkr_all, w_uv_h)


FLASH_STRIP = 256


def _flash_t_kernel(qi_ref, kj_ref, q_ref, k_ref, vt_ref, o_ref, m_sc, l_sc, acc_sc, *, tq, tk):
    pidx = pl.program_id(2)
    i = qi_ref[pidx]
    j = kj_ref[pidx]

    @pl.when(j == 0)
    def _():
        m_sc[...] = jnp.full(m_sc.shape, -1e30, F32)
        l_sc[...] = jnp.zeros(l_sc.shape, F32)
        acc_sc[...] = jnp.zeros(acc_sc.shape, F32)

    nstrip = tq // FLASH_STRIP

    def strip_mode(c, key_off):
        q_lo, q_hi = (c * FLASH_STRIP) // CHUNK, (c * FLASH_STRIP + FLASH_STRIP - 1) // CHUNK
        k_lo, k_hi = key_off // CHUNK, (key_off + tk - 1) // CHUNK
        if k_lo > q_hi:
            return "hidden"
        return "visible" if k_hi <= q_lo else "masked"

    def tile(key_off):
        modes = ["visible" if key_off is None else strip_mode(c, key_off) for c in range(nstrip)]
        live = [c for c in range(nstrip) if modes[c] != "hidden"]
        k = k_ref[0]
        vt = vt_ref[0]
        kchunk = (j * tk + lax.broadcasted_iota(jnp.int32, (tk, 1), 0)) >> 6
        s_v, p_v = {}, {}

        def scores(c):
            rows = slice(c * FLASH_STRIP, (c + 1) * FLASH_STRIP)
            s = _nt(k, q_ref[0, rows, :])
            vis = None
            if modes[c] == "masked":
                qchunk = (i * tq + c * FLASH_STRIP + lax.broadcasted_iota(jnp.int32, (1, FLASH_STRIP), 1)) >> 6
                vis = kchunk <= qchunk
                s = jnp.where(vis, s, -1e30)
            s_v[c] = (s, vis)

        def softmax(c):
            lanes = slice(c * FLASH_STRIP, (c + 1) * FLASH_STRIP)
            s, vis = s_v.pop(c)
            m_prev = m_sc[:, lanes]
            m_new = jnp.maximum(m_prev, jnp.max(s, axis=0, keepdims=True))
            alpha = jnp.exp2(m_prev - m_new)
            p = jnp.exp2(s - m_new)
            if vis is not None:
                p = jnp.where(vis, p, 0.0)
            l_sc[:, lanes] = alpha * l_sc[:, lanes] + jnp.sum(p, axis=0, keepdims=True)
            m_sc[:, lanes] = m_new
            p_v[c] = (p.astype(BF16), alpha)

        def values(c):
            lanes = slice(c * FLASH_STRIP, (c + 1) * FLASH_STRIP)
            p16, alpha = p_v.pop(c)
            acc_sc[:, lanes] = alpha * acc_sc[:, lanes] + jnp.dot(vt, p16, preferred_element_type=F32)

        for step in range(len(live) + 2):
            if step < len(live):
                scores(live[step])
            if 0 <= step - 1 < len(live):
                softmax(live[step - 1])
            if 0 <= step - 2 < len(live):
                values(live[step - 2])

    ratio = tq // tk

    @pl.when(j < i * ratio)
    def _():
        tile(None)

    for r in range(ratio):
        @pl.when(j == i * ratio + r)
        def _(r=r):
            tile(r * tk)

    @pl.when(j == (i * tq + tq - 1) // tk)
    def _():
        o_ref[0] = (acc_sc[...] / l_sc[...]).T.astype(o_ref.dtype)


def flash_attention_causal(q3, k3, vt3, *, tq=2048, tk=1024):
    bsz, s, _ = q3.shape
    tq, tk = min(tq, s), min(tk, s)
    assert s % tq == 0 and tq % tk == 0 and tq % FLASH_STRIP == 0 and tk % CHUNK == 0
    pairs = [(i, j) for i in range(s // tq) for j in range((i * tq + tq - 1) // tk + 1)]
    qi = jnp.asarray([p[0] for p in pairs], jnp.int32)
    kj = jnp.asarray([p[1] for p in pairs], jnp.int32)
    grid_spec = pltpu.PrefetchScalarGridSpec(
        num_scalar_prefetch=2,
        grid=(bsz, MLA_H, len(pairs)),
        in_specs=[pl.BlockSpec((1, tq, QSLOT), lambda b, h, p, qi, kj: (b, qi[p], h)),
                  pl.BlockSpec((1, tk, QSLOT), lambda b, h, p, qi, kj: (b, kj[p], h)),
                  pl.BlockSpec((1, MLA_V, tk), lambda b, h, p, qi, kj: (b, h, kj[p]))],
        out_specs=pl.BlockSpec((1, tq, MLA_V), lambda b, h, p, qi, kj: (b, qi[p], h)),
        scratch_shapes=[pltpu.VMEM((1, tq), F32), pltpu.VMEM((1, tq), F32), pltpu.VMEM((MLA_V, tq), F32)])
    return pl.pallas_call(
        functools.partial(_flash_t_kernel, tq=tq, tk=tk),
        out_shape=jax.ShapeDtypeStruct((bsz, s, BRANCH), BF16),
        grid_spec=grid_spec,
        compiler_params=_cp(("parallel", "parallel", "arbitrary")),
        name="flash_attention_causal",
    )(qi, kj, q3, k3, vt3)


def _vt_kernel(w_ref, x_ref, o_ref):
    o_ref[0] = _nt(w_ref[...], x_ref[0]).astype(o_ref.dtype)


def v_up_transposed(w_vt, latkr3, *, tn=512):
    bsz, s, _ = latkr3.shape
    tn = min(tn, s)
    nv = w_vt.shape[0]
    return pl.pallas_call(
        _vt_kernel,
        out_shape=jax.ShapeDtypeStruct((bsz, nv, s), BF16),
        grid=(bsz, s // tn),
        in_specs=[pl.BlockSpec((nv, LATKR), lambda b, t: (0, 0)),
                  pl.BlockSpec((1, tn, LATKR), lambda b, t: (b, t, 0))],
        out_specs=pl.BlockSpec((1, nv, tn), lambda b, t: (b, 0, t)),
        compiler_params=_cp(("parallel", "parallel")),
        name="v_up_transposed",
    )(w_vt, latkr3)


def _branch_kernel(oa_ref, ob_ref, oc_ref, w_ref, g0_ref, g1_ref, g2_ref, out_ref):
    gates = [jax.nn.sigmoid(g_ref[...]) for g_ref in (g0_ref, g1_ref, g2_ref)]
    us = [jnp.dot(o_ref[...], w_ref[n], preferred_element_type=F32)
          for n, o_ref in enumerate((oa_ref, ob_ref, oc_ref))]
    out_ref[...] = (gates[0] * us[0] + gates[1] * us[1] + gates[2] * us[2]).astype(out_ref.dtype)


def branch_merge(oa, ob, oc, w_branch, p):
    m = oa.shape[0]
    tm = min(1024, m)
    tn = 512
    nj = D_MODEL // tn
    o_spec = pl.BlockSpec((tm, BRANCH), lambda i, j: (i, 0))
    gate = lambda n: pl.BlockSpec((tm, tn), lambda i, j: (i, P_GATE // tn + n * nj + j))
    return pl.pallas_call(
        _branch_kernel,
        out_shape=jax.ShapeDtypeStruct((m, D_MODEL), BF16),
        grid=(m // tm, nj),
        in_specs=[o_spec, o_spec, o_spec,
                  pl.BlockSpec((3, BRANCH, tn), lambda i, j: (0, 0, j)),
                  gate(0), gate(1), gate(2)],
        out_specs=pl.BlockSpec((tm, tn), lambda i, j: (i, j)),
        compiler_params=_cp(("parallel", "parallel")),
        name="branch_merge",
    )(oa, ob, oc, w_branch, p, p, p)


def _pack_layer(l, w):
    w_in = w['w_in'][l]
    seg = lambda a, n: w_in[:, a:a + n]
    zeros = lambda n: jnp.zeros((D_MODEL, n), w_in.dtype)
    w_in_p = jnp.concatenate([
        seg(8032, 12288),
        seg(6688, 768), seg(7456, 512),
        seg(7968, 64), seg(4096, 8), seg(4104, 8), seg(6672, 16), zeros(32), zeros(128),
        seg(5136, 1536), seg(0, 3072), seg(3072, 1024), seg(4112, 1024)], axis=1).astype(BF16)
    w_uq = jnp.pad(w['mla_w_uq'][l], ((0, 0), (0, 0), (0, QSLOT - MLA_NOPE - MLA_ROPE)))
    w_uq = w_uq.reshape(MLA_Q_RANK, MLA_H * QSLOT).astype(BF16)
    wk = jnp.pad(w['mla_w_uk'][l], ((0, 0), (0, 0), (0, QSLOT - MLA_NOPE))).reshape(MLA_KV_RANK, MLA_H * QSLOT)
    eye = jnp.broadcast_to(jnp.eye(MLA_ROPE, dtype=F32)[:, None, :], (MLA_ROPE, MLA_H, MLA_ROPE))
    wkr = jnp.pad(eye, ((0, 0), (0, 0), (MLA_NOPE, QSLOT - MLA_NOPE - MLA_ROPE))).reshape(MLA_ROPE, MLA_H * QSLOT)
    wk = jnp.concatenate([wk, wkr, jnp.zeros((LATKR - MLA_KV_RANK - MLA_ROPE, MLA_H * QSLOT), F32)], axis=0)
    wv = jnp.pad(w['mla_w_uv'][l].reshape(MLA_KV_RANK, MLA_H * MLA_V), ((0, LATKR - MLA_KV_RANK), (0, 0)))
    w_k = wk.astype(BF16)
    w_vt = wv.T.astype(BF16)
    w_ukt = jnp.transpose(w['mla_w_uk'][l], (1, 2, 0)).astype(BF16)
    w_uv_h = jnp.transpose(w['mla_w_uv'][l], (1, 0, 2)).astype(BF16)

    def lane_row(vals, off):
        return jnp.zeros((1, 128), F32).at[0, off:off + vals.shape[0]].set(vals.astype(F32))

    return dict(
        attn_norm=w['attn_norm'][l], w_in=w_in_p,
        gdn_conv_w=w['gdn_conv_w'][l], gdn_alog=lane_row(w['gdn_A_log'][l], L_ALPHA),
        gdn_dtb=lane_row(w['gdn_dt_bias'][l], L_ALPHA), gdn_norm=w['gdn_norm'][l],
        ssm_conv_w=w['ssm_conv_w'][l], ssm_conv_b=w['ssm_conv_b'][l],
        ssm_alog=lane_row(w['ssm_A_log'][l], L_DT), ssm_dtb=lane_row(w['ssm_dt_bias'][l], L_DT),
        ssm_d=jnp.repeat(w['ssm_D'][l].astype(F32), SSM_P).reshape(1, BRANCH), ssm_norm=w['ssm_norm'][l],
        mla_q_norm=w['mla_q_norm'][l], w_uq=w_uq, mla_kv_norm=w['mla_kv_norm'][l], w_k=w_k, w_vt=w_vt, w_ukt=w_ukt, w_uv_h=w_uv_h,
        w_branch=w['w_branch'][l].astype(BF16), w_out=w['w_out'][l].astype(BF16),
        ffn_norm=w['ffn_norm'][l], w_up_il=interleave_cast_w_up(w['ffn_w_up'], l),
        ffn_conv_w=w['ffn_conv_w'][l], ffn_conv_b=w['ffn_conv_b'][l],
        w_down=w['ffn_w_down'][l].astype(BF16))


def _rope_tables(pos):
    half = MLA_ROPE // 2
    inv = ROPE_THETA ** (-jnp.arange(half, dtype=F32) / half)
    ang = pos.astype(F32)[:, None] * inv[None, :]
    cos, sin = jnp.cos(ang), jnp.sin(ang)
    n = pos.shape[0]
    z = lambda k: jnp.zeros((n, k), F32)
    cq = jnp.concatenate([jnp.ones((n, MLA_NOPE), F32), cos, cos, z(64)], axis=1)
    saq = jnp.concatenate([z(MLA_NOPE + half), sin, z(64)], axis=1)
    sbq = jnp.concatenate([z(MLA_NOPE), -sin, z(half + 64)], axis=1)
    ck = jnp.concatenate([cos, cos, z(64)], axis=1)
    sak = jnp.concatenate([z(half), sin, z(64)], axis=1)
    sbk = jnp.concatenate([-sin, z(half + 64)], axis=1)
    return cq, saq, sbq, ck, sak, sbk


def _pad_state(st):
    return jnp.pad(st, ((0, 0), (8 - st.shape[1], 0), (0, 0)))


def _layer(x, tabs, q_off, state, lw):
    latkr_past, gconv, gssm, mconv, mssm, fconv = state
    bsz, s, _ = x.shape
    m = bsz * s
    cq_t, saq_t, sbq_t, ck_t, sak_t, sbk_t = tabs
    x2 = x.reshape(m, D_MODEL)
    mm_tm = 512

    p = rmsnorm_matmul(x2, lw['attn_norm'], lw['w_in'], F32, tm=mm_tm, tn=1024, name="in_proj")
    p3 = p.reshape(bsz, s, P_TOT)

    qkv_act, gconv8 = causal_conv(p3, P_QKV, GDN_QKV, _pad_state(gconv), lw['gdn_conv_w'],
                                  jnp.zeros((GDN_QKV,), F32), act=True)
    o_a, gssm_new = gdn(qkv_act, p3, gssm, lw['gdn_alog'], lw['gdn_dtb'], lw['gdn_norm'])

    xbc_act, mconv8 = causal_conv(p3, P_XBC, SSM_CONV_DIM, _pad_state(mconv), lw['ssm_conv_w'],
                                  lw['ssm_conv_b'], act=True)
    o_b, mssm_new = ssd(xbc_act, p3, mssm.reshape(bsz, SSM_H // 2, 128, 128), lw['ssm_alog'], lw['ssm_dtb'],
                        lw['ssm_d'], lw['ssm_norm'])
    mssm_new = mssm_new.reshape(bsz, SSM_H, SSM_P, SSM_N)

    cqn, lat, kr, latkr = mla_prep(p, lw['mla_q_norm'], lw['mla_kv_norm'], ck_t, sak_t, sbk_t)
    tq_tab = min(mm_tm, m)
    q_full = matmul(cqn, lw['w_uq'], BF16, tm=mm_tm, tn=MLA_H * QSLOT, name="q_up",
                    extras=[(cq_t, (tq_tab, QSLOT), lambda i, j: (i, 0)),
                            (saq_t, (tq_tab, QSLOT), lambda i, j: (i, 0)),
                            (sbq_t, (tq_tab, QSLOT), lambda i, j: (i, 0))],
                    epilogue=_q_rope_epilogue)
    latkr3 = latkr.reshape(bsz, s, LATKR)
    q3 = q_full.reshape(bsz, s, MLA_H * QSLOT)
    if latkr_past is None and q_off == 0 and s % 512 == 0:
        k_full = matmul(latkr, lw['w_k'], BF16, tm=mm_tm, tn=1024, name="k_up")
        vt = v_up_transposed(lw['w_vt'], latkr3)
        o_c = flash_attention_causal(q3, k_full.reshape(bsz, s, MLA_H * QSLOT), vt)
    else:
        past = jnp.zeros((bsz, 0, LATKR), BF16) if latkr_past is None else latkr_past
        nk_real = past.shape[1] + s
        nk_pad = -(-nk_real // 128) * 128
        latkr_all = jnp.concatenate([past, latkr3, jnp.zeros((bsz, nk_pad - nk_real, LATKR), BF16)], axis=1)
        assert s & (s - 1) == 0 and MLA_H * s <= 512
        o_c = latent_attention(q_absorb(q_full, lw['w_ukt']), latkr_all, lw['w_uv_h'],
                               bsz=bsz, s=s, q_off=q_off, nk_real=nk_real)

    merged = branch_merge(o_a.reshape(m, BRANCH), o_b.reshape(m, BRANCH), o_c.reshape(m, BRANCH),
                          lw['w_branch'], p)
    x2 = matmul_residual(merged, lw['w_out'], x2, tm=mm_tm, tn=1024, name="out_proj")

    h2 = rmsnorm(x2, lw['ffn_norm'], BF16)
    if s % 512 == 0:
        act, fconv8 = ffn_up_fused(h2, lw['w_up_il'], _pad_state(fconv), lw['ffn_conv_w'], lw['ffn_conv_b'],
                                   bsz=bsz, tm=max(t for t in (512, 1024, 2048) if s % t == 0))
    else:
        up = matmul_tile_major(h2, lw['w_up_il'])
        act, fconv8 = ffn_conv_gate_il(up.reshape(bsz, s, 2 * D_FF), _pad_state(fconv), lw['ffn_conv_w'],
                                       lw['ffn_conv_b'])
    x2 = matmul_residual(act.reshape(m, D_FF), lw['w_down'], x2, tm=mm_tm, tn=512, name="ffn_down")

    new = (lat.reshape(bsz, s, MLA_KV_RANK), kr.reshape(bsz, s, MLA_ROPE),
           gconv8[:, 8 - gconv.shape[1]:], gssm_new, mconv8[:, 8 - mconv.shape[1]:], mssm_new,
           fconv8[:, 8 - fconv.shape[1]:])
    return x2.reshape(bsz, s, D_MODEL), new


def kernel(x_prompt, x_sample, cache_mla_latent, cache_mla_krope, state_gdn_conv, state_gdn_ssm, state_ssm_conv, state_ssm, state_ffn_conv, attn_norm, w_in, gdn_conv_w, gdn_A_log, gdn_dt_bias, gdn_norm, ssm_conv_w, ssm_conv_b, ssm_A_log, ssm_dt_bias, ssm_D, ssm_norm, mla_q_norm, mla_w_uq, mla_kv_norm, mla_w_uk, mla_w_uv, w_branch, w_out, ffn_norm, ffn_w_up, ffn_conv_w, ffn_conv_b, ffn_w_down, final_norm):
    weights = dict(attn_norm=attn_norm, w_in=w_in, gdn_conv_w=gdn_conv_w, gdn_A_log=gdn_A_log,
                   gdn_dt_bias=gdn_dt_bias, gdn_norm=gdn_norm, ssm_conv_w=ssm_conv_w, ssm_conv_b=ssm_conv_b,
                   ssm_A_log=ssm_A_log, ssm_dt_bias=ssm_dt_bias, ssm_D=ssm_D, ssm_norm=ssm_norm,
                   mla_q_norm=mla_q_norm, mla_w_uq=mla_w_uq, mla_kv_norm=mla_kv_norm, mla_w_uk=mla_w_uk,
                   mla_w_uv=mla_w_uv, w_branch=w_branch, w_out=w_out, ffn_norm=ffn_norm, ffn_w_up=ffn_w_up,
                   ffn_conv_w=ffn_conv_w, ffn_conv_b=ffn_conv_b, ffn_w_down=ffn_w_down)
    depth = w_in.shape[0]
    bp, sp, _ = x_prompt.shape
    bs, ss, _ = x_sample.shape
    n_past = cache_mla_latent.shape[2]
    pos_p = jnp.arange(sp, dtype=jnp.int32)
    pos_s = n_past + jnp.arange(ss, dtype=jnp.int32)
    tabs_p = _rope_tables(pos_p)
    tabs_s = tuple(jnp.tile(t, (bs, 1)) for t in _rope_tables(pos_s))
    zero_state = (
        None,
        jnp.zeros((bp,) + state_gdn_conv.shape[2:], F32),
        jnp.zeros((bp,) + state_gdn_ssm.shape[2:], F32),
        jnp.zeros((bp,) + state_ssm_conv.shape[2:], F32),
        jnp.zeros((bp,) + state_ssm.shape[2:], F32),
        jnp.zeros((bp,) + state_ffn_conv.shape[2:], F32),
    )
    xp, xs = x_prompt, x_sample
    new_p, new_s = [], []
    for l in range(depth):
        lw = _pack_layer(l, weights)
        xp, st_p = _layer(xp, tabs_p, 0, zero_state, lw)
        latkr_past = jnp.concatenate(
            [cache_mla_latent[l], cache_mla_krope[l],
             jnp.zeros((bs, n_past, LATKR - MLA_KV_RANK - MLA_ROPE), F32)], axis=-1).astype(BF16)
        st_in = (latkr_past, state_gdn_conv[l], state_gdn_ssm[l], state_ssm_conv[l], state_ssm[l],
                 state_ffn_conv[l])
        xs, st_s = _layer(xs, tabs_s, n_past, st_in, lw)
        new_p.append(st_p)
        new_s.append(st_s)
    y_prompt = rmsnorm(xp.reshape(bp * sp, D_MODEL), final_norm, F32).reshape(bp, sp, D_MODEL)
    y_sample = rmsnorm(xs.reshape(bs * ss, D_MODEL), final_norm, F32).reshape(bs, ss, D_MODEL)
    outs_p = [jnp.stack(t) for t in zip(*new_p)]
    outs_s = [jnp.stack(t) for t in zip(*new_s)]
    return (y_prompt, y_sample, *outs_p, *outs_s)
```

```python
import functools
import math

import jax
import jax.numpy as jnp
from jax import lax
from jax.experimental import pallas as pl
from jax.experimental.pallas import tpu as pltpu

F32 = jnp.float32
BF16 = jnp.bfloat16
HI = lax.Precision.HIGHEST

EPS = 1e-6
CHUNK = 64
D_MODEL = 4096
BRANCH = 1024
GDN_H, GDN_D = 8, 128
GDN_QKV = 3072
SSM_H, SSM_P, SSM_G, SSM_N = 16, 64, 2, 128
SSM_CONV_DIM = 1536
MLA_H, MLA_NOPE, MLA_ROPE, MLA_V = 8, 128, 64, 128
MLA_Q_RANK, MLA_KV_RANK = 768, 512
ROPE_THETA = 10000.0
D_FF = 11008
LATKR = 640
QSLOT = 256

P_GATE = 0
P_MLA = 12288
P_SMALL = P_MLA + 1280
P_XBC = 13824
P_QKV = 15360
P_ZA = 18432
P_ZB = 19456
P_TOT = 20480
L_BETA, L_ALPHA, L_DT = 64, 72, 80

VMEM_LIMIT = 56 * 1024 * 1024


def _cp(sem):
    return pltpu.CompilerParams(dimension_semantics=sem, vmem_limit_bytes=VMEM_LIMIT)


def _softplus(x):
    return jnp.maximum(x, 0.0) + jnp.log1p(jnp.exp(-jnp.abs(x)))


def _silu(x):
    return x * jax.nn.sigmoid(x)


def _nt(a, b):
    return lax.dot_general(a, b, (((1,), (1,)), ((), ())), preferred_element_type=F32)


def _tn(a, b):
    return lax.dot_general(a, b, (((0,), (0,)), ((), ())), preferred_element_type=F32)


def _rms_kernel(x_ref, w_ref, o_ref):
    x = x_ref[...]
    ms = jnp.mean(x * x, axis=-1, keepdims=True)
    o_ref[...] = (x * lax.rsqrt(ms + EPS) * w_ref[...]).astype(o_ref.dtype)


def rmsnorm(x, w, out_dtype):
    m, d = x.shape
    tm = min(256, m)
    return pl.pallas_call(
        _rms_kernel,
        out_shape=jax.ShapeDtypeStruct((m, d), out_dtype),
        grid=(m // tm,),
        in_specs=[pl.BlockSpec((tm, d), lambda i: (i, 0)),
                  pl.BlockSpec((1, d), lambda i: (0, 0))],
        out_specs=pl.BlockSpec((tm, d), lambda i: (i, 0)),
        compiler_params=_cp(("parallel",)),
        name="rmsnorm",
    )(x, w.reshape(1, d))


def _mm_kernel(*refs, nk, n_extra, epilogue):
    a_ref, b_ref = refs[0], refs[1]
    extras = refs[2:2 + n_extra]
    o_ref = refs[2 + n_extra]
    part = jnp.dot(a_ref[...], b_ref[...], preferred_element_type=F32)
    if nk == 1:
        o_ref[...] = epilogue(part, *extras).astype(o_ref.dtype)
        return
    acc_ref = refs[3 + n_extra]
    k = pl.program_id(2)

    @pl.when(k == 0)
    def _():
        acc_ref[...] = part

    @pl.when(k > 0)
    def _():
        acc_ref[...] += part

    @pl.when(k == nk - 1)
    def _():
        o_ref[...] = epilogue(acc_ref[...], *extras).astype(o_ref.dtype)


def matmul(a, b, out_dtype, *, tm, tn, tk=None, extras=(), epilogue=None, name="matmul"):
    m, kd = a.shape
    n = b.shape[1]
    tm, tn = min(tm, m), min(tn, n)
    while m % tm:
        tm //= 2
    tk = kd if tk is None else tk
    nk = kd // tk
    assert m % tm == 0 and n % tn == 0 and kd % tk == 0
    if epilogue is None:
        epilogue = lambda acc: acc
    in_specs = [pl.BlockSpec((tm, tk), lambda i, j, k: (i, k)),
                pl.BlockSpec((tk, tn), lambda i, j, k: (k, j))]
    args = [a, b]
    for arr, blk, imap in extras:
        in_specs.append(pl.BlockSpec(blk, functools.partial(lambda i, j, k, f: f(i, j), f=imap)))
        args.append(arr)
    scratch = [pltpu.VMEM((tm, tn), F32)] if nk > 1 else []
    return pl.pallas_call(
        functools.partial(_mm_kernel, nk=nk, n_extra=len(extras), epilogue=epilogue),
        out_shape=jax.ShapeDtypeStruct((m, n), out_dtype),
        grid=(m // tm, n // tn, nk),
        in_specs=in_specs,
        out_specs=pl.BlockSpec((tm, tn), lambda i, j, k: (i, j)),
        scratch_shapes=scratch,
        compiler_params=_cp(("parallel", "parallel", "arbitrary")),
        name=name,
    )(*args)


def _norm_mm_kernel(x_ref, nw_ref, b_ref, o_ref, a16_ref):
    @pl.when(pl.program_id(1) == 0)
    def _():
        x = x_ref[...]
        ms = jnp.mean(x * x, axis=-1, keepdims=True)
        a16_ref[...] = (x * lax.rsqrt(ms + EPS) * nw_ref[...]).astype(a16_ref.dtype)

    o_ref[...] = jnp.dot(a16_ref[...], b_ref[...], preferred_element_type=F32).astype(o_ref.dtype)


def rmsnorm_matmul(x, nw, b, out_dtype, *, tm, tn, name):
    m, kd = x.shape
    n = b.shape[1]
    tm = min(tm, m)
    assert m % tm == 0 and n % tn == 0
    return pl.pallas_call(
        _norm_mm_kernel,
        out_shape=jax.ShapeDtypeStruct((m, n), out_dtype),
        grid=(m // tm, n // tn),
        in_specs=[pl.BlockSpec((tm, kd), lambda i, j: (i, 0)),
                  pl.BlockSpec((1, kd), lambda i, j: (0, 0)),
                  pl.BlockSpec((kd, tn), lambda i, j: (0, j))],
        out_specs=pl.BlockSpec((tm, tn), lambda i, j: (i, j)),
        scratch_shapes=[pltpu.VMEM((tm, kd), BF16)],
        compiler_params=_cp(("parallel", "arbitrary")),
        name=name,
    )(x, nw.reshape(1, kd), b)


def _residual_epilogue(acc, r_ref):
    return acc + r_ref[...]


def matmul_residual(a, b, res, *, tm, tn, tk=None, name="matmul_residual"):
    tm_, tn_ = min(tm, a.shape[0]), min(tn, b.shape[1])
    return matmul(a, b, F32, tm=tm, tn=tn, tk=tk, name=name,
                  extras=[(res, (tm_, tn_), lambda i, j: (i, j))], epilogue=_residual_epilogue)


def _q_rope_epilogue(acc, c_ref, sa_ref, sb_ref):
    c, sa, sb = c_ref[...], sa_ref[...], sb_ref[...]
    scale = (MLA_NOPE + MLA_ROPE) ** -0.5 * math.log2(math.e)
    outs = []
    for h in range(MLA_H):
        xs = acc[:, h * QSLOT:(h + 1) * QSLOT]
        y = xs * c + pltpu.roll(xs, 32, 1) * sa + pltpu.roll(xs, QSLOT - 32, 1) * sb
        outs.append(y * scale)
    return jnp.concatenate(outs, axis=1)


def _conv_kernel(x_ref, st_ref, w_ref, b_ref, y_ref, ns_ref, xp_ref, *, width, ts, act):
    t = pl.program_id(2)
    nt = pl.num_programs(2)

    @pl.when(t == 0)
    def _():
        xp_ref[0:8, :] = st_ref[0]

    @pl.when(t > 0)
    def _():
        xp_ref[0:8, :] = xp_ref[ts:ts + 8, :]

    xp_ref[8:8 + ts, :] = x_ref[0]
    off = 8 - (width - 1)
    y = xp_ref[off:off + ts, :] * w_ref[0:1, :]
    for j in range(1, width):
        y = y + xp_ref[off + j:off + j + ts, :] * w_ref[j:j + 1, :]
    y = y + b_ref[...]
    if act:
        y = _silu(y)
    y_ref[0] = y.astype(y_ref.dtype)

    @pl.when(t == nt - 1)
    def _():
        ns_ref[0] = xp_ref[ts:ts + 8, :]


def causal_conv(x3, col_off, chans, state8, w, b, *, act, ct=512):
    bsz, s, _ = x3.shape
    width = w.shape[0]
    ts = min(2048, s)
    assert chans % ct == 0 and col_off % ct == 0 and s % ts == 0
    cb0 = col_off // ct
    return pl.pallas_call(
        functools.partial(_conv_kernel, width=width, ts=ts, act=act),
        out_shape=(jax.ShapeDtypeStruct((bsz, s, chans), F32),
                   jax.ShapeDtypeStruct((bsz, 8, chans), F32)),
        grid=(bsz, chans // ct, s // ts),
        in_specs=[pl.BlockSpec((1, ts, ct), lambda bi, c, t: (bi, t, cb0 + c)),
                  pl.BlockSpec((1, 8, ct), lambda bi, c, t: (bi, 0, c)),
                  pl.BlockSpec((width, ct), lambda bi, c, t: (0, c)),
                  pl.BlockSpec((1, ct), lambda bi, c, t: (0, c))],
        out_specs=(pl.BlockSpec((1, ts, ct), lambda bi, c, t: (bi, t, c)),
                   pl.BlockSpec((1, 8, ct), lambda bi, c, t: (bi, 0, c))),
        scratch_shapes=[pltpu.VMEM((ts + 8, ct), F32)],
        compiler_params=_cp(("parallel", "parallel", "arbitrary")),
        name="causal_conv",
    )(x3, state8, w, b.reshape(1, chans))


HALO = 16


FFN_TN = 256


def _interleave_ffn(x):
    lead = x.shape[:-1]
    x = x.reshape(*lead, 2, D_FF // FFN_TN, FFN_TN)
    return jnp.swapaxes(x, -3, -2).reshape(*lead, 2 * D_FF)


def _deinterleave_ffn(x):
    lead = x.shape[:-1]
    x = x.reshape(*lead, D_FF // FFN_TN, 2, FFN_TN)
    return jnp.swapaxes(x, -3, -2).reshape(*lead, 2 * D_FF)


def _il_cast_kernel(a_ref, b_ref, o_ref):
    o_ref[0, :, :FFN_TN] = a_ref[0].astype(o_ref.dtype)
    o_ref[0, :, FFN_TN:] = b_ref[0].astype(o_ref.dtype)


def interleave_cast_w_up(w_all, l, *, tk=1024):
    kd = w_all.shape[1]
    nc = D_FF // FFN_TN
    return pl.pallas_call(
        _il_cast_kernel,
        out_shape=jax.ShapeDtypeStruct((nc, kd, 2 * FFN_TN), BF16),
        grid=(kd // tk, nc),
        in_specs=[pl.BlockSpec((1, tk, FFN_TN), lambda k, j: (l, k, j)),
                  pl.BlockSpec((1, tk, FFN_TN), lambda k, j: (l, k, nc + j))],
        out_specs=pl.BlockSpec((1, tk, 2 * FFN_TN), lambda k, j: (j, k, 0)),
        compiler_params=_cp(("parallel", "parallel")),
        name="interleave_cast_w_up",
    )(w_all, w_all)


def _mm_tiles_kernel(a_ref, w_ref, o_ref):
    o_ref[...] = jnp.dot(a_ref[...], w_ref[0], preferred_element_type=F32)


def matmul_tile_major(a, w_tiles):
    m, kd = a.shape
    nt, _, tn = w_tiles.shape
    return pl.pallas_call(
        _mm_tiles_kernel,
        out_shape=jax.ShapeDtypeStruct((m, nt * tn), F32),
        grid=(nt,),
        in_specs=[pl.BlockSpec((m, kd), lambda j: (0, 0)),
                  pl.BlockSpec((1, kd, tn), lambda j: (j, 0, 0))],
        out_specs=pl.BlockSpec((m, tn), lambda j: (0, j)),
        compiler_params=_cp(("parallel",)),
        name="ffn_up",
    )(a, w_tiles)


def _ffn_conv_gate_il_kernel(x_ref, st_ref, cw_ref, cb_ref, y_ref, ns_ref, xp_ref, *, width, s):
    xp_ref[0:8, :] = st_ref[0]
    xp_ref[8:8 + s, :] = x_ref[0]
    off = 8 - (width - 1)
    y = xp_ref[off:off + s, :] * cw_ref[0:1, :]
    for t in range(1, width):
        y = y + xp_ref[off + t:off + t + s, :] * cw_ref[t:t + 1, :]
    y = y + cb_ref[...]
    ns_ref[0] = xp_ref[s:s + 8, :]
    for c in range(D_FF // FFN_TN):
        g = y[:, 2 * c * FFN_TN:(2 * c + 1) * FFN_TN]
        v = y[:, (2 * c + 1) * FFN_TN:(2 * c + 2) * FFN_TN]
        y_ref[0, :, c * FFN_TN:(c + 1) * FFN_TN] = (_silu(g) * v).astype(y_ref.dtype)


def ffn_conv_gate_il(up3, state8, cw, cb):
    bsz, s, n2 = up3.shape
    width = cw.shape[0]
    y, ns = pl.pallas_call(
        functools.partial(_ffn_conv_gate_il_kernel, width=width, s=s),
        out_shape=(jax.ShapeDtypeStruct((bsz, s, D_FF), BF16),
                   jax.ShapeDtypeStruct((bsz, 8, n2), F32)),
        grid=(bsz,),
        in_specs=[pl.BlockSpec((1, s, n2), lambda b: (b, 0, 0)),
                  pl.BlockSpec((1, 8, n2), lambda b: (b, 0, 0)),
                  pl.BlockSpec((width, n2), lambda b: (0, 0)),
                  pl.BlockSpec((1, n2), lambda b: (0, 0))],
        out_specs=(pl.BlockSpec((1, s, D_FF), lambda b: (b, 0, 0)),
                   pl.BlockSpec((1, 8, n2), lambda b: (b, 0, 0))),
        scratch_shapes=[pltpu.VMEM((s + 8, n2), F32)],
        compiler_params=_cp(("parallel",)),
        name="ffn_conv_gate_il",
    )(up3, _interleave_ffn(state8), _interleave_ffn(cw), _interleave_ffn(cb.reshape(1, n2)))
    return y, _deinterleave_ffn(ns)


FFN_SUB = 512


def _ffn_up_fused_kernel(a_ref, halo_ref, w_ref, st_ref, cw_ref, cb_ref, y_ref, ns_ref, aext_ref, *xp_refs,
                         tm, tiles_per_seq, width):
    i = pl.program_id(0)
    j = pl.program_id(1)

    @pl.when(j == 0)
    def _():
        aext_ref[0:HALO, :] = halo_ref[...]
        aext_ref[HALO:HALO + FFN_SUB, :] = a_ref[0:FFN_SUB, :]

    nsub = tm // FFN_SUB
    xp_refs[0][...] = jnp.dot(aext_ref[...], w_ref[0], preferred_element_type=F32)
    for r in range(1, nsub):
        xp_refs[r][...] = jnp.dot(a_ref[r * FFN_SUB - HALO:(r + 1) * FFN_SUB, :], w_ref[0],
                                  preferred_element_type=F32)
    first = (i % tiles_per_seq) == 0
    xp_refs[0][HALO - 8:HALO, :] = jnp.where(first, st_ref[0], xp_refs[0][HALO - 8:HALO, :])
    off = HALO - (width - 1)
    cw = cw_ref[...]
    cb = cb_ref[...]
    for r in range(nsub):
        xp_ref = xp_refs[r]
        y = xp_ref[off:off + FFN_SUB, :] * cw[0:1, :]
        for t in range(1, width):
            y = y + xp_ref[off + t:off + t + FFN_SUB, :] * cw[t:t + 1, :]
        y = y + cb
        y_ref[r * FFN_SUB:(r + 1) * FFN_SUB, :] = (_silu(y[:, :FFN_TN]) * y[:, FFN_TN:]).astype(y_ref.dtype)
    ns_ref[0] = xp_refs[nsub - 1][HALO + FFN_SUB - 8:HALO + FFN_SUB, :]


def ffn_up_fused(h2, w_up_il, state8, cw, cb, *, bsz, tm=1024):
    m, kd = h2.shape
    s = m // bsz
    tm = min(tm, s)
    assert s % tm == 0 and tm % FFN_SUB == 0
    tiles_per_seq = s // tm
    width = cw.shape[0]
    nc = D_FF // FFN_TN
    tn2 = 2 * FFN_TN
    y, ns = pl.pallas_call(
        functools.partial(_ffn_up_fused_kernel, tm=tm, tiles_per_seq=tiles_per_seq, width=width),
        out_shape=(jax.ShapeDtypeStruct((m, D_FF), BF16),
                   jax.ShapeDtypeStruct((m // tm, 8, 2 * D_FF), F32)),
        grid=(m // tm, nc),
        in_specs=[pl.BlockSpec((tm, kd), lambda i, j: (i, 0), pipeline_mode=pl.Buffered(1)),
                  pl.BlockSpec((HALO, kd), lambda i, j: (jnp.maximum(i * (tm // HALO) - 1, 0), 0)),
                  pl.BlockSpec((1, kd, tn2), lambda i, j: (j, 0, 0)),
                  pl.BlockSpec((1, 8, tn2), lambda i, j: (i // tiles_per_seq, 0, j)),
                  pl.BlockSpec((width, tn2), lambda i, j: (0, j)),
                  pl.BlockSpec((1, tn2), lambda i, j: (0, j))],
        out_specs=(pl.BlockSpec((tm, FFN_TN), lambda i, j: (i, j)),
                   pl.BlockSpec((1, 8, tn2), lambda i, j: (i, 0, j))),
        scratch_shapes=[pltpu.VMEM((FFN_SUB + HALO, kd), BF16)]
        + [pltpu.VMEM((FFN_SUB + HALO, tn2), F32) for _ in range(tm // FFN_SUB)],
        compiler_params=_cp(("arbitrary", "arbitrary")),
        name="ffn_up_fused",
    )(h2, h2, w_up_il, _interleave_ffn(state8), _interleave_ffn(cw), _interleave_ffn(cb.reshape(1, 2 * D_FF)))
    return y, _deinterleave_ffn(ns[tiles_per_seq - 1::tiles_per_seq])


def _chunk_masks(lt, c):
    sh = int(math.log2(c))
    ri = lax.broadcasted_iota(jnp.int32, (lt, lt), 0)
    ci = lax.broadcasted_iota(jnp.int32, (lt, lt), 1)
    same = (ri >> sh) == (ci >> sh)
    return ri, ci, same


def _split3(x):
    hi = x.astype(BF16)
    r = x - hi.astype(F32)
    mid = r.astype(BF16)
    lo = (r - mid.astype(F32)).astype(BF16)
    return hi, mid, lo


def _dot_exact_lhs(l16, x):
    hi, mid, lo = _split3(x)
    return (jnp.dot(l16, hi, preferred_element_type=F32) + jnp.dot(l16, mid, preferred_element_type=F32)
            + jnp.dot(l16, lo, preferred_element_type=F32))


def _dot_hi(a, b):
    a_hi = a.astype(BF16)
    a_lo = (a - a_hi.astype(F32)).astype(BF16)
    b_hi = b.astype(BF16)
    b_lo = (b - b_hi.astype(F32)).astype(BF16)
    return (jnp.dot(a_hi, b_hi, preferred_element_type=F32) + jnp.dot(a_hi, b_lo, preferred_element_type=F32)
            + jnp.dot(a_lo, b_hi, preferred_element_type=F32))


def _gdn_kernel(q_ref, k_ref, v_ref, sm_ref, z_ref, s0_ref, alog_ref, dtb_ref, nw_ref,
                o_ref, sout_ref, s_sc, vn_sc, *, lt, c, hps):
    hg = pl.program_id(1)
    t = pl.program_id(2)
    nt = pl.num_programs(2)

    @pl.when(t == 0)
    def _():
        s_sc[...] = s0_ref[0]

    sm = sm_ref[0]
    lane = lax.broadcasted_iota(jnp.int32, (1, 128), 1)
    beta_full = jax.nn.sigmoid(sm)
    g_full = -jnp.exp(alog_ref[...]) * _softplus(sm + dtb_ref[...])
    ri, ci, same = _chunk_masks(lt, c)
    tri = same & (ci <= ri)
    strict = same & (ci < ri)
    eye_b = ri == ci
    eye = jnp.where(eye_b, 1.0, 0.0)
    gc_full = _dot_exact_lhs(jnp.where(tri, 1.0, 0.0).astype(BF16), g_full)
    nw = nw_ref[...]

    hs = range(hps)
    cols = [slice(hh * GDN_D, (hh + 1) * GDN_D) for hh in hs]
    dot16 = lambda x, y: jnp.dot(x.astype(BF16), y.astype(BF16), preferred_element_type=F32)
    ks, k16s, kbs, gcs, ms, avs, qds, rhss = [], [], [], [], [], [], [], []
    for hh in hs:
        h = hg * hps + hh
        q = q_ref[0, :, cols[hh]]
        k = k_ref[0, :, cols[hh]]
        q = q * lax.rsqrt(jnp.sum(q * q, axis=-1, keepdims=True) + EPS) * (GDN_D ** -0.5)
        k = k * lax.rsqrt(jnp.sum(k * k, axis=-1, keepdims=True) + EPS)
        beta = jnp.sum(jnp.where(lane == L_BETA + h, beta_full, 0.0), axis=-1, keepdims=True)
        gc = jnp.sum(jnp.where(lane == L_ALPHA + h, gc_full, 0.0), axis=-1, keepdims=True)
        gr = jnp.sum(jnp.where(eye_b, gc, 0.0), axis=0, keepdims=True)
        decay = jnp.exp(jnp.where(tri, gc - gr, 0.0))
        kb = k * beta
        k16 = k.astype(BF16)
        ms.append(jnp.where(strict, _nt(kb.astype(BF16), k16) * decay, 0.0))
        avs.append(jnp.where(tri, _nt(q.astype(BF16), k16) * decay, 0.0).astype(BF16))
        eg = jnp.exp(gc)
        rhss.append(jnp.concatenate([v_ref[0, :, cols[hh]] * beta, kb * eg], axis=1).astype(BF16))
        qds.append((q * eg).astype(BF16))
        ks.append(k)
        gcs.append(gc)

    ps = [-m for m in ms]
    ts = [eye + p for p in ps]
    for _ in range(int(math.log2(c)) - 1):
        ps = [dot16(p, p) for p in ps]
        ts = [tt + dot16(p, tt) for p, tt in zip(ps, ts)]
    rs = [eye - _dot_hi(eye + m, tt) for m, tt in zip(ms, ts)]
    ts = [tt + dot16(tt, r) for tt, r in zip(ts, rs)]
    uws = [jnp.dot(tt.astype(BF16), rhs, preferred_element_type=F32) for tt, rhs in zip(ts, rhss)]
    us = [uw[:, :GDN_D] for uw in uws]
    ws = [uw[:, GDN_D:].astype(BF16) for uw in uws]

    ss = [s_sc[hh] for hh in hs]
    for hh in hs:
        vn_sc[hh] = us[hh].astype(BF16)
    for cc in range(lt // c):
        r0 = cc * c
        rows = slice(r0, r0 + c)
        for hh in hs:
            s16 = ss[hh].astype(BF16)
            vnew = us[hh][rows] - jnp.dot(ws[hh][rows], s16, preferred_element_type=F32)
            vn16 = vnew.astype(BF16)
            vn_sc[hh, rows, :] = vn16
            o = (jnp.dot(qds[hh][rows], s16, preferred_element_type=F32)
                 + jnp.dot(avs[hh][rows, :], vn_sc[hh], preferred_element_type=F32))
            glast = gcs[hh][r0 + c - 1:r0 + c, :]
            kd = (ks[hh][rows] * jnp.exp(glast - gcs[hh][rows])).astype(BF16)
            ss[hh] = ss[hh] * jnp.exp(glast) + _tn(kd, vn16)
            mean_sq = jnp.mean(o * o, axis=-1, keepdims=True)
            o = o * lax.rsqrt(mean_sq + EPS) * nw * _silu(z_ref[0, rows, cols[hh]])
            o_ref[0, rows, cols[hh]] = o.astype(o_ref.dtype)
    for hh in hs:
        s_sc[hh] = ss[hh]

    @pl.when(t == nt - 1)
    def _():
        sout_ref[0] = s_sc[...]


def gdn(qkv3, p3, s0, alog_row, dtb_row, norm_w, *, hps=8):
    bsz, s, _ = qkv3.shape
    c = min(CHUNK, s)
    lt = min(256, s)
    hw = hps * GDN_D
    ng = GDN_H // hps
    small_blk = P_SMALL // 128
    za_blk = P_ZA // hw
    blk = lambda off: pl.BlockSpec((1, lt, hw), lambda b, h, t: (b, t, off + h))
    return pl.pallas_call(
        functools.partial(_gdn_kernel, lt=lt, c=c, hps=hps),
        out_shape=(jax.ShapeDtypeStruct((bsz, s, BRANCH), BF16),
                   jax.ShapeDtypeStruct((bsz, GDN_H, GDN_D, GDN_D), F32)),
        grid=(bsz, ng, s // lt),
        in_specs=[blk(0), blk(ng), blk(2 * ng),
                  pl.BlockSpec((1, lt, 128), lambda b, h, t: (b, t, small_blk)),
                  blk(za_blk),
                  pl.BlockSpec((1, hps, GDN_D, GDN_D), lambda b, h, t: (b, h, 0, 0)),
                  pl.BlockSpec((1, 128), lambda b, h, t: (0, 0)),
                  pl.BlockSpec((1, 128), lambda b, h, t: (0, 0)),
                  pl.BlockSpec((1, 128), lambda b, h, t: (0, 0))],
        out_specs=(pl.BlockSpec((1, lt, hw), lambda b, h, t: (b, t, h)),
                   pl.BlockSpec((1, hps, GDN_D, GDN_D), lambda b, h, t: (b, h, 0, 0))),
        scratch_shapes=[pltpu.VMEM((hps, GDN_D, GDN_D), F32), pltpu.VMEM((hps, lt, GDN_D), BF16)],
        compiler_params=_cp(("parallel", "parallel", "arbitrary")),
        name="gdn",
    )(qkv3, qkv3, qkv3, p3, p3, s0, alog_row, dtb_row, norm_w.reshape(1, GDN_D))


def _ssd_kernel(x_ref, b_ref, c_ref, sm_ref, z_ref, h0_ref, alog_ref, dtb_ref, d_ref, nw_ref,
                y_ref, hout_ref, hs_sc, y_sc, *, lt, c):
    g = pl.program_id(1)
    t = pl.program_id(2)
    nt = pl.num_programs(2)
    npair = SSM_H // SSM_G // 2

    @pl.when(t == 0)
    def _():
        hs_sc[...] = h0_ref[0]

    sm = sm_ref[0]
    dt_full = _softplus(sm + dtb_ref[...])
    da = dt_full * (-jnp.exp(alog_ref[...]))
    ri, ci, same = _chunk_masks(lt, c)
    tri = same & (ci <= ri)
    eye_b = ri == ci
    acum_full = _dot_exact_lhs(jnp.where(tri, 1.0, 0.0).astype(BF16), da)
    nch = lt // c
    lasts = [acum_full[cc * c + c - 1:cc * c + c, :] for cc in range(nch)]
    alast_full = jnp.concatenate([jnp.broadcast_to(l, (c, 128)) for l in lasts], axis=0)
    ea_full = jnp.exp(acum_full)
    dtw_full = dt_full * jnp.exp(alast_full - acum_full)

    bm = b_ref[0]
    cm = c_ref[0]
    bm16 = bm.astype(BF16)
    cm16 = cm.astype(BF16)
    cb = jnp.where(tri, _nt(cm16, bm16), 0.0)
    lane = lax.broadcasted_iota(jnp.int32, (1, 128), 1)
    lo = lane < SSM_P
    rowlo = lax.broadcasted_iota(jnp.int32, (128, 1), 0) < SSM_P
    x = x_ref[0]
    z = z_ref[0]
    d_row = d_ref[...]

    def ext(full, l):
        return jnp.sum(jnp.where(lane == l, full, 0.0), axis=-1, keepdims=True)

    for j in range(npair):
        la = L_DT + g * (SSM_H // SSM_G) + 2 * j
        lb = la + 1
        ac_a, ac_b = ext(acum_full, la), ext(acum_full, lb)
        row_a = jnp.sum(jnp.where(eye_b, ac_a, 0.0), axis=0, keepdims=True)
        row_b = jnp.sum(jnp.where(eye_b, ac_b, 0.0), axis=0, keepdims=True)
        l_a = (jnp.exp(jnp.where(tri, ac_a - row_a, 0.0)) * cb).astype(BF16)
        l_b = (jnp.exp(jnp.where(tri, ac_b - row_b, 0.0)) * cb).astype(BF16)
        dt_pair = jnp.where(lo, ext(dt_full, la), ext(dt_full, lb))
        ea_pair = jnp.where(lo, ext(ea_full, la), ext(ea_full, lb))
        dtw_pair = jnp.where(lo, ext(dtw_full, la), ext(dtw_full, lb))
        xp = x[:, j * 128:(j + 1) * 128]
        xdt = xp * dt_pair
        y = (jnp.dot(l_a, jnp.where(lo, xdt, 0.0).astype(BF16), preferred_element_type=F32)
             + jnp.dot(l_b, jnp.where(lo, 0.0, xdt).astype(BF16), preferred_element_type=F32))
        xw = (xp * dtw_pair).astype(BF16)
        hs = hs_sc[j]
        yoffs = []
        for cc in range(nch):
            r0 = cc * c
            yoffs.append(_nt(cm16[r0:r0 + c], hs.astype(BF16)) * ea_pair[r0:r0 + c])
            cs = _tn(xw[r0:r0 + c], bm16[r0:r0 + c])
            al = jnp.where(rowlo, jnp.exp(ac_a[r0 + c - 1:r0 + c, :]), jnp.exp(ac_b[r0 + c - 1:r0 + c, :]))
            hs = hs * al + cs
        hs_sc[j] = hs
        yoff = yoffs[0] if nch == 1 else jnp.concatenate(yoffs, axis=0)
        ytot = (y + yoff + d_row[:, j * 128:(j + 1) * 128] * xp) * _silu(z[:, j * 128:(j + 1) * 128])
        y_sc[:, j * 128:(j + 1) * 128] = ytot

    yy = y_sc[...]
    ms = jnp.mean(yy * yy, axis=-1, keepdims=True)
    y_ref[0] = (yy * lax.rsqrt(ms + EPS) * nw_ref[...]).astype(y_ref.dtype)

    @pl.when(t == nt - 1)
    def _():
        hout_ref[0] = hs_sc[...]


def ssd(xbc3, p3, h0, alog_row, dtb_row, d_row, norm_w):
    bsz, s, _ = xbc3.shape
    c = min(CHUNK, s)
    lt = min(256, s)
    gw = BRANCH // SSM_G
    npair = SSM_H // SSM_G // 2
    small_blk = P_SMALL // 128
    zb_blk = P_ZB // gw
    return pl.pallas_call(
        functools.partial(_ssd_kernel, lt=lt, c=c),
        out_shape=(jax.ShapeDtypeStruct((bsz, s, BRANCH), BF16),
                   jax.ShapeDtypeStruct((bsz, SSM_H // 2, 128, 128), F32)),
        grid=(bsz, SSM_G, s // lt),
        in_specs=[pl.BlockSpec((1, lt, gw), lambda b, g, t: (b, t, g)),
                  pl.BlockSpec((1, lt, 128), lambda b, g, t: (b, t, BRANCH // 128 + g)),
                  pl.BlockSpec((1, lt, 128), lambda b, g, t: (b, t, BRANCH // 128 + SSM_G + g)),
                  pl.BlockSpec((1, lt, 128), lambda b, g, t: (b, t, small_blk)),
                  pl.BlockSpec((1, lt, gw), lambda b, g, t: (b, t, zb_blk + g)),
                  pl.BlockSpec((1, npair, 128, 128), lambda b, g, t: (b, g, 0, 0)),
                  pl.BlockSpec((1, 128), lambda b, g, t: (0, 0)),
                  pl.BlockSpec((1, 128), lambda b, g, t: (0, 0)),
                  pl.BlockSpec((1, gw), lambda b, g, t: (0, g)),
                  pl.BlockSpec((1, gw), lambda b, g, t: (0, g))],
        out_specs=(pl.BlockSpec((1, lt, gw), lambda b, g, t: (b, t, g)),
                   pl.BlockSpec((1, npair, 128, 128), lambda b, g, t: (b, g, 0, 0))),
        scratch_shapes=[pltpu.VMEM((npair, 128, 128), F32), pltpu.VMEM((lt, gw), F32)],
        compiler_params=_cp(("parallel", "parallel", "arbitrary")),
        name="ssd",
    )(xbc3, xbc3, xbc3, p3, p3, h0, alog_row, dtb_row, d_row, norm_w.reshape(1, BRANCH))


def _mla_prep_kernel(p_ref, qn_ref, kvn_ref, ck_ref, sak_ref, sbk_ref,
                     cq_ref, lat_ref, kr_ref, latkr_ref):
    blk = p_ref[...]
    cq = blk[:, 0:MLA_Q_RANK]
    cq = cq * lax.rsqrt(jnp.mean(cq * cq, axis=-1, keepdims=True) + EPS) * qn_ref[...]
    cq_ref[...] = cq.astype(cq_ref.dtype)
    ckv = blk[:, MLA_Q_RANK:MLA_Q_RANK + MLA_KV_RANK]
    lat = ckv * lax.rsqrt(jnp.mean(ckv * ckv, axis=-1, keepdims=True) + EPS) * kvn_ref[...]
    lat_ref[...] = lat
    sm = blk[:, 1280:1408]
    y = sm * ck_ref[...] + pltpu.roll(sm, 32, 1) * sak_ref[...] + pltpu.roll(sm, 96, 1) * sbk_ref[...]
    kr_ref[...] = y[:, 0:MLA_ROPE]
    latkr_ref[:, 0:MLA_KV_RANK] = lat.astype(latkr_ref.dtype)
    latkr_ref[:, MLA_KV_RANK:LATKR] = y.astype(latkr_ref.dtype)


def mla_prep(p, q_norm, kv_norm, ck, sak, sbk):
    m = p.shape[0]
    tm = min(256, m)
    mla_blk = P_MLA // 1536
    row = lambda n: pl.BlockSpec((tm, n), lambda i: (i, 0))
    return pl.pallas_call(
        _mla_prep_kernel,
        out_shape=(jax.ShapeDtypeStruct((m, MLA_Q_RANK), BF16),
                   jax.ShapeDtypeStruct((m, MLA_KV_RANK), F32),
                   jax.ShapeDtypeStruct((m, MLA_ROPE), F32),
                   jax.ShapeDtypeStruct((m, LATKR), BF16)),
        grid=(m // tm,),
        in_specs=[pl.BlockSpec((tm, 1536), lambda i: (i, mla_blk)),
                  pl.BlockSpec((1, MLA_Q_RANK), lambda i: (0, 0)),
                  pl.BlockSpec((1, MLA_KV_RANK), lambda i: (0, 0)),
                  row(128), row(128), row(128)],
        out_specs=(row(MLA_Q_RANK), row(MLA_KV_RANK), row(MLA_ROPE), row(LATKR)),
        compiler_params=_cp(("parallel",)),
        name="mla_prep",
    )(p, q_norm.reshape(1, -1), kv_norm.reshape(1, -1), ck, sak, sbk)


def _q_absorb_kernel(q_ref, wk_ref, o_ref):
    q = q_ref[...]
    o_ref[0, :, 0:MLA_KV_RANK] = jnp.dot(q[:, 0:MLA_NOPE], wk_ref[0],
                                         preferred_element_type=F32).astype(o_ref.dtype)
    o_ref[0, :, MLA_KV_RANK:LATKR] = q[:, MLA_NOPE:QSLOT]


def q_absorb(q_full, w_ukt):
    m = q_full.shape[0]
    return pl.pallas_call(
        _q_absorb_kernel,
        out_shape=jax.ShapeDtypeStruct((MLA_H, m, LATKR), BF16),
        grid=(MLA_H,),
        in_specs=[pl.BlockSpec((m, QSLOT), lambda h: (0, h)),
                  pl.BlockSpec((1, MLA_NOPE, MLA_KV_RANK), lambda h: (h, 0, 0))],
        out_specs=pl.BlockSpec((1, m, LATKR), lambda h: (h, 0, 0)),
        compiler_params=_cp(("parallel",)),
        name="q_absorb",
    )(q_full, w_ukt)


def _latent_attn_kernel(q_ref, kv_ref, wv_ref, o_ref, *, s, q_off, nk_real):
    kv = kv_ref[0]
    nk = kv.shape[0]
    q = q_ref[...].reshape(MLA_H * s, LATKR)
    sc = _nt(q, kv)
    tok = lax.broadcasted_iota(jnp.int32, (MLA_H * s, 1), 0) & (s - 1)
    kidx = lax.broadcasted_iota(jnp.int32, (1, nk), 1)
    vis = ((kidx >> 6) <= ((q_off + tok) >> 6)) & (kidx < nk_real)
    sc = jnp.where(vis, sc, -1e30)
    p = jnp.where(vis, jnp.exp2(sc - jnp.max(sc, axis=-1, keepdims=True)), 0.0)
    l = jnp.sum(p, axis=-1, keepdims=True)
    ctx = (jnp.dot(p.astype(BF16), kv, preferred_element_type=F32) / l).astype(BF16)
    for h in range(MLA_H):
        o_ref[0, :, h * MLA_V:(h + 1) * MLA_V] = jnp.dot(
            ctx[h * s:(h + 1) * s, 0:MLA_KV_RANK], wv_ref[h], preferred_element_type=F32).astype(o_ref.dtype)


def latent_attention(q_lat, latkr_all, w_uv_h, *, bsz, s, q_off, nk_real):
    tk_tot = latkr_all.shape[1]
    return pl.pallas_call(
        functools.partial(_latent_attn_kernel, s=s, q_off=q_off, nk_real=nk_real),
        out_shape=jax.ShapeDtypeStruct((bsz, s, BRANCH), BF16),
        grid=(bsz,),
        in_specs=[pl.BlockSpec((MLA_H, s, LATKR), lambda b: (0, b, 0)),
                  pl.BlockSpec((1, tk_tot, LATKR), lambda b: (b, 0, 0)),
                  pl.BlockSpec((MLA_H, MLA_KV_RANK, MLA_V), lambda b: (0, 0, 0))],
        out_specs=pl.BlockSpec((1, s, BRANCH), lambda b: (b, 0, 0)),
        compiler_params=_cp(("parallel",)),
        name="latent_attention",
    )(q_lat, latkr_all, w_uv_h)


FLASH_STRIP = 512


def _flash_t_kernel(qi_ref, kj_ref, q_ref, k_ref, vt_ref, o_ref, m_sc, l_sc, acc_sc, *, tq, tk):
    pidx = pl.program_id(2)
    i = qi_ref[pidx]
    j = kj_ref[pidx]

    @pl.when(j == 0)
    def _():
        m_sc[...] = jnp.full(m_sc.shape, -1e30, F32)
        l_sc[...] = jnp.zeros(l_sc.shape, F32)
        acc_sc[...] = jnp.zeros(acc_sc.shape, F32)

    nstrip = tq // FLASH_STRIP

    def strip_mode(c, key_off):
        q_lo, q_hi = (c * FLASH_STRIP) // CHUNK, (c * FLASH_STRIP + FLASH_STRIP - 1) // CHUNK
        k_lo, k_hi = key_off // CHUNK, (key_off + tk - 1) // CHUNK
        if k_lo > q_hi:
            return "hidden"
        return "visible" if k_hi <= q_lo else "masked"

    def tile(key_off):
        modes = ["visible" if key_off is None else strip_mode(c, key_off) for c in range(nstrip)]
        live = [c for c in range(nstrip) if modes[c] != "hidden"]
        k = k_ref[0]
        vt = vt_ref[0]
        kchunk = (j * tk + lax.broadcasted_iota(jnp.int32, (tk, 1), 0)) >> 6
        s_v, p_v = {}, {}

        def scores(c):
            rows = slice(c * FLASH_STRIP, (c + 1) * FLASH_STRIP)
            s = _nt(k, q_ref[0, rows, :])
            vis = None
            if modes[c] == "masked":
                qchunk = (i * tq + c * FLASH_STRIP + lax.broadcasted_iota(jnp.int32, (1, FLASH_STRIP), 1)) >> 6
                vis = kchunk <= qchunk
                s = jnp.where(vis, s, -1e30)
            s_v[c] = (s, vis)

        def softmax(c):
            lanes = slice(c * FLASH_STRIP, (c + 1) * FLASH_STRIP)
            s, vis = s_v.pop(c)
            m_prev = m_sc[:, lanes]
            m_new = jnp.maximum(m_prev, jnp.max(s, axis=0, keepdims=True))
            alpha = jnp.exp2(m_prev - m_new)
            p = jnp.exp2(s - m_new)
            if vis is not None:
                p = jnp.where(vis, p, 0.0)
            l_sc[:, lanes] = alpha * l_sc[:, lanes] + jnp.sum(p, axis=0, keepdims=True)
            m_sc[:, lanes] = m_new
            p_v[c] = (p.astype(BF16), alpha)

        def values(c):
            lanes = slice(c * FLASH_STRIP, (c + 1) * FLASH_STRIP)
            p16, alpha = p_v.pop(c)
            acc_sc[:, lanes] = alpha * acc_sc[:, lanes] + jnp.dot(vt, p16, preferred_element_type=F32)

        for step in range(len(live) + 2):
            if step < len(live):
                scores(live[step])
            if 0 <= step - 1 < len(live):
                softmax(live[step - 1])
            if 0 <= step - 2 < len(live):
                values(live[step - 2])

    ratio = tq // tk

    @pl.when(j < i * ratio)
    def _():
        tile(None)

    for r in range(ratio):
        @pl.when(j == i * ratio + r)
        def _(r=r):
            tile(r * tk)

    @pl.when(j == (i * tq + tq - 1) // tk)
    def _():
        o_ref[0] = (acc_sc[...] / l_sc[...]).T.astype(o_ref.dtype)


def flash_attention_causal(q3, k3, vt3, *, tq=2048, tk=1024):
    bsz, s, _ = q3.shape
    tq, tk = min(tq, s), min(tk, s)
    assert s % tq == 0 and tq % tk == 0 and tq % FLASH_STRIP == 0 and tk % CHUNK == 0
    pairs = [(i, j) for i in range(s // tq) for j in range((i * tq + tq - 1) // tk + 1)]
    qi = jnp.asarray([p[0] for p in pairs], jnp.int32)
    kj = jnp.asarray([p[1] for p in pairs], jnp.int32)
    grid_spec = pltpu.PrefetchScalarGridSpec(
        num_scalar_prefetch=2,
        grid=(bsz, MLA_H, len(pairs)),
        in_specs=[pl.BlockSpec((1, tq, QSLOT), lambda b, h, p, qi, kj: (b, qi[p], h)),
                  pl.BlockSpec((1, tk, QSLOT), lambda b, h, p, qi, kj: (b, kj[p], h)),
                  pl.BlockSpec((1, MLA_V, tk), lambda b, h, p, qi, kj: (b, h, kj[p]))],
        out_specs=pl.BlockSpec((1, tq, MLA_V), lambda b, h, p, qi, kj: (b, qi[p], h)),
        scratch_shapes=[pltpu.VMEM((1, tq), F32), pltpu.VMEM((1, tq), F32), pltpu.VMEM((MLA_V, tq), F32)])
    return pl.pallas_call(
        functools.partial(_flash_t_kernel, tq=tq, tk=tk),
        out_shape=jax.ShapeDtypeStruct((bsz, s, BRANCH), BF16),
        grid_spec=grid_spec,
        compiler_params=_cp(("parallel", "parallel", "arbitrary")),
        name="flash_attention_causal",
    )(qi, kj, q3, k3, vt3)


def _vt_kernel(w_ref, x_ref, o_ref):
    o_ref[0] = _nt(w_ref[...], x_ref[0]).astype(o_ref.dtype)


def v_up_transposed(w_vt, latkr3, *, tn=512):
    bsz, s, _ = latkr3.shape
    tn = min(tn, s)
    nv = w_vt.shape[0]
    return pl.pallas_call(
        _vt_kernel,
        out_shape=jax.ShapeDtypeStruct((bsz, nv, s), BF16),
        grid=(bsz, s // tn),
        in_specs=[pl.BlockSpec((nv, LATKR), lambda b, t: (0, 0)),
                  pl.BlockSpec((1, tn, LATKR), lambda b, t: (b, t, 0))],
        out_specs=pl.BlockSpec((1, nv, tn), lambda b, t: (b, 0, t)),
        compiler_params=_cp(("parallel", "parallel")),
        name="v_up_transposed",
    )(w_vt, latkr3)


def _branch_kernel(oa_ref, ob_ref, oc_ref, w_ref, g0_ref, g1_ref, g2_ref, out_ref):
    gates = [jax.nn.sigmoid(g_ref[...]) for g_ref in (g0_ref, g1_ref, g2_ref)]
    us = [jnp.dot(o_ref[...], w_ref[n], preferred_element_type=F32)
          for n, o_ref in enumerate((oa_ref, ob_ref, oc_ref))]
    out_ref[...] = (gates[0] * us[0] + gates[1] * us[1] + gates[2] * us[2]).astype(out_ref.dtype)


def branch_merge(oa, ob, oc, w_branch, p):
    m = oa.shape[0]
    tm = min(1024, m)
    tn = 512
    nj = D_MODEL // tn
    o_spec = pl.BlockSpec((tm, BRANCH), lambda i, j: (i, 0))
    gate = lambda n: pl.BlockSpec((tm, tn), lambda i, j: (i, P_GATE // tn + n * nj + j))
    return pl.pallas_call(
        _branch_kernel,
        out_shape=jax.ShapeDtypeStruct((m, D_MODEL), BF16),
        grid=(m // tm, nj),
        in_specs=[o_spec, o_spec, o_spec,
                  pl.BlockSpec((3, BRANCH, tn), lambda i, j: (0, 0, j)),
                  gate(0), gate(1), gate(2)],
        out_specs=pl.BlockSpec((tm, tn), lambda i, j: (i, j)),
        compiler_params=_cp(("parallel", "parallel")),
        name="branch_merge",
    )(oa, ob, oc, w_branch, p, p, p)


def _pack_layer(l, w):
    w_in = w['w_in'][l]
    seg = lambda a, n: w_in[:, a:a + n]
    zeros = lambda n: jnp.zeros((D_MODEL, n), w_in.dtype)
    w_in_p = jnp.concatenate([
        seg(8032, 12288),
        seg(6688, 768), seg(7456, 512),
        seg(7968, 64), seg(4096, 8), seg(4104, 8), seg(6672, 16), zeros(32), zeros(128),
        seg(5136, 1536), seg(0, 3072), seg(3072, 1024), seg(4112, 1024)], axis=1).astype(BF16)
    w_uq = jnp.pad(w['mla_w_uq'][l], ((0, 0), (0, 0), (0, QSLOT - MLA_NOPE - MLA_ROPE)))
    w_uq = w_uq.reshape(MLA_Q_RANK, MLA_H * QSLOT).astype(BF16)
    wk = jnp.pad(w['mla_w_uk'][l], ((0, 0), (0, 0), (0, QSLOT - MLA_NOPE))).reshape(MLA_KV_RANK, MLA_H * QSLOT)
    eye = jnp.broadcast_to(jnp.eye(MLA_ROPE, dtype=F32)[:, None, :], (MLA_ROPE, MLA_H, MLA_ROPE))
    wkr = jnp.pad(eye, ((0, 0), (0, 0), (MLA_NOPE, QSLOT - MLA_NOPE - MLA_ROPE))).reshape(MLA_ROPE, MLA_H * QSLOT)
    wk = jnp.concatenate([wk, wkr, jnp.zeros((LATKR - MLA_KV_RANK - MLA_ROPE, MLA_H * QSLOT), F32)], axis=0)
    wv = jnp.pad(w['mla_w_uv'][l].reshape(MLA_KV_RANK, MLA_H * MLA_V), ((0, LATKR - MLA_KV_RANK), (0, 0)))
    w_k = wk.astype(BF16)
    w_vt = wv.T.astype(BF16)
    w_ukt = jnp.transpose(w['mla_w_uk'][l], (1, 2, 0)).astype(BF16)
    w_uv_h = jnp.transpose(w['mla_w_uv'][l], (1, 0, 2)).astype(BF16)

    def lane_row(vals, off):
        return jnp.zeros((1, 128), F32).at[0, off:off + vals.shape[0]].set(vals.astype(F32))

    return dict(
        attn_norm=w['attn_norm'][l], w_in=w_in_p,
        gdn_conv_w=w['gdn_conv_w'][l], gdn_alog=lane_row(w['gdn_A_log'][l], L_ALPHA),
        gdn_dtb=lane_row(w['gdn_dt_bias'][l], L_ALPHA), gdn_norm=w['gdn_norm'][l],
        ssm_conv_w=w['ssm_conv_w'][l], ssm_conv_b=w['ssm_conv_b'][l],
        ssm_alog=lane_row(w['ssm_A_log'][l], L_DT), ssm_dtb=lane_row(w['ssm_dt_bias'][l], L_DT),
        ssm_d=jnp.repeat(w['ssm_D'][l].astype(F32), SSM_P).reshape(1, BRANCH), ssm_norm=w['ssm_norm'][l],
        mla_q_norm=w['mla_q_norm'][l], w_uq=w_uq, mla_kv_norm=w['mla_kv_norm'][l], w_k=w_k, w_vt=w_vt, w_ukt=w_ukt, w_uv_h=w_uv_h,
        w_branch=w['w_branch'][l].astype(BF16), w_out=w['w_out'][l].astype(BF16),
        ffn_norm=w['ffn_norm'][l], w_up_il=interleave_cast_w_up(w['ffn_w_up'], l),
        ffn_conv_w=w['ffn_conv_w'][l], ffn_conv_b=w['ffn_conv_b'][l],
        w_down=w['ffn_w_down'][l].astype(BF16))


def _rope_tables(pos):
    half = MLA_ROPE // 2
    inv = ROPE_THETA ** (-jnp.arange(half, dtype=F32) / half)
    ang = pos.astype(F32)[:, None] * inv[None, :]
    cos, sin = jnp.cos(ang), jnp.sin(ang)
    n = pos.shape[0]
    z = lambda k: jnp.zeros((n, k), F32)
    cq = jnp.concatenate([jnp.ones((n, MLA_NOPE), F32), cos, cos, z(64)], axis=1)
    saq = jnp.concatenate([z(MLA_NOPE + half), sin, z(64)], axis=1)
    sbq = jnp.concatenate([z(MLA_NOPE), -sin, z(half + 64)], axis=1)
    ck = jnp.concatenate([cos, cos, z(64)], axis=1)
    sak = jnp.concatenate([z(half), sin, z(64)], axis=1)
    sbk = jnp.concatenate([-sin, z(half + 64)], axis=1)
    return cq, saq, sbq, ck, sak, sbk


def _pad_state(st):
    return jnp.pad(st, ((0, 0), (8 - st.shape[1], 0), (0, 0)))


def _layer(x, tabs, q_off, state, lw):
    latkr_past, gconv, gssm, mconv, mssm, fconv = state
    bsz, s, _ = x.shape
    m = bsz * s
    cq_t, saq_t, sbq_t, ck_t, sak_t, sbk_t = tabs
    x2 = x.reshape(m, D_MODEL)
    mm_tm = 512

    p = rmsnorm_matmul(x2, lw['attn_norm'], lw['w_in'], F32, tm=mm_tm, tn=1024, name="in_proj")
    p3 = p.reshape(bsz, s, P_TOT)

    qkv_act, gconv8 = causal_conv(p3, P_QKV, GDN_QKV, _pad_state(gconv), lw['gdn_conv_w'],
                                  jnp.zeros((GDN_QKV,), F32), act=True)
    o_a, gssm_new = gdn(qkv_act, p3, gssm, lw['gdn_alog'], lw['gdn_dtb'], lw['gdn_norm'])

    xbc_act, mconv8 = causal_conv(p3, P_XBC, SSM_CONV_DIM, _pad_state(mconv), lw['ssm_conv_w'],
                                  lw['ssm_conv_b'], act=True)
    o_b, mssm_new = ssd(xbc_act, p3, mssm.reshape(bsz, SSM_H // 2, 128, 128), lw['ssm_alog'], lw['ssm_dtb'],
                        lw['ssm_d'], lw['ssm_norm'])
    mssm_new = mssm_new.reshape(bsz, SSM_H, SSM_P, SSM_N)

    cqn, lat, kr, latkr = mla_prep(p, lw['mla_q_norm'], lw['mla_kv_norm'], ck_t, sak_t, sbk_t)
    tq_tab = min(mm_tm, m)
    q_full = matmul(cqn, lw['w_uq'], BF16, tm=mm_tm, tn=MLA_H * QSLOT, name="q_up",
                    extras=[(cq_t, (tq_tab, QSLOT), lambda i, j: (i, 0)),
                            (saq_t, (tq_tab, QSLOT), lambda i, j: (i, 0)),
                            (sbq_t, (tq_tab, QSLOT), lambda i, j: (i, 0))],
                    epilogue=_q_rope_epilogue)
    latkr3 = latkr.reshape(bsz, s, LATKR)
    q3 = q_full.reshape(bsz, s, MLA_H * QSLOT)
    if latkr_past is None and q_off == 0 and s % 512 == 0:
        k_full = matmul(latkr, lw['w_k'], BF16, tm=mm_tm, tn=1024, name="k_up")
        vt = v_up_transposed(lw['w_vt'], latkr3)
        o_c = flash_attention_causal(q3, k_full.reshape(bsz, s, MLA_H * QSLOT), vt)
    else:
        past = jnp.zeros((bsz, 0, LATKR), BF16) if latkr_past is None else latkr_past
        nk_real = past.shape[1] + s
        nk_pad = -(-nk_real // 128) * 128
        latkr_all = jnp.concatenate([past, latkr3, jnp.zeros((bsz, nk_pad - nk_real, LATKR), BF16)], axis=1)
        assert s & (s - 1) == 0 and MLA_H * s <= 512
        o_c = latent_attention(q_absorb(q_full, lw['w_ukt']), latkr_all, lw['w_uv_h'],
                               bsz=bsz, s=s, q_off=q_off, nk_real=nk_real)

    merged = branch_merge(o_a.reshape(m, BRANCH), o_b.reshape(m, BRANCH), o_c.reshape(m, BRANCH),
                          lw['w_branch'], p)
    x2 = matmul_residual(merged, lw['w_out'], x2, tm=mm_tm, tn=1024, name="out_proj")

    h2 = rmsnorm(x2, lw['ffn_norm'], BF16)
    if s % 512 == 0:
        act, fconv8 = ffn_up_fused(h2, lw['w_up_il'], _pad_state(fconv), lw['ffn_conv_w'], lw['ffn_conv_b'],
                                   bsz=bsz, tm=max(t for t in (512, 1024, 2048) if s % t == 0))
    else:
        up = matmul_tile_major(h2, lw['w_up_il'])
        act, fconv8 = ffn_conv_gate_il(up.reshape(bsz, s, 2 * D_FF), _pad_state(fconv), lw['ffn_conv_w'],
                                       lw['ffn_conv_b'])
    x2 = matmul_residual(act.reshape(m, D_FF), lw['w_down'], x2, tm=mm_tm, tn=512, name="ffn_down")

    new = (lat.reshape(bsz, s, MLA_KV_RANK), kr.reshape(bsz, s, MLA_ROPE),
           gconv8[:, 8 - gconv.shape[1]:], gssm_new, mconv8[:, 8 - mconv.shape[1]:], mssm_new,
           fconv8[:, 8 - fconv.shape[1]:])
    return x2.reshape(bsz, s, D_MODEL), new


def kernel(x_prompt, x_sample, cache_mla_latent, cache_mla_krope, state_gdn_conv, state_gdn_ssm, state_ssm_conv, state_ssm, state_ffn_conv, attn_norm, w_in, gdn_conv_w, gdn_A_log, gdn_dt_bias, gdn_norm, ssm_conv_w, ssm_conv_b, ssm_A_log, ssm_dt_bias, ssm_D, ssm_norm, mla_q_norm, mla_w_uq, mla_kv_norm, mla_w_uk, mla_w_uv, w_branch, w_out, ffn_norm, ffn_w_up, ffn_conv_w, ffn_conv_b, ffn_w_down, final_norm):
    weights = dict(attn_norm=attn_norm, w_in=w_in, gdn_conv_w=gdn_conv_w, gdn_A_log=gdn_A_log,
                   gdn_dt_bias=gdn_dt_bias, gdn_norm=gdn_norm, ssm_conv_w=ssm_conv_w, ssm_conv_b=ssm_conv_b,
                   ssm_A_log=ssm_A_log, ssm_dt_bias=ssm_dt_bias, ssm_D=ssm_D, ssm_norm=ssm_norm,
                   mla_q_norm=mla_q_norm, mla_w_uq=mla_w_uq, mla_kv_norm=mla_kv_norm, mla_w_uk=mla_w_uk,
                   mla_w_uv=mla_w_uv, w_branch=w_branch, w_out=w_out, ffn_norm=ffn_norm, ffn_w_up=ffn_w_up,
                   ffn_conv_w=ffn_conv_w, ffn_conv_b=ffn_conv_b, ffn_w_down=ffn_w_down)
    depth = w_in.shape[0]
    bp, sp, _ = x_prompt.shape
    bs, ss, _ = x_sample.shape
    n_past = cache_mla_latent.shape[2]
    pos_p = jnp.arange(sp, dtype=jnp.int32)
    pos_s = n_past + jnp.arange(ss, dtype=jnp.int32)
    tabs_p = _rope_tables(pos_p)
    tabs_s = tuple(jnp.tile(t, (bs, 1)) for t in _rope_tables(pos_s))
    zero_state = (
        None,
        jnp.zeros((bp,) + state_gdn_conv.shape[2:], F32),
        jnp.zeros((bp,) + state_gdn_ssm.shape[2:], F32),
        jnp.zeros((bp,) + state_ssm_conv.shape[2:], F32),
        jnp.zeros((bp,) + state_ssm.shape[2:], F32),
        jnp.zeros((bp,) + state_ffn_conv.shape[2:], F32),
    )
    xp, xs = x_prompt, x_sample
    new_p, new_s = [], []
    for l in range(depth):
        lw = _pack_layer(l, weights)
        xp, st_p = _layer(xp, tabs_p, 0, zero_state, lw)
        latkr_past = jnp.concatenate(
            [cache_mla_latent[l], cache_mla_krope[l],
             jnp.zeros((bs, n_past, LATKR - MLA_KV_RANK - MLA_ROPE), F32)], axis=-1).astype(BF16)
        st_in = (latkr_past, state_gdn_conv[l], state_gdn_ssm[l], state_ssm_conv[l], state_ssm[l],
                 state_ffn_conv[l])
        xs, st_s = _layer(xs, tabs_s, n_past, st_in, lw)
        new_p.append(st_p)
        new_s.append(st_s)
    y_prompt = rmsnorm(xp.reshape(bp * sp, D_MODEL), final_norm, F32).reshape(bp, sp, D_MODEL)
    y_sample = rmsnorm(xs.reshape(bs * ss, D_MODEL), final_norm, F32).reshape(bs, ss, D_MODEL)
    outs_p = [jnp.stack(t) for t in zip(*new_p)]
    outs_s = [jnp.stack(t) for t in zip(*new_s)]
    return (y_prompt, y_sample, *outs_p, *outs_s)
```

```python
import functools
import math

import jax
import jax.numpy as jnp
from jax import lax
from jax.experimental import pallas as pl
from jax.experimental.pallas import tpu as pltpu

F32 = jnp.float32
BF16 = jnp.bfloat16
HI = lax.Precision.HIGHEST

EPS = 1e-6
CHUNK = 64
D_MODEL = 4096
BRANCH = 1024
GDN_H, GDN_D = 8, 128
GDN_QKV = 3072
SSM_H, SSM_P, SSM_G, SSM_N = 16, 64, 2, 128
SSM_CONV_DIM = 1536
MLA_H, MLA_NOPE, MLA_ROPE, MLA_V = 8, 128, 64, 128
MLA_Q_RANK, MLA_KV_RANK = 768, 512
ROPE_THETA = 10000.0
D_FF = 11008
LATKR = 640
QSLOT = 256

P_GATE = 0
P_MLA = 12288
P_SMALL = P_MLA + 1280
P_XBC = 13824
P_QKV = 15360
P_ZA = 18432
P_ZB = 19456
P_TOT = 20480
L_BETA, L_ALPHA, L_DT = 64, 72, 80

VMEM_LIMIT = 56 * 1024 * 1024


def _cp(sem):
    return pltpu.CompilerParams(dimension_semantics=sem, vmem_limit_bytes=VMEM_LIMIT)


def _softplus(x):
    return jnp.maximum(x, 0.0) + jnp.log1p(jnp.exp(-jnp.abs(x)))


def _silu(x):
    return x * jax.nn.sigmoid(x)


def _nt(a, b):
    return lax.dot_general(a, b, (((1,), (1,)), ((), ())), preferred_element_type=F32)


def _tn(a, b):
    return lax.dot_general(a, b, (((0,), (0,)), ((), ())), preferred_element_type=F32)


def _rms_kernel(x_ref, w_ref, o_ref):
    x = x_ref[...]
    ms = jnp.mean(x * x, axis=-1, keepdims=True)
    o_ref[...] = (x * lax.rsqrt(ms + EPS) * w_ref[...]).astype(o_ref.dtype)


def rmsnorm(x, w, out_dtype):
    m, d = x.shape
    tm = min(256, m)
    return pl.pallas_call(
        _rms_kernel,
        out_shape=jax.ShapeDtypeStruct((m, d), out_dtype),
        grid=(m // tm,),
        in_specs=[pl.BlockSpec((tm, d), lambda i: (i, 0)),
                  pl.BlockSpec((1, d), lambda i: (0, 0))],
        out_specs=pl.BlockSpec((tm, d), lambda i: (i, 0)),
        compiler_params=_cp(("parallel",)),
        name="rmsnorm",
    )(x, w.reshape(1, d))


def _mm_kernel(*refs, nk, n_extra, epilogue):
    a_ref, b_ref = refs[0], refs[1]
    extras = refs[2:2 + n_extra]
    o_ref = refs[2 + n_extra]
    part = jnp.dot(a_ref[...], b_ref[...], preferred_element_type=F32)
    if nk == 1:
        o_ref[...] = epilogue(part, *extras).astype(o_ref.dtype)
        return
    acc_ref = refs[3 + n_extra]
    k = pl.program_id(2)

    @pl.when(k == 0)
    def _():
        acc_ref[...] = part

    @pl.when(k > 0)
    def _():
        acc_ref[...] += part

    @pl.when(k == nk - 1)
    def _():
        o_ref[...] = epilogue(acc_ref[...], *extras).astype(o_ref.dtype)


def matmul(a, b, out_dtype, *, tm, tn, tk=None, extras=(), epilogue=None, name="matmul"):
    m, kd = a.shape
    n = b.shape[1]
    tm, tn = min(tm, m), min(tn, n)
    while m % tm:
        tm //= 2
    tk = kd if tk is None else tk
    nk = kd // tk
    assert m % tm == 0 and n % tn == 0 and kd % tk == 0
    if epilogue is None:
        epilogue = lambda acc: acc
    in_specs = [pl.BlockSpec((tm, tk), lambda i, j, k: (i, k)),
                pl.BlockSpec((tk, tn), lambda i, j, k: (k, j))]
    args = [a, b]
    for arr, blk, imap in extras:
        in_specs.append(pl.BlockSpec(blk, functools.partial(lambda i, j, k, f: f(i, j), f=imap)))
        args.append(arr)
    scratch = [pltpu.VMEM((tm, tn), F32)] if nk > 1 else []
    return pl.pallas_call(
        functools.partial(_mm_kernel, nk=nk, n_extra=len(extras), epilogue=epilogue),
        out_shape=jax.ShapeDtypeStruct((m, n), out_dtype),
        grid=(m // tm, n // tn, nk),
        in_specs=in_specs,
        out_specs=pl.BlockSpec((tm, tn), lambda i, j, k: (i, j)),
        scratch_shapes=scratch,
        compiler_params=_cp(("parallel", "parallel", "arbitrary")),
        name=name,
    )(*args)


def _norm_mm_kernel(x_ref, nw_ref, b_ref, o_ref, a16_ref):
    @pl.when(pl.program_id(1) == 0)
    def _():
        x = x_ref[...]
        ms = jnp.mean(x * x, axis=-1, keepdims=True)
        a16_ref[...] = (x * lax.rsqrt(ms + EPS) * nw_ref[...]).astype(a16_ref.dtype)

    o_ref[...] = jnp.dot(a16_ref[...], b_ref[...], preferred_element_type=F32).astype(o_ref.dtype)


def rmsnorm_matmul(x, nw, b, out_dtype, *, tm, tn, name):
    m, kd = x.shape
    n = b.shape[1]
    tm = min(tm, m)
    assert m % tm == 0 and n % tn == 0
    return pl.pallas_call(
        _norm_mm_kernel,
        out_shape=jax.ShapeDtypeStruct((m, n), out_dtype),
        grid=(m // tm, n // tn),
        in_specs=[pl.BlockSpec((tm, kd), lambda i, j: (i, 0)),
                  pl.BlockSpec((1, kd), lambda i, j: (0, 0)),
                  pl.BlockSpec((kd, tn), lambda i, j: (0, j))],
        out_specs=pl.BlockSpec((tm, tn), lambda i, j: (i, j)),
        scratch_shapes=[pltpu.VMEM((tm, kd), BF16)],
        compiler_params=_cp(("parallel", "arbitrary")),
        name=name,
    )(x, nw.reshape(1, kd), b)


def _residual_epilogue(acc, r_ref):
    return acc + r_ref[...]


def matmul_residual(a, b, res, *, tm, tn, tk=None, name="matmul_residual"):
    tm_, tn_ = min(tm, a.shape[0]), min(tn, b.shape[1])
    return matmul(a, b, F32, tm=tm, tn=tn, tk=tk, name=name,
                  extras=[(res, (tm_, tn_), lambda i, j: (i, j))], epilogue=_residual_epilogue)


def _q_rope_epilogue(acc, c_ref, sa_ref, sb_ref):
    c, sa, sb = c_ref[...], sa_ref[...], sb_ref[...]
    scale = (MLA_NOPE + MLA_ROPE) ** -0.5 * math.log2(math.e)
    outs = []
    for h in range(MLA_H):
        xs = acc[:, h * QSLOT:(h + 1) * QSLOT]
        y = xs * c + pltpu.roll(xs, 32, 1) * sa + pltpu.roll(xs, QSLOT - 32, 1) * sb
        outs.append(y * scale)
    return jnp.concatenate(outs, axis=1)


def _conv_kernel(x_ref, st_ref, w_ref, b_ref, y_ref, ns_ref, xp_ref, *, width, ts, act):
    t = pl.program_id(2)
    nt = pl.num_programs(2)

    @pl.when(t == 0)
    def _():
        xp_ref[0:8, :] = st_ref[0]

    @pl.when(t > 0)
    def _():
        xp_ref[0:8, :] = xp_ref[ts:ts + 8, :]

    xp_ref[8:8 + ts, :] = x_ref[0]
    off = 8 - (width - 1)
    y = xp_ref[off:off + ts, :] * w_ref[0:1, :]
    for j in range(1, width):
        y = y + xp_ref[off + j:off + j + ts, :] * w_ref[j:j + 1, :]
    y = y + b_ref[...]
    if act:
        y = _silu(y)
    y_ref[0] = y.astype(y_ref.dtype)

    @pl.when(t == nt - 1)
    def _():
        ns_ref[0] = xp_ref[ts:ts + 8, :]


def causal_conv(x3, col_off, chans, state8, w, b, *, act, ct=512):
    bsz, s, _ = x3.shape
    width = w.shape[0]
    ts = min(2048, s)
    assert chans % ct == 0 and col_off % ct == 0 and s % ts == 0
    cb0 = col_off // ct
    return pl.pallas_call(
        functools.partial(_conv_kernel, width=width, ts=ts, act=act),
        out_shape=(jax.ShapeDtypeStruct((bsz, s, chans), F32),
                   jax.ShapeDtypeStruct((bsz, 8, chans), F32)),
        grid=(bsz, chans // ct, s // ts),
        in_specs=[pl.BlockSpec((1, ts, ct), lambda bi, c, t: (bi, t, cb0 + c)),
                  pl.BlockSpec((1, 8, ct), lambda bi, c, t: (bi, 0, c)),
                  pl.BlockSpec((width, ct), lambda bi, c, t: (0, c)),
                  pl.BlockSpec((1, ct), lambda bi, c, t: (0, c))],
        out_specs=(pl.BlockSpec((1, ts, ct), lambda bi, c, t: (bi, t, c)),
                   pl.BlockSpec((1, 8, ct), lambda bi, c, t: (bi, 0, c))),
        scratch_shapes=[pltpu.VMEM((ts + 8, ct), F32)],
        compiler_params=_cp(("parallel", "parallel", "arbitrary")),
        name="causal_conv",
    )(x3, state8, w, b.reshape(1, chans))


HIST = 8


FFN_TN = 256


def _interleave_ffn(x):
    lead = x.shape[:-1]
    x = x.reshape(*lead, 2, D_FF // FFN_TN, FFN_TN)
    return jnp.swapaxes(x, -3, -2).reshape(*lead, 2 * D_FF)


def _deinterleave_ffn(x):
    lead = x.shape[:-1]
    x = x.reshape(*lead, D_FF // FFN_TN, 2, FFN_TN)
    return jnp.swapaxes(x, -3, -2).reshape(*lead, 2 * D_FF)


def _il_cast_kernel(a_ref, b_ref, o_ref):
    o_ref[0, :, :FFN_TN] = a_ref[0].astype(o_ref.dtype)
    o_ref[0, :, FFN_TN:] = b_ref[0].astype(o_ref.dtype)


def interleave_cast_w_up(w_all, l, *, tk=1024):
    kd = w_all.shape[1]
    nc = D_FF // FFN_TN
    return pl.pallas_call(
        _il_cast_kernel,
        out_shape=jax.ShapeDtypeStruct((nc, kd, 2 * FFN_TN), BF16),
        grid=(kd // tk, nc),
        in_specs=[pl.BlockSpec((1, tk, FFN_TN), lambda k, j: (l, k, j)),
                  pl.BlockSpec((1, tk, FFN_TN), lambda k, j: (l, k, nc + j))],
        out_specs=pl.BlockSpec((1, tk, 2 * FFN_TN), lambda k, j: (j, k, 0)),
        compiler_params=_cp(("parallel", "parallel")),
        name="interleave_cast_w_up",
    )(w_all, w_all)


def _mm_tiles_kernel(a_ref, w_ref, o_ref):
    o_ref[...] = jnp.dot(a_ref[...], w_ref[0], preferred_element_type=F32)


def matmul_tile_major(a, w_tiles):
    m, kd = a.shape
    nt, _, tn = w_tiles.shape
    return pl.pallas_call(
        _mm_tiles_kernel,
        out_shape=jax.ShapeDtypeStruct((m, nt * tn), F32),
        grid=(nt,),
        in_specs=[pl.BlockSpec((m, kd), lambda j: (0, 0)),
                  pl.BlockSpec((1, kd, tn), lambda j: (j, 0, 0))],
        out_specs=pl.BlockSpec((m, tn), lambda j: (0, j)),
        compiler_params=_cp(("parallel",)),
        name="ffn_up",
    )(a, w_tiles)


def _ffn_conv_gate_il_kernel(x_ref, st_ref, cw_ref, cb_ref, y_ref, ns_ref, xp_ref, *, width, s):
    xp_ref[0:8, :] = st_ref[0]
    xp_ref[8:8 + s, :] = x_ref[0]
    off = 8 - (width - 1)
    y = xp_ref[off:off + s, :] * cw_ref[0:1, :]
    for t in range(1, width):
        y = y + xp_ref[off + t:off + t + s, :] * cw_ref[t:t + 1, :]
    y = y + cb_ref[...]
    ns_ref[0] = xp_ref[s:s + 8, :]
    for c in range(D_FF // FFN_TN):
        g = y[:, 2 * c * FFN_TN:(2 * c + 1) * FFN_TN]
        v = y[:, (2 * c + 1) * FFN_TN:(2 * c + 2) * FFN_TN]
        y_ref[0, :, c * FFN_TN:(c + 1) * FFN_TN] = (_silu(g) * v).astype(y_ref.dtype)


def ffn_conv_gate_il(up3, state8, cw, cb):
    bsz, s, n2 = up3.shape
    width = cw.shape[0]
    y, ns = pl.pallas_call(
        functools.partial(_ffn_conv_gate_il_kernel, width=width, s=s),
        out_shape=(jax.ShapeDtypeStruct((bsz, s, D_FF), BF16),
                   jax.ShapeDtypeStruct((bsz, 8, n2), F32)),
        grid=(bsz,),
        in_specs=[pl.BlockSpec((1, s, n2), lambda b: (b, 0, 0)),
                  pl.BlockSpec((1, 8, n2), lambda b: (b, 0, 0)),
                  pl.BlockSpec((width, n2), lambda b: (0, 0)),
                  pl.BlockSpec((1, n2), lambda b: (0, 0))],
        out_specs=(pl.BlockSpec((1, s, D_FF), lambda b: (b, 0, 0)),
                   pl.BlockSpec((1, 8, n2), lambda b: (b, 0, 0))),
        scratch_shapes=[pltpu.VMEM((s + 8, n2), F32)],
        compiler_params=_cp(("parallel",)),
        name="ffn_conv_gate_il",
    )(up3, _interleave_ffn(state8), _interleave_ffn(cw), _interleave_ffn(cb.reshape(1, n2)))
    return y, _deinterleave_ffn(ns)


FFN_SUB = 256


def _ffn_row_bounds(tm):
    return tuple(range(0, tm + 1, min(tm, 2 * FFN_SUB)))


def _ffn_up_fused_kernel(a_ref, w_ref, st_ref, cw_ref, cb_ref, y_ref, ns_ref, carry_ref, *xp_refs,
                         tm, tiles_per_seq, width):
    i = pl.program_id(0)
    j = pl.program_id(1)

    @pl.when((i % tiles_per_seq) == 0)
    def _():
        carry_ref[j] = st_ref[0]

    bounds = _ffn_row_bounds(tm)
    nsub = len(bounds) - 1
    sizes = [bounds[r + 1] - bounds[r] for r in range(nsub)]
    off = HIST - (width - 1)
    cw = cw_ref[...]
    cb = cb_ref[...]

    def matmul_block(r):
        xp_refs[r][HIST:HIST + sizes[r], :] = jnp.dot(a_ref[bounds[r]:bounds[r + 1], :], w_ref[0],
                                                      preferred_element_type=F32)

    def conv_block(r):
        xp_ref = xp_refs[r]
        xp_ref[0:HIST, :] = carry_ref[j] if r == 0 else xp_refs[r - 1][sizes[r - 1]:sizes[r - 1] + HIST, :]
        y = xp_ref[off:off + sizes[r], :] * cw[0:1, :]
        for t in range(1, width):
            y = y + xp_ref[off + t:off + t + sizes[r], :] * cw[t:t + 1, :]
        y = y + cb
        y_ref[bounds[r]:bounds[r + 1], :] = (_silu(y[:, :FFN_TN]) * y[:, FFN_TN:]).astype(y_ref.dtype)

    matmul_block(0)
    for r in range(nsub):
        if r + 1 < nsub:
            matmul_block(r + 1)
        conv_block(r)
    last = xp_refs[nsub - 1][sizes[-1]:sizes[-1] + HIST, :]
    carry_ref[j] = last
    ns_ref[0] = last


def ffn_up_fused(h2, w_up_il, state8, cw, cb, *, bsz, tm=1024):
    m, kd = h2.shape
    s = m // bsz
    tm = min(tm, s)
    assert s % tm == 0 and tm % FFN_SUB == 0
    bounds = _ffn_row_bounds(tm)
    tiles_per_seq = s // tm
    width = cw.shape[0]
    nc = D_FF // FFN_TN
    tn2 = 2 * FFN_TN
    y, ns = pl.pallas_call(
        functools.partial(_ffn_up_fused_kernel, tm=tm, tiles_per_seq=tiles_per_seq, width=width),
        out_shape=(jax.ShapeDtypeStruct((m, D_FF), BF16),
                   jax.ShapeDtypeStruct((m // tm, 8, 2 * D_FF), F32)),
        grid=(m // tm, nc),
        in_specs=[pl.BlockSpec((tm, kd), lambda i, j: (i, 0), pipeline_mode=pl.Buffered(1)),
                  pl.BlockSpec((1, kd, tn2), lambda i, j: (j, 0, 0)),
                  pl.BlockSpec((1, 8, tn2), lambda i, j: (i // tiles_per_seq, 0, j)),
                  pl.BlockSpec((width, tn2), lambda i, j: (0, j)),
                  pl.BlockSpec((1, tn2), lambda i, j: (0, j))],
        out_specs=(pl.BlockSpec((tm, FFN_TN), lambda i, j: (i, j)),
                   pl.BlockSpec((1, 8, tn2), lambda i, j: (i, 0, j))),
        scratch_shapes=[pltpu.VMEM((nc, HIST, tn2), F32)]
        + [pltpu.VMEM((hi - lo + HIST, tn2), F32) for lo, hi in zip(bounds[:-1], bounds[1:])],
        compiler_params=_cp(("arbitrary", "arbitrary")),
        name="ffn_up_fused",
    )(h2, w_up_il, _interleave_ffn(state8), _interleave_ffn(cw), _interleave_ffn(cb.reshape(1, 2 * D_FF)))
    return y, _deinterleave_ffn(ns[tiles_per_seq - 1::tiles_per_seq])


def _chunk_masks(lt, c):
    sh = int(math.log2(c))
    ri = lax.broadcasted_iota(jnp.int32, (lt, lt), 0)
    ci = lax.broadcasted_iota(jnp.int32, (lt, lt), 1)
    same = (ri >> sh) == (ci >> sh)
    return ri, ci, same


def _split3(x):
    hi = x.astype(BF16)
    r = x - hi.astype(F32)
    mid = r.astype(BF16)
    lo = (r - mid.astype(F32)).astype(BF16)
    return hi, mid, lo


def _dot_exact_lhs(l16, x):
    hi, mid, lo = _split3(x)
    return (jnp.dot(l16, hi, preferred_element_type=F32) + jnp.dot(l16, mid, preferred_element_type=F32)
            + jnp.dot(l16, lo, preferred_element_type=F32))


def _dot_hi(a, b):
    a_hi = a.astype(BF16)
    a_lo = (a - a_hi.astype(F32)).astype(BF16)
    b_hi = b.astype(BF16)
    b_lo = (b - b_hi.astype(F32)).astype(BF16)
    return (jnp.dot(a_hi, b_hi, preferred_element_type=F32) + jnp.dot(a_hi, b_lo, preferred_element_type=F32)
            + jnp.dot(a_lo, b_hi, preferred_element_type=F32))


def _gdn_kernel(q_ref, k_ref, v_ref, sm_ref, z_ref, s0_ref, alog_ref, dtb_ref, nw_ref,
                o_ref, sout_ref, s_sc, vn_sc, *, lt, c, hps):
    hg = pl.program_id(1)
    t = pl.program_id(2)
    nt = pl.num_programs(2)

    @pl.when(t == 0)
    def _():
        s_sc[...] = s0_ref[0]

    sm = sm_ref[0]
    lane = lax.broadcasted_iota(jnp.int32, (1, 128), 1)
    beta_full = jax.nn.sigmoid(sm)
    g_full = -jnp.exp(alog_ref[...]) * _softplus(sm + dtb_ref[...])
    ri, ci, same = _chunk_masks(lt, c)
    tri = same & (ci <= ri)
    strict = same & (ci < ri)
    eye_b = ri == ci
    eye = jnp.where(eye_b, 1.0, 0.0)
    gc_full = _dot_exact_lhs(jnp.where(tri, 1.0, 0.0).astype(BF16), g_full)
    nw = nw_ref[...]

    hs = range(hps)
    cols = [slice(hh * GDN_D, (hh + 1) * GDN_D) for hh in hs]
    dot16 = lambda x, y: jnp.dot(x.astype(BF16), y.astype(BF16), preferred_element_type=F32)
    ks, k16s, kbs, gcs, ms, avs, qds, rhss = [], [], [], [], [], [], [], []
    for hh in hs:
        h = hg * hps + hh
        q = q_ref[0, :, cols[hh]]
        k = k_ref[0, :, cols[hh]]
        q = q * lax.rsqrt(jnp.sum(q * q, axis=-1, keepdims=True) + EPS) * (GDN_D ** -0.5)
        k = k * lax.rsqrt(jnp.sum(k * k, axis=-1, keepdims=True) + EPS)
        beta = jnp.sum(jnp.where(lane == L_BETA + h, beta_full, 0.0), axis=-1, keepdims=True)
        gc = jnp.sum(jnp.where(lane == L_ALPHA + h, gc_full, 0.0), axis=-1, keepdims=True)
        gr = jnp.sum(jnp.where(eye_b, gc, 0.0), axis=0, keepdims=True)
        decay = jnp.exp(jnp.where(tri, gc - gr, 0.0))
        kb = k * beta
        k16 = k.astype(BF16)
        ms.append(jnp.where(strict, _nt(kb.astype(BF16), k16) * decay, 0.0))
        avs.append(jnp.where(tri, _nt(q.astype(BF16), k16) * decay, 0.0).astype(BF16))
        eg = jnp.exp(gc)
        rhss.append(jnp.concatenate([v_ref[0, :, cols[hh]] * beta, kb * eg], axis=1).astype(BF16))
        qds.append((q * eg).astype(BF16))
        ks.append(k)
        gcs.append(gc)

    ps = [-m for m in ms]
    ts = [eye + p for p in ps]
    for _ in range(int(math.log2(c)) - 1):
        ps = [dot16(p, p) for p in ps]
        ts = [tt + dot16(p, tt) for p, tt in zip(ps, ts)]
    rs = [eye - _dot_hi(eye + m, tt) for m, tt in zip(ms, ts)]
    ts = [tt + dot16(tt, r) for tt, r in zip(ts, rs)]
    uws = [jnp.dot(tt.astype(BF16), rhs, preferred_element_type=F32) for tt, rhs in zip(ts, rhss)]
    us = [uw[:, :GDN_D] for uw in uws]
    ws = [uw[:, GDN_D:].astype(BF16) for uw in uws]

    ss = [s_sc[hh] for hh in hs]
    for hh in hs:
        vn_sc[hh] = us[hh].astype(BF16)
    for cc in range(lt // c):
        r0 = cc * c
        rows = slice(r0, r0 + c)
        for hh in hs:
            s16 = ss[hh].astype(BF16)
            vnew = us[hh][rows] - jnp.dot(ws[hh][rows], s16, preferred_element_type=F32)
            vn16 = vnew.astype(BF16)
            vn_sc[hh, rows, :] = vn16
            o = (jnp.dot(qds[hh][rows], s16, preferred_element_type=F32)
                 + jnp.dot(avs[hh][rows, :], vn_sc[hh], preferred_element_type=F32))
            glast = gcs[hh][r0 + c - 1:r0 + c, :]
            kd = (ks[hh][rows] * jnp.exp(glast - gcs[hh][rows])).astype(BF16)
            ss[hh] = ss[hh] * jnp.exp(glast) + _tn(kd, vn16)
            mean_sq = jnp.mean(o * o, axis=-1, keepdims=True)
            o = o * lax.rsqrt(mean_sq + EPS) * nw * _silu(z_ref[0, rows, cols[hh]])
            o_ref[0, rows, cols[hh]] = o.astype(o_ref.dtype)
    for hh in hs:
        s_sc[hh] = ss[hh]

    @pl.when(t == nt - 1)
    def _():
        sout_ref[0] = s_sc[...]


def gdn(qkv3, p3, s0, alog_row, dtb_row, norm_w, *, hps=8):
    bsz, s, _ = qkv3.shape
    c = min(CHUNK, s)
    lt = min(256, s)
    hw = hps * GDN_D
    ng = GDN_H // hps
    small_blk = P_SMALL // 128
    za_blk = P_ZA // hw
    blk = lambda off: pl.BlockSpec((1, lt, hw), lambda b, h, t: (b, t, off + h))
    return pl.pallas_call(
        functools.partial(_gdn_kernel, lt=lt, c=c, hps=hps),
        out_shape=(jax.ShapeDtypeStruct((bsz, s, BRANCH), BF16),
                   jax.ShapeDtypeStruct((bsz, GDN_H, GDN_D, GDN_D), F32)),
        grid=(bsz, ng, s // lt),
        in_specs=[blk(0), blk(ng), blk(2 * ng),
                  pl.BlockSpec((1, lt, 128), lambda b, h, t: (b, t, small_blk)),
                  blk(za_blk),
                  pl.BlockSpec((1, hps, GDN_D, GDN_D), lambda b, h, t: (b, h, 0, 0)),
                  pl.BlockSpec((1, 128), lambda b, h, t: (0, 0)),
                  pl.BlockSpec((1, 128), lambda b, h, t: (0, 0)),
                  pl.BlockSpec((1, 128), lambda b, h, t: (0, 0))],
        out_specs=(pl.BlockSpec((1, lt, hw), lambda b, h, t: (b, t, h)),
                   pl.BlockSpec((1, hps, GDN_D, GDN_D), lambda b, h, t: (b, h, 0, 0))),
        scratch_shapes=[pltpu.VMEM((hps, GDN_D, GDN_D), F32), pltpu.VMEM((hps, lt, GDN_D), BF16)],
        compiler_params=_cp(("parallel", "parallel", "arbitrary")),
        name="gdn",
    )(qkv3, qkv3, qkv3, p3, p3, s0, alog_row, dtb_row, norm_w.reshape(1, GDN_D))


def _ssd_kernel(x_ref, b_ref, c_ref, sm_ref, z_ref, h0_ref, alog_ref, dtb_ref, d_ref, nw_ref,
                y_ref, hout_ref, hs_sc, y_sc, *, lt, c):
    g = pl.program_id(1)
    t = pl.program_id(2)
    nt = pl.num_programs(2)
    npair = SSM_H // SSM_G // 2

    @pl.when(t == 0)
    def _():
        hs_sc[...] = h0_ref[0]

    sm = sm_ref[0]
    dt_full = _softplus(sm + dtb_ref[...])
    da = dt_full * (-jnp.exp(alog_ref[...]))
    ri, ci, same = _chunk_masks(lt, c)
    tri = same & (ci <= ri)
    eye_b = ri == ci
    acum_full = _dot_exact_lhs(jnp.where(tri, 1.0, 0.0).astype(BF16), da)
    nch = lt // c
    lasts = [acum_full[cc * c + c - 1:cc * c + c, :] for cc in range(nch)]
    alast_full = jnp.concatenate([jnp.broadcast_to(l, (c, 128)) for l in lasts], axis=0)
    ea_full = jnp.exp(acum_full)
    dtw_full = dt_full * jnp.exp(alast_full - acum_full)

    bm = b_ref[0]
    cm = c_ref[0]
    bm16 = bm.astype(BF16)
    cm16 = cm.astype(BF16)
    cb = jnp.where(tri, _nt(cm16, bm16), 0.0)
    lane = lax.broadcasted_iota(jnp.int32, (1, 128), 1)
    lo = lane < SSM_P
    rowlo = lax.broadcasted_iota(jnp.int32, (128, 1), 0) < SSM_P
    x = x_ref[0]
    z = z_ref[0]
    d_row = d_ref[...]

    def ext(full, l):
        return jnp.sum(jnp.where(lane == l, full, 0.0), axis=-1, keepdims=True)

    pairs = range(npair)
    acs, eas, xps, xws, ys = [], [], [], [], []
    for j in pairs:
        la = L_DT + g * (SSM_H // SSM_G) + 2 * j
        lb = la + 1
        ac_a, ac_b = ext(acum_full, la), ext(acum_full, lb)
        row_a = jnp.sum(jnp.where(eye_b, ac_a, 0.0), axis=0, keepdims=True)
        row_b = jnp.sum(jnp.where(eye_b, ac_b, 0.0), axis=0, keepdims=True)
        l_a = (jnp.exp(jnp.where(tri, ac_a - row_a, 0.0)) * cb).astype(BF16)
        l_b = (jnp.exp(jnp.where(tri, ac_b - row_b, 0.0)) * cb).astype(BF16)
        dt_pair = jnp.where(lo, ext(dt_full, la), ext(dt_full, lb))
        dtw_pair = jnp.where(lo, ext(dtw_full, la), ext(dtw_full, lb))
        xp = x[:, j * 128:(j + 1) * 128]
        xdt = xp * dt_pair
        ys.append(jnp.dot(l_a, jnp.where(lo, xdt, 0.0).astype(BF16), preferred_element_type=F32)
                  + jnp.dot(l_b, jnp.where(lo, 0.0, xdt).astype(BF16), preferred_element_type=F32))
        xws.append((xp * dtw_pair).astype(BF16))
        xps.append(xp)
        acs.append((ac_a, ac_b))
        eas.append(jnp.where(lo, ext(ea_full, la), ext(ea_full, lb)))

    hss = [hs_sc[j] for j in pairs]
    yoffs = [[] for _ in pairs]
    for cc in range(nch):
        r0 = cc * c
        for j in pairs:
            ac_a, ac_b = acs[j]
            yoffs[j].append(_nt(cm16[r0:r0 + c], hss[j].astype(BF16)) * eas[j][r0:r0 + c])
            cs = _tn(xws[j][r0:r0 + c], bm16[r0:r0 + c])
            al = jnp.where(rowlo, jnp.exp(ac_a[r0 + c - 1:r0 + c, :]), jnp.exp(ac_b[r0 + c - 1:r0 + c, :]))
            hss[j] = hss[j] * al + cs
    for j in pairs:
        hs_sc[j] = hss[j]
        yoff = yoffs[j][0] if nch == 1 else jnp.concatenate(yoffs[j], axis=0)
        cols = slice(j * 128, (j + 1) * 128)
        y_sc[:, cols] = (ys[j] + yoff + d_row[:, cols] * xps[j]) * _silu(z[:, cols])

    yy = y_sc[...]
    ms = jnp.mean(yy * yy, axis=-1, keepdims=True)
    y_ref[0] = (yy * lax.rsqrt(ms + EPS) * nw_ref[...]).astype(y_ref.dtype)

    @pl.when(t == nt - 1)
    def _():
        hout_ref[0] = hs_sc[...]


def ssd(xbc3, p3, h0, alog_row, dtb_row, d_row, norm_w):
    bsz, s, _ = xbc3.shape
    c = min(CHUNK, s)
    lt = min(256, s)
    gw = BRANCH // SSM_G
    npair = SSM_H // SSM_G // 2
    small_blk = P_SMALL // 128
    zb_blk = P_ZB // gw
    return pl.pallas_call(
        functools.partial(_ssd_kernel, lt=lt, c=c),
        out_shape=(jax.ShapeDtypeStruct((bsz, s, BRANCH), BF16),
                   jax.ShapeDtypeStruct((bsz, SSM_H // 2, 128, 128), F32)),
        grid=(bsz, SSM_G, s // lt),
        in_specs=[pl.BlockSpec((1, lt, gw), lambda b, g, t: (b, t, g)),
                  pl.BlockSpec((1, lt, 128), lambda b, g, t: (b, t, BRANCH // 128 + g)),
                  pl.BlockSpec((1, lt, 128), lambda b, g, t: (b, t, BRANCH // 128 + SSM_G + g)),
                  pl.BlockSpec((1, lt, 128), lambda b, g, t: (b, t, small_blk)),
                  pl.BlockSpec((1, lt, gw), lambda b, g, t: (b, t, zb_blk + g)),
                  pl.BlockSpec((1, npair, 128, 128), lambda b, g, t: (b, g, 0, 0)),
                  pl.BlockSpec((1, 128), lambda b, g, t: (0, 0)),
                  pl.BlockSpec((1, 128), lambda b, g, t: (0, 0)),
                  pl.BlockSpec((1, gw), lambda b, g, t: (0, g)),
                  pl.BlockSpec((1, gw), lambda b, g, t: (0, g))],
        out_specs=(pl.BlockSpec((1, lt, gw), lambda b, g, t: (b, t, g)),
                   pl.BlockSpec((1, npair, 128, 128), lambda b, g, t: (b, g, 0, 0))),
        scratch_shapes=[pltpu.VMEM((npair, 128, 128), F32), pltpu.VMEM((lt, gw), F32)],
        compiler_params=_cp(("parallel", "parallel", "arbitrary")),
        name="ssd",
    )(xbc3, xbc3, xbc3, p3, p3, h0, alog_row, dtb_row, d_row, norm_w.reshape(1, BRANCH))


def _mla_prep_kernel(p_ref, qn_ref, kvn_ref, ck_ref, sak_ref, sbk_ref,
                     cq_ref, lat_ref, kr_ref, latkr_ref):
    blk = p_ref[...]
    cq = blk[:, 0:MLA_Q_RANK]
    cq = cq * lax.rsqrt(jnp.mean(cq * cq, axis=-1, keepdims=True) + EPS) * qn_ref[...]
    cq_ref[...] = cq.astype(cq_ref.dtype)
    ckv = blk[:, MLA_Q_RANK:MLA_Q_RANK + MLA_KV_RANK]
    lat = ckv * lax.rsqrt(jnp.mean(ckv * ckv, axis=-1, keepdims=True) + EPS) * kvn_ref[...]
    lat_ref[...] = lat
    sm = blk[:, 1280:1408]
    y = sm * ck_ref[...] + pltpu.roll(sm, 32, 1) * sak_ref[...] + pltpu.roll(sm, 96, 1) * sbk_ref[...]
    kr_ref[...] = y[:, 0:MLA_ROPE]
    latkr_ref[:, 0:MLA_KV_RANK] = lat.astype(latkr_ref.dtype)
    latkr_ref[:, MLA_KV_RANK:LATKR] = y.astype(latkr_ref.dtype)


def mla_prep(p, q_norm, kv_norm, ck, sak, sbk):
    m = p.shape[0]
    tm = min(256, m)
    mla_blk = P_MLA // 1536
    row = lambda n: pl.BlockSpec((tm, n), lambda i: (i, 0))
    return pl.pallas_call(
        _mla_prep_kernel,
        out_shape=(jax.ShapeDtypeStruct((m, MLA_Q_RANK), BF16),
                   jax.ShapeDtypeStruct((m, MLA_KV_RANK), F32),
                   jax.ShapeDtypeStruct((m, MLA_ROPE), F32),
                   jax.ShapeDtypeStruct((m, LATKR), BF16)),
        grid=(m // tm,),
        in_specs=[pl.BlockSpec((tm, 1536), lambda i: (i, mla_blk)),
                  pl.BlockSpec((1, MLA_Q_RANK), lambda i: (0, 0)),
                  pl.BlockSpec((1, MLA_KV_RANK), lambda i: (0, 0)),
                  row(128), row(128), row(128)],
        out_specs=(row(MLA_Q_RANK), row(MLA_KV_RANK), row(MLA_ROPE), row(LATKR)),
        compiler_params=_cp(("parallel",)),
        name="mla_prep",
    )(p, q_norm.reshape(1, -1), kv_norm.reshape(1, -1), ck, sak, sbk)


def _pack_cache_kernel(lat_ref, kr_ref, o_ref):
    o_ref[0, :, 0:MLA_KV_RANK] = lat_ref[0, 0].astype(o_ref.dtype)
    o_ref[0, :, MLA_KV_RANK:MLA_KV_RANK + MLA_ROPE] = kr_ref[0, 0].astype(o_ref.dtype)
    o_ref[0, :, MLA_KV_RANK + MLA_ROPE:LATKR] = jnp.zeros((o_ref.shape[1], LATKR - MLA_KV_RANK - MLA_ROPE),
                                                          o_ref.dtype)


def pack_latent_cache(cache_lat, cache_kr, l, *, tp=1024):
    _, bsz, n_past, _ = cache_lat.shape
    tp = min(tp, n_past)
    assert n_past % tp == 0
    return pl.pallas_call(
        _pack_cache_kernel,
        out_shape=jax.ShapeDtypeStruct((bsz, n_past, LATKR), BF16),
        grid=(bsz, n_past // tp),
        in_specs=[pl.BlockSpec((1, 1, tp, MLA_KV_RANK), lambda b, t: (l, b, t, 0)),
                  pl.BlockSpec((1, 1, tp, MLA_ROPE), lambda b, t: (l, b, t, 0))],
        out_specs=pl.BlockSpec((1, tp, LATKR), lambda b, t: (b, t, 0)),
        compiler_params=_cp(("parallel", "parallel")),
        name="pack_latent_cache",
    )(cache_lat, cache_kr)


def _q_absorb_kernel(q_ref, wk_ref, o_ref):
    q = q_ref[...]
    o_ref[0, :, 0:MLA_KV_RANK] = jnp.dot(q[:, 0:MLA_NOPE], wk_ref[0],
                                         preferred_element_type=F32).astype(o_ref.dtype)
    o_ref[0, :, MLA_KV_RANK:LATKR] = q[:, MLA_NOPE:QSLOT]


def q_absorb(q_full, w_ukt):
    m = q_full.shape[0]
    return pl.pallas_call(
        _q_absorb_kernel,
        out_shape=jax.ShapeDtypeStruct((MLA_H, m, LATKR), BF16),
        grid=(MLA_H,),
        in_specs=[pl.BlockSpec((m, QSLOT), lambda h: (0, h)),
                  pl.BlockSpec((1, MLA_NOPE, MLA_KV_RANK), lambda h: (h, 0, 0))],
        out_specs=pl.BlockSpec((1, m, LATKR), lambda h: (h, 0, 0)),
        compiler_params=_cp(("parallel",)),
        name="q_absorb",
    )(q_full, w_ukt)


def _latent_attn_kernel(q_ref, kv_ref, wv_ref, o_ref, *, s, q_off, nk_real):
    kv = kv_ref[0]
    nk = kv.shape[0]
    q = q_ref[...].reshape(MLA_H * s, LATKR)
    sc = _nt(q, kv)
    tok = lax.broadcasted_iota(jnp.int32, (MLA_H * s, 1), 0) & (s - 1)
    kidx = lax.broadcasted_iota(jnp.int32, (1, nk), 1)
    vis = ((kidx >> 6) <= ((q_off + tok) >> 6)) & (kidx < nk_real)
    sc = jnp.where(vis, sc, -1e30)
    p = jnp.where(vis, jnp.exp2(sc - jnp.max(sc, axis=-1, keepdims=True)), 0.0)
    l = jnp.sum(p, axis=-1, keepdims=True)
    ctx = (jnp.dot(p.astype(BF16), kv, preferred_element_type=F32) / l).astype(BF16)
    for h in range(MLA_H):
        o_ref[0, :, h * MLA_V:(h + 1) * MLA_V] = jnp.dot(
            ctx[h * s:(h + 1) * s, 0:MLA_KV_RANK], wv_ref[h], preferred_element_type=F32).astype(o_ref.dtype)


def latent_attention(q_lat, latkr_all, w_uv_h, *, bsz, s, q_off, nk_real):
    tk_tot = latkr_all.shape[1]
    return pl.pallas_call(
        functools.partial(_latent_attn_kernel, s=s, q_off=q_off, nk_real=nk_real),
        out_shape=jax.ShapeDtypeStruct((bsz, s, BRANCH), BF16),
        grid=(bsz,),
        in_specs=[pl.BlockSpec((MLA_H, s, LATKR), lambda b: (0, b, 0)),
                  pl.BlockSpec((1, tk_tot, LATKR), lambda b: (b, 0, 0)),
                  pl.BlockSpec((MLA_H, MLA_KV_RANK, MLA_V), lambda b: (0, 0, 0))],
        out_specs=pl.BlockSpec((1, s, BRANCH), lambda b: (b, 0, 0)),
        compiler_params=_cp(("parallel",)),
        name="latent_attention",
    )(q_lat, latkr_all, w_uv_h)


FLASH_STRIP = 512


def _flash_t_kernel(qi_ref, kj_ref, q_ref, k_ref, vt_ref, o_ref, m_sc, l_sc, acc_sc, *, tq, tk):
    pidx = pl.program_id(2)
    i = qi_ref[pidx]
    j = kj_ref[pidx]

    @pl.when(j == 0)
    def _():
        m_sc[...] = jnp.full(m_sc.shape, -1e30, F32)
        l_sc[...] = jnp.zeros(l_sc.shape, F32)
        acc_sc[...] = jnp.zeros(acc_sc.shape, F32)

    nstrip = tq // FLASH_STRIP

    def strip_mode(c, key_off):
        q_lo, q_hi = (c * FLASH_STRIP) // CHUNK, (c * FLASH_STRIP + FLASH_STRIP - 1) // CHUNK
        k_lo, k_hi = key_off // CHUNK, (key_off + tk - 1) // CHUNK
        if k_lo > q_hi:
            return "hidden"
        return "visible" if k_hi <= q_lo else "masked"

    def tile(key_off):
        modes = ["visible" if key_off is None else strip_mode(c, key_off) for c in range(nstrip)]
        live = [c for c in range(nstrip) if modes[c] != "hidden"]
        k = k_ref[0]
        vt = vt_ref[0]
        kchunk = (j * tk + lax.broadcasted_iota(jnp.int32, (tk, 1), 0)) >> 6
        s_v, p_v = {}, {}

        def scores(c):
            rows = slice(c * FLASH_STRIP, (c + 1) * FLASH_STRIP)
            s = _nt(k, q_ref[0, rows, :])
            vis = None
            if modes[c] == "masked":
                qchunk = (i * tq + c * FLASH_STRIP + lax.broadcasted_iota(jnp.int32, (1, FLASH_STRIP), 1)) >> 6
                vis = kchunk <= qchunk
                s = jnp.where(vis, s, -1e30)
            s_v[c] = (s, vis)

        def softmax(c):
            lanes = slice(c * FLASH_STRIP, (c + 1) * FLASH_STRIP)
            s, vis = s_v.pop(c)
            m_prev = m_sc[:, lanes]
            m_new = jnp.maximum(m_prev, jnp.max(s, axis=0, keepdims=True))
            alpha = jnp.exp2(m_prev - m_new)
            p = jnp.exp2(s - m_new)
            if vis is not None:
                p = jnp.where(vis, p, 0.0)
            l_sc[:, lanes] = alpha * l_sc[:, lanes] + jnp.sum(p, axis=0, keepdims=True)
            m_sc[:, lanes] = m_new
            p_v[c] = (p.astype(BF16), alpha)

        def values(c):
            lanes = slice(c * FLASH_STRIP, (c + 1) * FLASH_STRIP)
            p16, alpha = p_v.pop(c)
            acc_sc[:, lanes] = alpha * acc_sc[:, lanes] + jnp.dot(vt, p16, preferred_element_type=F32)

        for step in range(len(live) + 2):
            if step < len(live):
                scores(live[step])
            if 0 <= step - 1 < len(live):
                softmax(live[step - 1])
            if 0 <= step - 2 < len(live):
                values(live[step - 2])

    ratio = tq // tk

    @pl.when(j < i * ratio)
    def _():
        tile(None)

    for r in range(ratio):
        @pl.when(j == i * ratio + r)
        def _(r=r):
            tile(r * tk)

    @pl.when(j == (i * tq + tq - 1) // tk)
    def _():
        o_ref[0] = (acc_sc[...] / l_sc[...]).T.astype(o_ref.dtype)


def flash_attention_causal(q3, k3, vt3, *, tq=2048, tk=1024):
    bsz, s, _ = q3.shape
    tq, tk = min(tq, s), min(tk, s)
    assert s % tq == 0 and tq % tk == 0 and tq % FLASH_STRIP == 0 and tk % CHUNK == 0
    pairs = [(i, j) for i in range(s // tq) for j in range((i * tq + tq - 1) // tk + 1)]
    qi = jnp.asarray([p[0] for p in pairs], jnp.int32)
    kj = jnp.asarray([p[1] for p in pairs], jnp.int32)
    grid_spec = pltpu.PrefetchScalarGridSpec(
        num_scalar_prefetch=2,
        grid=(bsz, MLA_H, len(pairs)),
        in_specs=[pl.BlockSpec((1, tq, QSLOT), lambda b, h, p, qi, kj: (b, qi[p], h)),
                  pl.BlockSpec((1, tk, QSLOT), lambda b, h, p, qi, kj: (b, kj[p], h)),
                  pl.BlockSpec((1, MLA_V, tk), lambda b, h, p, qi, kj: (b, h, kj[p]))],
        out_specs=pl.BlockSpec((1, tq, MLA_V), lambda b, h, p, qi, kj: (b, qi[p], h)),
        scratch_shapes=[pltpu.VMEM((1, tq), F32), pltpu.VMEM((1, tq), F32), pltpu.VMEM((MLA_V, tq), F32)])
    return pl.pallas_call(
        functools.partial(_flash_t_kernel, tq=tq, tk=tk),
        out_shape=jax.ShapeDtypeStruct((bsz, s, BRANCH), BF16),
        grid_spec=grid_spec,
        compiler_params=_cp(("parallel", "parallel", "arbitrary")),
        name="flash_attention_causal",
    )(qi, kj, q3, k3, vt3)


def _vt_kernel(w_ref, x_ref, o_ref):
    o_ref[0] = _nt(w_ref[...], x_ref[0]).astype(o_ref.dtype)


def v_up_transposed(w_vt, latkr3, *, tn=512):
    bsz, s, _ = latkr3.shape
    tn = min(tn, s)
    nv = w_vt.shape[0]
    return pl.pallas_call(
        _vt_kernel,
        out_shape=jax.ShapeDtypeStruct((bsz, nv, s), BF16),
        grid=(bsz, s // tn),
        in_specs=[pl.BlockSpec((nv, LATKR), lambda b, t: (0, 0)),
                  pl.BlockSpec((1, tn, LATKR), lambda b, t: (b, t, 0))],
        out_specs=pl.BlockSpec((1, nv, tn), lambda b, t: (b, 0, t)),
        compiler_params=_cp(("parallel", "parallel")),
        name="v_up_transposed",
    )(w_vt, latkr3)


def _branch_kernel(oa_ref, ob_ref, oc_ref, w_ref, g0_ref, g1_ref, g2_ref, out_ref):
    gates = [jax.nn.sigmoid(g_ref[...]) for g_ref in (g0_ref, g1_ref, g2_ref)]
    us = [jnp.dot(o_ref[...], w_ref[n], preferred_element_type=F32)
          for n, o_ref in enumerate((oa_ref, ob_ref, oc_ref))]
    out_ref[...] = (gates[0] * us[0] + gates[1] * us[1] + gates[2] * us[2]).astype(out_ref.dtype)


def branch_merge(oa, ob, oc, w_branch, p):
    m = oa.shape[0]
    tm = min(1024, m)
    tn = 512
    nj = D_MODEL // tn
    o_spec = pl.BlockSpec((tm, BRANCH), lambda i, j: (i, 0))
    gate = lambda n: pl.BlockSpec((tm, tn), lambda i, j: (i, P_GATE // tn + n * nj + j))
    return pl.pallas_call(
        _branch_kernel,
        out_shape=jax.ShapeDtypeStruct((m, D_MODEL), BF16),
        grid=(m // tm, nj),
        in_specs=[o_spec, o_spec, o_spec,
                  pl.BlockSpec((3, BRANCH, tn), lambda i, j: (0, 0, j)),
                  gate(0), gate(1), gate(2)],
        out_specs=pl.BlockSpec((tm, tn), lambda i, j: (i, j)),
        compiler_params=_cp(("parallel", "parallel")),
        name="branch_merge",
    )(oa, ob, oc, w_branch, p, p, p)


def _pack_layer(l, w):
    w_in = w['w_in'][l]
    seg = lambda a, n: w_in[:, a:a + n]
    zeros = lambda n: jnp.zeros((D_MODEL, n), w_in.dtype)
    w_in_p = jnp.concatenate([
        seg(8032, 12288),
        seg(6688, 768), seg(7456, 512),
        seg(7968, 64), seg(4096, 8), seg(4104, 8), seg(6672, 16), zeros(32), zeros(128),
        seg(5136, 1536), seg(0, 3072), seg(3072, 1024), seg(4112, 1024)], axis=1).astype(BF16)
    w_uq = jnp.pad(w['mla_w_uq'][l], ((0, 0), (0, 0), (0, QSLOT - MLA_NOPE - MLA_ROPE)))
    w_uq = w_uq.reshape(MLA_Q_RANK, MLA_H * QSLOT).astype(BF16)
    wk = jnp.pad(w['mla_w_uk'][l], ((0, 0), (0, 0), (0, QSLOT - MLA_NOPE))).reshape(MLA_KV_RANK, MLA_H * QSLOT)
    eye = jnp.broadcast_to(jnp.eye(MLA_ROPE, dtype=F32)[:, None, :], (MLA_ROPE, MLA_H, MLA_ROPE))
    wkr = jnp.pad(eye, ((0, 0), (0, 0), (MLA_NOPE, QSLOT - MLA_NOPE - MLA_ROPE))).reshape(MLA_ROPE, MLA_H * QSLOT)
    wk = jnp.concatenate([wk, wkr, jnp.zeros((LATKR - MLA_KV_RANK - MLA_ROPE, MLA_H * QSLOT), F32)], axis=0)
    wv = jnp.pad(w['mla_w_uv'][l].reshape(MLA_KV_RANK, MLA_H * MLA_V), ((0, LATKR - MLA_KV_RANK), (0, 0)))
    w_k = wk.astype(BF16)
    w_vt = wv.T.astype(BF16)
    w_ukt = jnp.transpose(w['mla_w_uk'][l], (1, 2, 0)).astype(BF16)
    w_uv_h = jnp.transpose(w['mla_w_uv'][l], (1, 0, 2)).astype(BF16)

    def lane_row(vals, off):
        return jnp.zeros((1, 128), F32).at[0, off:off + vals.shape[0]].set(vals.astype(F32))

    return dict(
        attn_norm=w['attn_norm'][l], w_in=w_in_p,
        gdn_conv_w=w['gdn_conv_w'][l], gdn_alog=lane_row(w['gdn_A_log'][l], L_ALPHA),
        gdn_dtb=lane_row(w['gdn_dt_bias'][l], L_ALPHA), gdn_norm=w['gdn_norm'][l],
        ssm_conv_w=w['ssm_conv_w'][l], ssm_conv_b=w['ssm_conv_b'][l],
        ssm_alog=lane_row(w['ssm_A_log'][l], L_DT), ssm_dtb=lane_row(w['ssm_dt_bias'][l], L_DT),
        ssm_d=jnp.repeat(w['ssm_D'][l].astype(F32), SSM_P).reshape(1, BRANCH), ssm_norm=w['ssm_norm'][l],
        mla_q_norm=w['mla_q_norm'][l], w_uq=w_uq, mla_kv_norm=w['mla_kv_norm'][l], w_k=w_k, w_vt=w_vt, w_ukt=w_ukt, w_uv_h=w_uv_h,
        w_branch=w['w_branch'][l].astype(BF16), w_out=w['w_out'][l].astype(BF16),
        ffn_norm=w['ffn_norm'][l], w_up_il=interleave_cast_w_up(w['ffn_w_up'], l),
        ffn_conv_w=w['ffn_conv_w'][l], ffn_conv_b=w['ffn_conv_b'][l],
        w_down=w['ffn_w_down'][l].astype(BF16))


def _rope_tables(pos):
    half = MLA_ROPE // 2
    inv = ROPE_THETA ** (-jnp.arange(half, dtype=F32) / half)
    ang = pos.astype(F32)[:, None] * inv[None, :]
    cos, sin = jnp.cos(ang), jnp.sin(ang)
    n = pos.shape[0]
    z = lambda k: jnp.zeros((n, k), F32)
    cq = jnp.concatenate([jnp.ones((n, MLA_NOPE), F32), cos, cos, z(64)], axis=1)
    saq = jnp.concatenate([z(MLA_NOPE + half), sin, z(64)], axis=1)
    sbq = jnp.concatenate([z(MLA_NOPE), -sin, z(half + 64)], axis=1)
    ck = jnp.concatenate([cos, cos, z(64)], axis=1)
    sak = jnp.concatenate([z(half), sin, z(64)], axis=1)
    sbk = jnp.concatenate([-sin, z(half + 64)], axis=1)
    return cq, saq, sbq, ck, sak, sbk


def _pad_state(st):
    return jnp.pad(st, ((0, 0), (8 - st.shape[1], 0), (0, 0)))


def _layer(x, tabs, q_off, state, lw):
    latkr_past, gconv, gssm, mconv, mssm, fconv = state
    bsz, s, _ = x.shape
    m = bsz * s
    cq_t, saq_t, sbq_t, ck_t, sak_t, sbk_t = tabs
    x2 = x.reshape(m, D_MODEL)
    mm_tm = 512

    p = rmsnorm_matmul(x2, lw['attn_norm'], lw['w_in'], F32, tm=mm_tm, tn=1024, name="in_proj")
    p3 = p.reshape(bsz, s, P_TOT)

    qkv_act, gconv8 = causal_conv(p3, P_QKV, GDN_QKV, _pad_state(gconv), lw['gdn_conv_w'],
                                  jnp.zeros((GDN_QKV,), F32), act=True)
    o_a, gssm_new = gdn(qkv_act, p3, gssm, lw['gdn_alog'], lw['gdn_dtb'], lw['gdn_norm'])

    xbc_act, mconv8 = causal_conv(p3, P_XBC, SSM_CONV_DIM, _pad_state(mconv), lw['ssm_conv_w'],
                                  lw['ssm_conv_b'], act=True)
    o_b, mssm_new = ssd(xbc_act, p3, mssm.reshape(bsz, SSM_H // 2, 128, 128), lw['ssm_alog'], lw['ssm_dtb'],
                        lw['ssm_d'], lw['ssm_norm'])
    mssm_new = mssm_new.reshape(bsz, SSM_H, SSM_P, SSM_N)

    cqn, lat, kr, latkr = mla_prep(p, lw['mla_q_norm'], lw['mla_kv_norm'], ck_t, sak_t, sbk_t)
    tq_tab = min(mm_tm, m)
    q_full = matmul(cqn, lw['w_uq'], BF16, tm=mm_tm, tn=MLA_H * QSLOT, name="q_up",
                    extras=[(cq_t, (tq_tab, QSLOT), lambda i, j: (i, 0)),
                            (saq_t, (tq_tab, QSLOT), lambda i, j: (i, 0)),
                            (sbq_t, (tq_tab, QSLOT), lambda i, j: (i, 0))],
                    epilogue=_q_rope_epilogue)
    latkr3 = latkr.reshape(bsz, s, LATKR)
    q3 = q_full.reshape(bsz, s, MLA_H * QSLOT)
    if latkr_past is None and q_off == 0 and s % 512 == 0:
        k_full = matmul(latkr, lw['w_k'], BF16, tm=mm_tm, tn=1024, name="k_up")
        vt = v_up_transposed(lw['w_vt'], latkr3)
        o_c = flash_attention_causal(q3, k_full.reshape(bsz, s, MLA_H * QSLOT), vt)
    else:
        past = jnp.zeros((bsz, 0, LATKR), BF16) if latkr_past is None else latkr_past
        nk_real = past.shape[1] + s
        nk_pad = -(-nk_real // 128) * 128
        latkr_all = jnp.concatenate([past, latkr3, jnp.zeros((bsz, nk_pad - nk_real, LATKR), BF16)], axis=1)
        assert s & (s - 1) == 0 and MLA_H * s <= 512
        o_c = latent_attention(q_absorb(q_full, lw['w_ukt']), latkr_all, lw['w_uv_h'],
                               bsz=bsz, s=s, q_off=q_off, nk_real=nk_real)

    merged = branch_merge(o_a.reshape(m, BRANCH), o_b.reshape(m, BRANCH), o_c.reshape(m, BRANCH),
                          lw['w_branch'], p)
    x2 = matmul_residual(merged, lw['w_out'], x2, tm=mm_tm, tn=1024, name="out_proj")

    h2 = rmsnorm(x2, lw['ffn_norm'], BF16)
    if s % 512 == 0:
        act, fconv8 = ffn_up_fused(h2, lw['w_up_il'], _pad_state(fconv), lw['ffn_conv_w'], lw['ffn_conv_b'],
                                   bsz=bsz, tm=max(t for t in (512, 1024, 2048) if s % t == 0))
    else:
        up = matmul_tile_major(h2, lw['w_up_il'])
        act, fconv8 = ffn_conv_gate_il(up.reshape(bsz, s, 2 * D_FF), _pad_state(fconv), lw['ffn_conv_w'],
                                       lw['ffn_conv_b'])
    x2 = matmul_residual(act.reshape(m, D_FF), lw['w_down'], x2, tm=mm_tm, tn=512, name="ffn_down")

    new = (lat.reshape(bsz, s, MLA_KV_RANK), kr.reshape(bsz, s, MLA_ROPE),
           gconv8[:, 8 - gconv.shape[1]:], gssm_new, mconv8[:, 8 - mconv.shape[1]:], mssm_new,
           fconv8[:, 8 - fconv.shape[1]:])
    return x2.reshape(bsz, s, D_MODEL), new


def kernel(x_prompt, x_sample, cache_mla_latent, cache_mla_krope, state_gdn_conv, state_gdn_ssm, state_ssm_conv, state_ssm, state_ffn_conv, attn_norm, w_in, gdn_conv_w, gdn_A_log, gdn_dt_bias, gdn_norm, ssm_conv_w, ssm_conv_b, ssm_A_log, ssm_dt_bias, ssm_D, ssm_norm, mla_q_norm, mla_w_uq, mla_kv_norm, mla_w_uk, mla_w_uv, w_branch, w_out, ffn_norm, ffn_w_up, ffn_conv_w, ffn_conv_b, ffn_w_down, final_norm):
    weights = dict(attn_norm=attn_norm, w_in=w_in, gdn_conv_w=gdn_conv_w, gdn_A_log=gdn_A_log,
                   gdn_dt_bias=gdn_dt_bias, gdn_norm=gdn_norm, ssm_conv_w=ssm_conv_w, ssm_conv_b=ssm_conv_b,
                   ssm_A_log=ssm_A_log, ssm_dt_bias=ssm_dt_bias, ssm_D=ssm_D, ssm_norm=ssm_norm,
                   mla_q_norm=mla_q_norm, mla_w_uq=mla_w_uq, mla_kv_norm=mla_kv_norm, mla_w_uk=mla_w_uk,
                   mla_w_uv=mla_w_uv, w_branch=w_branch, w_out=w_out, ffn_norm=ffn_norm, ffn_w_up=ffn_w_up,
                   ffn_conv_w=ffn_conv_w, ffn_conv_b=ffn_conv_b, ffn_w_down=ffn_w_down)
    depth = w_in.shape[0]
    bp, sp, _ = x_prompt.shape
    bs, ss, _ = x_sample.shape
    n_past = cache_mla_latent.shape[2]
    pos_p = jnp.arange(sp, dtype=jnp.int32)
    pos_s = n_past + jnp.arange(ss, dtype=jnp.int32)
    tabs_p = _rope_tables(pos_p)
    tabs_s = tuple(jnp.tile(t, (bs, 1)) for t in _rope_tables(pos_s))
    zero_state = (
        None,
        jnp.zeros((bp,) + state_gdn_conv.shape[2:], F32),
        jnp.zeros((bp,) + state_gdn_ssm.shape[2:], F32),
        jnp.zeros((bp,) + state_ssm_conv.shape[2:], F32),
        jnp.zeros((bp,) + state_ssm.shape[2:], F32),
        jnp.zeros((bp,) + state_ffn_conv.shape[2:], F32),
    )
    xp, xs = x_prompt, x_sample
    new_p, new_s = [], []
    for l in range(depth):
        lw = _pack_layer(l, weights)
        xp, st_p = _layer(xp, tabs_p, 0, zero_state, lw)
        latkr_past = pack_latent_cache(cache_mla_latent, cache_mla_krope, l)
        st_in = (latkr_past, state_gdn_conv[l], state_gdn_ssm[l], state_ssm_conv[l], state_ssm[l],
                 state_ffn_conv[l])
        xs, st_s = _layer(xs, tabs_s, n_past, st_in, lw)
        new_p.append(st_p)
        new_s.append(st_s)
    y_prompt = rmsnorm(xp.reshape(bp * sp, D_MODEL), final_norm, F32).reshape(bp, sp, D_MODEL)
    y_sample = rmsnorm(xs.reshape(bs * ss, D_MODEL), final_norm, F32).reshape(bs, ss, D_MODEL)
    outs_p = [jnp.stack(t) for t in zip(*new_p)]
    outs_s = [jnp.stack(t) for t in zip(*new_s)]
    return (y_prompt, y_sample, *outs_p, *outs_s)
```

```python
import functools
import math

import jax
import jax.numpy as jnp
from jax import lax
from jax.experimental import pallas as pl
from jax.experimental.pallas import tpu as pltpu

F32 = jnp.float32
BF16 = jnp.bfloat16
HI = lax.Precision.HIGHEST

EPS = 1e-6
CHUNK = 64
D_MODEL = 4096
BRANCH = 1024
GDN_H, GDN_D = 8, 128
GDN_QKV = 3072
SSM_H, SSM_P, SSM_G, SSM_N = 16, 64, 2, 128
SSM_CONV_DIM = 1536
MLA_H, MLA_NOPE, MLA_ROPE, MLA_V = 8, 128, 64, 128
MLA_Q_RANK, MLA_KV_RANK = 768, 512
ROPE_THETA = 10000.0
D_FF = 11008
LATKR = 640
QSLOT = 256

P_GATE = 0
P_MLA = 12288
P_SMALL = P_MLA + 1280
P_XBC = 13824
P_QKV = 15360
P_ZA = 18432
P_ZB = 19456
P_TOT = 20480
L_BETA, L_ALPHA, L_DT = 64, 72, 80

VMEM_LIMIT = 56 * 1024 * 1024


def _cp(sem):
    return pltpu.CompilerParams(dimension_semantics=sem, vmem_limit_bytes=VMEM_LIMIT)


def _softplus(x):
    return jnp.maximum(x, 0.0) + jnp.log1p(jnp.exp(-jnp.abs(x)))


def _silu(x):
    return x * jax.nn.sigmoid(x)


def _nt(a, b):
    return lax.dot_general(a, b, (((1,), (1,)), ((), ())), preferred_element_type=F32)


def _tn(a, b):
    return lax.dot_general(a, b, (((0,), (0,)), ((), ())), preferred_element_type=F32)


def _rms_kernel(x_ref, w_ref, o_ref):
    x = x_ref[...]
    ms = jnp.mean(x * x, axis=-1, keepdims=True)
    o_ref[...] = (x * lax.rsqrt(ms + EPS) * w_ref[...]).astype(o_ref.dtype)


def rmsnorm(x, w, out_dtype):
    m, d = x.shape
    tm = min(256, m)
    return pl.pallas_call(
        _rms_kernel,
        out_shape=jax.ShapeDtypeStruct((m, d), out_dtype),
        grid=(m // tm,),
        in_specs=[pl.BlockSpec((tm, d), lambda i: (i, 0)),
                  pl.BlockSpec((1, d), lambda i: (0, 0))],
        out_specs=pl.BlockSpec((tm, d), lambda i: (i, 0)),
        compiler_params=_cp(("parallel",)),
        name="rmsnorm",
    )(x, w.reshape(1, d))


def _mm_kernel(*refs, nk, n_extra, epilogue):
    a_ref, b_ref = refs[0], refs[1]
    extras = refs[2:2 + n_extra]
    o_ref = refs[2 + n_extra]
    part = jnp.dot(a_ref[...], b_ref[...], preferred_element_type=F32)
    if nk == 1:
        o_ref[...] = epilogue(part, *extras).astype(o_ref.dtype)
        return
    acc_ref = refs[3 + n_extra]
    k = pl.program_id(2)

    @pl.when(k == 0)
    def _():
        acc_ref[...] = part

    @pl.when(k > 0)
    def _():
        acc_ref[...] += part

    @pl.when(k == nk - 1)
    def _():
        o_ref[...] = epilogue(acc_ref[...], *extras).astype(o_ref.dtype)


def matmul(a, b, out_dtype, *, tm, tn, tk=None, extras=(), epilogue=None, name="matmul"):
    m, kd = a.shape
    n = b.shape[1]
    tm, tn = min(tm, m), min(tn, n)
    while m % tm:
        tm //= 2
    tk = kd if tk is None else tk
    nk = kd // tk
    assert m % tm == 0 and n % tn == 0 and kd % tk == 0
    if epilogue is None:
        epilogue = lambda acc: acc
    in_specs = [pl.BlockSpec((tm, tk), lambda i, j, k: (i, k)),
                pl.BlockSpec((tk, tn), lambda i, j, k: (k, j))]
    args = [a, b]
    for arr, blk, imap in extras:
        in_specs.append(pl.BlockSpec(blk, functools.partial(lambda i, j, k, f: f(i, j), f=imap)))
        args.append(arr)
    scratch = [pltpu.VMEM((tm, tn), F32)] if nk > 1 else []
    return pl.pallas_call(
        functools.partial(_mm_kernel, nk=nk, n_extra=len(extras), epilogue=epilogue),
        out_shape=jax.ShapeDtypeStruct((m, n), out_dtype),
        grid=(m // tm, n // tn, nk),
        in_specs=in_specs,
        out_specs=pl.BlockSpec((tm, tn), lambda i, j, k: (i, j)),
        scratch_shapes=scratch,
        compiler_params=_cp(("parallel", "parallel", "arbitrary")),
        name=name,
    )(*args)


def _norm_mm_kernel(x_ref, nw_ref, b_ref, o_ref, a16_ref):
    @pl.when(pl.program_id(1) == 0)
    def _():
        x = x_ref[...]
        ms = jnp.mean(x * x, axis=-1, keepdims=True)
        a16_ref[...] = (x * lax.rsqrt(ms + EPS) * nw_ref[...]).astype(a16_ref.dtype)

    o_ref[...] = jnp.dot(a16_ref[...], b_ref[...], preferred_element_type=F32).astype(o_ref.dtype)


def rmsnorm_matmul(x, nw, b, out_dtype, *, tm, tn, name):
    m, kd = x.shape
    n = b.shape[1]
    tm = min(tm, m)
    assert m % tm == 0 and n % tn == 0
    return pl.pallas_call(
        _norm_mm_kernel,
        out_shape=jax.ShapeDtypeStruct((m, n), out_dtype),
        grid=(m // tm, n // tn),
        in_specs=[pl.BlockSpec((tm, kd), lambda i, j: (i, 0)),
                  pl.BlockSpec((1, kd), lambda i, j: (0, 0)),
                  pl.BlockSpec((kd, tn), lambda i, j: (0, j))],
        out_specs=pl.BlockSpec((tm, tn), lambda i, j: (i, j)),
        scratch_shapes=[pltpu.VMEM((tm, kd), BF16)],
        compiler_params=_cp(("parallel", "arbitrary")),
        name=name,
    )(x, nw.reshape(1, kd), b)


def _residual_epilogue(acc, r_ref):
    return acc + r_ref[...]


def matmul_residual(a, b, res, *, tm, tn, tk=None, name="matmul_residual"):
    tm_, tn_ = min(tm, a.shape[0]), min(tn, b.shape[1])
    return matmul(a, b, F32, tm=tm, tn=tn, tk=tk, name=name,
                  extras=[(res, (tm_, tn_), lambda i, j: (i, j))], epilogue=_residual_epilogue)


def _q_rope_epilogue(acc, c_ref, sa_ref, sb_ref):
    c, sa, sb = c_ref[...], sa_ref[...], sb_ref[...]
    scale = (MLA_NOPE + MLA_ROPE) ** -0.5 * math.log2(math.e)
    outs = []
    for h in range(MLA_H):
        xs = acc[:, h * QSLOT:(h + 1) * QSLOT]
        y = xs * c + pltpu.roll(xs, 32, 1) * sa + pltpu.roll(xs, QSLOT - 32, 1) * sb
        outs.append(y * scale)
    return jnp.concatenate(outs, axis=1)


def _conv_kernel(x_ref, st_ref, w_ref, b_ref, y_ref, ns_ref, xp_ref, *, width, ts, act):
    t = pl.program_id(2)
    nt = pl.num_programs(2)

    @pl.when(t == 0)
    def _():
        xp_ref[0:8, :] = st_ref[0]

    @pl.when(t > 0)
    def _():
        xp_ref[0:8, :] = xp_ref[ts:ts + 8, :]

    xp_ref[8:8 + ts, :] = x_ref[0]
    off = 8 - (width - 1)
    y = xp_ref[off:off + ts, :] * w_ref[0:1, :]
    for j in range(1, width):
        y = y + xp_ref[off + j:off + j + ts, :] * w_ref[j:j + 1, :]
    y = y + b_ref[...]
    if act:
        y = _silu(y)
    y_ref[0] = y.astype(y_ref.dtype)

    @pl.when(t == nt - 1)
    def _():
        ns_ref[0] = xp_ref[ts:ts + 8, :]


def causal_conv(x3, col_off, chans, state8, w, b, *, act, ct=512):
    bsz, s, _ = x3.shape
    width = w.shape[0]
    ts = min(2048, s)
    assert chans % ct == 0 and col_off % ct == 0 and s % ts == 0
    cb0 = col_off // ct
    return pl.pallas_call(
        functools.partial(_conv_kernel, width=width, ts=ts, act=act),
        out_shape=(jax.ShapeDtypeStruct((bsz, s, chans), F32),
                   jax.ShapeDtypeStruct((bsz, 8, chans), F32)),
        grid=(bsz, chans // ct, s // ts),
        in_specs=[pl.BlockSpec((1, ts, ct), lambda bi, c, t: (bi, t, cb0 + c)),
                  pl.BlockSpec((1, 8, ct), lambda bi, c, t: (bi, 0, c)),
                  pl.BlockSpec((width, ct), lambda bi, c, t: (0, c)),
                  pl.BlockSpec((1, ct), lambda bi, c, t: (0, c))],
        out_specs=(pl.BlockSpec((1, ts, ct), lambda bi, c, t: (bi, t, c)),
                   pl.BlockSpec((1, 8, ct), lambda bi, c, t: (bi, 0, c))),
        scratch_shapes=[pltpu.VMEM((ts + 8, ct), F32)],
        compiler_params=_cp(("parallel", "parallel", "arbitrary")),
        name="causal_conv",
    )(x3, state8, w, b.reshape(1, chans))


HIST = 8


FFN_TN = 256


def _interleave_ffn(x):
    lead = x.shape[:-1]
    x = x.reshape(*lead, 2, D_FF // FFN_TN, FFN_TN)
    return jnp.swapaxes(x, -3, -2).reshape(*lead, 2 * D_FF)


def _deinterleave_ffn(x):
    lead = x.shape[:-1]
    x = x.reshape(*lead, D_FF // FFN_TN, 2, FFN_TN)
    return jnp.swapaxes(x, -3, -2).reshape(*lead, 2 * D_FF)


def _il_cast_kernel(a_ref, b_ref, o_ref):
    o_ref[0, :, :FFN_TN] = a_ref[0].astype(o_ref.dtype)
    o_ref[0, :, FFN_TN:] = b_ref[0].astype(o_ref.dtype)


def interleave_cast_w_up(w_all, l, *, tk=1024):
    kd = w_all.shape[1]
    nc = D_FF // FFN_TN
    return pl.pallas_call(
        _il_cast_kernel,
        out_shape=jax.ShapeDtypeStruct((nc, kd, 2 * FFN_TN), BF16),
        grid=(kd // tk, nc),
        in_specs=[pl.BlockSpec((1, tk, FFN_TN), lambda k, j: (l, k, j)),
                  pl.BlockSpec((1, tk, FFN_TN), lambda k, j: (l, k, nc + j))],
        out_specs=pl.BlockSpec((1, tk, 2 * FFN_TN), lambda k, j: (j, k, 0)),
        compiler_params=_cp(("parallel", "parallel")),
        name="interleave_cast_w_up",
    )(w_all, w_all)


def _mm_tiles_kernel(a_ref, w_ref, o_ref):
    o_ref[...] = jnp.dot(a_ref[...], w_ref[0], preferred_element_type=F32)


def matmul_tile_major(a, w_tiles):
    m, kd = a.shape
    nt, _, tn = w_tiles.shape
    return pl.pallas_call(
        _mm_tiles_kernel,
        out_shape=jax.ShapeDtypeStruct((m, nt * tn), F32),
        grid=(nt,),
        in_specs=[pl.BlockSpec((m, kd), lambda j: (0, 0)),
                  pl.BlockSpec((1, kd, tn), lambda j: (j, 0, 0))],
        out_specs=pl.BlockSpec((m, tn), lambda j: (0, j)),
        compiler_params=_cp(("parallel",)),
        name="ffn_up",
    )(a, w_tiles)


def _ffn_conv_gate_il_kernel(x_ref, st_ref, cw_ref, cb_ref, y_ref, ns_ref, xp_ref, *, width, s):
    xp_ref[0:8, :] = st_ref[0]
    xp_ref[8:8 + s, :] = x_ref[0]
    off = 8 - (width - 1)
    y = xp_ref[off:off + s, :] * cw_ref[0:1, :]
    for t in range(1, width):
        y = y + xp_ref[off + t:off + t + s, :] * cw_ref[t:t + 1, :]
    y = y + cb_ref[...]
    ns_ref[0] = xp_ref[s:s + 8, :]
    for c in range(D_FF // FFN_TN):
        g = y[:, 2 * c * FFN_TN:(2 * c + 1) * FFN_TN]
        v = y[:, (2 * c + 1) * FFN_TN:(2 * c + 2) * FFN_TN]
        y_ref[0, :, c * FFN_TN:(c + 1) * FFN_TN] = (_silu(g) * v).astype(y_ref.dtype)


def ffn_conv_gate_il(up3, state8, cw, cb):
    bsz, s, n2 = up3.shape
    width = cw.shape[0]
    y, ns = pl.pallas_call(
        functools.partial(_ffn_conv_gate_il_kernel, width=width, s=s),
        out_shape=(jax.ShapeDtypeStruct((bsz, s, D_FF), BF16),
                   jax.ShapeDtypeStruct((bsz, 8, n2), F32)),
        grid=(bsz,),
        in_specs=[pl.BlockSpec((1, s, n2), lambda b: (b, 0, 0)),
                  pl.BlockSpec((1, 8, n2), lambda b: (b, 0, 0)),
                  pl.BlockSpec((width, n2), lambda b: (0, 0)),
                  pl.BlockSpec((1, n2), lambda b: (0, 0))],
        out_specs=(pl.BlockSpec((1, s, D_FF), lambda b: (b, 0, 0)),
                   pl.BlockSpec((1, 8, n2), lambda b: (b, 0, 0))),
        scratch_shapes=[pltpu.VMEM((s + 8, n2), F32)],
        compiler_params=_cp(("parallel",)),
        name="ffn_conv_gate_il",
    )(up3, _interleave_ffn(state8), _interleave_ffn(cw), _interleave_ffn(cb.reshape(1, n2)))
    return y, _deinterleave_ffn(ns)


FFN_SUB = 256


def _ffn_row_bounds(tm):
    return tuple(range(0, tm + 1, min(tm, 2 * FFN_SUB)))


def _ffn_up_fused_kernel(a_ref, w_ref, st_ref, cw_ref, cb_ref, y_ref, ns_ref, carry_ref, *xp_refs,
                         tm, tiles_per_seq, width):
    i = pl.program_id(0)
    j = pl.program_id(1)

    @pl.when((i % tiles_per_seq) == 0)
    def _():
        carry_ref[j] = st_ref[0]

    bounds = _ffn_row_bounds(tm)
    nsub = len(bounds) - 1
    sizes = [bounds[r + 1] - bounds[r] for r in range(nsub)]
    off = HIST - (width - 1)
    cw = cw_ref[...]
    cb = cb_ref[...]

    def matmul_block(r):
        xp_refs[r][HIST:HIST + sizes[r], :] = jnp.dot(a_ref[bounds[r]:bounds[r + 1], :], w_ref[0],
                                                      preferred_element_type=F32)

    def conv_block(r):
        xp_ref = xp_refs[r]
        xp_ref[0:HIST, :] = carry_ref[j] if r == 0 else xp_refs[r - 1][sizes[r - 1]:sizes[r - 1] + HIST, :]
        y = xp_ref[off:off + sizes[r], :] * cw[0:1, :]
        for t in range(1, width):
            y = y + xp_ref[off + t:off + t + sizes[r], :] * cw[t:t + 1, :]
        y = y + cb
        y_ref[bounds[r]:bounds[r + 1], :] = (_silu(y[:, :FFN_TN]) * y[:, FFN_TN:]).astype(y_ref.dtype)

    matmul_block(0)
    for r in range(nsub):
        if r + 1 < nsub:
            matmul_block(r + 1)
        conv_block(r)
    last = xp_refs[nsub - 1][sizes[-1]:sizes[-1] + HIST, :]
    carry_ref[j] = last
    ns_ref[0] = last


def ffn_up_fused(h2, w_up_il, state8, cw, cb, *, bsz, tm=1024):
    m, kd = h2.shape
    s = m // bsz
    tm = min(tm, s)
    assert s % tm == 0 and tm % FFN_SUB == 0
    bounds = _ffn_row_bounds(tm)
    tiles_per_seq = s // tm
    width = cw.shape[0]
    nc = D_FF // FFN_TN
    tn2 = 2 * FFN_TN
    y, ns = pl.pallas_call(
        functools.partial(_ffn_up_fused_kernel, tm=tm, tiles_per_seq=tiles_per_seq, width=width),
        out_shape=(jax.ShapeDtypeStruct((m, D_FF), BF16),
                   jax.ShapeDtypeStruct((m // tm, 8, 2 * D_FF), F32)),
        grid=(m // tm, nc),
        in_specs=[pl.BlockSpec((tm, kd), lambda i, j: (i, 0), pipeline_mode=pl.Buffered(1)),
                  pl.BlockSpec((1, kd, tn2), lambda i, j: (j, 0, 0)),
                  pl.BlockSpec((1, 8, tn2), lambda i, j: (i // tiles_per_seq, 0, j)),
                  pl.BlockSpec((width, tn2), lambda i, j: (0, j)),
                  pl.BlockSpec((1, tn2), lambda i, j: (0, j))],
        out_specs=(pl.BlockSpec((tm, FFN_TN), lambda i, j: (i, j)),
                   pl.BlockSpec((1, 8, tn2), lambda i, j: (i, 0, j))),
        scratch_shapes=[pltpu.VMEM((nc, HIST, tn2), F32)]
        + [pltpu.VMEM((hi - lo + HIST, tn2), F32) for lo, hi in zip(bounds[:-1], bounds[1:])],
        compiler_params=_cp(("arbitrary", "arbitrary")),
        name="ffn_up_fused",
    )(h2, w_up_il, _interleave_ffn(state8), _interleave_ffn(cw), _interleave_ffn(cb.reshape(1, 2 * D_FF)))
    return y, _deinterleave_ffn(ns[tiles_per_seq - 1::tiles_per_seq])


def _chunk_masks(lt, c):
    sh = int(math.log2(c))
    ri = lax.broadcasted_iota(jnp.int32, (lt, lt), 0)
    ci = lax.broadcasted_iota(jnp.int32, (lt, lt), 1)
    same = (ri >> sh) == (ci >> sh)
    return ri, ci, same


def _split3(x):
    hi = x.astype(BF16)
    r = x - hi.astype(F32)
    mid = r.astype(BF16)
    lo = (r - mid.astype(F32)).astype(BF16)
    return hi, mid, lo


def _dot_exact_lhs(l16, x):
    hi, mid, lo = _split3(x)
    return (jnp.dot(l16, hi, preferred_element_type=F32) + jnp.dot(l16, mid, preferred_element_type=F32)
            + jnp.dot(l16, lo, preferred_element_type=F32))


def _dot_hi(a, b):
    a_hi = a.astype(BF16)
    a_lo = (a - a_hi.astype(F32)).astype(BF16)
    b_hi = b.astype(BF16)
    b_lo = (b - b_hi.astype(F32)).astype(BF16)
    return (jnp.dot(a_hi, b_hi, preferred_element_type=F32) + jnp.dot(a_hi, b_lo, preferred_element_type=F32)
            + jnp.dot(a_lo, b_hi, preferred_element_type=F32))


def _gdn_kernel(q_ref, k_ref, v_ref, sm_ref, z_ref, s0_ref, alog_ref, dtb_ref, nw_ref,
                o_ref, sout_ref, s_sc, vn_sc, *, lt, c, hps):
    hg = pl.program_id(1)
    t = pl.program_id(2)
    nt = pl.num_programs(2)

    @pl.when(t == 0)
    def _():
        s_sc[...] = s0_ref[0]

    sm = sm_ref[0]
    lane = lax.broadcasted_iota(jnp.int32, (1, 128), 1)
    beta_full = jax.nn.sigmoid(sm)
    g_full = -jnp.exp(alog_ref[...]) * _softplus(sm + dtb_ref[...])
    ri, ci, same = _chunk_masks(lt, c)
    tri = same & (ci <= ri)
    strict = same & (ci < ri)
    eye_b = ri == ci
    eye = jnp.where(eye_b, 1.0, 0.0)
    gc_full = _dot_exact_lhs(jnp.where(tri, 1.0, 0.0).astype(BF16), g_full)
    nw = nw_ref[...]

    hs = range(hps)
    cols = [slice(hh * GDN_D, (hh + 1) * GDN_D) for hh in hs]
    dot16 = lambda x, y: jnp.dot(x.astype(BF16), y.astype(BF16), preferred_element_type=F32)
    ks, k16s, kbs, gcs, ms, avs, qds, rhss = [], [], [], [], [], [], [], []
    for hh in hs:
        h = hg * hps + hh
        q = q_ref[0, :, cols[hh]]
        k = k_ref[0, :, cols[hh]]
        q = q * lax.rsqrt(jnp.sum(q * q, axis=-1, keepdims=True) + EPS) * (GDN_D ** -0.5)
        k = k * lax.rsqrt(jnp.sum(k * k, axis=-1, keepdims=True) + EPS)
        beta = jnp.sum(jnp.where(lane == L_BETA + h, beta_full, 0.0), axis=-1, keepdims=True)
        gc = jnp.sum(jnp.where(lane == L_ALPHA + h, gc_full, 0.0), axis=-1, keepdims=True)
        gr = jnp.sum(jnp.where(eye_b, gc, 0.0), axis=0, keepdims=True)
        decay = jnp.exp(jnp.where(tri, gc - gr, 0.0))
        kb = k * beta
        k16 = k.astype(BF16)
        ms.append(jnp.where(strict, _nt(kb.astype(BF16), k16) * decay, 0.0))
        avs.append(jnp.where(tri, _nt(q.astype(BF16), k16) * decay, 0.0).astype(BF16))
        eg = jnp.exp(gc)
        rhss.append(jnp.concatenate([v_ref[0, :, cols[hh]] * beta, kb * eg], axis=1).astype(BF16))
        qds.append((q * eg).astype(BF16))
        ks.append(k)
        gcs.append(gc)

    ps = [-m for m in ms]
    ts = [eye + p for p in ps]
    for _ in range(int(math.log2(c)) - 1):
        ps = [dot16(p, p) for p in ps]
        ts = [tt + dot16(p, tt) for p, tt in zip(ps, ts)]
    rs = [eye - _dot_hi(eye + m, tt) for m, tt in zip(ms, ts)]
    ts = [tt + dot16(tt, r) for tt, r in zip(ts, rs)]
    uws = [jnp.dot(tt.astype(BF16), rhs, preferred_element_type=F32) for tt, rhs in zip(ts, rhss)]
    us = [uw[:, :GDN_D] for uw in uws]
    ws = [uw[:, GDN_D:].astype(BF16) for uw in uws]

    ss = [s_sc[hh] for hh in hs]
    for hh in hs:
        vn_sc[hh] = us[hh].astype(BF16)
    for cc in range(lt // c):
        r0 = cc * c
        rows = slice(r0, r0 + c)
        for hh in hs:
            s16 = ss[hh].astype(BF16)
            vnew = us[hh][rows] - jnp.dot(ws[hh][rows], s16, preferred_element_type=F32)
            vn16 = vnew.astype(BF16)
            vn_sc[hh, rows, :] = vn16
            o = (jnp.dot(qds[hh][rows], s16, preferred_element_type=F32)
                 + jnp.dot(avs[hh][rows, :], vn_sc[hh], preferred_element_type=F32))
            glast = gcs[hh][r0 + c - 1:r0 + c, :]
            kd = (ks[hh][rows] * jnp.exp(glast - gcs[hh][rows])).astype(BF16)
            ss[hh] = ss[hh] * jnp.exp(glast) + _tn(kd, vn16)
            mean_sq = jnp.mean(o * o, axis=-1, keepdims=True)
            o = o * lax.rsqrt(mean_sq + EPS) * nw * _silu(z_ref[0, rows, cols[hh]])
            o_ref[0, rows, cols[hh]] = o.astype(o_ref.dtype)
    for hh in hs:
        s_sc[hh] = ss[hh]

    @pl.when(t == nt - 1)
    def _():
        sout_ref[0] = s_sc[...]


def gdn(qkv3, p3, s0, alog_row, dtb_row, norm_w, *, hps=8):
    bsz, s, _ = qkv3.shape
    c = min(CHUNK, s)
    lt = min(256, s)
    hw = hps * GDN_D
    ng = GDN_H // hps
    small_blk = P_SMALL // 128
    za_blk = P_ZA // hw
    blk = lambda off: pl.BlockSpec((1, lt, hw), lambda b, h, t: (b, t, off + h))
    return pl.pallas_call(
        functools.partial(_gdn_kernel, lt=lt, c=c, hps=hps),
        out_shape=(jax.ShapeDtypeStruct((bsz, s, BRANCH), BF16),
                   jax.ShapeDtypeStruct((bsz, GDN_H, GDN_D, GDN_D), F32)),
        grid=(bsz, ng, s // lt),
        in_specs=[blk(0), blk(ng), blk(2 * ng),
                  pl.BlockSpec((1, lt, 128), lambda b, h, t: (b, t, small_blk)),
                  blk(za_blk),
                  pl.BlockSpec((1, hps, GDN_D, GDN_D), lambda b, h, t: (b, h, 0, 0)),
                  pl.BlockSpec((1, 128), lambda b, h, t: (0, 0)),
                  pl.BlockSpec((1, 128), lambda b, h, t: (0, 0)),
                  pl.BlockSpec((1, 128), lambda b, h, t: (0, 0))],
        out_specs=(pl.BlockSpec((1, lt, hw), lambda b, h, t: (b, t, h)),
                   pl.BlockSpec((1, hps, GDN_D, GDN_D), lambda b, h, t: (b, h, 0, 0))),
        scratch_shapes=[pltpu.VMEM((hps, GDN_D, GDN_D), F32), pltpu.VMEM((hps, lt, GDN_D), BF16)],
        compiler_params=_cp(("parallel", "parallel", "arbitrary")),
        name="gdn",
    )(qkv3, qkv3, qkv3, p3, p3, s0, alog_row, dtb_row, norm_w.reshape(1, GDN_D))


def _ssd_kernel(x_ref, b_ref, c_ref, sm_ref, z_ref, h0_ref, alog_ref, dtb_ref, d_ref, nw_ref,
                y_ref, hout_ref, hs_sc, y_sc, *, lt, c):
    g = pl.program_id(1)
    t = pl.program_id(2)
    nt = pl.num_programs(2)
    npair = SSM_H // SSM_G // 2

    @pl.when(t == 0)
    def _():
        hs_sc[...] = h0_ref[0]

    sm = sm_ref[0]
    dt_full = _softplus(sm + dtb_ref[...])
    da = dt_full * (-jnp.exp(alog_ref[...]))
    ri, ci, same = _chunk_masks(lt, c)
    tri = same & (ci <= ri)
    eye_b = ri == ci
    acum_full = _dot_exact_lhs(jnp.where(tri, 1.0, 0.0).astype(BF16), da)
    nch = lt // c
    lasts = [acum_full[cc * c + c - 1:cc * c + c, :] for cc in range(nch)]
    alast_full = jnp.concatenate([jnp.broadcast_to(l, (c, 128)) for l in lasts], axis=0)
    ea_full = jnp.exp(acum_full)
    dtw_full = dt_full * jnp.exp(alast_full - acum_full)

    bm = b_ref[0]
    cm = c_ref[0]
    bm16 = bm.astype(BF16)
    cm16 = cm.astype(BF16)
    cb = jnp.where(tri, _nt(cm16, bm16), 0.0)
    lane = lax.broadcasted_iota(jnp.int32, (1, 128), 1)
    lo = lane < SSM_P
    rowlo = lax.broadcasted_iota(jnp.int32, (128, 1), 0) < SSM_P
    x = x_ref[0]
    z = z_ref[0]
    d_row = d_ref[...]

    def ext(full, l):
        return jnp.sum(jnp.where(lane == l, full, 0.0), axis=-1, keepdims=True)

    pairs = range(npair)
    acs, eas, xps, xws, ys = [], [], [], [], []
    for j in pairs:
        la = L_DT + g * (SSM_H // SSM_G) + 2 * j
        lb = la + 1
        ac_a, ac_b = ext(acum_full, la), ext(acum_full, lb)
        row_a = jnp.sum(jnp.where(eye_b, ac_a, 0.0), axis=0, keepdims=True)
        row_b = jnp.sum(jnp.where(eye_b, ac_b, 0.0), axis=0, keepdims=True)
        l_a = (jnp.exp(jnp.where(tri, ac_a - row_a, 0.0)) * cb).astype(BF16)
        l_b = (jnp.exp(jnp.where(tri, ac_b - row_b, 0.0)) * cb).astype(BF16)
        dt_pair = jnp.where(lo, ext(dt_full, la), ext(dt_full, lb))
        dtw_pair = jnp.where(lo, ext(dtw_full, la), ext(dtw_full, lb))
        xp = x[:, j * 128:(j + 1) * 128]
        xdt = xp * dt_pair
        ys.append(jnp.dot(l_a, jnp.where(lo, xdt, 0.0).astype(BF16), preferred_element_type=F32)
                  + jnp.dot(l_b, jnp.where(lo, 0.0, xdt).astype(BF16), preferred_element_type=F32))
        xws.append((xp * dtw_pair).astype(BF16))
        xps.append(xp)
        acs.append((ac_a, ac_b))
        eas.append(jnp.where(lo, ext(ea_full, la), ext(ea_full, lb)))

    hss = [hs_sc[j] for j in pairs]
    yoffs = [[] for _ in pairs]
    for cc in range(nch):
        r0 = cc * c
        for j in pairs:
            ac_a, ac_b = acs[j]
            yoffs[j].append(_nt(cm16[r0:r0 + c], hss[j].astype(BF16)) * eas[j][r0:r0 + c])
            cs = _tn(xws[j][r0:r0 + c], bm16[r0:r0 + c])
            al = jnp.where(rowlo, jnp.exp(ac_a[r0 + c - 1:r0 + c, :]), jnp.exp(ac_b[r0 + c - 1:r0 + c, :]))
            hss[j] = hss[j] * al + cs
    for j in pairs:
        hs_sc[j] = hss[j]
        yoff = yoffs[j][0] if nch == 1 else jnp.concatenate(yoffs[j], axis=0)
        cols = slice(j * 128, (j + 1) * 128)
        y_sc[:, cols] = (ys[j] + yoff + d_row[:, cols] * xps[j]) * _silu(z[:, cols])

    yy = y_sc[...]
    ms = jnp.mean(yy * yy, axis=-1, keepdims=True)
    y_ref[0] = (yy * lax.rsqrt(ms + EPS) * nw_ref[...]).astype(y_ref.dtype)

    @pl.when(t == nt - 1)
    def _():
        hout_ref[0] = hs_sc[...]


def ssd(xbc3, p3, h0, alog_row, dtb_row, d_row, norm_w):
    bsz, s, _ = xbc3.shape
    c = min(CHUNK, s)
    lt = min(256, s)
    gw = BRANCH // SSM_G
    npair = SSM_H // SSM_G // 2
    small_blk = P_SMALL // 128
    zb_blk = P_ZB // gw
    return pl.pallas_call(
        functools.partial(_ssd_kernel, lt=lt, c=c),
        out_shape=(jax.ShapeDtypeStruct((bsz, s, BRANCH), BF16),
                   jax.ShapeDtypeStruct((bsz, SSM_H // 2, 128, 128), F32)),
        grid=(bsz, SSM_G, s // lt),
        in_specs=[pl.BlockSpec((1, lt, gw), lambda b, g, t: (b, t, g)),
                  pl.BlockSpec((1, lt, 128), lambda b, g, t: (b, t, BRANCH // 128 + g)),
                  pl.BlockSpec((1, lt, 128), lambda b, g, t: (b, t, BRANCH // 128 + SSM_G + g)),
                  pl.BlockSpec((1, lt, 128), lambda b, g, t: (b, t, small_blk)),
                  pl.BlockSpec((1, lt, gw), lambda b, g, t: (b, t, zb_blk + g)),
                  pl.BlockSpec((1, npair, 128, 128), lambda b, g, t: (b, g, 0, 0)),
                  pl.BlockSpec((1, 128), lambda b, g, t: (0, 0)),
                  pl.BlockSpec((1, 128), lambda b, g, t: (0, 0)),
                  pl.BlockSpec((1, gw), lambda b, g, t: (0, g)),
                  pl.BlockSpec((1, gw), lambda b, g, t: (0, g))],
        out_specs=(pl.BlockSpec((1, lt, gw), lambda b, g, t: (b, t, g)),
                   pl.BlockSpec((1, npair, 128, 128), lambda b, g, t: (b, g, 0, 0))),
        scratch_shapes=[pltpu.VMEM((npair, 128, 128), F32), pltpu.VMEM((lt, gw), F32)],
        compiler_params=_cp(("parallel", "parallel", "arbitrary")),
        name="ssd",
    )(xbc3, xbc3, xbc3, p3, p3, h0, alog_row, dtb_row, d_row, norm_w.reshape(1, BRANCH))


def _mla_prep_kernel(p_ref, qn_ref, kvn_ref, ck_ref, sak_ref, sbk_ref,
                     cq_ref, lat_ref, kr_ref, latkr_ref):
    blk = p_ref[...]
    cq = blk[:, 0:MLA_Q_RANK]
    cq = cq * lax.rsqrt(jnp.mean(cq * cq, axis=-1, keepdims=True) + EPS) * qn_ref[...]
    cq_ref[...] = cq.astype(cq_ref.dtype)
    ckv = blk[:, MLA_Q_RANK:MLA_Q_RANK + MLA_KV_RANK]
    lat = ckv * lax.rsqrt(jnp.mean(ckv * ckv, axis=-1, keepdims=True) + EPS) * kvn_ref[...]
    lat_ref[...] = lat
    sm = blk[:, 1280:1408]
    y = sm * ck_ref[...] + pltpu.roll(sm, 32, 1) * sak_ref[...] + pltpu.roll(sm, 96, 1) * sbk_ref[...]
    kr_ref[...] = y[:, 0:MLA_ROPE]
    latkr_ref[:, 0:MLA_KV_RANK] = lat.astype(latkr_ref.dtype)
    latkr_ref[:, MLA_KV_RANK:LATKR] = y.astype(latkr_ref.dtype)


def mla_prep(p, q_norm, kv_norm, ck, sak, sbk):
    m = p.shape[0]
    tm = min(256, m)
    mla_blk = P_MLA // 1536
    row = lambda n: pl.BlockSpec((tm, n), lambda i: (i, 0))
    return pl.pallas_call(
        _mla_prep_kernel,
        out_shape=(jax.ShapeDtypeStruct((m, MLA_Q_RANK), BF16),
                   jax.ShapeDtypeStruct((m, MLA_KV_RANK), F32),
                   jax.ShapeDtypeStruct((m, MLA_ROPE), F32),
                   jax.ShapeDtypeStruct((m, LATKR), BF16)),
        grid=(m // tm,),
        in_specs=[pl.BlockSpec((tm, 1536), lambda i: (i, mla_blk)),
                  pl.BlockSpec((1, MLA_Q_RANK), lambda i: (0, 0)),
                  pl.BlockSpec((1, MLA_KV_RANK), lambda i: (0, 0)),
                  row(128), row(128), row(128)],
        out_specs=(row(MLA_Q_RANK), row(MLA_KV_RANK), row(MLA_ROPE), row(LATKR)),
        compiler_params=_cp(("parallel",)),
        name="mla_prep",
    )(p, q_norm.reshape(1, -1), kv_norm.reshape(1, -1), ck, sak, sbk)


def _pack_cache_kernel(lat_ref, kr_ref, o_ref):
    o_ref[0, :, 0:MLA_KV_RANK] = lat_ref[0, 0].astype(o_ref.dtype)
    o_ref[0, :, MLA_KV_RANK:MLA_KV_RANK + MLA_ROPE] = kr_ref[0, 0].astype(o_ref.dtype)
    o_ref[0, :, MLA_KV_RANK + MLA_ROPE:LATKR] = jnp.zeros((o_ref.shape[1], LATKR - MLA_KV_RANK - MLA_ROPE),
                                                          o_ref.dtype)


def pack_latent_cache(cache_lat, cache_kr, l, *, tp=1024):
    _, bsz, n_past, _ = cache_lat.shape
    tp = min(tp, n_past)
    assert n_past % tp == 0
    return pl.pallas_call(
        _pack_cache_kernel,
        out_shape=jax.ShapeDtypeStruct((bsz, n_past, LATKR), BF16),
        grid=(bsz, n_past // tp),
        in_specs=[pl.BlockSpec((1, 1, tp, MLA_KV_RANK), lambda b, t: (l, b, t, 0)),
                  pl.BlockSpec((1, 1, tp, MLA_ROPE), lambda b, t: (l, b, t, 0))],
        out_specs=pl.BlockSpec((1, tp, LATKR), lambda b, t: (b, t, 0)),
        compiler_params=_cp(("parallel", "parallel")),
        name="pack_latent_cache",
    )(cache_lat, cache_kr)


def _q_absorb_kernel(q_ref, wk_ref, o_ref):
    q = q_ref[...]
    o_ref[0, :, 0:MLA_KV_RANK] = jnp.dot(q[:, 0:MLA_NOPE], wk_ref[0],
                                         preferred_element_type=F32).astype(o_ref.dtype)
    o_ref[0, :, MLA_KV_RANK:LATKR] = q[:, MLA_NOPE:QSLOT]


def q_absorb(q_full, w_ukt):
    m = q_full.shape[0]
    return pl.pallas_call(
        _q_absorb_kernel,
        out_shape=jax.ShapeDtypeStruct((MLA_H, m, LATKR), BF16),
        grid=(MLA_H,),
        in_specs=[pl.BlockSpec((m, QSLOT), lambda h: (0, h)),
                  pl.BlockSpec((1, MLA_NOPE, MLA_KV_RANK), lambda h: (h, 0, 0))],
        out_specs=pl.BlockSpec((1, m, LATKR), lambda h: (h, 0, 0)),
        compiler_params=_cp(("parallel",)),
        name="q_absorb",
    )(q_full, w_ukt)


def _latent_attn_kernel(q_ref, kv_ref, wv_ref, o_ref, *, s, q_off, nk_real):
    kv = kv_ref[0]
    nk = kv.shape[0]
    q = q_ref[...].reshape(MLA_H * s, LATKR)
    sc = _nt(q, kv)
    tok = lax.broadcasted_iota(jnp.int32, (MLA_H * s, 1), 0) & (s - 1)
    kidx = lax.broadcasted_iota(jnp.int32, (1, nk), 1)
    vis = ((kidx >> 6) <= ((q_off + tok) >> 6)) & (kidx < nk_real)
    sc = jnp.where(vis, sc, -1e30)
    p = jnp.where(vis, jnp.exp2(sc - jnp.max(sc, axis=-1, keepdims=True)), 0.0)
    l = jnp.sum(p, axis=-1, keepdims=True)
    ctx = (jnp.dot(p.astype(BF16), kv, preferred_element_type=F32) / l).astype(BF16)
    for h in range(MLA_H):
        o_ref[0, :, h * MLA_V:(h + 1) * MLA_V] = jnp.dot(
            ctx[h * s:(h + 1) * s, 0:MLA_KV_RANK], wv_ref[h], preferred_element_type=F32).astype(o_ref.dtype)


def latent_attention(q_lat, latkr_all, w_uv_h, *, bsz, s, q_off, nk_real):
    tk_tot = latkr_all.shape[1]
    return pl.pallas_call(
        functools.partial(_latent_attn_kernel, s=s, q_off=q_off, nk_real=nk_real),
        out_shape=jax.ShapeDtypeStruct((bsz, s, BRANCH), BF16),
        grid=(bsz,),
        in_specs=[pl.BlockSpec((MLA_H, s, LATKR), lambda b: (0, b, 0)),
                  pl.BlockSpec((1, tk_tot, LATKR), lambda b: (b, 0, 0)),
                  pl.BlockSpec((MLA_H, MLA_KV_RANK, MLA_V), lambda b: (0, 0, 0))],
        out_specs=pl.BlockSpec((1, s, BRANCH), lambda b: (b, 0, 0)),
        compiler_params=_cp(("parallel",)),
        name="latent_attention",
    )(q_lat, latkr_all, w_uv_h)


FLASH_STRIP = 512


def _flash_t_kernel(qi_ref, kj_ref, q_ref, k_ref, vt_ref, o_ref, m_sc, l_sc, acc_sc, *, tq, tk, hps):
    pidx = pl.program_id(2)
    i = qi_ref[pidx]
    j = kj_ref[pidx]

    @pl.when(j == 0)
    def _():
        m_sc[...] = jnp.full(m_sc.shape, -1e30, F32)
        l_sc[...] = jnp.zeros(l_sc.shape, F32)
        acc_sc[...] = jnp.zeros(acc_sc.shape, F32)

    nstrip = tq // FLASH_STRIP

    def strip_mode(c, key_off):
        q_lo, q_hi = (c * FLASH_STRIP) // CHUNK, (c * FLASH_STRIP + FLASH_STRIP - 1) // CHUNK
        k_lo, k_hi = key_off // CHUNK, (key_off + tk - 1) // CHUNK
        if k_lo > q_hi:
            return "hidden"
        return "visible" if k_hi <= q_lo else "masked"

    def tile(key_off):
        modes = ["visible" if key_off is None else strip_mode(c, key_off) for c in range(nstrip)]
        live = [(hh, c) for hh in range(hps) for c in range(nstrip) if modes[c] != "hidden"]
        kchunk = (j * tk + lax.broadcasted_iota(jnp.int32, (tk, 1), 0)) >> 6
        s_v, p_v = {}, {}

        def scores(item):
            hh, c = item
            rows = slice(c * FLASH_STRIP, (c + 1) * FLASH_STRIP)
            slot = slice(hh * QSLOT, (hh + 1) * QSLOT)
            s = _nt(k_ref[0, :, slot], q_ref[0, rows, slot])
            vis = None
            if modes[c] == "masked":
                qchunk = (i * tq + c * FLASH_STRIP + lax.broadcasted_iota(jnp.int32, (1, FLASH_STRIP), 1)) >> 6
                vis = kchunk <= qchunk
                s = jnp.where(vis, s, -1e30)
            s_v[item] = (s, vis)

        def softmax(item):
            hh, c = item
            lanes = slice(c * FLASH_STRIP, (c + 1) * FLASH_STRIP)
            s, vis = s_v.pop(item)
            m_prev = m_sc[hh, :, lanes]
            m_new = jnp.maximum(m_prev, jnp.max(s, axis=0, keepdims=True))
            alpha = jnp.exp2(m_prev - m_new)
            p = jnp.exp2(s - m_new)
            if vis is not None:
                p = jnp.where(vis, p, 0.0)
            l_sc[hh, :, lanes] = alpha * l_sc[hh, :, lanes] + jnp.sum(p, axis=0, keepdims=True)
            m_sc[hh, :, lanes] = m_new
            p_v[item] = (p.astype(BF16), alpha)

        def values(item):
            hh, c = item
            lanes = slice(c * FLASH_STRIP, (c + 1) * FLASH_STRIP)
            p16, alpha = p_v.pop(item)
            acc_sc[hh, :, lanes] = alpha * acc_sc[hh, :, lanes] + jnp.dot(
                vt_ref[0, hh * MLA_V:(hh + 1) * MLA_V, :], p16, preferred_element_type=F32)

        for step in range(len(live) + 2):
            if step < len(live):
                scores(live[step])
            if 0 <= step - 1 < len(live):
                softmax(live[step - 1])
            if 0 <= step - 2 < len(live):
                values(live[step - 2])

    ratio = tq // tk

    @pl.when(j < i * ratio)
    def _():
        tile(None)

    for r in range(ratio):
        @pl.when(j == i * ratio + r)
        def _(r=r):
            tile(r * tk)

    @pl.when(j == (i * tq + tq - 1) // tk)
    def _():
        for hh in range(hps):
            o_ref[0, :, hh * MLA_V:(hh + 1) * MLA_V] = (acc_sc[hh] / l_sc[hh]).T.astype(o_ref.dtype)


def flash_attention_causal(q3, k3, vt3, *, tq=2048, tk=1024, hps=4):
    bsz, s, _ = q3.shape
    tq, tk = min(tq, s), min(tk, s)
    assert s % tq == 0 and tq % tk == 0 and tq % FLASH_STRIP == 0 and tk % CHUNK == 0
    pairs = [(i, j) for i in range(s // tq) for j in range((i * tq + tq - 1) // tk + 1)]
    qi = jnp.asarray([p[0] for p in pairs], jnp.int32)
    kj = jnp.asarray([p[1] for p in pairs], jnp.int32)
    grid_spec = pltpu.PrefetchScalarGridSpec(
        num_scalar_prefetch=2,
        grid=(bsz, MLA_H // hps, len(pairs)),
        in_specs=[pl.BlockSpec((1, tq, hps * QSLOT), lambda b, h, p, qi, kj: (b, qi[p], h)),
                  pl.BlockSpec((1, tk, hps * QSLOT), lambda b, h, p, qi, kj: (b, kj[p], h)),
                  pl.BlockSpec((1, hps * MLA_V, tk), lambda b, h, p, qi, kj: (b, h, kj[p]))],
        out_specs=pl.BlockSpec((1, tq, hps * MLA_V), lambda b, h, p, qi, kj: (b, qi[p], h)),
        scratch_shapes=[pltpu.VMEM((hps, 1, tq), F32), pltpu.VMEM((hps, 1, tq), F32),
                        pltpu.VMEM((hps, MLA_V, tq), F32)])
    return pl.pallas_call(
        functools.partial(_flash_t_kernel, tq=tq, tk=tk, hps=hps),
        out_shape=jax.ShapeDtypeStruct((bsz, s, BRANCH), BF16),
        grid_spec=grid_spec,
        compiler_params=_cp(("parallel", "parallel", "arbitrary")),
        name="flash_attention_causal",
    )(qi, kj, q3, k3, vt3)


def _vt_kernel(w_ref, x_ref, o_ref):
    o_ref[0] = _nt(w_ref[...], x_ref[0]).astype(o_ref.dtype)


def v_up_transposed(w_vt, latkr3, *, tn=512):
    bsz, s, _ = latkr3.shape
    tn = min(tn, s)
    nv = w_vt.shape[0]
    return pl.pallas_call(
        _vt_kernel,
        out_shape=jax.ShapeDtypeStruct((bsz, nv, s), BF16),
        grid=(bsz, s // tn),
        in_specs=[pl.BlockSpec((nv, LATKR), lambda b, t: (0, 0)),
                  pl.BlockSpec((1, tn, LATKR), lambda b, t: (b, t, 0))],
        out_specs=pl.BlockSpec((1, nv, tn), lambda b, t: (b, 0, t)),
        compiler_params=_cp(("parallel", "parallel")),
        name="v_up_transposed",
    )(w_vt, latkr3)


def _branch_kernel(oa_ref, ob_ref, oc_ref, w_ref, g0_ref, g1_ref, g2_ref, out_ref):
    gates = [jax.nn.sigmoid(g_ref[...]) for g_ref in (g0_ref, g1_ref, g2_ref)]
    us = [jnp.dot(o_ref[...], w_ref[n], preferred_element_type=F32)
          for n, o_ref in enumerate((oa_ref, ob_ref, oc_ref))]
    out_ref[...] = (gates[0] * us[0] + gates[1] * us[1] + gates[2] * us[2]).astype(out_ref.dtype)


def branch_merge(oa, ob, oc, w_branch, p):
    m = oa.shape[0]
    tm = min(1024, m)
    tn = 512
    nj = D_MODEL // tn
    o_spec = pl.BlockSpec((tm, BRANCH), lambda i, j: (i, 0))
    gate = lambda n: pl.BlockSpec((tm, tn), lambda i, j: (i, P_GATE // tn + n * nj + j))
    return pl.pallas_call(
        _branch_kernel,
        out_shape=jax.ShapeDtypeStruct((m, D_MODEL), BF16),
        grid=(m // tm, nj),
        in_specs=[o_spec, o_spec, o_spec,
                  pl.BlockSpec((3, BRANCH, tn), lambda i, j: (0, 0, j)),
                  gate(0), gate(1), gate(2)],
        out_specs=pl.BlockSpec((tm, tn), lambda i, j: (i, j)),
        compiler_params=_cp(("parallel", "parallel")),
        name="branch_merge",
    )(oa, ob, oc, w_branch, p, p, p)


def _pack_layer(l, w):
    w_in = w['w_in'][l]
    seg = lambda a, n: w_in[:, a:a + n]
    zeros = lambda n: jnp.zeros((D_MODEL, n), w_in.dtype)
    w_in_p = jnp.concatenate([
        seg(8032, 12288),
        seg(6688, 768), seg(7456, 512),
        seg(7968, 64), seg(4096, 8), seg(4104, 8), seg(6672, 16), zeros(32), zeros(128),
        seg(5136, 1536), seg(0, 3072), seg(3072, 1024), seg(4112, 1024)], axis=1).astype(BF16)
    w_uq = jnp.pad(w['mla_w_uq'][l], ((0, 0), (0, 0), (0, QSLOT - MLA_NOPE - MLA_ROPE)))
    w_uq = w_uq.reshape(MLA_Q_RANK, MLA_H * QSLOT).astype(BF16)
    wk = jnp.pad(w['mla_w_uk'][l], ((0, 0), (0, 0), (0, QSLOT - MLA_NOPE))).reshape(MLA_KV_RANK, MLA_H * QSLOT)
    eye = jnp.broadcast_to(jnp.eye(MLA_ROPE, dtype=F32)[:, None, :], (MLA_ROPE, MLA_H, MLA_ROPE))
    wkr = jnp.pad(eye, ((0, 0), (0, 0), (MLA_NOPE, QSLOT - MLA_NOPE - MLA_ROPE))).reshape(MLA_ROPE, MLA_H * QSLOT)
    wk = jnp.concatenate([wk, wkr, jnp.zeros((LATKR - MLA_KV_RANK - MLA_ROPE, MLA_H * QSLOT), F32)], axis=0)
    wv = jnp.pad(w['mla_w_uv'][l].reshape(MLA_KV_RANK, MLA_H * MLA_V), ((0, LATKR - MLA_KV_RANK), (0, 0)))
    w_k = wk.astype(BF16)
    w_vt = wv.T.astype(BF16)
    w_ukt = jnp.transpose(w['mla_w_uk'][l], (1, 2, 0)).astype(BF16)
    w_uv_h = jnp.transpose(w['mla_w_uv'][l], (1, 0, 2)).astype(BF16)

    def lane_row(vals, off):
        return jnp.zeros((1, 128), F32).at[0, off:off + vals.shape[0]].set(vals.astype(F32))

    return dict(
        attn_norm=w['attn_norm'][l], w_in=w_in_p,
        gdn_conv_w=w['gdn_conv_w'][l], gdn_alog=lane_row(w['gdn_A_log'][l], L_ALPHA),
        gdn_dtb=lane_row(w['gdn_dt_bias'][l], L_ALPHA), gdn_norm=w['gdn_norm'][l],
        ssm_conv_w=w['ssm_conv_w'][l], ssm_conv_b=w['ssm_conv_b'][l],
        ssm_alog=lane_row(w['ssm_A_log'][l], L_DT), ssm_dtb=lane_row(w['ssm_dt_bias'][l], L_DT),
        ssm_d=jnp.repeat(w['ssm_D'][l].astype(F32), SSM_P).reshape(1, BRANCH), ssm_norm=w['ssm_norm'][l],
        mla_q_norm=w['mla_q_norm'][l], w_uq=w_uq, mla_kv_norm=w['mla_kv_norm'][l], w_k=w_k, w_vt=w_vt, w_ukt=w_ukt, w_uv_h=w_uv_h,
        w_branch=w['w_branch'][l].astype(BF16), w_out=w['w_out'][l].astype(BF16),
        ffn_norm=w['ffn_norm'][l], w_up_il=interleave_cast_w_up(w['ffn_w_up'], l),
        ffn_conv_w=w['ffn_conv_w'][l], ffn_conv_b=w['ffn_conv_b'][l],
        w_down=w['ffn_w_down'][l].astype(BF16))


def _rope_tables(pos):
    half = MLA_ROPE // 2
    inv = ROPE_THETA ** (-jnp.arange(half, dtype=F32) / half)
    ang = pos.astype(F32)[:, None] * inv[None, :]
    cos, sin = jnp.cos(ang), jnp.sin(ang)
    n = pos.shape[0]
    z = lambda k: jnp.zeros((n, k), F32)
    cq = jnp.concatenate([jnp.ones((n, MLA_NOPE), F32), cos, cos, z(64)], axis=1)
    saq = jnp.concatenate([z(MLA_NOPE + half), sin, z(64)], axis=1)
    sbq = jnp.concatenate([z(MLA_NOPE), -sin, z(half + 64)], axis=1)
    ck = jnp.concatenate([cos, cos, z(64)], axis=1)
    sak = jnp.concatenate([z(half), sin, z(64)], axis=1)
    sbk = jnp.concatenate([-sin, z(half + 64)], axis=1)
    return cq, saq, sbq, ck, sak, sbk


def _pad_state(st):
    return jnp.pad(st, ((0, 0), (8 - st.shape[1], 0), (0, 0)))


def _layer(x, tabs, q_off, state, lw):
    latkr_past, gconv, gssm, mconv, mssm, fconv = state
    bsz, s, _ = x.shape
    m = bsz * s
    cq_t, saq_t, sbq_t, ck_t, sak_t, sbk_t = tabs
    x2 = x.reshape(m, D_MODEL)
    mm_tm = 512

    p = rmsnorm_matmul(x2, lw['attn_norm'], lw['w_in'], F32, tm=mm_tm, tn=1024, name="in_proj")
    p3 = p.reshape(bsz, s, P_TOT)

    qkv_act, gconv8 = causal_conv(p3, P_QKV, GDN_QKV, _pad_state(gconv), lw['gdn_conv_w'],
                                  jnp.zeros((GDN_QKV,), F32), act=True)
    o_a, gssm_new = gdn(qkv_act, p3, gssm, lw['gdn_alog'], lw['gdn_dtb'], lw['gdn_norm'])

    xbc_act, mconv8 = causal_conv(p3, P_XBC, SSM_CONV_DIM, _pad_state(mconv), lw['ssm_conv_w'],
                                  lw['ssm_conv_b'], act=True)
    o_b, mssm_new = ssd(xbc_act, p3, mssm.reshape(bsz, SSM_H // 2, 128, 128), lw['ssm_alog'], lw['ssm_dtb'],
                        lw['ssm_d'], lw['ssm_norm'])
    mssm_new = mssm_new.reshape(bsz, SSM_H, SSM_P, SSM_N)

    cqn, lat, kr, latkr = mla_prep(p, lw['mla_q_norm'], lw['mla_kv_norm'], ck_t, sak_t, sbk_t)
    tq_tab = min(mm_tm, m)
    q_full = matmul(cqn, lw['w_uq'], BF16, tm=mm_tm, tn=MLA_H * QSLOT, name="q_up",
                    extras=[(cq_t, (tq_tab, QSLOT), lambda i, j: (i, 0)),
                            (saq_t, (tq_tab, QSLOT), lambda i, j: (i, 0)),
                            (sbq_t, (tq_tab, QSLOT), lambda i, j: (i, 0))],
                    epilogue=_q_rope_epilogue)
    latkr3 = latkr.reshape(bsz, s, LATKR)
    q3 = q_full.reshape(bsz, s, MLA_H * QSLOT)
    if latkr_past is None and q_off == 0 and s % 512 == 0:
        k_full = matmul(latkr, lw['w_k'], BF16, tm=mm_tm, tn=1024, name="k_up")
        vt = v_up_transposed(lw['w_vt'], latkr3)
        o_c = flash_attention_causal(q3, k_full.reshape(bsz, s, MLA_H * QSLOT), vt)
    else:
        past = jnp.zeros((bsz, 0, LATKR), BF16) if latkr_past is None else latkr_past
        nk_real = past.shape[1] + s
        nk_pad = -(-nk_real // 128) * 128
        latkr_all = jnp.concatenate([past, latkr3, jnp.zeros((bsz, nk_pad - nk_real, LATKR), BF16)], axis=1)
        assert s & (s - 1) == 0 and MLA_H * s <= 512
        o_c = latent_attention(q_absorb(q_full, lw['w_ukt']), latkr_all, lw['w_uv_h'],
                               bsz=bsz, s=s, q_off=q_off, nk_real=nk_real)

    merged = branch_merge(o_a.reshape(m, BRANCH), o_b.reshape(m, BRANCH), o_c.reshape(m, BRANCH),
                          lw['w_branch'], p)
    x2 = matmul_residual(merged, lw['w_out'], x2, tm=1024, tn=1024, name="out_proj")

    h2 = rmsnorm(x2, lw['ffn_norm'], BF16)
    if s % 512 == 0:
        act, fconv8 = ffn_up_fused(h2, lw['w_up_il'], _pad_state(fconv), lw['ffn_conv_w'], lw['ffn_conv_b'],
                                   bsz=bsz, tm=max(t for t in (512, 1024, 2048) if s % t == 0))
    else:
        up = matmul_tile_major(h2, lw['w_up_il'])
        act, fconv8 = ffn_conv_gate_il(up.reshape(bsz, s, 2 * D_FF), _pad_state(fconv), lw['ffn_conv_w'],
                                       lw['ffn_conv_b'])
    x2 = matmul_residual(act.reshape(m, D_FF), lw['w_down'], x2, tm=mm_tm, tn=512, name="ffn_down")

    new = (lat.reshape(bsz, s, MLA_KV_RANK), kr.reshape(bsz, s, MLA_ROPE),
           gconv8[:, 8 - gconv.shape[1]:], gssm_new, mconv8[:, 8 - mconv.shape[1]:], mssm_new,
           fconv8[:, 8 - fconv.shape[1]:])
    return x2.reshape(bsz, s, D_MODEL), new


def kernel(x_prompt, x_sample, cache_mla_latent, cache_mla_krope, state_gdn_conv, state_gdn_ssm, state_ssm_conv, state_ssm, state_ffn_conv, attn_norm, w_in, gdn_conv_w, gdn_A_log, gdn_dt_bias, gdn_norm, ssm_conv_w, ssm_conv_b, ssm_A_log, ssm_dt_bias, ssm_D, ssm_norm, mla_q_norm, mla_w_uq, mla_kv_norm, mla_w_uk, mla_w_uv, w_branch, w_out, ffn_norm, ffn_w_up, ffn_conv_w, ffn_conv_b, ffn_w_down, final_norm):
    weights = dict(attn_norm=attn_norm, w_in=w_in, gdn_conv_w=gdn_conv_w, gdn_A_log=gdn_A_log,
                   gdn_dt_bias=gdn_dt_bias, gdn_norm=gdn_norm, ssm_conv_w=ssm_conv_w, ssm_conv_b=ssm_conv_b,
                   ssm_A_log=ssm_A_log, ssm_dt_bias=ssm_dt_bias, ssm_D=ssm_D, ssm_norm=ssm_norm,
                   mla_q_norm=mla_q_norm, mla_w_uq=mla_w_uq, mla_kv_norm=mla_kv_norm, mla_w_uk=mla_w_uk,
                   mla_w_uv=mla_w_uv, w_branch=w_branch, w_out=w_out, ffn_norm=ffn_norm, ffn_w_up=ffn_w_up,
                   ffn_conv_w=ffn_conv_w, ffn_conv_b=ffn_conv_b, ffn_w_down=ffn_w_down)
    depth = w_in.shape[0]
    bp, sp, _ = x_prompt.shape
    bs, ss, _ = x_sample.shape
    n_past = cache_mla_latent.shape[2]
    pos_p = jnp.arange(sp, dtype=jnp.int32)
    pos_s = n_past + jnp.arange(ss, dtype=jnp.int32)
    tabs_p = _rope_tables(pos_p)
    tabs_s = tuple(jnp.tile(t, (bs, 1)) for t in _rope_tables(pos_s))
    zero_state = (
        None,
        jnp.zeros((bp,) + state_gdn_conv.shape[2:], F32),
        jnp.zeros((bp,) + state_gdn_ssm.shape[2:], F32),
        jnp.zeros((bp,) + state_ssm_conv.shape[2:], F32),
        jnp.zeros((bp,) + state_ssm.shape[2:], F32),
        jnp.zeros((bp,) + state_ffn_conv.shape[2:], F32),
    )
    xp, xs = x_prompt, x_sample
    new_p, new_s = [], []
    for l in range(depth):
        lw = _pack_layer(l, weights)
        xp, st_p = _layer(xp, tabs_p, 0, zero_state, lw)
        latkr_past = pack_latent_cache(cache_mla_latent, cache_mla_krope, l)
        st_in = (latkr_past, state_gdn_conv[l], state_gdn_ssm[l], state_ssm_conv[l], state_ssm[l],
                 state_ffn_conv[l])
        xs, st_s = _layer(xs, tabs_s, n_past, st_in, lw)
        new_p.append(st_p)
        new_s.append(st_s)
    y_prompt = rmsnorm(xp.reshape(bp * sp, D_MODEL), final_norm, F32).reshape(bp, sp, D_MODEL)
    y_sample = rmsnorm(xs.reshape(bs * ss, D_MODEL), final_norm, F32).reshape(bs, ss, D_MODEL)
    outs_p = [jnp.stack(t) for t in zip(*new_p)]
    outs_s = [jnp.stack(t) for t in zip(*new_s)]
    return (y_prompt, y_sample, *outs_p, *outs_s)
```

```python
import functools
import math

import jax
import jax.numpy as jnp
from jax import lax
from jax.experimental import pallas as pl
from jax.experimental.pallas import tpu as pltpu

F32 = jnp.float32
BF16 = jnp.bfloat16
HI = lax.Precision.HIGHEST

EPS = 1e-6
CHUNK = 64
D_MODEL = 4096
BRANCH = 1024
GDN_H, GDN_D = 8, 128
GDN_QKV = 3072
SSM_H, SSM_P, SSM_G, SSM_N = 16, 64, 2, 128
SSM_CONV_DIM = 1536
MLA_H, MLA_NOPE, MLA_ROPE, MLA_V = 8, 128, 64, 128
MLA_Q_RANK, MLA_KV_RANK = 768, 512
ROPE_THETA = 10000.0
D_FF = 11008
LATKR = 640
QSLOT = 256

P_GATE = 0
P_MLA = 12288
P_SMALL = P_MLA + 1280
P_XBC = 13824
P_QKV = 15360
P_ZA = 18432
P_ZB = 19456
P_TOT = 20480
L_BETA, L_ALPHA, L_DT = 64, 72, 80

VMEM_LIMIT = 56 * 1024 * 1024


def _cp(sem):
    return pltpu.CompilerParams(dimension_semantics=sem, vmem_limit_bytes=VMEM_LIMIT)


def _softplus(x):
    return jnp.maximum(x, 0.0) + jnp.log1p(jnp.exp(-jnp.abs(x)))


def _silu(x):
    return x * jax.nn.sigmoid(x)


def _nt(a, b):
    return lax.dot_general(a, b, (((1,), (1,)), ((), ())), preferred_element_type=F32)


def _tn(a, b):
    return lax.dot_general(a, b, (((0,), (0,)), ((), ())), preferred_element_type=F32)


def _rms_kernel(x_ref, w_ref, o_ref):
    x = x_ref[...]
    ms = jnp.mean(x * x, axis=-1, keepdims=True)
    o_ref[...] = (x * lax.rsqrt(ms + EPS) * w_ref[...]).astype(o_ref.dtype)


def rmsnorm(x, w, out_dtype):
    m, d = x.shape
    tm = min(256, m)
    return pl.pallas_call(
        _rms_kernel,
        out_shape=jax.ShapeDtypeStruct((m, d), out_dtype),
        grid=(m // tm,),
        in_specs=[pl.BlockSpec((tm, d), lambda i: (i, 0)),
                  pl.BlockSpec((1, d), lambda i: (0, 0))],
        out_specs=pl.BlockSpec((tm, d), lambda i: (i, 0)),
        compiler_params=_cp(("parallel",)),
        name="rmsnorm",
    )(x, w.reshape(1, d))


def _mm_kernel(*refs, nk, n_extra, epilogue):
    a_ref, b_ref = refs[0], refs[1]
    extras = refs[2:2 + n_extra]
    o_ref = refs[2 + n_extra]
    part = jnp.dot(a_ref[...], b_ref[...], preferred_element_type=F32)
    if nk == 1:
        o_ref[...] = epilogue(part, *extras).astype(o_ref.dtype)
        return
    acc_ref = refs[3 + n_extra]
    k = pl.program_id(2)

    @pl.when(k == 0)
    def _():
        acc_ref[...] = part

    @pl.when(k > 0)
    def _():
        acc_ref[...] += part

    @pl.when(k == nk - 1)
    def _():
        o_ref[...] = epilogue(acc_ref[...], *extras).astype(o_ref.dtype)


def matmul(a, b, out_dtype, *, tm, tn, tk=None, extras=(), epilogue=None, name="matmul"):
    m, kd = a.shape
    n = b.shape[1]
    tm, tn = min(tm, m), min(tn, n)
    while m % tm:
        tm //= 2
    tk = kd if tk is None else tk
    nk = kd // tk
    assert m % tm == 0 and n % tn == 0 and kd % tk == 0
    if epilogue is None:
        epilogue = lambda acc: acc
    in_specs = [pl.BlockSpec((tm, tk), lambda i, j, k: (i, k)),
                pl.BlockSpec((tk, tn), lambda i, j, k: (k, j))]
    args = [a, b]
    for arr, blk, imap in extras:
        in_specs.append(pl.BlockSpec(blk, functools.partial(lambda i, j, k, f: f(i, j), f=imap)))
        args.append(arr)
    scratch = [pltpu.VMEM((tm, tn), F32)] if nk > 1 else []
    return pl.pallas_call(
        functools.partial(_mm_kernel, nk=nk, n_extra=len(extras), epilogue=epilogue),
        out_shape=jax.ShapeDtypeStruct((m, n), out_dtype),
        grid=(m // tm, n // tn, nk),
        in_specs=in_specs,
        out_specs=pl.BlockSpec((tm, tn), lambda i, j, k: (i, j)),
        scratch_shapes=scratch,
        compiler_params=_cp(("parallel", "parallel", "arbitrary")),
        name=name,
    )(*args)


def _norm_mm_kernel(x_ref, nw_ref, b_ref, o_ref, a16_ref):
    @pl.when(pl.program_id(1) == 0)
    def _():
        x = x_ref[...]
        ms = jnp.mean(x * x, axis=-1, keepdims=True)
        a16_ref[...] = (x * lax.rsqrt(ms + EPS) * nw_ref[...]).astype(a16_ref.dtype)

    o_ref[...] = jnp.dot(a16_ref[...], b_ref[...], preferred_element_type=F32).astype(o_ref.dtype)


def rmsnorm_matmul(x, nw, b, out_dtype, *, tm, tn, name):
    m, kd = x.shape
    n = b.shape[1]
    tm = min(tm, m)
    assert m % tm == 0 and n % tn == 0
    return pl.pallas_call(
        _norm_mm_kernel,
        out_shape=jax.ShapeDtypeStruct((m, n), out_dtype),
        grid=(m // tm, n // tn),
        in_specs=[pl.BlockSpec((tm, kd), lambda i, j: (i, 0)),
                  pl.BlockSpec((1, kd), lambda i, j: (0, 0)),
                  pl.BlockSpec((kd, tn), lambda i, j: (0, j))],
        out_specs=pl.BlockSpec((tm, tn), lambda i, j: (i, j)),
        scratch_shapes=[pltpu.VMEM((tm, kd), BF16)],
        compiler_params=_cp(("parallel", "arbitrary")),
        name=name,
    )(x, nw.reshape(1, kd), b)


def _residual_epilogue(acc, r_ref):
    return acc + r_ref[...]


def matmul_residual(a, b, res, *, tm, tn, tk=None, name="matmul_residual"):
    tm_, tn_ = min(tm, a.shape[0]), min(tn, b.shape[1])
    return matmul(a, b, F32, tm=tm, tn=tn, tk=tk, name=name,
                  extras=[(res, (tm_, tn_), lambda i, j: (i, j))], epilogue=_residual_epilogue)


def _q_rope_epilogue(acc, c_ref, sa_ref, sb_ref):
    c, sa, sb = c_ref[...], sa_ref[...], sb_ref[...]
    scale = (MLA_NOPE + MLA_ROPE) ** -0.5 * math.log2(math.e)
    outs = []
    for h in range(MLA_H):
        xs = acc[:, h * QSLOT:(h + 1) * QSLOT]
        y = xs * c + pltpu.roll(xs, 32, 1) * sa + pltpu.roll(xs, QSLOT - 32, 1) * sb
        outs.append(y * scale)
    return jnp.concatenate(outs, axis=1)


def _conv_kernel(x_ref, st_ref, w_ref, b_ref, y_ref, ns_ref, xp_ref, *, width, ts, act):
    t = pl.program_id(2)
    nt = pl.num_programs(2)

    @pl.when(t == 0)
    def _():
        xp_ref[0:8, :] = st_ref[0]

    @pl.when(t > 0)
    def _():
        xp_ref[0:8, :] = xp_ref[ts:ts + 8, :]

    xp_ref[8:8 + ts, :] = x_ref[0]
    off = 8 - (width - 1)
    y = xp_ref[off:off + ts, :] * w_ref[0:1, :]
    for j in range(1, width):
        y = y + xp_ref[off + j:off + j + ts, :] * w_ref[j:j + 1, :]
    y = y + b_ref[...]
    if act:
        y = _silu(y)
    y_ref[0] = y.astype(y_ref.dtype)

    @pl.when(t == nt - 1)
    def _():
        ns_ref[0] = xp_ref[ts:ts + 8, :]


def causal_conv(x3, col_off, chans, state8, w, b, *, act, ct=512):
    bsz, s, _ = x3.shape
    width = w.shape[0]
    ts = min(2048, s)
    assert chans % ct == 0 and col_off % ct == 0 and s % ts == 0
    cb0 = col_off // ct
    return pl.pallas_call(
        functools.partial(_conv_kernel, width=width, ts=ts, act=act),
        out_shape=(jax.ShapeDtypeStruct((bsz, s, chans), F32),
                   jax.ShapeDtypeStruct((bsz, 8, chans), F32)),
        grid=(bsz, chans // ct, s // ts),
        in_specs=[pl.BlockSpec((1, ts, ct), lambda bi, c, t: (bi, t, cb0 + c)),
                  pl.BlockSpec((1, 8, ct), lambda bi, c, t: (bi, 0, c)),
                  pl.BlockSpec((width, ct), lambda bi, c, t: (0, c)),
                  pl.BlockSpec((1, ct), lambda bi, c, t: (0, c))],
        out_specs=(pl.BlockSpec((1, ts, ct), lambda bi, c, t: (bi, t, c)),
                   pl.BlockSpec((1, 8, ct), lambda bi, c, t: (bi, 0, c))),
        scratch_shapes=[pltpu.VMEM((ts + 8, ct), F32)],
        compiler_params=_cp(("parallel", "parallel", "arbitrary")),
        name="causal_conv",
    )(x3, state8, w, b.reshape(1, chans))


HIST = 8


FFN_TN = 256


def _interleave_ffn(x):
    lead = x.shape[:-1]
    x = x.reshape(*lead, 2, D_FF // FFN_TN, FFN_TN)
    return jnp.swapaxes(x, -3, -2).reshape(*lead, 2 * D_FF)


def _deinterleave_ffn(x):
    lead = x.shape[:-1]
    x = x.reshape(*lead, D_FF // FFN_TN, 2, FFN_TN)
    return jnp.swapaxes(x, -3, -2).reshape(*lead, 2 * D_FF)


def _il_cast_kernel(a_ref, b_ref, o_ref):
    o_ref[0, :, :FFN_TN] = a_ref[0].astype(o_ref.dtype)
    o_ref[0, :, FFN_TN:] = b_ref[0].astype(o_ref.dtype)


def interleave_cast_w_up(w_all, l, *, tk=1024):
    kd = w_all.shape[1]
    nc = D_FF // FFN_TN
    return pl.pallas_call(
        _il_cast_kernel,
        out_shape=jax.ShapeDtypeStruct((nc, kd, 2 * FFN_TN), BF16),
        grid=(kd // tk, nc),
        in_specs=[pl.BlockSpec((1, tk, FFN_TN), lambda k, j: (l, k, j)),
                  pl.BlockSpec((1, tk, FFN_TN), lambda k, j: (l, k, nc + j))],
        out_specs=pl.BlockSpec((1, tk, 2 * FFN_TN), lambda k, j: (j, k, 0)),
        compiler_params=_cp(("parallel", "parallel")),
        name="interleave_cast_w_up",
    )(w_all, w_all)


def _mm_tiles_kernel(a_ref, w_ref, o_ref):
    o_ref[...] = jnp.dot(a_ref[...], w_ref[0], preferred_element_type=F32)


def matmul_tile_major(a, w_tiles):
    m, kd = a.shape
    nt, _, tn = w_tiles.shape
    return pl.pallas_call(
        _mm_tiles_kernel,
        out_shape=jax.ShapeDtypeStruct((m, nt * tn), F32),
        grid=(nt,),
        in_specs=[pl.BlockSpec((m, kd), lambda j: (0, 0)),
                  pl.BlockSpec((1, kd, tn), lambda j: (j, 0, 0))],
        out_specs=pl.BlockSpec((m, tn), lambda j: (0, j)),
        compiler_params=_cp(("parallel",)),
        name="ffn_up",
    )(a, w_tiles)


def _ffn_conv_gate_il_kernel(x_ref, st_ref, cw_ref, cb_ref, y_ref, ns_ref, xp_ref, *, width, s):
    xp_ref[0:8, :] = st_ref[0]
    xp_ref[8:8 + s, :] = x_ref[0]
    off = 8 - (width - 1)
    y = xp_ref[off:off + s, :] * cw_ref[0:1, :]
    for t in range(1, width):
        y = y + xp_ref[off + t:off + t + s, :] * cw_ref[t:t + 1, :]
    y = y + cb_ref[...]
    ns_ref[0] = xp_ref[s:s + 8, :]
    for c in range(D_FF // FFN_TN):
        g = y[:, 2 * c * FFN_TN:(2 * c + 1) * FFN_TN]
        v = y[:, (2 * c + 1) * FFN_TN:(2 * c + 2) * FFN_TN]
        y_ref[0, :, c * FFN_TN:(c + 1) * FFN_TN] = (_silu(g) * v).astype(y_ref.dtype)


def ffn_conv_gate_il(up3, state8, cw, cb):
    bsz, s, n2 = up3.shape
    width = cw.shape[0]
    y, ns = pl.pallas_call(
        functools.partial(_ffn_conv_gate_il_kernel, width=width, s=s),
        out_shape=(jax.ShapeDtypeStruct((bsz, s, D_FF), BF16),
                   jax.ShapeDtypeStruct((bsz, 8, n2), F32)),
        grid=(bsz,),
        in_specs=[pl.BlockSpec((1, s, n2), lambda b: (b, 0, 0)),
                  pl.BlockSpec((1, 8, n2), lambda b: (b, 0, 0)),
                  pl.BlockSpec((width, n2), lambda b: (0, 0)),
                  pl.BlockSpec((1, n2), lambda b: (0, 0))],
        out_specs=(pl.BlockSpec((1, s, D_FF), lambda b: (b, 0, 0)),
                   pl.BlockSpec((1, 8, n2), lambda b: (b, 0, 0))),
        scratch_shapes=[pltpu.VMEM((s + 8, n2), F32)],
        compiler_params=_cp(("parallel",)),
        name="ffn_conv_gate_il",
    )(up3, _interleave_ffn(state8), _interleave_ffn(cw), _interleave_ffn(cb.reshape(1, n2)))
    return y, _deinterleave_ffn(ns)


FFN_SUB = 256


def _ffn_row_bounds(tm):
    return tuple(range(0, tm + 1, min(tm, 2 * FFN_SUB)))


def _ffn_up_fused_kernel(a_ref, w_ref, st_ref, cw_ref, cb_ref, y_ref, ns_ref, carry_ref, *xp_refs,
                         tm, tiles_per_seq, width):
    i = pl.program_id(0)
    j = pl.program_id(1)

    @pl.when((i % tiles_per_seq) == 0)
    def _():
        carry_ref[j] = st_ref[0]

    bounds = _ffn_row_bounds(tm)
    nsub = len(bounds) - 1
    sizes = [bounds[r + 1] - bounds[r] for r in range(nsub)]
    off = HIST - (width - 1)
    cw = cw_ref[...]
    cb = cb_ref[...]

    def matmul_block(r):
        xp_refs[r][HIST:HIST + sizes[r], :] = jnp.dot(a_ref[bounds[r]:bounds[r + 1], :], w_ref[0],
                                                      preferred_element_type=F32)

    def conv_block(r):
        xp_ref = xp_refs[r]
        xp_ref[0:HIST, :] = carry_ref[j] if r == 0 else xp_refs[r - 1][sizes[r - 1]:sizes[r - 1] + HIST, :]
        y = xp_ref[off:off + sizes[r], :] * cw[0:1, :]
        for t in range(1, width):
            y = y + xp_ref[off + t:off + t + sizes[r], :] * cw[t:t + 1, :]
        y = y + cb
        y_ref[bounds[r]:bounds[r + 1], :] = (_silu(y[:, :FFN_TN]) * y[:, FFN_TN:]).astype(y_ref.dtype)

    matmul_block(0)
    for r in range(nsub):
        if r + 1 < nsub:
            matmul_block(r + 1)
        conv_block(r)
    last = xp_refs[nsub - 1][sizes[-1]:sizes[-1] + HIST, :]
    carry_ref[j] = last
    ns_ref[0] = last


def ffn_up_fused(h2, w_up_il, state8, cw, cb, *, bsz, tm=1024):
    m, kd = h2.shape
    s = m // bsz
    tm = min(tm, s)
    assert s % tm == 0 and tm % FFN_SUB == 0
    bounds = _ffn_row_bounds(tm)
    tiles_per_seq = s // tm
    width = cw.shape[0]
    nc = D_FF // FFN_TN
    tn2 = 2 * FFN_TN
    y, ns = pl.pallas_call(
        functools.partial(_ffn_up_fused_kernel, tm=tm, tiles_per_seq=tiles_per_seq, width=width),
        out_shape=(jax.ShapeDtypeStruct((m, D_FF), BF16),
                   jax.ShapeDtypeStruct((m // tm, 8, 2 * D_FF), F32)),
        grid=(m // tm, nc),
        in_specs=[pl.BlockSpec((tm, kd), lambda i, j: (i, 0), pipeline_mode=pl.Buffered(1)),
                  pl.BlockSpec((1, kd, tn2), lambda i, j: (j, 0, 0)),
                  pl.BlockSpec((1, 8, tn2), lambda i, j: (i // tiles_per_seq, 0, j)),
                  pl.BlockSpec((width, tn2), lambda i, j: (0, j)),
                  pl.BlockSpec((1, tn2), lambda i, j: (0, j))],
        out_specs=(pl.BlockSpec((tm, FFN_TN), lambda i, j: (i, j)),
                   pl.BlockSpec((1, 8, tn2), lambda i, j: (i, 0, j))),
        scratch_shapes=[pltpu.VMEM((nc, HIST, tn2), F32)]
        + [pltpu.VMEM((hi - lo + HIST, tn2), F32) for lo, hi in zip(bounds[:-1], bounds[1:])],
        compiler_params=_cp(("arbitrary", "arbitrary")),
        name="ffn_up_fused",
    )(h2, w_up_il, _interleave_ffn(state8), _interleave_ffn(cw), _interleave_ffn(cb.reshape(1, 2 * D_FF)))
    return y, _deinterleave_ffn(ns[tiles_per_seq - 1::tiles_per_seq])


def _chunk_masks(lt, c):
    sh = int(math.log2(c))
    ri = lax.broadcasted_iota(jnp.int32, (lt, lt), 0)
    ci = lax.broadcasted_iota(jnp.int32, (lt, lt), 1)
    same = (ri >> sh) == (ci >> sh)
    return ri, ci, same


def _split3(x):
    hi = x.astype(BF16)
    r = x - hi.astype(F32)
    mid = r.astype(BF16)
    lo = (r - mid.astype(F32)).astype(BF16)
    return hi, mid, lo


def _dot_exact_lhs(l16, x):
    hi, mid, lo = _split3(x)
    return (jnp.dot(l16, hi, preferred_element_type=F32) + jnp.dot(l16, mid, preferred_element_type=F32)
            + jnp.dot(l16, lo, preferred_element_type=F32))


def _dot_hi(a, b):
    a_hi = a.astype(BF16)
    a_lo = (a - a_hi.astype(F32)).astype(BF16)
    b_hi = b.astype(BF16)
    b_lo = (b - b_hi.astype(F32)).astype(BF16)
    return (jnp.dot(a_hi, b_hi, preferred_element_type=F32) + jnp.dot(a_hi, b_lo, preferred_element_type=F32)
            + jnp.dot(a_lo, b_hi, preferred_element_type=F32))


def _gdn_kernel(q_ref, k_ref, v_ref, sm_ref, z_ref, s0_ref, alog_ref, dtb_ref, nw_ref,
                o_ref, sout_ref, s_sc, vn_sc, *, lt, c, hps):
    hg = pl.program_id(1)
    t = pl.program_id(2)
    nt = pl.num_programs(2)

    @pl.when(t == 0)
    def _():
        s_sc[...] = s0_ref[0]

    sm = sm_ref[0]
    lane = lax.broadcasted_iota(jnp.int32, (1, 128), 1)
    beta_full = jax.nn.sigmoid(sm)
    g_full = -jnp.exp(alog_ref[...]) * _softplus(sm + dtb_ref[...])
    ri, ci, same = _chunk_masks(lt, c)
    tri = same & (ci <= ri)
    strict = same & (ci < ri)
    eye_b = ri == ci
    eye = jnp.where(eye_b, 1.0, 0.0)
    gc_full = _dot_exact_lhs(jnp.where(tri, 1.0, 0.0).astype(BF16), g_full)
    nw = nw_ref[...]

    hs = range(hps)
    cols = [slice(hh * GDN_D, (hh + 1) * GDN_D) for hh in hs]
    dot16 = lambda x, y: jnp.dot(x.astype(BF16), y.astype(BF16), preferred_element_type=F32)
    ks, k16s, kbs, gcs, ms, avs, qds, rhss = [], [], [], [], [], [], [], []
    for hh in hs:
        h = hg * hps + hh
        q = q_ref[0, :, cols[hh]]
        k = k_ref[0, :, cols[hh]]
        q = q * lax.rsqrt(jnp.sum(q * q, axis=-1, keepdims=True) + EPS) * (GDN_D ** -0.5)
        k = k * lax.rsqrt(jnp.sum(k * k, axis=-1, keepdims=True) + EPS)
        beta = jnp.sum(jnp.where(lane == L_BETA + h, beta_full, 0.0), axis=-1, keepdims=True)
        gc = jnp.sum(jnp.where(lane == L_ALPHA + h, gc_full, 0.0), axis=-1, keepdims=True)
        gr = jnp.sum(jnp.where(eye_b, gc, 0.0), axis=0, keepdims=True)
        decay = jnp.exp(jnp.where(tri, gc - gr, 0.0))
        kb = k * beta
        k16 = k.astype(BF16)
        ms.append(jnp.where(strict, _nt(kb.astype(BF16), k16) * decay, 0.0))
        avs.append(jnp.where(tri, _nt(q.astype(BF16), k16) * decay, 0.0).astype(BF16))
        eg = jnp.exp(gc)
        rhss.append(jnp.concatenate([v_ref[0, :, cols[hh]] * beta, kb * eg], axis=1).astype(BF16))
        qds.append((q * eg).astype(BF16))
        ks.append(k)
        gcs.append(gc)

    ps = [-m for m in ms]
    ts = [eye + p for p in ps]
    for _ in range(int(math.log2(c)) - 1):
        ps = [dot16(p, p) for p in ps]
        ts = [tt + dot16(p, tt) for p, tt in zip(ps, ts)]
    rs = [eye - _dot_hi(eye + m, tt) for m, tt in zip(ms, ts)]
    ts = [tt + dot16(tt, r) for tt, r in zip(ts, rs)]
    uws = [jnp.dot(tt.astype(BF16), rhs, preferred_element_type=F32) for tt, rhs in zip(ts, rhss)]
    us = [uw[:, :GDN_D] for uw in uws]
    ws = [uw[:, GDN_D:].astype(BF16) for uw in uws]

    ss = [s_sc[hh] for hh in hs]
    for hh in hs:
        vn_sc[hh] = us[hh].astype(BF16)
    for cc in range(lt // c):
        r0 = cc * c
        rows = slice(r0, r0 + c)
        for hh in hs:
            s16 = ss[hh].astype(BF16)
            vnew = us[hh][rows] - jnp.dot(ws[hh][rows], s16, preferred_element_type=F32)
            vn16 = vnew.astype(BF16)
            vn_sc[hh, rows, :] = vn16
            o = (jnp.dot(qds[hh][rows], s16, preferred_element_type=F32)
                 + jnp.dot(avs[hh][rows, :], vn_sc[hh], preferred_element_type=F32))
            glast = gcs[hh][r0 + c - 1:r0 + c, :]
            kd = (ks[hh][rows] * jnp.exp(glast - gcs[hh][rows])).astype(BF16)
            ss[hh] = ss[hh] * jnp.exp(glast) + _tn(kd, vn16)
            mean_sq = jnp.mean(o * o, axis=-1, keepdims=True)
            o = o * lax.rsqrt(mean_sq + EPS) * nw * _silu(z_ref[0, rows, cols[hh]])
            o_ref[0, rows, cols[hh]] = o.astype(o_ref.dtype)
    for hh in hs:
        s_sc[hh] = ss[hh]

    @pl.when(t == nt - 1)
    def _():
        sout_ref[0] = s_sc[...]


def gdn(qkv3, p3, s0, alog_row, dtb_row, norm_w, *, hps=8):
    bsz, s, _ = qkv3.shape
    c = min(CHUNK, s)
    lt = min(256, s)
    hw = hps * GDN_D
    ng = GDN_H // hps
    small_blk = P_SMALL // 128
    za_blk = P_ZA // hw
    blk = lambda off: pl.BlockSpec((1, lt, hw), lambda b, h, t: (b, t, off + h))
    return pl.pallas_call(
        functools.partial(_gdn_kernel, lt=lt, c=c, hps=hps),
        out_shape=(jax.ShapeDtypeStruct((bsz, s, BRANCH), BF16),
                   jax.ShapeDtypeStruct((bsz, GDN_H, GDN_D, GDN_D), F32)),
        grid=(bsz, ng, s // lt),
        in_specs=[blk(0), blk(ng), blk(2 * ng),
                  pl.BlockSpec((1, lt, 128), lambda b, h, t: (b, t, small_blk)),
                  blk(za_blk),
                  pl.BlockSpec((1, hps, GDN_D, GDN_D), lambda b, h, t: (b, h, 0, 0)),
                  pl.BlockSpec((1, 128), lambda b, h, t: (0, 0)),
                  pl.BlockSpec((1, 128), lambda b, h, t: (0, 0)),
                  pl.BlockSpec((1, 128), lambda b, h, t: (0, 0))],
        out_specs=(pl.BlockSpec((1, lt, hw), lambda b, h, t: (b, t, h)),
                   pl.BlockSpec((1, hps, GDN_D, GDN_D), lambda b, h, t: (b, h, 0, 0))),
        scratch_shapes=[pltpu.VMEM((hps, GDN_D, GDN_D), F32), pltpu.VMEM((hps, lt, GDN_D), BF16)],
        compiler_params=_cp(("parallel", "parallel", "arbitrary")),
        name="gdn",
    )(qkv3, qkv3, qkv3, p3, p3, s0, alog_row, dtb_row, norm_w.reshape(1, GDN_D))


def _ssd_kernel(x_ref, b_ref, c_ref, sm_ref, z_ref, h0_ref, alog_ref, dtb_ref, d_ref, nw_ref,
                y_ref, hout_ref, hs_sc, y_sc, *, lt, c):
    g = pl.program_id(1)
    t = pl.program_id(2)
    nt = pl.num_programs(2)
    npair = SSM_H // SSM_G // 2

    @pl.when(t == 0)
    def _():
        hs_sc[...] = h0_ref[0]

    sm = sm_ref[0]
    dt_full = _softplus(sm + dtb_ref[...])
    da = dt_full * (-jnp.exp(alog_ref[...]))
    ri, ci, same = _chunk_masks(lt, c)
    tri = same & (ci <= ri)
    eye_b = ri == ci
    acum_full = _dot_exact_lhs(jnp.where(tri, 1.0, 0.0).astype(BF16), da)
    nch = lt // c
    lasts = [acum_full[cc * c + c - 1:cc * c + c, :] for cc in range(nch)]
    alast_full = jnp.concatenate([jnp.broadcast_to(l, (c, 128)) for l in lasts], axis=0)
    ea_full = jnp.exp(acum_full)
    dtw_full = dt_full * jnp.exp(alast_full - acum_full)

    bm = b_ref[0]
    cm = c_ref[0]
    bm16 = bm.astype(BF16)
    cm16 = cm.astype(BF16)
    cb = jnp.where(tri, _nt(cm16, bm16), 0.0)
    lane = lax.broadcasted_iota(jnp.int32, (1, 128), 1)
    lo = lane < SSM_P
    rowlo = lax.broadcasted_iota(jnp.int32, (128, 1), 0) < SSM_P
    x = x_ref[0]
    z = z_ref[0]
    d_row = d_ref[...]

    def ext(full, l):
        return jnp.sum(jnp.where(lane == l, full, 0.0), axis=-1, keepdims=True)

    pairs = range(npair)
    acs, eas, xps, xws, ys = [], [], [], [], []
    for j in pairs:
        la = L_DT + g * (SSM_H // SSM_G) + 2 * j
        lb = la + 1
        ac_a, ac_b = ext(acum_full, la), ext(acum_full, lb)
        row_a = jnp.sum(jnp.where(eye_b, ac_a, 0.0), axis=0, keepdims=True)
        row_b = jnp.sum(jnp.where(eye_b, ac_b, 0.0), axis=0, keepdims=True)
        l_a = (jnp.exp(jnp.where(tri, ac_a - row_a, 0.0)) * cb).astype(BF16)
        l_b = (jnp.exp(jnp.where(tri, ac_b - row_b, 0.0)) * cb).astype(BF16)
        dt_pair = jnp.where(lo, ext(dt_full, la), ext(dt_full, lb))
        dtw_pair = jnp.where(lo, ext(dtw_full, la), ext(dtw_full, lb))
        xp = x[:, j * 128:(j + 1) * 128]
        xdt = xp * dt_pair
        ys.append(jnp.dot(l_a, jnp.where(lo, xdt, 0.0).astype(BF16), preferred_element_type=F32)
                  + jnp.dot(l_b, jnp.where(lo, 0.0, xdt).astype(BF16), preferred_element_type=F32))
        xws.append((xp * dtw_pair).astype(BF16))
        xps.append(xp)
        acs.append((ac_a, ac_b))
        eas.append(jnp.where(lo, ext(ea_full, la), ext(ea_full, lb)))

    hss = [hs_sc[j] for j in pairs]
    yoffs = [[] for _ in pairs]
    for cc in range(nch):
        r0 = cc * c
        for j in pairs:
            ac_a, ac_b = acs[j]
            yoffs[j].append(_nt(cm16[r0:r0 + c], hss[j].astype(BF16)) * eas[j][r0:r0 + c])
            cs = _tn(xws[j][r0:r0 + c], bm16[r0:r0 + c])
            al = jnp.where(rowlo, jnp.exp(ac_a[r0 + c - 1:r0 + c, :]), jnp.exp(ac_b[r0 + c - 1:r0 + c, :]))
            hss[j] = hss[j] * al + cs
    for j in pairs:
        hs_sc[j] = hss[j]
        yoff = yoffs[j][0] if nch == 1 else jnp.concatenate(yoffs[j], axis=0)
        cols = slice(j * 128, (j + 1) * 128)
        y_sc[:, cols] = (ys[j] + yoff + d_row[:, cols] * xps[j]) * _silu(z[:, cols])

    yy = y_sc[...]
    ms = jnp.mean(yy * yy, axis=-1, keepdims=True)
    y_ref[0] = (yy * lax.rsqrt(ms + EPS) * nw_ref[...]).astype(y_ref.dtype)

    @pl.when(t == nt - 1)
    def _():
        hout_ref[0] = hs_sc[...]


def ssd(xbc3, p3, h0, alog_row, dtb_row, d_row, norm_w):
    bsz, s, _ = xbc3.shape
    c = min(CHUNK, s)
    lt = min(256, s)
    gw = BRANCH // SSM_G
    npair = SSM_H // SSM_G // 2
    small_blk = P_SMALL // 128
    zb_blk = P_ZB // gw
    return pl.pallas_call(
        functools.partial(_ssd_kernel, lt=lt, c=c),
        out_shape=(jax.ShapeDtypeStruct((bsz, s, BRANCH), BF16),
                   jax.ShapeDtypeStruct((bsz, SSM_H // 2, 128, 128), F32)),
        grid=(bsz, SSM_G, s // lt),
        in_specs=[pl.BlockSpec((1, lt, gw), lambda b, g, t: (b, t, g)),
                  pl.BlockSpec((1, lt, 128), lambda b, g, t: (b, t, BRANCH // 128 + g)),
                  pl.BlockSpec((1, lt, 128), lambda b, g, t: (b, t, BRANCH // 128 + SSM_G + g)),
                  pl.BlockSpec((1, lt, 128), lambda b, g, t: (b, t, small_blk)),
                  pl.BlockSpec((1, lt, gw), lambda b, g, t: (b, t, zb_blk + g)),
                  pl.BlockSpec((1, npair, 128, 128), lambda b, g, t: (b, g, 0, 0)),
                  pl.BlockSpec((1, 128), lambda b, g, t: (0, 0)),
                  pl.BlockSpec((1, 128), lambda b, g, t: (0, 0)),
                  pl.BlockSpec((1, gw), lambda b, g, t: (0, g)),
                  pl.BlockSpec((1, gw), lambda b, g, t: (0, g))],
        out_specs=(pl.BlockSpec((1, lt, gw), lambda b, g, t: (b, t, g)),
                   pl.BlockSpec((1, npair, 128, 128), lambda b, g, t: (b, g, 0, 0))),
        scratch_shapes=[pltpu.VMEM((npair, 128, 128), F32), pltpu.VMEM((lt, gw), F32)],
        compiler_params=_cp(("parallel", "parallel", "arbitrary")),
        name="ssd",
    )(xbc3, xbc3, xbc3, p3, p3, h0, alog_row, dtb_row, d_row, norm_w.reshape(1, BRANCH))


def _mla_prep_kernel(p_ref, qn_ref, kvn_ref, ck_ref, sak_ref, sbk_ref,
                     cq_ref, lat_ref, kr_ref, latkr_ref):
    blk = p_ref[...]
    cq = blk[:, 0:MLA_Q_RANK]
    cq = cq * lax.rsqrt(jnp.mean(cq * cq, axis=-1, keepdims=True) + EPS) * qn_ref[...]
    cq_ref[...] = cq.astype(cq_ref.dtype)
    ckv = blk[:, MLA_Q_RANK:MLA_Q_RANK + MLA_KV_RANK]
    lat = ckv * lax.rsqrt(jnp.mean(ckv * ckv, axis=-1, keepdims=True) + EPS) * kvn_ref[...]
    lat_ref[...] = lat
    sm = blk[:, 1280:1408]
    y = sm * ck_ref[...] + pltpu.roll(sm, 32, 1) * sak_ref[...] + pltpu.roll(sm, 96, 1) * sbk_ref[...]
    kr_ref[...] = y[:, 0:MLA_ROPE]
    latkr_ref[:, 0:MLA_KV_RANK] = lat.astype(latkr_ref.dtype)
    latkr_ref[:, MLA_KV_RANK:LATKR] = y.astype(latkr_ref.dtype)


def mla_prep(p, q_norm, kv_norm, ck, sak, sbk):
    m = p.shape[0]
    tm = min(256, m)
    mla_blk = P_MLA // 1536
    row = lambda n: pl.BlockSpec((tm, n), lambda i: (i, 0))
    return pl.pallas_call(
        _mla_prep_kernel,
        out_shape=(jax.ShapeDtypeStruct((m, MLA_Q_RANK), BF16),
                   jax.ShapeDtypeStruct((m, MLA_KV_RANK), F32),
                   jax.ShapeDtypeStruct((m, MLA_ROPE), F32),
                   jax.ShapeDtypeStruct((m, LATKR), BF16)),
        grid=(m // tm,),
        in_specs=[pl.BlockSpec((tm, 1536), lambda i: (i, mla_blk)),
                  pl.BlockSpec((1, MLA_Q_RANK), lambda i: (0, 0)),
                  pl.BlockSpec((1, MLA_KV_RANK), lambda i: (0, 0)),
                  row(128), row(128), row(128)],
        out_specs=(row(MLA_Q_RANK), row(MLA_KV_RANK), row(MLA_ROPE), row(LATKR)),
        compiler_params=_cp(("parallel",)),
        name="mla_prep",
    )(p, q_norm.reshape(1, -1), kv_norm.reshape(1, -1), ck, sak, sbk)


def _pack_cache_kernel(lat_ref, kr_ref, o_ref):
    o_ref[0, :, 0:MLA_KV_RANK] = lat_ref[0, 0].astype(o_ref.dtype)
    o_ref[0, :, MLA_KV_RANK:MLA_KV_RANK + MLA_ROPE] = kr_ref[0, 0].astype(o_ref.dtype)
    o_ref[0, :, MLA_KV_RANK + MLA_ROPE:LATKR] = jnp.zeros((o_ref.shape[1], LATKR - MLA_KV_RANK - MLA_ROPE),
                                                          o_ref.dtype)


def pack_latent_cache(cache_lat, cache_kr, l, *, tp=1024):
    _, bsz, n_past, _ = cache_lat.shape
    tp = min(tp, n_past)
    assert n_past % tp == 0
    return pl.pallas_call(
        _pack_cache_kernel,
        out_shape=jax.ShapeDtypeStruct((bsz, n_past, LATKR), BF16),
        grid=(bsz, n_past // tp),
        in_specs=[pl.BlockSpec((1, 1, tp, MLA_KV_RANK), lambda b, t: (l, b, t, 0)),
                  pl.BlockSpec((1, 1, tp, MLA_ROPE), lambda b, t: (l, b, t, 0))],
        out_specs=pl.BlockSpec((1, tp, LATKR), lambda b, t: (b, t, 0)),
        compiler_params=_cp(("parallel", "parallel")),
        name="pack_latent_cache",
    )(cache_lat, cache_kr)


def _q_absorb_kernel(q_ref, wk_ref, o_ref):
    q = q_ref[...]
    o_ref[0, :, 0:MLA_KV_RANK] = jnp.dot(q[:, 0:MLA_NOPE], wk_ref[0],
                                         preferred_element_type=F32).astype(o_ref.dtype)
    o_ref[0, :, MLA_KV_RANK:LATKR] = q[:, MLA_NOPE:QSLOT]


def q_absorb(q_full, w_ukt):
    m = q_full.shape[0]
    return pl.pallas_call(
        _q_absorb_kernel,
        out_shape=jax.ShapeDtypeStruct((MLA_H, m, LATKR), BF16),
        grid=(MLA_H,),
        in_specs=[pl.BlockSpec((m, QSLOT), lambda h: (0, h)),
                  pl.BlockSpec((1, MLA_NOPE, MLA_KV_RANK), lambda h: (h, 0, 0))],
        out_specs=pl.BlockSpec((1, m, LATKR), lambda h: (h, 0, 0)),
        compiler_params=_cp(("parallel",)),
        name="q_absorb",
    )(q_full, w_ukt)


def _latent_attn_kernel(q_ref, kv_ref, wv_ref, o_ref, *, s, q_off, nk_real):
    kv = kv_ref[0]
    nk = kv.shape[0]
    q = q_ref[...].reshape(MLA_H * s, LATKR)
    sc = _nt(q, kv)
    tok = lax.broadcasted_iota(jnp.int32, (MLA_H * s, 1), 0) & (s - 1)
    kidx = lax.broadcasted_iota(jnp.int32, (1, nk), 1)
    vis = ((kidx >> 6) <= ((q_off + tok) >> 6)) & (kidx < nk_real)
    sc = jnp.where(vis, sc, -1e30)
    p = jnp.where(vis, jnp.exp2(sc - jnp.max(sc, axis=-1, keepdims=True)), 0.0)
    l = jnp.sum(p, axis=-1, keepdims=True)
    ctx = (jnp.dot(p.astype(BF16), kv, preferred_element_type=F32) / l).astype(BF16)
    for h in range(MLA_H):
        o_ref[0, :, h * MLA_V:(h + 1) * MLA_V] = jnp.dot(
            ctx[h * s:(h + 1) * s, 0:MLA_KV_RANK], wv_ref[h], preferred_element_type=F32).astype(o_ref.dtype)


def latent_attention(q_lat, latkr_all, w_uv_h, *, bsz, s, q_off, nk_real):
    tk_tot = latkr_all.shape[1]
    return pl.pallas_call(
        functools.partial(_latent_attn_kernel, s=s, q_off=q_off, nk_real=nk_real),
        out_shape=jax.ShapeDtypeStruct((bsz, s, BRANCH), BF16),
        grid=(bsz,),
        in_specs=[pl.BlockSpec((MLA_H, s, LATKR), lambda b: (0, b, 0)),
                  pl.BlockSpec((1, tk_tot, LATKR), lambda b: (b, 0, 0)),
                  pl.BlockSpec((MLA_H, MLA_KV_RANK, MLA_V), lambda b: (0, 0, 0))],
        out_specs=pl.BlockSpec((1, s, BRANCH), lambda b: (b, 0, 0)),
        compiler_params=_cp(("parallel",)),
        name="latent_attention",
    )(q_lat, latkr_all, w_uv_h)


FLASH_STRIP = 512


def _flash_t_kernel(qi_ref, kj_ref, q_ref, k_ref, vt_ref, o_ref, m_sc, l_sc, acc_sc, *, tq, tk, hps):
    pidx = pl.program_id(2)
    i = qi_ref[pidx]
    j = kj_ref[pidx]

    @pl.when(j == 0)
    def _():
        m_sc[...] = jnp.full(m_sc.shape, -1e30, F32)
        l_sc[...] = jnp.zeros(l_sc.shape, F32)
        acc_sc[...] = jnp.zeros(acc_sc.shape, F32)

    nstrip = tq // FLASH_STRIP

    def strip_mode(c, key_off):
        q_lo, q_hi = (c * FLASH_STRIP) // CHUNK, (c * FLASH_STRIP + FLASH_STRIP - 1) // CHUNK
        k_lo, k_hi = key_off // CHUNK, (key_off + tk - 1) // CHUNK
        if k_lo > q_hi:
            return "hidden"
        return "visible" if k_hi <= q_lo else "masked"

    def tile(key_off):
        modes = ["visible" if key_off is None else strip_mode(c, key_off) for c in range(nstrip)]
        live = [(hh, c) for hh in range(hps) for c in range(nstrip) if modes[c] != "hidden"]
        kchunk = (j * tk + lax.broadcasted_iota(jnp.int32, (tk, 1), 0)) >> 6
        s_v, p_v = {}, {}

        def scores(item):
            hh, c = item
            rows = slice(c * FLASH_STRIP, (c + 1) * FLASH_STRIP)
            slot = slice(hh * QSLOT, (hh + 1) * QSLOT)
            s = _nt(k_ref[0, :, slot], q_ref[0, rows, slot])
            vis = None
            if modes[c] == "masked":
                qchunk = (i * tq + c * FLASH_STRIP + lax.broadcasted_iota(jnp.int32, (1, FLASH_STRIP), 1)) >> 6
                vis = kchunk <= qchunk
                s = jnp.where(vis, s, -1e30)
            s_v[item] = (s, vis)

        def softmax(item):
            hh, c = item
            lanes = slice(c * FLASH_STRIP, (c + 1) * FLASH_STRIP)
            s, vis = s_v.pop(item)
            m_prev = m_sc[hh, :, lanes]
            m_new = jnp.maximum(m_prev, jnp.max(s, axis=0, keepdims=True))
            alpha = jnp.exp2(m_prev - m_new)
            p = jnp.exp2(s - m_new)
            if vis is not None:
                p = jnp.where(vis, p, 0.0)
            l_sc[hh, :, lanes] = alpha * l_sc[hh, :, lanes] + jnp.sum(p, axis=0, keepdims=True)
            m_sc[hh, :, lanes] = m_new
            p_v[item] = (p.astype(BF16), alpha)

        def values(item):
            hh, c = item
            lanes = slice(c * FLASH_STRIP, (c + 1) * FLASH_STRIP)
            p16, alpha = p_v.pop(item)
            acc_sc[hh, :, lanes] = alpha * acc_sc[hh, :, lanes] + jnp.dot(
                vt_ref[0, hh * MLA_V:(hh + 1) * MLA_V, :], p16, preferred_element_type=F32)

        for step in range(len(live) + 2):
            if step < len(live):
                scores(live[step])
            if 0 <= step - 1 < len(live):
                softmax(live[step - 1])
            if 0 <= step - 2 < len(live):
                values(live[step - 2])

    ratio = tq // tk

    @pl.when(j < i * ratio)
    def _():
        tile(None)

    for r in range(ratio):
        @pl.when(j == i * ratio + r)
        def _(r=r):
            tile(r * tk)

    @pl.when(j == (i * tq + tq - 1) // tk)
    def _():
        for hh in range(hps):
            o_ref[0, :, hh * MLA_V:(hh + 1) * MLA_V] = (acc_sc[hh] / l_sc[hh]).T.astype(o_ref.dtype)


def flash_attention_causal(q3, k3, vt3, *, tq=2048, tk=1024, hps=4):
    bsz, s, _ = q3.shape
    tq, tk = min(tq, s), min(tk, s)
    assert s % tq == 0 and tq % tk == 0 and tq % FLASH_STRIP == 0 and tk % CHUNK == 0
    pairs = [(i, j) for i in range(s // tq) for j in range((i * tq + tq - 1) // tk + 1)]
    qi = jnp.asarray([p[0] for p in pairs], jnp.int32)
    kj = jnp.asarray([p[1] for p in pairs], jnp.int32)
    grid_spec = pltpu.PrefetchScalarGridSpec(
        num_scalar_prefetch=2,
        grid=(bsz, MLA_H // hps, len(pairs)),
        in_specs=[pl.BlockSpec((1, tq, hps * QSLOT), lambda b, h, p, qi, kj: (b, qi[p], h)),
                  pl.BlockSpec((1, tk, hps * QSLOT), lambda b, h, p, qi, kj: (b, kj[p], h)),
                  pl.BlockSpec((1, hps * MLA_V, tk), lambda b, h, p, qi, kj: (b, h, kj[p]))],
        out_specs=pl.BlockSpec((1, tq, hps * MLA_V), lambda b, h, p, qi, kj: (b, qi[p], h)),
        scratch_shapes=[pltpu.VMEM((hps, 1, tq), F32), pltpu.VMEM((hps, 1, tq), F32),
                        pltpu.VMEM((hps, MLA_V, tq), F32)])
    return pl.pallas_call(
        functools.partial(_flash_t_kernel, tq=tq, tk=tk, hps=hps),
        out_shape=jax.ShapeDtypeStruct((bsz, s, BRANCH), BF16),
        grid_spec=grid_spec,
        compiler_params=_cp(("parallel", "parallel", "arbitrary")),
        name="flash_attention_causal",
    )(qi, kj, q3, k3, vt3)


def _kv_kernel(wk_ref, wvt_ref, x_ref, k_ref, vt_ref):
    x = x_ref[0]
    k_ref[0] = jnp.dot(x, wk_ref[...], preferred_element_type=F32).astype(k_ref.dtype)
    vt_ref[0] = _nt(wvt_ref[...], x).astype(vt_ref.dtype)


def kv_up(w_k, w_vt, latkr3, *, tn=512):
    bsz, s, _ = latkr3.shape
    tn = min(tn, s)
    nk, nv = w_k.shape[1], w_vt.shape[0]
    return pl.pallas_call(
        _kv_kernel,
        out_shape=(jax.ShapeDtypeStruct((bsz, s, nk), BF16), jax.ShapeDtypeStruct((bsz, nv, s), BF16)),
        grid=(bsz, s // tn),
        in_specs=[pl.BlockSpec((LATKR, nk), lambda b, t: (0, 0)),
                  pl.BlockSpec((nv, LATKR), lambda b, t: (0, 0)),
                  pl.BlockSpec((1, tn, LATKR), lambda b, t: (b, t, 0))],
        out_specs=(pl.BlockSpec((1, tn, nk), lambda b, t: (b, t, 0)),
                   pl.BlockSpec((1, nv, tn), lambda b, t: (b, 0, t))),
        compiler_params=_cp(("parallel", "parallel")),
        name="kv_up",
    )(w_k, w_vt, latkr3)


def _branch_kernel(oa_ref, ob_ref, oc_ref, w_ref, g0_ref, g1_ref, g2_ref, out_ref):
    gates = [jax.nn.sigmoid(g_ref[...]) for g_ref in (g0_ref, g1_ref, g2_ref)]
    us = [jnp.dot(o_ref[...], w_ref[n], preferred_element_type=F32)
          for n, o_ref in enumerate((oa_ref, ob_ref, oc_ref))]
    out_ref[...] = (gates[0] * us[0] + gates[1] * us[1] + gates[2] * us[2]).astype(out_ref.dtype)


def branch_merge(oa, ob, oc, w_branch, p):
    m = oa.shape[0]
    tm = min(1024, m)
    tn = 512
    nj = D_MODEL // tn
    o_spec = pl.BlockSpec((tm, BRANCH), lambda i, j: (i, 0))
    gate = lambda n: pl.BlockSpec((tm, tn), lambda i, j: (i, P_GATE // tn + n * nj + j))
    return pl.pallas_call(
        _branch_kernel,
        out_shape=jax.ShapeDtypeStruct((m, D_MODEL), BF16),
        grid=(m // tm, nj),
        in_specs=[o_spec, o_spec, o_spec,
                  pl.BlockSpec((3, BRANCH, tn), lambda i, j: (0, 0, j)),
                  gate(0), gate(1), gate(2)],
        out_specs=pl.BlockSpec((tm, tn), lambda i, j: (i, j)),
        compiler_params=_cp(("parallel", "parallel")),
        name="branch_merge",
    )(oa, ob, oc, w_branch, p, p, p)


def _pack_layer(l, w):
    w_in = w['w_in'][l]
    seg = lambda a, n: w_in[:, a:a + n]
    zeros = lambda n: jnp.zeros((D_MODEL, n), w_in.dtype)
    w_in_p = jnp.concatenate([
        seg(8032, 12288),
        seg(6688, 768), seg(7456, 512),
        seg(7968, 64), seg(4096, 8), seg(4104, 8), seg(6672, 16), zeros(32), zeros(128),
        seg(5136, 1536), seg(0, 3072), seg(3072, 1024), seg(4112, 1024)], axis=1).astype(BF16)
    w_uq = jnp.pad(w['mla_w_uq'][l], ((0, 0), (0, 0), (0, QSLOT - MLA_NOPE - MLA_ROPE)))
    w_uq = w_uq.reshape(MLA_Q_RANK, MLA_H * QSLOT).astype(BF16)
    wk = jnp.pad(w['mla_w_uk'][l], ((0, 0), (0, 0), (0, QSLOT - MLA_NOPE))).reshape(MLA_KV_RANK, MLA_H * QSLOT)
    eye = jnp.broadcast_to(jnp.eye(MLA_ROPE, dtype=F32)[:, None, :], (MLA_ROPE, MLA_H, MLA_ROPE))
    wkr = jnp.pad(eye, ((0, 0), (0, 0), (MLA_NOPE, QSLOT - MLA_NOPE - MLA_ROPE))).reshape(MLA_ROPE, MLA_H * QSLOT)
    wk = jnp.concatenate([wk, wkr, jnp.zeros((LATKR - MLA_KV_RANK - MLA_ROPE, MLA_H * QSLOT), F32)], axis=0)
    wv = jnp.pad(w['mla_w_uv'][l].reshape(MLA_KV_RANK, MLA_H * MLA_V), ((0, LATKR - MLA_KV_RANK), (0, 0)))
    w_k = wk.astype(BF16)
    w_vt = wv.T.astype(BF16)
    w_ukt = jnp.transpose(w['mla_w_uk'][l], (1, 2, 0)).astype(BF16)
    w_uv_h = jnp.transpose(w['mla_w_uv'][l], (1, 0, 2)).astype(BF16)

    def lane_row(vals, off):
        return jnp.zeros((1, 128), F32).at[0, off:off + vals.shape[0]].set(vals.astype(F32))

    return dict(
        attn_norm=w['attn_norm'][l], w_in=w_in_p,
        gdn_conv_w=w['gdn_conv_w'][l], gdn_alog=lane_row(w['gdn_A_log'][l], L_ALPHA),
        gdn_dtb=lane_row(w['gdn_dt_bias'][l], L_ALPHA), gdn_norm=w['gdn_norm'][l],
        ssm_conv_w=w['ssm_conv_w'][l], ssm_conv_b=w['ssm_conv_b'][l],
        ssm_alog=lane_row(w['ssm_A_log'][l], L_DT), ssm_dtb=lane_row(w['ssm_dt_bias'][l], L_DT),
        ssm_d=jnp.repeat(w['ssm_D'][l].astype(F32), SSM_P).reshape(1, BRANCH), ssm_norm=w['ssm_norm'][l],
        mla_q_norm=w['mla_q_norm'][l], w_uq=w_uq, mla_kv_norm=w['mla_kv_norm'][l], w_k=w_k, w_vt=w_vt, w_ukt=w_ukt, w_uv_h=w_uv_h,
        w_branch=w['w_branch'][l].astype(BF16), w_out=w['w_out'][l].astype(BF16),
        ffn_norm=w['ffn_norm'][l], w_up_il=interleave_cast_w_up(w['ffn_w_up'], l),
        ffn_conv_w=w['ffn_conv_w'][l], ffn_conv_b=w['ffn_conv_b'][l],
        w_down=w['ffn_w_down'][l].astype(BF16))


def _rope_tables(pos):
    half = MLA_ROPE // 2
    inv = ROPE_THETA ** (-jnp.arange(half, dtype=F32) / half)
    ang = pos.astype(F32)[:, None] * inv[None, :]
    cos, sin = jnp.cos(ang), jnp.sin(ang)
    n = pos.shape[0]
    z = lambda k: jnp.zeros((n, k), F32)
    cq = jnp.concatenate([jnp.ones((n, MLA_NOPE), F32), cos, cos, z(64)], axis=1)
    saq = jnp.concatenate([z(MLA_NOPE + half), sin, z(64)], axis=1)
    sbq = jnp.concatenate([z(MLA_NOPE), -sin, z(half + 64)], axis=1)
    ck = jnp.concatenate([cos, cos, z(64)], axis=1)
    sak = jnp.concatenate([z(half), sin, z(64)], axis=1)
    sbk = jnp.concatenate([-sin, z(half + 64)], axis=1)
    return cq, saq, sbq, ck, sak, sbk


def _pad_state(st):
    return jnp.pad(st, ((0, 0), (8 - st.shape[1], 0), (0, 0)))


def _layer(x, tabs, q_off, state, lw):
    latkr_past, gconv, gssm, mconv, mssm, fconv = state
    bsz, s, _ = x.shape
    m = bsz * s
    cq_t, saq_t, sbq_t, ck_t, sak_t, sbk_t = tabs
    x2 = x.reshape(m, D_MODEL)
    mm_tm = 512

    p = rmsnorm_matmul(x2, lw['attn_norm'], lw['w_in'], F32, tm=mm_tm, tn=1024, name="in_proj")
    p3 = p.reshape(bsz, s, P_TOT)

    qkv_act, gconv8 = causal_conv(p3, P_QKV, GDN_QKV, _pad_state(gconv), lw['gdn_conv_w'],
                                  jnp.zeros((GDN_QKV,), F32), act=True)
    o_a, gssm_new = gdn(qkv_act, p3, gssm, lw['gdn_alog'], lw['gdn_dtb'], lw['gdn_norm'])

    xbc_act, mconv8 = causal_conv(p3, P_XBC, SSM_CONV_DIM, _pad_state(mconv), lw['ssm_conv_w'],
                                  lw['ssm_conv_b'], act=True)
    o_b, mssm_new = ssd(xbc_act, p3, mssm.reshape(bsz, SSM_H // 2, 128, 128), lw['ssm_alog'], lw['ssm_dtb'],
                        lw['ssm_d'], lw['ssm_norm'])
    mssm_new = mssm_new.reshape(bsz, SSM_H, SSM_P, SSM_N)

    cqn, lat, kr, latkr = mla_prep(p, lw['mla_q_norm'], lw['mla_kv_norm'], ck_t, sak_t, sbk_t)
    tq_tab = min(mm_tm, m)
    q_full = matmul(cqn, lw['w_uq'], BF16, tm=mm_tm, tn=MLA_H * QSLOT, name="q_up",
                    extras=[(cq_t, (tq_tab, QSLOT), lambda i, j: (i, 0)),
                            (saq_t, (tq_tab, QSLOT), lambda i, j: (i, 0)),
                            (sbq_t, (tq_tab, QSLOT), lambda i, j: (i, 0))],
                    epilogue=_q_rope_epilogue)
    latkr3 = latkr.reshape(bsz, s, LATKR)
    q3 = q_full.reshape(bsz, s, MLA_H * QSLOT)
    if latkr_past is None and q_off == 0 and s % 512 == 0:
        k3, vt = kv_up(lw['w_k'], lw['w_vt'], latkr3)
        o_c = flash_attention_causal(q3, k3, vt)
    else:
        past = jnp.zeros((bsz, 0, LATKR), BF16) if latkr_past is None else latkr_past
        nk_real = past.shape[1] + s
        nk_pad = -(-nk_real // 128) * 128
        latkr_all = jnp.concatenate([past, latkr3, jnp.zeros((bsz, nk_pad - nk_real, LATKR), BF16)], axis=1)
        assert s & (s - 1) == 0 and MLA_H * s <= 512
        o_c = latent_attention(q_absorb(q_full, lw['w_ukt']), latkr_all, lw['w_uv_h'],
                               bsz=bsz, s=s, q_off=q_off, nk_real=nk_real)

    merged = branch_merge(o_a.reshape(m, BRANCH), o_b.reshape(m, BRANCH), o_c.reshape(m, BRANCH),
                          lw['w_branch'], p)
    x2 = matmul_residual(merged, lw['w_out'], x2, tm=1024, tn=1024, name="out_proj")

    h2 = rmsnorm(x2, lw['ffn_norm'], BF16)
    if s % 512 == 0:
        act, fconv8 = ffn_up_fused(h2, lw['w_up_il'], _pad_state(fconv), lw['ffn_conv_w'], lw['ffn_conv_b'],
                                   bsz=bsz, tm=max(t for t in (512, 1024, 2048) if s % t == 0))
    else:
        up = matmul_tile_major(h2, lw['w_up_il'])
        act, fconv8 = ffn_conv_gate_il(up.reshape(bsz, s, 2 * D_FF), _pad_state(fconv), lw['ffn_conv_w'],
                                       lw['ffn_conv_b'])
    x2 = matmul_residual(act.reshape(m, D_FF), lw['w_down'], x2, tm=mm_tm, tn=512, name="ffn_down")

    new = (lat.reshape(bsz, s, MLA_KV_RANK), kr.reshape(bsz, s, MLA_ROPE),
           gconv8[:, 8 - gconv.shape[1]:], gssm_new, mconv8[:, 8 - mconv.shape[1]:], mssm_new,
           fconv8[:, 8 - fconv.shape[1]:])
    return x2.reshape(bsz, s, D_MODEL), new


def kernel(x_prompt, x_sample, cache_mla_latent, cache_mla_krope, state_gdn_conv, state_gdn_ssm, state_ssm_conv, state_ssm, state_ffn_conv, attn_norm, w_in, gdn_conv_w, gdn_A_log, gdn_dt_bias, gdn_norm, ssm_conv_w, ssm_conv_b, ssm_A_log, ssm_dt_bias, ssm_D, ssm_norm, mla_q_norm, mla_w_uq, mla_kv_norm, mla_w_uk, mla_w_uv, w_branch, w_out, ffn_norm, ffn_w_up, ffn_conv_w, ffn_conv_b, ffn_w_down, final_norm):
    weights = dict(attn_norm=attn_norm, w_in=w_in, gdn_conv_w=gdn_conv_w, gdn_A_log=gdn_A_log,
                   gdn_dt_bias=gdn_dt_bias, gdn_norm=gdn_norm, ssm_conv_w=ssm_conv_w, ssm_conv_b=ssm_conv_b,
                   ssm_A_log=ssm_A_log, ssm_dt_bias=ssm_dt_bias, ssm_D=ssm_D, ssm_norm=ssm_norm,
                   mla_q_norm=mla_q_norm, mla_w_uq=mla_w_uq, mla_kv_norm=mla_kv_norm, mla_w_uk=mla_w_uk,
                   mla_w_uv=mla_w_uv, w_branch=w_branch, w_out=w_out, ffn_norm=ffn_norm, ffn_w_up=ffn_w_up,
                   ffn_conv_w=ffn_conv_w, ffn_conv_b=ffn_conv_b, ffn_w_down=ffn_w_down)
    depth = w_in.shape[0]
    bp, sp, _ = x_prompt.shape
    bs, ss, _ = x_sample.shape
    n_past = cache_mla_latent.shape[2]
    pos_p = jnp.arange(sp, dtype=jnp.int32)
    pos_s = n_past + jnp.arange(ss, dtype=jnp.int32)
    tabs_p = _rope_tables(pos_p)
    tabs_s = tuple(jnp.tile(t, (bs, 1)) for t in _rope_tables(pos_s))
    zero_state = (
        None,
        jnp.zeros((bp,) + state_gdn_conv.shape[2:], F32),
        jnp.zeros((bp,) + state_gdn_ssm.shape[2:], F32),
        jnp.zeros((bp,) + state_ssm_conv.shape[2:], F32),
        jnp.zeros((bp,) + state_ssm.shape[2:], F32),
        jnp.zeros((bp,) + state_ffn_conv.shape[2:], F32),
    )
    xp, xs = x_prompt, x_sample
    new_p, new_s = [], []
    for l in range(depth):
        lw = _pack_layer(l, weights)
        xp, st_p = _layer(xp, tabs_p, 0, zero_state, lw)
        latkr_past = pack_latent_cache(cache_mla_latent, cache_mla_krope, l)
        st_in = (latkr_past, state_gdn_conv[l], state_gdn_ssm[l], state_ssm_conv[l], state_ssm[l],
                 state_ffn_conv[l])
        xs, st_s = _layer(xs, tabs_s, n_past, st_in, lw)
        new_p.append(st_p)
        new_s.append(st_s)
    y_prompt = rmsnorm(xp.reshape(bp * sp, D_MODEL), final_norm, F32).reshape(bp, sp, D_MODEL)
    y_sample = rmsnorm(xs.reshape(bs * ss, D_MODEL), final_norm, F32).reshape(bs, ss, D_MODEL)
    outs_p = [jnp.stack(t) for t in zip(*new_p)]
    outs_s = [jnp.stack(t) for t in zip(*new_s)]
    return (y_prompt, y_sample, *outs_p, *outs_s)
```
